```python
import jax, jax.numpy as jnp
from jax import lax
import numpy as np

D_MODEL = 1024
BATCH = 2
SEQ = 8192
DEPTH = 4

GRID_W = 64
CTX_LEN = 256
EPS = 1e-6

MLA_HEADS = 8
MLA_NOPE = 64
MLA_ROPE = 32
MLA_QK = MLA_NOPE + MLA_ROPE
MLA_V = 64
Q_LORA = 256
KV_LORA = 128
ROPE_THETA = 10000.0
Q_BLOCK = 128

ML_HEADS = 4
ML_DH = 64
ML_W = ML_HEADS * ML_DH
ML_CONV = 5
ML_CHUNK = 64

GLA_HEADS = 4
GLA_DK = 32
GLA_DV = 64
GLA_LR = 16
GLA_TAU = 16.0
GLA_CHUNK = 64

N_GROUPS = 4
EXP_PER_GROUP = 8
N_EXPERTS = N_GROUPS * EXP_PER_GROUP
TOP_K = 2
D_EXPERT = 256
MOE_BLOCK = 128

IN_WIDTHS = (Q_LORA, KV_LORA, MLA_ROPE, ML_W, ML_W, ML_W, 4 * ML_HEADS,
             GLA_HEADS * GLA_DK, GLA_HEADS * GLA_DK, GLA_HEADS * GLA_DV, GLA_HEADS * GLA_DV, 2 * GLA_LR)
D_IN = sum(IN_WIDTHS)
D_MIX = MLA_HEADS * MLA_V + ML_W + GLA_HEADS * GLA_DV

kernel_name = 'hybrid_mla_mlstm_gla_hmoe_dit'


def _split_cols(p):
    offs = np.cumsum(IN_WIDTHS)[:-1].tolist()
    return jnp.split(p, offs, axis=-1)


def _rms_norm(x, w):
    xf = x.astype(jnp.float32)
    y = xf * lax.rsqrt(jnp.mean(xf * xf, axis=-1, keepdims=True) + EPS)
    return (y * w.astype(jnp.float32)).astype(x.dtype)


def _modulate(x, w, shift, scale):
    return _rms_norm(x, w) * (1 + scale) + shift


def _axial_rope(n_tokens, dtype):
    rows = n_tokens // GRID_W
    row = jnp.broadcast_to(jnp.arange(rows, dtype=jnp.float32)[:, None], (rows, GRID_W)).reshape(-1)
    col = jnp.broadcast_to(jnp.arange(GRID_W, dtype=jnp.float32)[None, :], (rows, GRID_W)).reshape(-1)
    n_freq = MLA_ROPE // 4
    inv = ROPE_THETA ** (-jnp.arange(n_freq, dtype=jnp.float32) / n_freq)
    ang = jnp.concatenate([row[:, None] * inv, col[:, None] * inv], axis=-1)
    return jnp.cos(ang).astype(dtype), jnp.sin(ang).astype(dtype)


def _apply_rope(x, cos, sin):
    half = MLA_ROPE // 2
    x1, x2 = x[..., :half], x[..., half:]
    return jnp.concatenate([x1 * cos - x2 * sin, x1 * sin + x2 * cos], axis=-1)


def _mla_qkv(cq, ckv, kr, q_a_norm, w_uq, kv_a_norm, w_ukv, q_norm_w, k_norm_w, rope):
    b, t, _ = cq.shape
    q = (_rms_norm(cq, q_a_norm) @ w_uq).reshape(b, t, MLA_HEADS, MLA_QK)
    kv = (_rms_norm(ckv, kv_a_norm) @ w_ukv).reshape(b, t, MLA_HEADS, MLA_NOPE + MLA_V)
    k = jnp.concatenate([kv[..., :MLA_NOPE],
                         jnp.broadcast_to(kr[:, :, None, :], (b, t, MLA_HEADS, MLA_ROPE))], axis=-1)
    v = kv[..., MLA_NOPE:]
    q = _rms_norm(q, q_norm_w)
    k = _rms_norm(k, k_norm_w)
    if rope is not None:
        cos, sin = rope[0][:, None, :], rope[1][:, None, :]
        q = jnp.concatenate([q[..., :MLA_NOPE], _apply_rope(q[..., MLA_NOPE:], cos, sin)], axis=-1)
        k = jnp.concatenate([k[..., :MLA_NOPE], _apply_rope(k[..., MLA_NOPE:], cos, sin)], axis=-1)
    return q, k, v


def _attend(q, k, v):
    s = jnp.einsum('bqhd,bkhd->bhqk', q, k, preferred_element_type=jnp.float32) * (MLA_QK ** -0.5)
    p = jax.nn.softmax(s, axis=-1).astype(v.dtype)
    return jnp.einsum('bhqk,bkhd->bqhd', p, v)


def _latent_attention(q, k_lat, v_lat, k_ctx, v_ctx):
    b, s, h, _ = q.shape
    k_all = jnp.concatenate([k_lat, k_ctx], axis=1)
    v_all = jnp.concatenate([v_lat, v_ctx], axis=1)
    qb = q.reshape(b, s // Q_BLOCK, Q_BLOCK, h, MLA_QK).transpose(1, 0, 2, 3, 4)
    o = lax.map(lambda qi: _attend(qi, k_all, v_all), qb)
    return o.transpose(1, 0, 2, 3, 4).reshape(b, s, h * MLA_V)


def _dwconv(x, w, b):
    k, ch = w.shape
    y = lax.conv_general_dilated(x, w[:, None, :].astype(x.dtype), window_strides=(1,),
                                 padding=[(k // 2, k // 2)],
                                 dimension_numbers=('NWC', 'WIO', 'NWC'),
                                 feature_group_count=ch)
    return y + b


def _mlstm_prep(xm, v, gates, conv_w, conv_b, wq, wk, gate_b):
    b, t, _ = xm.shape
    xconv = jax.nn.silu(_dwconv(xm, conv_w, conv_b))
    xh = xconv.reshape(b, t, ML_HEADS, ML_DH)
    q = jnp.einsum('bthd,hde->bhte', xh, wq).astype(jnp.float32) * (ML_DH ** -0.5)
    k = jnp.einsum('bthd,hde->bhte', xh, wk).astype(jnp.float32)
    v = v.reshape(b, t, ML_HEADS, ML_DH).transpose(0, 2, 1, 3).astype(jnp.float32)
    g = (gates.astype(jnp.float32) + gate_b.astype(jnp.float32)).reshape(b, t, 2, 2, ML_HEADS)
    g = g.transpose(2, 3, 0, 4, 1)
    logi = g[:, 0]
    logf = jax.nn.log_sigmoid(g[:, 1])
    return xconv, (q, k, v, logi[0], logf[0]), (q, k, v, logi[1], logf[1])


def _mlstm_scan(q, k, v, logi, logf, state):
    n_chunk = q.shape[2] // ML_CHUNK
    tri = jnp.tril(jnp.ones((ML_CHUNK, ML_CHUNK), dtype=bool))

    def chunks(a):
        return jnp.moveaxis(a.reshape(a.shape[:2] + (n_chunk, ML_CHUNK) + a.shape[3:]), 2, 0)

    def step(carry, xs):
        c_st, n_st, m_st = carry
        qc, kc, vc, ic, fc = xs
        bcum = jnp.cumsum(fc, axis=-1)
        d = jnp.where(tri, bcum[..., :, None] - bcum[..., None, :] + ic[..., None, :], -jnp.inf)
        inter = bcum + m_st[..., None]
        m_t = jnp.maximum(inter, jnp.max(d, axis=-1))
        s = jnp.einsum('bhtd,bhjd->bhtj', qc, kc) * jnp.exp(d - m_t[..., None])
        w_inter = jnp.exp(inter - m_t)
        num = (jnp.einsum('bhtj,bhjv->bhtv', s, vc)
               + w_inter[..., None] * jnp.einsum('bhtd,bhdv->bhtv', qc, c_st))
        den = jnp.sum(s, axis=-1) + w_inter * jnp.einsum('bhtd,bhd->bht', qc, n_st)
        h = num / jnp.maximum(jnp.abs(den), jnp.exp(-m_t))[..., None]
        b_end = bcum[..., -1]
        g = b_end[..., None] - bcum + ic
        m_new = jnp.maximum(b_end + m_st, jnp.max(g, axis=-1))
        w_j = jnp.exp(g - m_new[..., None])
        decay = jnp.exp(b_end + m_st - m_new)
        c_new = decay[..., None, None] * c_st + jnp.einsum('bhj,bhjd,bhjv->bhdv', w_j, kc, vc)
        n_new = decay[..., None] * n_st + jnp.einsum('bhj,bhjd->bhd', w_j, kc)
        return (c_new, n_new, m_new), h

    state, hs = lax.scan(step, state, tuple(chunks(a) for a in (q, k, v, logi, logf)))
    h = jnp.moveaxis(hs, 0, 2)
    return h.reshape(h.shape[:2] + (-1, h.shape[-1])), state


def _mlstm_out(h, xconv, o, norm_w, skip):
    b, nh, t, dv = h.shape
    hn = _rms_norm(h.transpose(0, 2, 1, 3), norm_w.reshape(nh, dv)).reshape(b, t, nh * dv)
    return jax.nn.sigmoid(o) * (hn.astype(xconv.dtype) + skip * xconv)


def _gla_prep(q, k, v, a_lr, wa, ba):
    b, t, _ = q.shape

    def heads(a, d):
        return a.reshape(b, t, GLA_HEADS, d).transpose(0, 2, 1, 3).astype(jnp.float32)

    q = heads(q, GLA_DK) * (GLA_DK ** -0.5)
    k = heads(k, GLA_DK)
    v = heads(v, GLA_DV)
    a = a_lr.reshape(b, t, 2, GLA_LR)
    pre = jnp.einsum('btur,urk->ubtk', a, wa).astype(jnp.float32) + ba[:, None, None, :].astype(jnp.float32)
    loga = jax.nn.log_sigmoid(pre) / GLA_TAU
    loga = loga.reshape(2, b, t, GLA_HEADS, GLA_DK).transpose(0, 1, 3, 2, 4)
    return (q, k, v, loga[0]), (q, k, v, loga[1])


def _gla_scan(q, k, v, loga, state):
    n_chunk = q.shape[2] // GLA_CHUNK
    tri = jnp.tril(jnp.ones((GLA_CHUNK, GLA_CHUNK), dtype=bool))[:, :, None]

    def chunks(a):
        return jnp.moveaxis(a.reshape(a.shape[:2] + (n_chunk, GLA_CHUNK) + a.shape[3:]), 2, 0)

    def step(s_st, xs):
        qc, kc, vc, ac = xs
        bcum = jnp.cumsum(ac, axis=2)
        diff = jnp.where(tri, bcum[:, :, :, None, :] - bcum[:, :, None, :, :], -jnp.inf)
        att = jnp.einsum('bhtd,bhjd,bhtjd->bhtj', qc, kc, jnp.exp(diff))
        o = jnp.einsum('bhtj,bhjv->bhtv', att, vc) + jnp.einsum('bhtd,bhdv->bhtv', qc * jnp.exp(bcum), s_st)
        b_end = bcum[:, :, -1]
        s_new = (jnp.exp(b_end)[..., None] * s_st
                 + jnp.einsum('bhjd,bhjv->bhdv', kc * jnp.exp(b_end[:, :, None, :] - bcum), vc))
        return s_new, o

    state, os_ = lax.scan(step, state, tuple(chunks(a) for a in (q, k, v, loga)))
    o = jnp.moveaxis(os_, 0, 2)
    return o.reshape(o.shape[:2] + (-1, o.shape[-1])), state


def _gla_out(o, r, norm_w):
    b, nh, t, dv = o.shape
    on = _rms_norm(o.transpose(0, 2, 1, 3), norm_w.reshape(nh, dv)).reshape(b, t, nh * dv)
    return on.astype(r.dtype) * jax.nn.silu(r)


def _bidirectional(scan_fn, init, ctx_fwd, ctx_bwd, lat_fwd, lat_bwd):
    def flip(args):
        return tuple(jnp.flip(a, axis=2) for a in args)

    h_cf, st_f = scan_fn(*ctx_fwd, init)
    h_lf, _ = scan_fn(*lat_fwd, st_f)
    h_cb, st_b = scan_fn(*flip(ctx_bwd), init)
    h_lb, _ = scan_fn(*flip(lat_bwd), st_b)
    return h_cf + jnp.flip(h_cb, axis=2), h_lf + jnp.flip(h_lb, axis=2)


def _moe(h, w_grp, b_grp, w_erouter, b_erouter, w_gate, w_up, w_down):
    n, d = h.shape
    hb = h.reshape(n // MOE_BLOCK, MOE_BLOCK, d)

    def blk(t):
        g_prob = jax.nn.softmax((t @ w_grp + b_grp).astype(jnp.float32), axis=-1)
        g_w, g_i = lax.top_k(g_prob, 1)
        e_logit = (t @ w_erouter + b_erouter).astype(jnp.float32).reshape(-1, N_GROUPS, EXP_PER_GROUP)
        e_sel = e_logit[jnp.arange(t.shape[0]), g_i[:, 0]]
        e_w, e_i = lax.top_k(jax.nn.softmax(e_sel, axis=-1), TOP_K)
        e_w = e_w / jnp.sum(e_w, axis=-1, keepdims=True)
        wts = g_w * e_w
        ids = g_i * EXP_PER_GROUP + e_i
        gate = jnp.sum(jax.nn.one_hot(ids, N_EXPERTS, dtype=jnp.float32) * wts[..., None], axis=1)
        a = jax.nn.silu(jnp.einsum('nd,edf->nef', t, w_gate)) * jnp.einsum('nd,edf->nef', t, w_up)
        a = a * gate[..., None].astype(a.dtype)
        return jnp.einsum('nef,efd->nd', a, w_down)

    return lax.map(blk, hb).reshape(n, d)


def _layer(x, xc, c, c_ctx, w_mod, b_mod, norm1_w, w_in, q_a_norm, w_uq, kv_a_norm, w_ukv,
           q_norm_w, k_norm_w, ml_conv_w, ml_conv_b, ml_wq, ml_wk, ml_gate_b, ml_norm_w, ml_skip,
           gla_wa, gla_ba, gla_norm_w, w_out, norm2_w, w_grp, b_grp, w_erouter, b_erouter,
           w_gate, w_up, w_down, rope, need_ctx):
    b, s, d = x.shape
    mod_l = jnp.split((jax.nn.silu(c) @ w_mod + b_mod)[:, None, :], 6, axis=-1)
    mod_c = jnp.split((jax.nn.silu(c_ctx) @ w_mod + b_mod)[None, None, :], 6, axis=-1)

    (cq_l, ckv_l, kr_l, mx_l, mv_l, mo_l, mg_l, gq_l, gk_l, gv_l, gr_l, ga_l) = _split_cols(
        _modulate(x, norm1_w, mod_l[0], mod_l[1]) @ w_in)
    (cq_c, ckv_c, kr_c, mx_c, mv_c, mo_c, mg_c, gq_c, gk_c, gv_c, gr_c, ga_c) = _split_cols(
        _modulate(xc, norm1_w, mod_c[0], mod_c[1]) @ w_in)

    q_l, k_l, v_l = _mla_qkv(cq_l, ckv_l, kr_l, q_a_norm, w_uq, kv_a_norm, w_ukv, q_norm_w, k_norm_w, rope)
    q_c, k_c, v_c = _mla_qkv(cq_c, ckv_c, kr_c, q_a_norm, w_uq, kv_a_norm, w_ukv, q_norm_w, k_norm_w, None)
    a_l = _latent_attention(q_l, k_l, v_l, k_c, v_c)

    xconv_l, mf_l, mb_l = _mlstm_prep(mx_l, mv_l, mg_l, ml_conv_w, ml_conv_b, ml_wq, ml_wk, ml_gate_b)
    xconv_c, mf_c, mb_c = _mlstm_prep(mx_c, mv_c, mg_c, ml_conv_w, ml_conv_b, ml_wq, ml_wk, ml_gate_b)
    ml_init = (jnp.zeros((b, ML_HEADS, ML_DH, ML_DH), jnp.float32),
               jnp.zeros((b, ML_HEADS, ML_DH), jnp.float32),
               jnp.zeros((b, ML_HEADS), jnp.float32))
    mh_c, mh_l = _bidirectional(_mlstm_scan, ml_init, mf_c, mb_c, mf_l, mb_l)
    m_l = _mlstm_out(mh_l, xconv_l, mo_l, ml_norm_w, ml_skip)

    gf_l, gb_l = _gla_prep(gq_l, gk_l, gv_l, ga_l, gla_wa, gla_ba)
    gf_c, gb_c = _gla_prep(gq_c, gk_c, gv_c, ga_c, gla_wa, gla_ba)
    gla_init = jnp.zeros((b, GLA_HEADS, GLA_DK, GLA_DV), jnp.float32)
    go_c, go_l = _bidirectional(_gla_scan, gla_init, gf_c, gb_c, gf_l, gb_l)
    g_l = _gla_out(go_l, gr_l, gla_norm_w)

    x = x + mod_l[2] * (jnp.concatenate([a_l, m_l, g_l], axis=-1) @ w_out)
    h2 = _modulate(x, norm2_w, mod_l[3], mod_l[4])
    x = x + mod_l[5] * _moe(h2.reshape(b * s, d), w_grp, b_grp, w_erouter, b_erouter,
                           w_gate, w_up, w_down).reshape(b, s, d)

    if need_ctx:
        n_ctx = xc.shape[1]
        a_c = _attend(q_c, k_c, v_c).reshape(b, n_ctx, MLA_HEADS * MLA_V)
        m_c = _mlstm_out(mh_c, xconv_c, mo_c, ml_norm_w, ml_skip)
        g_c = _gla_out(go_c, gr_c, gla_norm_w)
        xc = xc + mod_c[2] * (jnp.concatenate([a_c, m_c, g_c], axis=-1) @ w_out)
        h2c = _modulate(xc, norm2_w, mod_c[3], mod_c[4])
        xc = xc + mod_c[5] * _moe(h2c.reshape(b * n_ctx, d), w_grp, b_grp, w_erouter, b_erouter,
                                 w_gate, w_up, w_down).reshape(b, n_ctx, d)
    return x, xc


def setup_inputs(seed: int = 0) -> dict:
    key = jax.random.key(seed)
    ks = iter(jax.random.split(key, 48))

    def nrm(shape, scale):
        return scale * jax.random.normal(next(ks), shape, dtype=jnp.float32)

    def gain(shape):
        return 1.0 + nrm(shape, 0.02)

    L, D = DEPTH, D_MODEL
    ib = nrm((L, 2, 1, ML_HEADS), 0.1)
    fb = jnp.linspace(3.0, 6.0, ML_HEADS, dtype=jnp.float32) + nrm((L, 2, 1, ML_HEADS), 0.1)
    ml_gate_b = jnp.concatenate([ib, fb], axis=2).reshape(L, 4 * ML_HEADS)
    return {
        'x': nrm((BATCH, SEQ, D), 1.0),
        'c': nrm((BATCH, D), 1.0),
        'ctx': nrm((BATCH, CTX_LEN, D), 1.0),
        'c_ctx': nrm((D,), 1.0),
        'w_mod': nrm((L, D, 6 * D), 0.5 * D ** -0.5),
        'b_mod': nrm((L, 6 * D), 0.01),
        'norm1_w': gain((L, D)),
        'w_in': nrm((L, D, D_IN), D ** -0.5),
        'q_a_norm': gain((L, Q_LORA)),
        'w_uq': nrm((L, Q_LORA, MLA_HEADS * MLA_QK), Q_LORA ** -0.5),
        'kv_a_norm': gain((L, KV_LORA)),
        'w_ukv': nrm((L, KV_LORA, MLA_HEADS * (MLA_NOPE + MLA_V)), KV_LORA ** -0.5),
        'q_norm_w': gain((L, MLA_QK)),
        'k_norm_w': gain((L, MLA_QK)),
        'ml_conv_w': nrm((L, ML_CONV, ML_W), ML_CONV ** -0.5),
        'ml_conv_b': nrm((L, ML_W), 0.01),
        'ml_wq': nrm((L, ML_HEADS, ML_DH, ML_DH), ML_DH ** -0.5),
        'ml_wk': nrm((L, ML_HEADS, ML_DH, ML_DH), ML_DH ** -0.5),
        'ml_gate_b': ml_gate_b,
        'ml_norm_w': gain((L, ML_W)),
        'ml_skip': gain((L, ML_W)),
        'gla_wa': nrm((L, 2, GLA_LR, GLA_HEADS * GLA_DK), GLA_LR ** -0.5),
        'gla_ba': nrm((L, 2, GLA_HEADS * GLA_DK), 0.1),
        'gla_norm_w': gain((L, GLA_HEADS * GLA_DV)),
        'w_out': nrm((L, D_MIX, D), D_MIX ** -0.5),
        'norm2_w': gain((L, D)),
        'w_grp': nrm((L, D, N_GROUPS), D ** -0.5),
        'b_grp': nrm((L, N_GROUPS), 0.01),
        'w_erouter': nrm((L, D, N_EXPERTS), D ** -0.5),
        'b_erouter': nrm((L, N_EXPERTS), 0.01),
        'w_gate': nrm((L, N_EXPERTS, D, D_EXPERT), D ** -0.5),
        'w_up': nrm((L, N_EXPERTS, D, D_EXPERT), D ** -0.5),
        'w_down': nrm((L, N_EXPERTS, D_EXPERT, D), D_EXPERT ** -0.5),
    }


def reference(x, c, ctx, c_ctx, w_mod, b_mod, norm1_w, w_in, q_a_norm, w_uq, kv_a_norm, w_ukv,
              q_norm_w, k_norm_w, ml_conv_w, ml_conv_b, ml_wq, ml_wk, ml_gate_b, ml_norm_w, ml_skip,
              gla_wa, gla_ba, gla_norm_w, w_out, norm2_w, w_grp, b_grp, w_erouter, b_erouter,
              w_gate, w_up, w_down):
    rope = _axial_rope(x.shape[1], x.dtype)
    xc = ctx
    for l in range(DEPTH):
        x, xc = _layer(x, xc, c, c_ctx, w_mod[l], b_mod[l], norm1_w[l], w_in[l], q_a_norm[l], w_uq[l],
                       kv_a_norm[l], w_ukv[l], q_norm_w[l], k_norm_w[l], ml_conv_w[l], ml_conv_b[l],
                       ml_wq[l], ml_wk[l], ml_gate_b[l], ml_norm_w[l], ml_skip[l], gla_wa[l], gla_ba[l],
                       gla_norm_w[l], w_out[l], norm2_w[l], w_grp[l], b_grp[l], w_erouter[l], b_erouter[l],
                       w_gate[l], w_up[l], w_down[l], rope=rope, need_ctx=(l < DEPTH - 1))
    return x
```

```python
import functools

import jax
import jax.numpy as jnp
import numpy as np
from jax import lax
from jax.experimental import pallas as pl
from jax.experimental.pallas import tpu as pltpu

F32 = jnp.float32
BF16 = jnp.bfloat16
HIGHEST = lax.Precision.HIGHEST

EPS = 1e-6
GRID_W = 64
ROPE_THETA = 10000.0

MLA_HEADS = 8
MLA_NOPE = 64
MLA_ROPE = 32
MLA_QK = MLA_NOPE + MLA_ROPE
MLA_V = 64
Q_LORA = 256
KV_LORA = 128

ML_HEADS = 4
ML_DH = 64
ML_W = ML_HEADS * ML_DH
ML_CONV = 5

GLA_HEADS = 4
GLA_DK = 32
GLA_DV = 64
GLA_LR = 16
GLA_TAU = 16.0

N_GROUPS = 4
EXP_PER_GROUP = 8
N_EXPERTS = N_GROUPS * EXP_PER_GROUP
D_EXPERT = 256

LANE = 128
TM = 256
ML_CHUNK = 256
GLA_CHUNK = 128
VMEM_LIMIT = 56 * 1024 * 1024

C_CQ, C_MX, C_MV, C_MO, C_GV, C_GR = 0, 256, 512, 768, 1024, 1280
C_CKV, C_GQ, C_GK, C_SMF, C_SMB = 1536, 1664, 1792, 1920, 2048
D_INP = 2176
SM_GATE = 0
SM_GA = 8
SM_KR = 64
R_GRP = 0
R_EXP = 4


def _cparams(sem):
    return pltpu.CompilerParams(dimension_semantics=sem, vmem_limit_bytes=VMEM_LIMIT)


def _silu(x):
    return x * jax.nn.sigmoid(x)


def _log_sigmoid(x):
    return -(jnp.maximum(-x, 0.0) + jnp.log1p(jnp.exp(-jnp.abs(x))))


def _nt_dot(a, b, **kw):
    return lax.dot_general(a, b, (((1,), (1,)), ((), ())), preferred_element_type=F32, **kw)


def _tn_dot(a, b, **kw):
    return lax.dot_general(a, b, (((0,), (0,)), ((), ())), preferred_element_type=F32, **kw)


def _mods_kernel(cc_ref, w_ref, b_ref, o_ref):
    a = _silu(cc_ref[...])
    o_ref[0] = jnp.dot(a, w_ref[0], precision=HIGHEST, preferred_element_type=F32) + b_ref[0]


def _mods(cc, w_mod, b_mod):
    depth, d, d6 = w_mod.shape
    nb = 1536
    return pl.pallas_call(
        _mods_kernel,
        grid=(depth, d6 // nb),
        in_specs=[
            pl.BlockSpec((8, d), lambda l, j: (0, 0)),
            pl.BlockSpec((1, d, nb), lambda l, j: (l, 0, j)),
            pl.BlockSpec((1, 1, nb), lambda l, j: (l, 0, j)),
        ],
        out_specs=pl.BlockSpec((1, 8, nb), lambda l, j: (l, 0, j)),
        out_shape=jax.ShapeDtypeStruct((depth, 8, d6), F32),
        compiler_params=_cparams(("arbitrary", "arbitrary")),
        name="adaln_mods",
    )(cc, w_mod, b_mod.reshape(depth, 1, d6))


def _mod_row(b, i):
    return jnp.where(i == 0, 2, b)


def _inproj_kernel(x_ref, mod_ref, nw_ref, w_ref, o_ref):
    x = x_ref[0]
    y = x * lax.rsqrt(jnp.mean(x * x, axis=-1, keepdims=True) + EPS) * nw_ref[...]
    mod = mod_ref[0]
    h = y * (1.0 + mod[1:2]) + mod[0:1]
    o_ref[0] = jnp.dot(h.astype(BF16), w_ref[...], preferred_element_type=F32)


def _inproj(xs, mods, norm_w, w_in_p):
    b, t, d = xs.shape
    nt = t // TM
    return pl.pallas_call(
        _inproj_kernel,
        grid=(b, nt),
        in_specs=[
            pl.BlockSpec((1, TM, d), lambda bi, i: (bi, i, 0)),
            pl.BlockSpec((1, 6, d), lambda bi, i: (_mod_row(bi, i), 0, 0)),
            pl.BlockSpec((1, d), lambda bi, i: (0, 0)),
            pl.BlockSpec((d, D_INP), lambda bi, i: (0, 0)),
        ],
        out_specs=pl.BlockSpec((1, TM, D_INP), lambda bi, i: (bi, i, 0)),
        out_shape=jax.ShapeDtypeStruct((b, t, D_INP), F32),
        compiler_params=_cparams(("parallel", "parallel")),
        name="in_proj",
    )(xs, mods, norm_w.reshape(1, d), w_in_p)


def _mla_prep_kernel(cq_ref, ckv_ref, sm_ref, qan_ref, wuq_ref, kvan_ref, wuk_ref, wuv_ref,
                     qn_ref, kn_ref, cos_ref, sa_ref, sb_ref, q_ref, k_ref, v_ref):
    cq = cq_ref[0]
    cqn = cq * lax.rsqrt(jnp.mean(cq * cq, axis=-1, keepdims=True) + EPS) * qan_ref[...]
    qall = jnp.dot(cqn.astype(BF16), wuq_ref[...], preferred_element_type=F32)
    ckv = ckv_ref[0]
    ckvn = (ckv * lax.rsqrt(jnp.mean(ckv * ckv, axis=-1, keepdims=True) + EPS)
            * kvan_ref[...]).astype(BF16)
    kall = jnp.dot(ckvn, wuk_ref[...], preferred_element_type=F32)
    vall = jnp.dot(ckvn, wuv_ref[...], preferred_element_type=F32)
    lane = lax.broadcasted_iota(jnp.int32, (1, LANE), 1)
    kr = jnp.where((lane >= SM_KR) & (lane < SM_KR + MLA_ROPE), sm_ref[0], 0.0)
    cos, sa, sb = cos_ref[...], sa_ref[...], sb_ref[...]

    def rope(x):
        return x * cos + pltpu.roll(x, 16, 1) * sa + pltpu.roll(x, LANE - 16, 1) * sb

    def head_norm(x, w):
        return x * lax.rsqrt(jnp.sum(x * x, axis=-1, keepdims=True) * (1.0 / MLA_QK) + EPS) * w

    for h in range(MLA_HEADS):
        sl = slice(h * LANE, (h + 1) * LANE)
        qh = rope(head_norm(qall[:, sl], qn_ref[...]))
        q_ref[0, h] = (qh * (MLA_QK ** -0.5)).astype(BF16)
        kh = rope(head_norm(kall[:, sl] + kr, kn_ref[...]))
        k_ref[0, h] = kh.astype(BF16)
        v_ref[0, h] = jnp.where(lane == MLA_V, 1.0, vall[:, sl]).astype(BF16)


def _mla_prep(p, q_a_norm, wuq_p, kv_a_norm, wuk_p, wuv_p, qn_p, kn_p, cos_t, sa_t, sb_t):
    b, t, _ = p.shape
    nt = t // TM
    hw = MLA_HEADS * LANE
    full = lambda shape: pl.BlockSpec(shape, lambda bi, i: (0,) * len(shape))
    tab = pl.BlockSpec((TM, LANE), lambda bi, i: (i, 0))
    out = pl.BlockSpec((1, MLA_HEADS, TM, LANE), lambda bi, i: (bi, 0, i, 0))
    shp = jax.ShapeDtypeStruct((b, MLA_HEADS, t, LANE), BF16)
    return pl.pallas_call(
        _mla_prep_kernel,
        grid=(b, nt),
        in_specs=[
            pl.BlockSpec((1, TM, Q_LORA), lambda bi, i: (bi, i, C_CQ // Q_LORA)),
            pl.BlockSpec((1, TM, KV_LORA), lambda bi, i: (bi, i, C_CKV // KV_LORA)),
            pl.BlockSpec((1, TM, LANE), lambda bi, i: (bi, i, C_SMF // LANE)),
            full((1, Q_LORA)), full((Q_LORA, hw)), full((1, KV_LORA)),
            full((KV_LORA, hw)), full((KV_LORA, hw)), full((1, LANE)), full((1, LANE)),
            tab, tab, tab,
        ],
        out_specs=[out, out, out],
        out_shape=[shp, shp, shp],
        compiler_params=_cparams(("parallel", "parallel")),
        name="mla_prep",
    )(p, p, p, q_a_norm.reshape(1, -1), wuq_p, kv_a_norm.reshape(1, -1), wuk_p, wuv_p,
      qn_p, kn_p, cos_t, sa_t, sb_t)


ATT_HP = 2


def _attn_kernel(q_ref, k_ref, v_ref, o_ref, *, n_kv):
    nkv = jnp.where(pl.program_id(2) == 0, 1, n_kv)
    outs = []
    for hh in range(ATT_HP):
        q = q_ref[0, hh]

        def body(j, carry, hh=hh, q=q):
            m, acc = carry
            off = pl.multiple_of(j * TM, TM)
            kb = k_ref[0, hh, pl.ds(off, TM), :]
            vb = v_ref[0, hh, pl.ds(off, TM), :]
            s = _nt_dot(q, kb)
            m_new = jnp.maximum(m, jnp.max(s, axis=-1, keepdims=True))
            alpha = jnp.exp(m - m_new)
            p = jnp.exp(s - m_new)
            acc = acc * alpha + jnp.dot(p.astype(BF16), vb, preferred_element_type=F32)
            return m_new, acc

        m0 = jnp.full((TM, 1), -jnp.inf, F32)
        acc0 = jnp.zeros((TM, LANE), F32)
        _, acc = lax.fori_loop(0, nkv, body, (m0, acc0))
        outs.append(acc[:, :MLA_V] / acc[:, MLA_V:MLA_V + 1])
    o_ref[0] = jnp.concatenate(outs, axis=-1).astype(BF16)


def _attention(q, k, v):
    b, h, t, _ = q.shape
    nt = t // TM
    kv = pl.BlockSpec((1, ATT_HP, t, LANE), lambda bi, hp, i: (bi, hp, 0, 0))
    return pl.pallas_call(
        functools.partial(_attn_kernel, n_kv=nt),
        grid=(b, h // ATT_HP, nt),
        in_specs=[pl.BlockSpec((1, ATT_HP, TM, LANE), lambda bi, hp, i: (bi, hp, i, 0)), kv, kv],
        out_specs=pl.BlockSpec((1, TM, ATT_HP * MLA_V), lambda bi, hp, i: (bi, i, hp)),
        out_shape=jax.ShapeDtypeStruct((b, t, h * MLA_V), BF16),
        compiler_params=_cparams(("parallel", "parallel", "arbitrary")),
        name="mla_attention",
    )(q, k, v)


def _ml_prep_kernel(x_ref, prev_ref, next_ref, cw_ref, cb_ref, wq_ref, wk_ref,
                    xc_ref, q_ref, k_ref, *, n_tiles):
    i = pl.program_id(1)
    x = x_ref[0]
    prev = jnp.where(i <= 1, 0.0, prev_ref[0])
    nxt = jnp.where((i == 0) | (i == n_tiles - 1), 0.0, next_ref[0])
    ext = jnp.concatenate([prev, x, nxt], axis=0)
    n_ext = TM + 16
    cw = cw_ref[...]
    acc = jnp.zeros((TM, ML_W), F32) + cb_ref[...]
    for kk in range(ML_CONV):
        sh = (ML_CONV // 2 - kk) % n_ext
        shifted = ext if sh == 0 else pltpu.roll(ext, sh, 0)
        acc = acc + cw[kk:kk + 1] * shifted[8:8 + TM]
    xc = _silu(acc)
    xc_ref[0] = xc
    xb = xc.astype(BF16)
    q_ref[0] = jnp.dot(xb, wq_ref[...], preferred_element_type=F32).astype(BF16)
    k_ref[0] = jnp.dot(xb, wk_ref[...], preferred_element_type=F32).astype(BF16)


def _ml_prep(p, conv_w8, conv_b, wq_bd, wk_bd):
    b, t, _ = p.shape
    nt = t // TM
    r8 = TM // 8
    full = lambda shape: pl.BlockSpec(shape, lambda bi, i: (0,) * len(shape))
    cb = C_MX // ML_W
    blk = pl.BlockSpec((1, TM, ML_W), lambda bi, i: (bi, i, 0))
    return pl.pallas_call(
        functools.partial(_ml_prep_kernel, n_tiles=nt),
        grid=(b, nt),
        in_specs=[
            pl.BlockSpec((1, TM, ML_W), lambda bi, i: (bi, i, cb)),
            pl.BlockSpec((1, 8, ML_W), lambda bi, i: (bi, jnp.maximum(i * r8 - 1, 0), cb)),
            pl.BlockSpec((1, 8, ML_W), lambda bi, i: (bi, jnp.minimum((i + 1) * r8, nt * r8 - 1), cb)),
            full((8, ML_W)), full((1, ML_W)), full((ML_W, ML_W)), full((ML_W, ML_W)),
        ],
        out_specs=[blk, blk, blk],
        out_shape=[jax.ShapeDtypeStruct((b, t, ML_W), F32),
                   jax.ShapeDtypeStruct((b, t, ML_W), BF16),
                   jax.ShapeDtypeStruct((b, t, ML_W), BF16)],
        compiler_params=_cparams(("parallel", "parallel")),
        name="mlstm_prep",
    )(p, p, p, conv_w8, conv_b.reshape(1, ML_W), wq_bd, wk_bd)


def _scan_chunk(d, step, n_ctx_chunks, n_chunks):
    bwd = jnp.where(step < n_ctx_chunks, n_ctx_chunks - 1 - step, n_chunks - 1 - (step - n_ctx_chunks))
    return jnp.where(d == 0, step, bwd)


def _ml_scan_kernel(q_ref, k_ref, v_ref, sm_ref, gb_ref, h_ref, s_ref, m_ref):
    d = pl.program_id(0)
    L = ML_CHUNK

    @pl.when(pl.program_id(2) == 0)
    def _():
        s_ref[...] = jnp.zeros_like(s_ref)
        m_ref[...] = jnp.zeros_like(m_ref)

    row = lax.broadcasted_iota(jnp.int32, (L, L), 0)
    col = lax.broadcasted_iota(jnp.int32, (L, L), 1)
    mask = (row - col) * (1 - 2 * d) >= 0
    tri = mask.astype(F32)
    lane = lax.broadcasted_iota(jnp.int32, (1, LANE), 1)
    eye8 = (lax.broadcasted_iota(jnp.int32, (8, LANE), 0)
            == lax.broadcasted_iota(jnp.int32, (8, LANE), 1)).astype(F32)

    g = sm_ref[0] + gb_ref[0]
    lf = _log_sigmoid(g)
    bc = jnp.dot(tri, lf, precision=HIGHEST, preferred_element_type=F32)
    g_rows = _nt_dot(eye8, g, precision=HIGHEST)
    bc_rows = _nt_dot(eye8, bc, precision=HIGHEST)

    q = q_ref[0]
    k = k_ref[0]
    v = v_ref[0]
    outs = []
    for pair in range(ML_HEADS // 2):
        sl = slice(pair * LANE, (pair + 1) * LANE)
        q_blk, k_blk, v_blk = q[:, sl], k[:, sl], v[:, sl]
        pair_out = []
        for sub in range(2):
            h = pair * 2 + sub
            head_lanes = (lane >= sub * ML_DH) & (lane < (sub + 1) * ML_DH)
            qh = jnp.where(head_lanes, q_blk, jnp.zeros_like(q_blk))
            vs = v_blk if sub == 0 else pltpu.roll(v_blk, ML_DH, 1)
            v_ext = jnp.where(lane < ML_DH, vs, jnp.where(lane == ML_DH, 1.0, 0.0)).astype(BF16)

            li_c = g[:, SM_GATE + h:SM_GATE + h + 1]
            bc_c = bc[:, SM_GATE + 4 + h:SM_GATE + 5 + h]
            li_r = g_rows[h:h + 1, :]
            bc_r = bc_rows[4 + h:5 + h, :]
            m_st = m_ref[h][0:1, 0:1]

            dmat = jnp.where(mask, bc_c + (li_r - bc_r), -jnp.inf)
            inter = bc_c + m_st
            m_t = jnp.maximum(inter, jnp.max(dmat, axis=-1, keepdims=True))
            e = jnp.exp(dmat - m_t)
            s = (_nt_dot(qh, k_blk) * e).astype(BF16)
            tot = (jnp.dot(s, v_ext, preferred_element_type=F32)
                   + jnp.exp(inter - m_t) * jnp.dot(qh, s_ref[h].astype(BF16),
                                                    preferred_element_type=F32))
            den = tot[:, ML_DH:ML_DH + 1]
            pair_out.append(tot / jnp.maximum(jnp.abs(den), jnp.exp(-m_t)))

            b_end = jnp.where(d == 0, bc_c[L - 1:L], bc_c[0:1])
            g_col = b_end - bc_c + li_c
            m_new = jnp.maximum(b_end + m_st, jnp.max(g_col, axis=0, keepdims=True))
            kw = jnp.where(head_lanes, k_blk.astype(F32) * jnp.exp(g_col - m_new), 0.0).astype(BF16)
            s_ref[h] = jnp.exp(b_end + m_st - m_new) * s_ref[h] + _tn_dot(kw, v_ext)
            m_ref[h] = jnp.broadcast_to(m_new, m_ref.shape[1:])
        outs.append(jnp.where(lane < ML_DH, pair_out[0], pltpu.roll(pair_out[1], ML_DH, 1)))
    h_ref[0, 0] = jnp.concatenate(outs, axis=-1)


def _ml_scan(q, k, p, gate_b_p):
    b, t, _ = q.shape
    nc = t // ML_CHUNK
    chunk = functools.partial(_scan_chunk, n_ctx_chunks=TM // ML_CHUNK, n_chunks=nc)
    blk = lambda cb: pl.BlockSpec((1, ML_CHUNK, ML_W), lambda d, bi, s: (bi, chunk(d, s), cb))
    return pl.pallas_call(
        _ml_scan_kernel,
        grid=(2, b, nc),
        in_specs=[
            blk(0), blk(0), blk(C_MV // ML_W),
            pl.BlockSpec((1, ML_CHUNK, LANE), lambda d, bi, s: (bi, chunk(d, s), C_SMF // LANE + d)),
            pl.BlockSpec((1, 1, LANE), lambda d, bi, s: (d, 0, 0)),
        ],
        out_specs=pl.BlockSpec((1, 1, ML_CHUNK, ML_W), lambda d, bi, s: (d, bi, chunk(d, s), 0)),
        out_shape=jax.ShapeDtypeStruct((2, b, t, ML_W), F32),
        scratch_shapes=[pltpu.VMEM((ML_HEADS, LANE, LANE), F32), pltpu.VMEM((ML_HEADS, 8, LANE), F32)],
        compiler_params=_cparams(("parallel", "parallel", "arbitrary")),
        name="mlstm_scan",
    )(q, k, p, p, gate_b_p)


def _gla_scan_kernel(q_ref, k_ref, v_ref, sm_ref, wa_ref, ba_ref, o_ref, s_ref):
    d = pl.program_id(0)
    L = GLA_CHUNK

    @pl.when(pl.program_id(2) == 0)
    def _():
        s_ref[...] = jnp.zeros_like(s_ref)

    row = lax.broadcasted_iota(jnp.int32, (L, L), 0)
    col = lax.broadcasted_iota(jnp.int32, (L, L), 1)
    mask = (row - col) * (1 - 2 * d) >= 0
    tri = mask.astype(F32)
    lane_k = lax.broadcasted_iota(jnp.int32, (1, GLA_HEADS * GLA_DK), 1)
    lane_v = lax.broadcasted_iota(jnp.int32, (1, GLA_HEADS * GLA_DV), 1)
    eye = (lax.broadcasted_iota(jnp.int32, (LANE, LANE), 0)
           == lax.broadcasted_iota(jnp.int32, (LANE, LANE), 1)).astype(F32)

    pre = jnp.dot(sm_ref[0], wa_ref[0], precision=HIGHEST, preferred_element_type=F32) + ba_ref[0]
    loga = _log_sigmoid(pre) * (1.0 / GLA_TAU)
    bc = jnp.dot(tri, loga, precision=HIGHEST, preferred_element_type=F32)
    ref_row = bc[L // 2 - 1:L // 2]
    b_end = jnp.where(d == 0, bc[L - 1:L], bc[0:1])

    q = q_ref[0] * (GLA_DK ** -0.5)
    k = k_ref[0]
    v = v_ref[0].astype(BF16)
    q_in = (q * jnp.exp(bc - ref_row))
    k_in = (k * jnp.exp(ref_row - bc)).astype(BF16)
    q_st = (q * jnp.exp(bc)).astype(BF16)
    k_st = (k * jnp.exp(b_end - bc)).astype(BF16)

    blockdiag = (lax.broadcasted_iota(jnp.int32, s_ref.shape, 0) // GLA_DK
                 == lax.broadcasted_iota(jnp.int32, s_ref.shape, 1) // GLA_DV)
    s_old = s_ref[...]
    o = jnp.dot(q_st, jnp.where(blockdiag, s_old, 0.0).astype(BF16), preferred_element_type=F32)
    for h in range(GLA_HEADS):
        qh = jnp.where(lane_k // GLA_DK == h, q_in, 0.0).astype(BF16)
        att = jnp.where(mask, _nt_dot(qh, k_in), 0.0).astype(BF16)
        oh = jnp.dot(att, v, preferred_element_type=F32)
        o = o + jnp.where(lane_v // GLA_DV == h, oh, 0.0)
    o_ref[0, 0] = o

    decay_col = jnp.exp(_nt_dot(eye, jnp.broadcast_to(b_end, (8, LANE)), precision=HIGHEST)[:, 0:1])
    s_ref[...] = decay_col * s_old + _tn_dot(k_st, v)


def _gla_scan(p, wa_p, ba_p):
    b, t, _ = p.shape
    nc = t // GLA_CHUNK
    chunk = functools.partial(_scan_chunk, n_ctx_chunks=TM // GLA_CHUNK, n_chunks=nc)
    kw, vw = GLA_HEADS * GLA_DK, GLA_HEADS * GLA_DV
    return pl.pallas_call(
        _gla_scan_kernel,
        grid=(2, b, nc),
        in_specs=[
            pl.BlockSpec((1, GLA_CHUNK, kw), lambda d, bi, s: (bi, chunk(d, s), C_GQ // kw)),
            pl.BlockSpec((1, GLA_CHUNK, kw), lambda d, bi, s: (bi, chunk(d, s), C_GK // kw)),
            pl.BlockSpec((1, GLA_CHUNK, vw), lambda d, bi, s: (bi, chunk(d, s), C_GV // vw)),
            pl.BlockSpec((1, GLA_CHUNK, LANE), lambda d, bi, s: (bi, chunk(d, s), C_SMF // LANE + d)),
            pl.BlockSpec((1, LANE, kw), lambda d, bi, s: (d, 0, 0)),
            pl.BlockSpec((1, 1, kw), lambda d, bi, s: (d, 0, 0)),
        ],
        out_specs=pl.BlockSpec((1, 1, GLA_CHUNK, vw), lambda d, bi, s: (d, bi, chunk(d, s), 0)),
        out_shape=jax.ShapeDtypeStruct((2, b, t, vw), F32),
        scratch_shapes=[pltpu.VMEM((kw, vw), F32)],
        compiler_params=_cparams(("parallel", "parallel", "arbitrary")),
        name="gla_scan",
    )(p, p, p, p, wa_p, ba_p)


def _outproj_kernel(a_ref, mhf_ref, mhb_ref, xc_ref, mo_ref, gof_ref, gob_ref, gr_ref, x_ref, mod_ref,
                    mnw_ref, msk_ref, gnw_ref, wout_ref, n2w_ref, wr_ref, br_ref,
                    xo_ref, h2_ref, gate_ref):
    grp64 = (lax.broadcasted_iota(jnp.int32, (ML_W, ML_W), 0) // ML_DH
             == lax.broadcasted_iota(jnp.int32, (ML_W, ML_W), 1) // ML_DH).astype(F32) * (1.0 / ML_DH)

    def head_norm(x, w):
        ms = jnp.dot(x * x, grp64, precision=HIGHEST, preferred_element_type=F32)
        return x * lax.rsqrt(ms + EPS) * w

    m_l = jax.nn.sigmoid(mo_ref[0]) * (head_norm(mhf_ref[0, 0] + mhb_ref[0, 0], mnw_ref[...])
                                       + msk_ref[...] * xc_ref[0])
    g_l = head_norm(gof_ref[0, 0] + gob_ref[0, 0], gnw_ref[...]) * _silu(gr_ref[0])
    na = MLA_HEADS * MLA_V
    res = (jnp.dot(a_ref[0], wout_ref[0:na], preferred_element_type=F32)
           + jnp.dot(m_l.astype(BF16), wout_ref[na:na + ML_W], preferred_element_type=F32)
           + jnp.dot(g_l.astype(BF16), wout_ref[na + ML_W:], preferred_element_type=F32))
    mod = mod_ref[0]
    x = x_ref[0] + mod[2:3] * res
    xo_ref[0] = x
    h2 = (x * lax.rsqrt(jnp.mean(x * x, axis=-1, keepdims=True) + EPS) * n2w_ref[...]
          * (1.0 + mod[4:5]) + mod[3:4])
    h2_ref[0] = h2.astype(BF16)

    logits = jnp.dot(h2, wr_ref[...], precision=HIGHEST, preferred_element_type=F32) + br_ref[...]
    lane = lax.broadcasted_iota(jnp.int32, (1, LANE), 1)
    lane_f = lane.astype(F32)
    neg = -jnp.inf
    gl = jnp.where(lane < R_EXP, logits, neg)
    gmax = jnp.max(gl, axis=-1, keepdims=True)
    g_w = 1.0 / jnp.sum(jnp.exp(gl - gmax), axis=-1, keepdims=True)
    g_i = jnp.min(jnp.where(gl == gmax, lane_f, float(LANE)), axis=-1, keepdims=True)
    grp_of_lane = ((lane - R_EXP) // EXP_PER_GROUP).astype(F32)
    in_grp = (lane >= R_EXP) & (lane < R_EXP + N_EXPERTS) & (grp_of_lane == g_i)
    el = jnp.where(in_grp, logits, neg)
    m1 = jnp.max(el, axis=-1, keepdims=True)
    i1 = jnp.min(jnp.where(el == m1, lane_f, float(LANE)), axis=-1, keepdims=True)
    el2 = jnp.where(lane_f == i1, neg, el)
    m2 = jnp.max(el2, axis=-1, keepdims=True)
    i2 = jnp.min(jnp.where(el2 == m2, lane_f, float(LANE)), axis=-1, keepdims=True)
    p2 = jnp.exp(m2 - m1)
    w1 = 1.0 / (1.0 + p2)
    gate_ref[0] = g_w * (jnp.where(lane_f == i1, w1, 0.0) + jnp.where(lane_f == i2, p2 * w1, 0.0))


def _outproj(a, mh, xconv, p, go, xs, mods, ml_norm_w, ml_skip, gla_norm_w, w_out_b, norm2_w, wr_p, br_p):
    b, t, d = xs.shape
    nt = t // TM
    full = lambda shape: pl.BlockSpec(shape, lambda bi, i: (0,) * len(shape))
    tok = lambda w, cb=0: pl.BlockSpec((1, TM, w), lambda bi, i: (bi, i, cb))
    dirblk = lambda dd: pl.BlockSpec((1, 1, TM, ML_W), lambda bi, i: (dd, bi, i, 0))
    return pl.pallas_call(
        _outproj_kernel,
        grid=(b, nt),
        in_specs=[
            tok(MLA_HEADS * MLA_V), dirblk(0), dirblk(1), tok(ML_W), tok(ML_W, C_MO // ML_W),
            dirblk(0), dirblk(1), tok(ML_W, C_GR // ML_W), tok(d),
            pl.BlockSpec((1, 6, d), lambda bi, i: (_mod_row(bi, i), 0, 0)),
            full((1, ML_W)), full((1, ML_W)), full((1, ML_W)), full((d, d)), full((1, d)),
            full((d, LANE)), full((1, LANE)),
        ],
        out_specs=[tok(d), tok(d), tok(LANE)],
        out_shape=[jax.ShapeDtypeStruct((b, t, d), F32), jax.ShapeDtypeStruct((b, t, d), BF16),
                   jax.ShapeDtypeStruct((b, t, LANE), F32)],
        compiler_params=_cparams(("parallel", "parallel")),
        name="out_proj_router",
    )(a, mh, mh, xconv, p, go, go, p, xs, mods, ml_norm_w.reshape(1, -1), ml_skip.reshape(1, -1),
      gla_norm_w.reshape(1, -1), w_out_b, norm2_w.reshape(1, -1), wr_p, br_p)


def _moe_kernel(h2_ref, gate_ref, x_ref, mod_ref, wgu_ref, wd_ref, o_ref, acc_ref):
    e = pl.program_id(2)

    @pl.when(e == 0)
    def _():
        acc_ref[...] = jnp.zeros_like(acc_ref)

    gu = jnp.dot(h2_ref[0], wgu_ref[0], preferred_element_type=F32)
    lane = lax.broadcasted_iota(jnp.int32, (1, LANE), 1)
    g = jnp.sum(jnp.where(lane == e + R_EXP, gate_ref[0], 0.0), axis=-1, keepdims=True)
    act = (_silu(gu[:, :D_EXPERT]) * gu[:, D_EXPERT:] * g).astype(BF16)
    acc_ref[...] += jnp.dot(act, wd_ref[0], preferred_element_type=F32)

    @pl.when(e == N_EXPERTS - 1)
    def _():
        o_ref[0] = x_ref[0] + mod_ref[0][5:6] * acc_ref[...]


def _moe(h2, gate, xs, mods, wgu, wd):
    b, t, d = xs.shape
    nt = t // TM
    tok = lambda w: pl.BlockSpec((1, TM, w), lambda bi, i, e: (bi, i, 0))
    return pl.pallas_call(
        _moe_kernel,
        grid=(b, nt, N_EXPERTS),
        in_specs=[
            tok(d), tok(LANE), tok(d),
            pl.BlockSpec((1, 6, d), lambda bi, i, e: (_mod_row(bi, i), 0, 0)),
            pl.BlockSpec((1, d, 2 * D_EXPERT), lambda bi, i, e: (e, 0, 0)),
            pl.BlockSpec((1, D_EXPERT, d), lambda bi, i, e: (e, 0, 0)),
        ],
        out_specs=tok(d),
        out_shape=jax.ShapeDtypeStruct((b, t, d), F32),
        scratch_shapes=[pltpu.VMEM((TM, d), F32)],
        compiler_params=_cparams(("parallel", "parallel", "arbitrary")),
        name="moe_experts",
    )(h2, gate, xs, mods, wgu, wd)


def _rope_tables(n_ctx, n_lat):
    rows = n_lat // GRID_W
    row = jnp.broadcast_to(jnp.arange(rows, dtype=F32)[:, None], (rows, GRID_W)).reshape(-1)
    col = jnp.broadcast_to(jnp.arange(GRID_W, dtype=F32)[None, :], (rows, GRID_W)).reshape(-1)
    n_freq = MLA_ROPE // 4
    inv = ROPE_THETA ** (-jnp.arange(n_freq, dtype=F32) / n_freq)
    ang = jnp.concatenate([row[:, None] * inv, col[:, None] * inv], axis=-1)
    cos, sin = jnp.cos(ang), jnp.sin(ang)
    half = MLA_ROPE // 2
    z = lambda w: jnp.zeros((n_lat, w), F32)
    o = lambda w: jnp.ones((n_lat, w), F32)
    tail = LANE - MLA_QK
    cos_t = jnp.concatenate([o(MLA_NOPE), cos, cos, o(tail)], axis=-1)
    sa_t = jnp.concatenate([z(MLA_NOPE + half), sin, z(tail)], axis=-1)
    sb_t = jnp.concatenate([z(MLA_NOPE), -sin, z(half + tail)], axis=-1)
    ctx1 = jnp.ones((n_ctx, LANE), F32)
    ctx0 = jnp.zeros((n_ctx, LANE), F32)
    return (jnp.concatenate([ctx1, cos_t], 0), jnp.concatenate([ctx0, sa_t], 0),
            jnp.concatenate([ctx0, sb_t], 0))


def _pad_cols(a, width):
    return jnp.pad(a, [(0, 0)] * (a.ndim - 1) + [(0, width - a.shape[-1])])


def _layer_weights(w_in, w_uq, w_ukv, q_norm_w, k_norm_w, ml_conv_w, ml_wq, ml_wk, ml_gate_b,
                   gla_wa, gla_ba, w_out, w_grp, b_grp, w_erouter, b_erouter, w_gate, w_up, w_down):
    d = w_in.shape[0]
    o = np.cumsum((0, Q_LORA, KV_LORA, MLA_ROPE, ML_W, ML_W, ML_W, 4 * ML_HEADS, GLA_HEADS * GLA_DK,
                   GLA_HEADS * GLA_DK, GLA_HEADS * GLA_DV, GLA_HEADS * GLA_DV, 2 * GLA_LR))
    seg = lambda j: w_in[:, o[j]:o[j + 1]]
    cq, ckv, kr, mx, mv, mo, mg, gq, gk, gv, gr, ga = (seg(j) for j in range(12))
    z = lambda w: jnp.zeros((d, w), F32)

    def small(di):
        return jnp.concatenate([mg[:, di * 8:(di + 1) * 8], ga[:, di * GLA_LR:(di + 1) * GLA_LR],
                                z(SM_KR - SM_GA - GLA_LR), kr, z(LANE - SM_KR - MLA_ROPE)], axis=-1)

    w_in_p = jnp.concatenate([cq, mx, mv, mo, gv, gr, ckv, gq, gk, small(0), small(1)], axis=-1).astype(BF16)

    wuq_p = _pad_cols(w_uq.reshape(Q_LORA, MLA_HEADS, MLA_QK), LANE).reshape(Q_LORA, -1).astype(BF16)
    ukv = w_ukv.reshape(KV_LORA, MLA_HEADS, MLA_NOPE + MLA_V)
    wuk_p = _pad_cols(ukv[..., :MLA_NOPE], LANE).reshape(KV_LORA, -1).astype(BF16)
    wuv_p = _pad_cols(ukv[..., MLA_NOPE:], LANE).reshape(KV_LORA, -1).astype(BF16)
    qn_p = _pad_cols(q_norm_w.reshape(1, -1), LANE)
    kn_p = _pad_cols(k_norm_w.reshape(1, -1), LANE)

    conv_w8 = jnp.pad(ml_conv_w, ((0, 8 - ML_CONV), (0, 0)))
    bd = lambda w: jax.scipy.linalg.block_diag(*[w[h] for h in range(ML_HEADS)])
    wq_bd = (bd(ml_wq) * (ML_DH ** -0.5)).astype(BF16)
    wk_bd = bd(ml_wk).astype(BF16)
    gate_b_p = _pad_cols(ml_gate_b.reshape(2, 1, 2 * ML_HEADS), LANE)

    wa_p = jnp.pad(gla_wa, ((0, 0), (SM_GA, LANE - SM_GA - GLA_LR), (0, 0)))
    ba_p = gla_ba.reshape(2, 1, -1)

    wr_p = _pad_cols(jnp.concatenate([w_grp, w_erouter], axis=-1), LANE)
    br_p = _pad_cols(jnp.concatenate([b_grp, b_erouter]).reshape(1, -1), LANE)
    wgu = jnp.concatenate([w_gate, w_up], axis=-1).astype(BF16)
    return dict(w_in_p=w_in_p, wuq_p=wuq_p, wuk_p=wuk_p, wuv_p=wuv_p, qn_p=qn_p, kn_p=kn_p,
                conv_w8=conv_w8, wq_bd=wq_bd, wk_bd=wk_bd, gate_b_p=gate_b_p, wa_p=wa_p, ba_p=ba_p,
                w_out_b=w_out.astype(BF16), wr_p=wr_p, br_p=br_p, wgu=wgu, wd=w_down.astype(BF16))


def kernel(x, c, ctx, c_ctx, w_mod, b_mod, norm1_w, w_in, q_a_norm, w_uq, kv_a_norm, w_ukv,
           q_norm_w, k_norm_w, ml_conv_w, ml_conv_b, ml_wq, ml_wk, ml_gate_b, ml_norm_w, ml_skip,
           gla_wa, gla_ba, gla_norm_w, w_out, norm2_w, w_grp, b_grp, w_erouter, b_erouter,
           w_gate, w_up, w_down):
    b, s, d = x.shape
    n_ctx = ctx.shape[1]
    depth = w_mod.shape[0]
    assert n_ctx == TM and s % TM == 0 and b == 2

    cc = jnp.concatenate([c, c_ctx[None, :], jnp.zeros((8 - b - 1, d), F32)], axis=0)
    mods_all = _mods(cc, w_mod, b_mod).reshape(depth, 8, 6, d)
    cos_t, sa_t, sb_t = _rope_tables(n_ctx, s)
    xs = jnp.concatenate([ctx, x], axis=1)

    for l in range(depth):
        w = _layer_weights(w_in[l], w_uq[l], w_ukv[l], q_norm_w[l], k_norm_w[l], ml_conv_w[l],
                           ml_wq[l], ml_wk[l], ml_gate_b[l], gla_wa[l], gla_ba[l], w_out[l],
                           w_grp[l], b_grp[l], w_erouter[l], b_erouter[l], w_gate[l], w_up[l], w_down[l])
        mods = mods_all[l]
        p = _inproj(xs, mods, norm1_w[l], w["w_in_p"])
        q, k, v = _mla_prep(p, q_a_norm[l], w["wuq_p"], kv_a_norm[l], w["wuk_p"], w["wuv_p"],
                            w["qn_p"], w["kn_p"], cos_t, sa_t, sb_t)
        a = _attention(q, k, v)
        xconv, mq, mk = _ml_prep(p, w["conv_w8"], ml_conv_b[l], w["wq_bd"], w["wk_bd"])
        mh = _ml_scan(mq, mk, p, w["gate_b_p"])
        go = _gla_scan(p, w["wa_p"], w["ba_p"])
        xs, h2, gate = _outproj(a, mh, xconv, p, go, xs, mods, ml_norm_w[l], ml_skip[l], gla_norm_w[l],
                                w["w_out_b"], norm2_w[l], w["wr_p"], w["br_p"])
        xs = _moe(h2, gate, xs, mods, w["wgu"], w["wd"])
    return xs[:, n_ctx:, :]
```

```python
import functools

import jax
import jax.numpy as jnp
import numpy as np
from jax import lax
from jax.experimental import pallas as pl
from jax.experimental.pallas import tpu as pltpu

F32 = jnp.float32
BF16 = jnp.bfloat16
HIGHEST = lax.Precision.HIGHEST

EPS = 1e-6
GRID_W = 64
ROPE_THETA = 10000.0

MLA_HEADS = 8
MLA_NOPE = 64
MLA_ROPE = 32
MLA_QK = MLA_NOPE + MLA_ROPE
MLA_V = 64
Q_LORA = 256
KV_LORA = 128

ML_HEADS = 4
ML_DH = 64
ML_W = ML_HEADS * ML_DH
ML_CONV = 5

GLA_HEADS = 4
GLA_DK = 32
GLA_DV = 64
GLA_LR = 16
GLA_TAU = 16.0

N_GROUPS = 4
EXP_PER_GROUP = 8
N_EXPERTS = N_GROUPS * EXP_PER_GROUP
D_EXPERT = 256

LANE = 128
TM = 256
ML_CHUNK = 256
GLA_CHUNK = 128
VMEM_LIMIT = 56 * 1024 * 1024

C_CQ, C_MX, C_MV, C_MO, C_GV, C_GR = 0, 256, 512, 768, 1024, 1280
C_CKV, C_GQ, C_GK, C_SMF, C_SMB = 1536, 1664, 1792, 1920, 2048
D_INP = 2176
SM_GATE = 0
SM_GA = 8
SM_KR = 64
R_GRP = 0
R_EXP = 4
RT_E1 = 0
RT_W1 = 2


def _cparams(sem):
    return pltpu.CompilerParams(dimension_semantics=sem, vmem_limit_bytes=VMEM_LIMIT)


def _silu(x):
    return x * jax.nn.sigmoid(x)


def _log_sigmoid(x):
    return -(jnp.maximum(-x, 0.0) + jnp.log1p(jnp.exp(-jnp.abs(x))))


def _nt_dot(a, b, **kw):
    return lax.dot_general(a, b, (((1,), (1,)), ((), ())), preferred_element_type=F32, **kw)


def _tn_dot(a, b, **kw):
    return lax.dot_general(a, b, (((0,), (0,)), ((), ())), preferred_element_type=F32, **kw)


def _mods_kernel(cc_ref, w_ref, b_ref, o_ref):
    a = _silu(cc_ref[...])
    o_ref[0] = jnp.dot(a, w_ref[0], precision=HIGHEST, preferred_element_type=F32) + b_ref[0]


def _mods(cc, w_mod, b_mod):
    depth, d, d6 = w_mod.shape
    nb = 1536
    return pl.pallas_call(
        _mods_kernel,
        grid=(depth, d6 // nb),
        in_specs=[
            pl.BlockSpec((8, d), lambda l, j: (0, 0)),
            pl.BlockSpec((1, d, nb), lambda l, j: (l, 0, j)),
            pl.BlockSpec((1, 1, nb), lambda l, j: (l, 0, j)),
        ],
        out_specs=pl.BlockSpec((1, 8, nb), lambda l, j: (l, 0, j)),
        out_shape=jax.ShapeDtypeStruct((depth, 8, d6), F32),
        compiler_params=_cparams(("arbitrary", "arbitrary")),
        name="adaln_mods",
    )(cc, w_mod, b_mod.reshape(depth, 1, d6))


def _mod_row(b, i):
    return jnp.where(i == 0, 2, b)


def _inproj_kernel(x_ref, mod_ref, nw_ref, w_ref, o_ref):
    x = x_ref[0]
    y = x * lax.rsqrt(jnp.mean(x * x, axis=-1, keepdims=True) + EPS) * nw_ref[...]
    mod = mod_ref[0]
    h = y * (1.0 + mod[1:2]) + mod[0:1]
    o_ref[0] = jnp.dot(h.astype(BF16), w_ref[...], preferred_element_type=F32)


def _inproj(xs, mods, norm_w, w_in_p):
    b, t, d = xs.shape
    nt = t // TM
    return pl.pallas_call(
        _inproj_kernel,
        grid=(b, nt),
        in_specs=[
            pl.BlockSpec((1, TM, d), lambda bi, i: (bi, i, 0)),
            pl.BlockSpec((1, 6, d), lambda bi, i: (_mod_row(bi, i), 0, 0)),
            pl.BlockSpec((1, d), lambda bi, i: (0, 0)),
            pl.BlockSpec((d, D_INP), lambda bi, i: (0, 0)),
        ],
        out_specs=pl.BlockSpec((1, TM, D_INP), lambda bi, i: (bi, i, 0)),
        out_shape=jax.ShapeDtypeStruct((b, t, D_INP), F32),
        compiler_params=_cparams(("parallel", "parallel")),
        name="in_proj",
    )(xs, mods, norm_w.reshape(1, d), w_in_p)


def _mla_prep_kernel(cq_ref, ckv_ref, sm_ref, qan_ref, wuq_ref, kvan_ref, wuk_ref, wuv_ref,
                     qn_ref, kn_ref, cos_ref, sa_ref, sb_ref, q_ref, k_ref, v_ref):
    cq = cq_ref[0]
    cqn = cq * lax.rsqrt(jnp.mean(cq * cq, axis=-1, keepdims=True) + EPS) * qan_ref[...]
    qall = jnp.dot(cqn.astype(BF16), wuq_ref[...], preferred_element_type=F32)
    ckv = ckv_ref[0]
    ckvn = (ckv * lax.rsqrt(jnp.mean(ckv * ckv, axis=-1, keepdims=True) + EPS)
            * kvan_ref[...]).astype(BF16)
    kall = jnp.dot(ckvn, wuk_ref[...], preferred_element_type=F32)
    vall = jnp.dot(ckvn, wuv_ref[...], preferred_element_type=F32)
    lane = lax.broadcasted_iota(jnp.int32, (1, LANE), 1)
    kr = jnp.where((lane >= SM_KR) & (lane < SM_KR + MLA_ROPE), sm_ref[0], 0.0)
    cos, sa, sb = cos_ref[...], sa_ref[...], sb_ref[...]

    def rope(x):
        return x * cos + pltpu.roll(x, 16, 1) * sa + pltpu.roll(x, LANE - 16, 1) * sb

    def head_norm(x, w):
        return x * lax.rsqrt(jnp.sum(x * x, axis=-1, keepdims=True) * (1.0 / MLA_QK) + EPS) * w

    for h in range(MLA_HEADS):
        sl = slice(h * LANE, (h + 1) * LANE)
        qh = rope(head_norm(qall[:, sl], qn_ref[...]))
        q_ref[0, h] = (qh * (MLA_QK ** -0.5)).astype(BF16)
        kh = rope(head_norm(kall[:, sl] + kr, kn_ref[...]))
        k_ref[0, h] = kh.astype(BF16)
        v_ref[0, h] = jnp.where(lane == MLA_V, 1.0, vall[:, sl]).astype(BF16)


def _mla_prep(p, q_a_norm, wuq_p, kv_a_norm, wuk_p, wuv_p, qn_p, kn_p, cos_t, sa_t, sb_t):
    b, t, _ = p.shape
    nt = t // TM
    hw = MLA_HEADS * LANE
    full = lambda shape: pl.BlockSpec(shape, lambda bi, i: (0,) * len(shape))
    tab = pl.BlockSpec((TM, LANE), lambda bi, i: (i, 0))
    out = pl.BlockSpec((1, MLA_HEADS, TM, LANE), lambda bi, i: (bi, 0, i, 0))
    shp = jax.ShapeDtypeStruct((b, MLA_HEADS, t, LANE), BF16)
    return pl.pallas_call(
        _mla_prep_kernel,
        grid=(b, nt),
        in_specs=[
            pl.BlockSpec((1, TM, Q_LORA), lambda bi, i: (bi, i, C_CQ // Q_LORA)),
            pl.BlockSpec((1, TM, KV_LORA), lambda bi, i: (bi, i, C_CKV // KV_LORA)),
            pl.BlockSpec((1, TM, LANE), lambda bi, i: (bi, i, C_SMF // LANE)),
            full((1, Q_LORA)), full((Q_LORA, hw)), full((1, KV_LORA)),
            full((KV_LORA, hw)), full((KV_LORA, hw)), full((1, LANE)), full((1, LANE)),
            tab, tab, tab,
        ],
        out_specs=[out, out, out],
        out_shape=[shp, shp, shp],
        compiler_params=_cparams(("parallel", "parallel")),
        name="mla_prep",
    )(p, p, p, q_a_norm.reshape(1, -1), wuq_p, kv_a_norm.reshape(1, -1), wuk_p, wuv_p,
      qn_p, kn_p, cos_t, sa_t, sb_t)


ATT_HP = 4
ATT_TK = 512


def _attn_kernel(q_ref, k_ref, v_ref, o_ref, *, n_blk):
    n_it = jnp.where(pl.program_id(2) == 0, 0, n_blk)

    def step(hh, kb, vb, m, acc):
        s = _nt_dot(q_ref[0, hh], kb)
        m_new = jnp.max(s, axis=-1, keepdims=True)
        if m is None:
            p = jnp.exp(s - m_new)
            return m_new, jnp.dot(p.astype(BF16), vb, preferred_element_type=F32)
        m_new = jnp.maximum(m, m_new)
        p = jnp.exp(s - m_new)
        return m_new, acc * jnp.exp(m - m_new) + jnp.dot(p.astype(BF16), vb, preferred_element_type=F32)

    init = []
    for hh in range(ATT_HP):
        init.extend(step(hh, k_ref[0, hh, 0:TM, :], v_ref[0, hh, 0:TM, :], None, None))

    def body(j, carry):
        off = pl.multiple_of(TM + j * ATT_TK, TM)
        new = []
        for hh in range(ATT_HP):
            new.extend(step(hh, k_ref[0, hh, pl.ds(off, ATT_TK), :], v_ref[0, hh, pl.ds(off, ATT_TK), :],
                            carry[2 * hh], carry[2 * hh + 1]))
        return tuple(new)

    carry = lax.fori_loop(0, n_it, body, tuple(init))
    outs = [carry[2 * hh + 1][:, :MLA_V] / carry[2 * hh + 1][:, MLA_V:MLA_V + 1] for hh in range(ATT_HP)]
    o_ref[0] = jnp.concatenate(outs, axis=-1).astype(BF16)


def _attention(q, k, v):
    b, h, t, _ = q.shape
    nt = t // TM
    assert (t - TM) % ATT_TK == 0
    kv = pl.BlockSpec((1, ATT_HP, t, LANE), lambda bi, hp, i: (bi, hp, 0, 0))
    return pl.pallas_call(
        functools.partial(_attn_kernel, n_blk=(t - TM) // ATT_TK),
        grid=(b, h // ATT_HP, nt),
        in_specs=[pl.BlockSpec((1, ATT_HP, TM, LANE), lambda bi, hp, i: (bi, hp, i, 0)), kv, kv],
        out_specs=pl.BlockSpec((1, TM, ATT_HP * MLA_V), lambda bi, hp, i: (bi, i, hp)),
        out_shape=jax.ShapeDtypeStruct((b, t, h * MLA_V), BF16),
        compiler_params=_cparams(("parallel", "parallel", "arbitrary")),
        name="mla_attention",
    )(q, k, v)


def _ml_prep_kernel(x_ref, prev_ref, next_ref, cw_ref, cb_ref, wq_ref, wk_ref,
                    xc_ref, q_ref, k_ref, *, n_tiles):
    i = pl.program_id(1)
    x = x_ref[0]
    prev = jnp.where(i <= 1, 0.0, prev_ref[0])
    nxt = jnp.where((i == 0) | (i == n_tiles - 1), 0.0, next_ref[0])
    ext = jnp.concatenate([prev, x, nxt], axis=0)
    n_ext = TM + 16
    cw = cw_ref[...]
    acc = jnp.zeros((TM, ML_W), F32) + cb_ref[...]
    for kk in range(ML_CONV):
        sh = (ML_CONV // 2 - kk) % n_ext
        shifted = ext if sh == 0 else pltpu.roll(ext, sh, 0)
        acc = acc + cw[kk:kk + 1] * shifted[8:8 + TM]
    xc = _silu(acc)
    xc_ref[0] = xc
    xb = xc.astype(BF16)
    q_ref[0] = jnp.dot(xb, wq_ref[...], preferred_element_type=F32).astype(BF16)
    k_ref[0] = jnp.dot(xb, wk_ref[...], preferred_element_type=F32).astype(BF16)


def _ml_prep(p, conv_w8, conv_b, wq_bd, wk_bd):
    b, t, _ = p.shape
    nt = t // TM
    r8 = TM // 8
    full = lambda shape: pl.BlockSpec(shape, lambda bi, i: (0,) * len(shape))
    cb = C_MX // ML_W
    blk = pl.BlockSpec((1, TM, ML_W), lambda bi, i: (bi, i, 0))
    return pl.pallas_call(
        functools.partial(_ml_prep_kernel, n_tiles=nt),
        grid=(b, nt),
        in_specs=[
            pl.BlockSpec((1, TM, ML_W), lambda bi, i: (bi, i, cb)),
            pl.BlockSpec((1, 8, ML_W), lambda bi, i: (bi, jnp.maximum(i * r8 - 1, 0), cb)),
            pl.BlockSpec((1, 8, ML_W), lambda bi, i: (bi, jnp.minimum((i + 1) * r8, nt * r8 - 1), cb)),
            full((8, ML_W)), full((1, ML_W)), full((ML_W, ML_W)), full((ML_W, ML_W)),
        ],
        out_specs=[blk, blk, blk],
        out_shape=[jax.ShapeDtypeStruct((b, t, ML_W), F32),
                   jax.ShapeDtypeStruct((b, t, ML_W), BF16),
                   jax.ShapeDtypeStruct((b, t, ML_W), BF16)],
        compiler_params=_cparams(("parallel", "parallel")),
        name="mlstm_prep",
    )(p, p, p, conv_w8, conv_b.reshape(1, ML_W), wq_bd, wk_bd)


def _scan_chunk(d, step, n_ctx_chunks, n_chunks):
    bwd = jnp.where(step < n_ctx_chunks, n_ctx_chunks - 1 - step, n_chunks - 1 - (step - n_ctx_chunks))
    return jnp.where(d == 0, step, bwd)


def _ml_scan_kernel(q_ref, k_ref, v_ref, sm_ref, gb_ref, h_ref, s_ref, m_ref):
    d = pl.program_id(0)
    L = ML_CHUNK

    @pl.when(pl.program_id(2) == 0)
    def _():
        s_ref[...] = jnp.zeros_like(s_ref)
        m_ref[...] = jnp.zeros_like(m_ref)

    row = lax.broadcasted_iota(jnp.int32, (L, L), 0)
    col = lax.broadcasted_iota(jnp.int32, (L, L), 1)
    mask = (row - col) * (1 - 2 * d) >= 0
    tri = mask.astype(F32)
    lane = lax.broadcasted_iota(jnp.int32, (1, LANE), 1)
    eye8 = (lax.broadcasted_iota(jnp.int32, (8, LANE), 0)
            == lax.broadcasted_iota(jnp.int32, (8, LANE), 1)).astype(F32)

    g = sm_ref[0] + gb_ref[0]
    lf = _log_sigmoid(g)
    bc = jnp.dot(tri, lf, precision=HIGHEST, preferred_element_type=F32)
    g_rows = _nt_dot(eye8, g, precision=HIGHEST)
    bc_rows = _nt_dot(eye8, bc, precision=HIGHEST)

    q = q_ref[0]
    k = k_ref[0]
    v = v_ref[0]
    outs = []
    for pair in range(ML_HEADS // 2):
        sl = slice(pair * LANE, (pair + 1) * LANE)
        q_blk, k_blk, v_blk = q[:, sl], k[:, sl], v[:, sl]
        pair_out = []
        for sub in range(2):
            h = pair * 2 + sub
            head_lanes = (lane >= sub * ML_DH) & (lane < (sub + 1) * ML_DH)
            qh = jnp.where(head_lanes, q_blk, jnp.zeros_like(q_blk))
            vs = v_blk if sub == 0 else pltpu.roll(v_blk, ML_DH, 1)
            v_ext = jnp.where(lane < ML_DH, vs, jnp.where(lane == ML_DH, 1.0, 0.0)).astype(BF16)

            li_c = g[:, SM_GATE + h:SM_GATE + h + 1]
            bc_c = bc[:, SM_GATE + 4 + h:SM_GATE + 5 + h]
            li_r = g_rows[h:h + 1, :]
            bc_r = bc_rows[4 + h:5 + h, :]
            m_st = m_ref[h][0:1, 0:1]

            dmat = jnp.where(mask, bc_c + (li_r - bc_r), -jnp.inf)
            inter = bc_c + m_st
            m_t = jnp.maximum(inter, jnp.max(dmat, axis=-1, keepdims=True))
            e = jnp.exp(dmat - m_t)
            s = (_nt_dot(qh, k_blk) * e).astype(BF16)
            tot = (jnp.dot(s, v_ext, preferred_element_type=F32)
                   + jnp.exp(inter - m_t) * jnp.dot(qh, s_ref[h].astype(BF16),
                                                    preferred_element_type=F32))
            den = tot[:, ML_DH:ML_DH + 1]
            pair_out.append(tot / jnp.maximum(jnp.abs(den), jnp.exp(-m_t)))

            b_end = jnp.where(d == 0, bc_c[L - 1:L], bc_c[0:1])
            g_col = b_end - bc_c + li_c
            m_new = jnp.maximum(b_end + m_st, jnp.max(g_col, axis=0, keepdims=True))
            kw = jnp.where(head_lanes, k_blk.astype(F32) * jnp.exp(g_col - m_new), 0.0).astype(BF16)
            s_ref[h] = jnp.exp(b_end + m_st - m_new) * s_ref[h] + _tn_dot(kw, v_ext)
            m_ref[h] = jnp.broadcast_to(m_new, m_ref.shape[1:])
        outs.append(jnp.where(lane < ML_DH, pair_out[0], pltpu.roll(pair_out[1], ML_DH, 1)))
    h_ref[0, 0] = jnp.concatenate(outs, axis=-1)


def _ml_scan(q, k, p, gate_b_p):
    b, t, _ = q.shape
    nc = t // ML_CHUNK
    chunk = functools.partial(_scan_chunk, n_ctx_chunks=TM // ML_CHUNK, n_chunks=nc)
    blk = lambda cb: pl.BlockSpec((1, ML_CHUNK, ML_W), lambda d, bi, s: (bi, chunk(d, s), cb))
    return pl.pallas_call(
        _ml_scan_kernel,
        grid=(2, b, nc),
        in_specs=[
            blk(0), blk(0), blk(C_MV // ML_W),
            pl.BlockSpec((1, ML_CHUNK, LANE), lambda d, bi, s: (bi, chunk(d, s), C_SMF // LANE + d)),
            pl.BlockSpec((1, 1, LANE), lambda d, bi, s: (d, 0, 0)),
        ],
        out_specs=pl.BlockSpec((1, 1, ML_CHUNK, ML_W), lambda d, bi, s: (d, bi, chunk(d, s), 0)),
        out_shape=jax.ShapeDtypeStruct((2, b, t, ML_W), F32),
        scratch_shapes=[pltpu.VMEM((ML_HEADS, LANE, LANE), F32), pltpu.VMEM((ML_HEADS, 8, LANE), F32)],
        compiler_params=_cparams(("parallel", "parallel", "arbitrary")),
        name="mlstm_scan",
    )(q, k, p, p, gate_b_p)


def _gla_scan_kernel(q_ref, k_ref, v_ref, sm_ref, wa_ref, ba_ref, o_ref, s_ref):
    d = pl.program_id(0)
    L = GLA_CHUNK

    @pl.when(pl.program_id(2) == 0)
    def _():
        s_ref[...] = jnp.zeros_like(s_ref)

    row = lax.broadcasted_iota(jnp.int32, (L, L), 0)
    col = lax.broadcasted_iota(jnp.int32, (L, L), 1)
    mask = (row - col) * (1 - 2 * d) >= 0
    tri = mask.astype(F32)
    lane_k = lax.broadcasted_iota(jnp.int32, (1, GLA_HEADS * GLA_DK), 1)
    lane_v = lax.broadcasted_iota(jnp.int32, (1, GLA_HEADS * GLA_DV), 1)
    eye = (lax.broadcasted_iota(jnp.int32, (LANE, LANE), 0)
           == lax.broadcasted_iota(jnp.int32, (LANE, LANE), 1)).astype(F32)

    pre = jnp.dot(sm_ref[0], wa_ref[0], precision=HIGHEST, preferred_element_type=F32) + ba_ref[0]
    loga = _log_sigmoid(pre) * (1.0 / GLA_TAU)
    bc = jnp.dot(tri, loga, precision=HIGHEST, preferred_element_type=F32)
    ref_row = bc[L // 2 - 1:L // 2]
    b_end = jnp.where(d == 0, bc[L - 1:L], bc[0:1])

    q = q_ref[0] * (GLA_DK ** -0.5)
    k = k_ref[0]
    v = v_ref[0].astype(BF16)
    q_in = (q * jnp.exp(bc - ref_row))
    k_in = (k * jnp.exp(ref_row - bc)).astype(BF16)
    q_st = (q * jnp.exp(bc)).astype(BF16)
    k_st = (k * jnp.exp(b_end - bc)).astype(BF16)

    blockdiag = (lax.broadcasted_iota(jnp.int32, s_ref.shape, 0) // GLA_DK
                 == lax.broadcasted_iota(jnp.int32, s_ref.shape, 1) // GLA_DV)
    s_old = s_ref[...]
    o = jnp.dot(q_st, jnp.where(blockdiag, s_old, 0.0).astype(BF16), preferred_element_type=F32)
    for h in range(GLA_HEADS):
        qh = jnp.where(lane_k // GLA_DK == h, q_in, 0.0).astype(BF16)
        att = jnp.where(mask, _nt_dot(qh, k_in), 0.0).astype(BF16)
        oh = jnp.dot(att, v, preferred_element_type=F32)
        o = o + jnp.where(lane_v // GLA_DV == h, oh, 0.0)
    o_ref[0, 0] = o

    decay_col = jnp.exp(_nt_dot(eye, jnp.broadcast_to(b_end, (8, LANE)), precision=HIGHEST)[:, 0:1])
    s_ref[...] = decay_col * s_old + _tn_dot(k_st, v)


def _gla_scan(p, wa_p, ba_p):
    b, t, _ = p.shape
    nc = t // GLA_CHUNK
    chunk = functools.partial(_scan_chunk, n_ctx_chunks=TM // GLA_CHUNK, n_chunks=nc)
    kw, vw = GLA_HEADS * GLA_DK, GLA_HEADS * GLA_DV
    return pl.pallas_call(
        _gla_scan_kernel,
        grid=(2, b, nc),
        in_specs=[
            pl.BlockSpec((1, GLA_CHUNK, kw), lambda d, bi, s: (bi, chunk(d, s), C_GQ // kw)),
            pl.BlockSpec((1, GLA_CHUNK, kw), lambda d, bi, s: (bi, chunk(d, s), C_GK // kw)),
            pl.BlockSpec((1, GLA_CHUNK, vw), lambda d, bi, s: (bi, chunk(d, s), C_GV // vw)),
            pl.BlockSpec((1, GLA_CHUNK, LANE), lambda d, bi, s: (bi, chunk(d, s), C_SMF // LANE + d)),
            pl.BlockSpec((1, LANE, kw), lambda d, bi, s: (d, 0, 0)),
            pl.BlockSpec((1, 1, kw), lambda d, bi, s: (d, 0, 0)),
        ],
        out_specs=pl.BlockSpec((1, 1, GLA_CHUNK, vw), lambda d, bi, s: (d, bi, chunk(d, s), 0)),
        out_shape=jax.ShapeDtypeStruct((2, b, t, vw), F32),
        scratch_shapes=[pltpu.VMEM((kw, vw), F32)],
        compiler_params=_cparams(("parallel", "parallel", "arbitrary")),
        name="gla_scan",
    )(p, p, p, p, wa_p, ba_p)


def _outproj_kernel(a_ref, mhf_ref, mhb_ref, xc_ref, mo_ref, gof_ref, gob_ref, gr_ref, x_ref, mod_ref,
                    mnw_ref, msk_ref, gnw_ref, wout_ref, n2w_ref, wr_ref, br_ref,
                    xo_ref, h2_ref, route_ref):
    grp64 = (lax.broadcasted_iota(jnp.int32, (ML_W, ML_W), 0) // ML_DH
             == lax.broadcasted_iota(jnp.int32, (ML_W, ML_W), 1) // ML_DH).astype(F32) * (1.0 / ML_DH)

    def head_norm(x, w):
        ms = jnp.dot(x * x, grp64, precision=HIGHEST, preferred_element_type=F32)
        return x * lax.rsqrt(ms + EPS) * w

    m_l = jax.nn.sigmoid(mo_ref[0]) * (head_norm(mhf_ref[0, 0] + mhb_ref[0, 0], mnw_ref[...])
                                       + msk_ref[...] * xc_ref[0])
    g_l = head_norm(gof_ref[0, 0] + gob_ref[0, 0], gnw_ref[...]) * _silu(gr_ref[0])
    na = MLA_HEADS * MLA_V
    res = (jnp.dot(a_ref[0], wout_ref[0:na], preferred_element_type=F32)
           + jnp.dot(m_l.astype(BF16), wout_ref[na:na + ML_W], preferred_element_type=F32)
           + jnp.dot(g_l.astype(BF16), wout_ref[na + ML_W:], preferred_element_type=F32))
    mod = mod_ref[0]
    x = x_ref[0] + mod[2:3] * res
    xo_ref[0] = x
    h2 = (x * lax.rsqrt(jnp.mean(x * x, axis=-1, keepdims=True) + EPS) * n2w_ref[...]
          * (1.0 + mod[4:5]) + mod[3:4])
    h2_ref[0] = h2

    logits = jnp.dot(h2, wr_ref[...], precision=HIGHEST, preferred_element_type=F32) + br_ref[...]
    lane = lax.broadcasted_iota(jnp.int32, (1, LANE), 1)
    lane_f = lane.astype(F32)
    neg = -jnp.inf
    gl = jnp.where(lane < R_EXP, logits, neg)
    gmax = jnp.max(gl, axis=-1, keepdims=True)
    g_w = 1.0 / jnp.sum(jnp.exp(gl - gmax), axis=-1, keepdims=True)
    g_i = jnp.min(jnp.where(gl == gmax, lane_f, float(LANE)), axis=-1, keepdims=True)
    grp_of_lane = ((lane - R_EXP) // EXP_PER_GROUP).astype(F32)
    in_grp = (lane >= R_EXP) & (lane < R_EXP + N_EXPERTS) & (grp_of_lane == g_i)
    el = jnp.where(in_grp, logits, neg)
    m1 = jnp.max(el, axis=-1, keepdims=True)
    i1 = jnp.min(jnp.where(el == m1, lane_f, float(LANE)), axis=-1, keepdims=True)
    el2 = jnp.where(lane_f == i1, neg, el)
    m2 = jnp.max(el2, axis=-1, keepdims=True)
    i2 = jnp.min(jnp.where(el2 == m2, lane_f, float(LANE)), axis=-1, keepdims=True)
    p2 = jnp.exp(m2 - m1)
    w1 = g_w / (1.0 + p2)
    route_ref[0] = jnp.where(lane == RT_E1, i1 - R_EXP, jnp.where(lane == RT_E1 + 1, i2 - R_EXP,
                             jnp.where(lane == RT_W1, w1, jnp.where(lane == RT_W1 + 1, p2 * w1, 0.0))))


def _outproj(a, mh, xconv, p, go, xs, mods, ml_norm_w, ml_skip, gla_norm_w, w_out_b, norm2_w, wr_p, br_p):
    b, t, d = xs.shape
    nt = t // TM
    full = lambda shape: pl.BlockSpec(shape, lambda bi, i: (0,) * len(shape))
    tok = lambda w, cb=0: pl.BlockSpec((1, TM, w), lambda bi, i: (bi, i, cb))
    dirblk = lambda dd: pl.BlockSpec((1, 1, TM, ML_W), lambda bi, i: (dd, bi, i, 0))
    return pl.pallas_call(
        _outproj_kernel,
        grid=(b, nt),
        in_specs=[
            tok(MLA_HEADS * MLA_V), dirblk(0), dirblk(1), tok(ML_W), tok(ML_W, C_MO // ML_W),
            dirblk(0), dirblk(1), tok(ML_W, C_GR // ML_W), tok(d),
            pl.BlockSpec((1, 6, d), lambda bi, i: (_mod_row(bi, i), 0, 0)),
            full((1, ML_W)), full((1, ML_W)), full((1, ML_W)), full((d, d)), full((1, d)),
            full((d, LANE)), full((1, LANE)),
        ],
        out_specs=[tok(d), tok(d), tok(LANE)],
        out_shape=[jax.ShapeDtypeStruct((b, t, d), F32), jax.ShapeDtypeStruct((b, t, d), F32),
                   jax.ShapeDtypeStruct((b, t, LANE), F32)],
        compiler_params=_cparams(("parallel", "parallel")),
        name="out_proj_router",
    )(a, mh, mh, xconv, p, go, go, p, xs, mods, ml_norm_w.reshape(1, -1), ml_skip.reshape(1, -1),
      gla_norm_w.reshape(1, -1), w_out_b, norm2_w.reshape(1, -1), wr_p, br_p)


def _dispatch(route, n_tiles):
    n = route.shape[0]
    flat = route[:, RT_E1:RT_E1 + 2].astype(jnp.int32).reshape(-1)
    onehot = (flat[:, None] == jnp.arange(N_EXPERTS, dtype=jnp.int32)[None, :]).astype(jnp.int32)
    csum = jnp.cumsum(onehot, axis=0)
    rank = jnp.sum(csum * onehot, axis=1) - 1
    padded = (csum[-1] + TM - 1) // TM * TM
    ends = jnp.cumsum(padded)
    pos = (ends - padded)[flat] + rank
    src = jnp.zeros((n_tiles * TM,), jnp.int32).at[pos].set(jnp.arange(2 * n, dtype=jnp.int32) // 2)
    tile_start = jnp.arange(n_tiles, dtype=jnp.int32) * TM
    tile_exp = jnp.minimum(jnp.sum((ends[None, :] <= tile_start[:, None]).astype(jnp.int32), axis=1),
                           N_EXPERTS - 1)
    tile_on = (tile_start < ends[-1]).astype(jnp.int32)
    pos = pos.reshape(n // TM, TM, 2)
    return (src.reshape(n_tiles, 1, TM), tile_exp, tile_on,
            pos[:, :, 0].reshape(n // TM, 1, TM), pos[:, :, 1].reshape(n // TM, 1, TM))


def _gather_rows(src_hbm, idx_ref, dst, sem):
    def body(j, carry):
        pltpu.make_async_copy(src_hbm.at[pl.ds(idx_ref[0, 0, j], 1)], dst.at[pl.ds(j, 1)], sem).start()
        return carry

    lax.fori_loop(0, TM, body, 0, unroll=8)


def _wait_rows(src_hbm, dst, sem):
    pltpu.make_async_copy(src_hbm.at[pl.ds(0, TM)], dst, sem).wait()


def _experts_kernel(texp_ref, ton_ref, idx_ref, idxn_ref, h2_hbm, wgu_ref, wd_ref, y_ref, buf, sem):
    r = pl.program_id(0)
    slot = r % 2

    @pl.when(r == 0)
    def _():
        _gather_rows(h2_hbm, idx_ref, buf.at[0], sem.at[0])

    last = pl.num_programs(0) - 1

    @pl.when(jnp.logical_and(r < last, ton_ref[jnp.minimum(r + 1, last)] == 1))
    def _():
        _gather_rows(h2_hbm, idxn_ref, buf.at[1 - slot], sem.at[1 - slot])

    @pl.when(ton_ref[r] == 1)
    def _():
        _wait_rows(h2_hbm, buf.at[slot], sem.at[slot])
        gu = jnp.dot(buf[slot].astype(BF16), wgu_ref[0], preferred_element_type=F32)
        act = (_silu(gu[:, :D_EXPERT]) * gu[:, D_EXPERT:]).astype(BF16)
        y_ref[...] = jnp.dot(act, wd_ref[0], preferred_element_type=F32)

    @pl.when(ton_ref[r] == 0)
    def _():
        y_ref[...] = jnp.zeros_like(y_ref)


def _experts(h2, src, tile_exp, tile_on, wgu, wd):
    n, d = h2.shape
    n_tiles = src.shape[0]
    idx = lambda nxt: pl.BlockSpec((1, 1, TM), lambda r, te, to: (jnp.minimum(r + nxt, n_tiles - 1), 0, 0),
                                   memory_space=pltpu.SMEM)
    return pl.pallas_call(
        _experts_kernel,
        grid_spec=pltpu.PrefetchScalarGridSpec(
            num_scalar_prefetch=2,
            grid=(n_tiles,),
            in_specs=[
                idx(0), idx(1),
                pl.BlockSpec(memory_space=pl.ANY),
                pl.BlockSpec((1, d, 2 * D_EXPERT), lambda r, te, to: (te[r], 0, 0)),
                pl.BlockSpec((1, D_EXPERT, d), lambda r, te, to: (te[r], 0, 0)),
            ],
            out_specs=pl.BlockSpec((TM, d), lambda r, te, to: (r, 0)),
            scratch_shapes=[pltpu.VMEM((2, TM, d), F32), pltpu.SemaphoreType.DMA((2,))],
        ),
        out_shape=jax.ShapeDtypeStruct((n_tiles * TM, d), F32),
        compiler_params=_cparams(("arbitrary",)),
        name="moe_experts",
    )(tile_exp, tile_on, src, src, h2, wgu, wd)


def _combine_kernel(p1_ref, p2_ref, p1n_ref, p2n_ref, route_ref, x_ref, mod_ref, y_hbm, o_ref, buf, sem):
    g = pl.program_id(0)
    slot = g % 2

    def gather(pa, pb, s):
        _gather_rows(y_hbm, pa, buf.at[s, 0], sem.at[s])
        _gather_rows(y_hbm, pb, buf.at[s, 1], sem.at[s])

    @pl.when(g == 0)
    def _():
        gather(p1_ref, p2_ref, 0)

    @pl.when(g + 1 < pl.num_programs(0))
    def _():
        gather(p1n_ref, p2n_ref, 1 - slot)

    _wait_rows(y_hbm, buf.at[slot, 0], sem.at[slot])
    _wait_rows(y_hbm, buf.at[slot, 1], sem.at[slot])
    route = route_ref[...]
    moe = route[:, RT_W1:RT_W1 + 1] * buf[slot, 0] + route[:, RT_W1 + 1:RT_W1 + 2] * buf[slot, 1]
    o_ref[...] = x_ref[...] + mod_ref[0][5:6] * moe


def _combine(y, pos1, pos2, route, xs2, mods, nt):
    n, d = xs2.shape
    n_tok_tiles = n // TM
    idx = lambda nxt: pl.BlockSpec((1, 1, TM), lambda g: (jnp.minimum(g + nxt, n_tok_tiles - 1), 0, 0),
                                   memory_space=pltpu.SMEM)
    tok = lambda w: pl.BlockSpec((TM, w), lambda g: (g, 0))
    return pl.pallas_call(
        _combine_kernel,
        grid=(n_tok_tiles,),
        in_specs=[
            idx(0), idx(0), idx(1), idx(1), tok(LANE), tok(d),
            pl.BlockSpec((1, 6, d), lambda g: (_mod_row(g // nt, g % nt), 0, 0)),
            pl.BlockSpec(memory_space=pl.ANY),
        ],
        out_specs=tok(d),
        out_shape=jax.ShapeDtypeStruct((n, d), F32),
        scratch_shapes=[pltpu.VMEM((2, 2, TM, d), F32), pltpu.SemaphoreType.DMA((2,))],
        compiler_params=_cparams(("arbitrary",)),
        name="moe_combine",
    )(pos1, pos2, pos1, pos2, route, xs2, mods, y)


def _moe(h2, route, xs, mods, wgu, wd):
    b, t, d = xs.shape
    n = b * t
    n_tiles = 2 * n // TM + N_EXPERTS
    route2 = route.reshape(n, LANE)
    src, tile_exp, tile_on, pos1, pos2 = _dispatch(route2, n_tiles)
    y = _experts(h2.reshape(n, d), src, tile_exp, tile_on, wgu, wd)
    return _combine(y, pos1, pos2, route2, xs.reshape(n, d), mods, t // TM).reshape(b, t, d)


def _rope_tables(n_ctx, n_lat):
    rows = n_lat // GRID_W
    row = jnp.broadcast_to(jnp.arange(rows, dtype=F32)[:, None], (rows, GRID_W)).reshape(-1)
    col = jnp.broadcast_to(jnp.arange(GRID_W, dtype=F32)[None, :], (rows, GRID_W)).reshape(-1)
    n_freq = MLA_ROPE // 4
    inv = ROPE_THETA ** (-jnp.arange(n_freq, dtype=F32) / n_freq)
    ang = jnp.concatenate([row[:, None] * inv, col[:, None] * inv], axis=-1)
    cos, sin = jnp.cos(ang), jnp.sin(ang)
    half = MLA_ROPE // 2
    z = lambda w: jnp.zeros((n_lat, w), F32)
    o = lambda w: jnp.ones((n_lat, w), F32)
    tail = LANE - MLA_QK
    cos_t = jnp.concatenate([o(MLA_NOPE), cos, cos, o(tail)], axis=-1)
    sa_t = jnp.concatenate([z(MLA_NOPE + half), sin, z(tail)], axis=-1)
    sb_t = jnp.concatenate([z(MLA_NOPE), -sin, z(half + tail)], axis=-1)
    ctx1 = jnp.ones((n_ctx, LANE), F32)
    ctx0 = jnp.zeros((n_ctx, LANE), F32)
    return (jnp.concatenate([ctx1, cos_t], 0), jnp.concatenate([ctx0, sa_t], 0),
            jnp.concatenate([ctx0, sb_t], 0))


def _pad_cols(a, width):
    return jnp.pad(a, [(0, 0)] * (a.ndim - 1) + [(0, width - a.shape[-1])])


def _layer_weights(w_in, w_uq, w_ukv, q_norm_w, k_norm_w, ml_conv_w, ml_wq, ml_wk, ml_gate_b,
                   gla_wa, gla_ba, w_out, w_grp, b_grp, w_erouter, b_erouter, w_gate, w_up, w_down):
    d = w_in.shape[0]
    o = np.cumsum((0, Q_LORA, KV_LORA, MLA_ROPE, ML_W, ML_W, ML_W, 4 * ML_HEADS, GLA_HEADS * GLA_DK,
                   GLA_HEADS * GLA_DK, GLA_HEADS * GLA_DV, GLA_HEADS * GLA_DV, 2 * GLA_LR))
    seg = lambda j: w_in[:, o[j]:o[j + 1]]
    cq, ckv, kr, mx, mv, mo, mg, gq, gk, gv, gr, ga = (seg(j) for j in range(12))
    z = lambda w: jnp.zeros((d, w), F32)

    def small(di):
        return jnp.concatenate([mg[:, di * 8:(di + 1) * 8], ga[:, di * GLA_LR:(di + 1) * GLA_LR],
                                z(SM_KR - SM_GA - GLA_LR), kr, z(LANE - SM_KR - MLA_ROPE)], axis=-1)

    w_in_p = jnp.concatenate([cq, mx, mv, mo, gv, gr, ckv, gq, gk, small(0), small(1)], axis=-1).astype(BF16)

    wuq_p = _pad_cols(w_uq.reshape(Q_LORA, MLA_HEADS, MLA_QK), LANE).reshape(Q_LORA, -1).astype(BF16)
    ukv = w_ukv.reshape(KV_LORA, MLA_HEADS, MLA_NOPE + MLA_V)
    wuk_p = _pad_cols(ukv[..., :MLA_NOPE], LANE).reshape(KV_LORA, -1).astype(BF16)
    wuv_p = _pad_cols(ukv[..., MLA_NOPE:], LANE).reshape(KV_LORA, -1).astype(BF16)
    qn_p = _pad_cols(q_norm_w.reshape(1, -1), LANE)
    kn_p = _pad_cols(k_norm_w.reshape(1, -1), LANE)

    conv_w8 = jnp.pad(ml_conv_w, ((0, 8 - ML_CONV), (0, 0)))
    bd = lambda w: jax.scipy.linalg.block_diag(*[w[h] for h in range(ML_HEADS)])
    wq_bd = (bd(ml_wq) * (ML_DH ** -0.5)).astype(BF16)
    wk_bd = bd(ml_wk).astype(BF16)
    gate_b_p = _pad_cols(ml_gate_b.reshape(2, 1, 2 * ML_HEADS), LANE)

    wa_p = jnp.pad(gla_wa, ((0, 0), (SM_GA, LANE - SM_GA - GLA_LR), (0, 0)))
    ba_p = gla_ba.reshape(2, 1, -1)

    wr_p = _pad_cols(jnp.concatenate([w_grp, w_erouter], axis=-1), LANE)
    br_p = _pad_cols(jnp.concatenate([b_grp, b_erouter]).reshape(1, -1), LANE)
    wgu = jnp.concatenate([w_gate, w_up], axis=-1).astype(BF16)
    return dict(w_in_p=w_in_p, wuq_p=wuq_p, wuk_p=wuk_p, wuv_p=wuv_p, qn_p=qn_p, kn_p=kn_p,
                conv_w8=conv_w8, wq_bd=wq_bd, wk_bd=wk_bd, gate_b_p=gate_b_p, wa_p=wa_p, ba_p=ba_p,
                w_out_b=w_out.astype(BF16), wr_p=wr_p, br_p=br_p, wgu=wgu, wd=w_down.astype(BF16))


def kernel(x, c, ctx, c_ctx, w_mod, b_mod, norm1_w, w_in, q_a_norm, w_uq, kv_a_norm, w_ukv,
           q_norm_w, k_norm_w, ml_conv_w, ml_conv_b, ml_wq, ml_wk, ml_gate_b, ml_norm_w, ml_skip,
           gla_wa, gla_ba, gla_norm_w, w_out, norm2_w, w_grp, b_grp, w_erouter, b_erouter,
           w_gate, w_up, w_down):
    b, s, d = x.shape
    n_ctx = ctx.shape[1]
    depth = w_mod.shape[0]
    assert n_ctx == TM and s % TM == 0 and b == 2

    cc = jnp.concatenate([c, c_ctx[None, :], jnp.zeros((8 - b - 1, d), F32)], axis=0)
    mods_all = _mods(cc, w_mod, b_mod).reshape(depth, 8, 6, d)
    cos_t, sa_t, sb_t = _rope_tables(n_ctx, s)
    xs = jnp.concatenate([ctx, x], axis=1)

    for l in range(depth):
        w = _layer_weights(w_in[l], w_uq[l], w_ukv[l], q_norm_w[l], k_norm_w[l], ml_conv_w[l],
                           ml_wq[l], ml_wk[l], ml_gate_b[l], gla_wa[l], gla_ba[l], w_out[l],
                           w_grp[l], b_grp[l], w_erouter[l], b_erouter[l], w_gate[l], w_up[l], w_down[l])
        mods = mods_all[l]
        p = _inproj(xs, mods, norm1_w[l], w["w_in_p"])
        q, k, v = _mla_prep(p, q_a_norm[l], w["wuq_p"], kv_a_norm[l], w["wuk_p"], w["wuv_p"],
                            w["qn_p"], w["kn_p"], cos_t, sa_t, sb_t)
        a = _attention(q, k, v)
        xconv, mq, mk = _ml_prep(p, w["conv_w8"], ml_conv_b[l], w["wq_bd"], w["wk_bd"])
        mh = _ml_scan(mq, mk, p, w["gate_b_p"])
        go = _gla_scan(p, w["wa_p"], w["ba_p"])
        xs, h2, route = _outproj(a, mh, xconv, p, go, xs, mods, ml_norm_w[l], ml_skip[l], gla_norm_w[l],
                                w["w_out_b"], norm2_w[l], w["wr_p"], w["br_p"])
        xs = _moe(h2, route, xs, mods, w["wgu"], w["wd"])
    return xs[:, n_ctx:, :]
```

```python
import functools

import jax
import jax.numpy as jnp
import numpy as np
from jax import lax
from jax.experimental import pallas as pl
from jax.experimental.pallas import tpu as pltpu

F32 = jnp.float32
BF16 = jnp.bfloat16
HIGHEST = lax.Precision.HIGHEST

EPS = 1e-6
GRID_W = 64
ROPE_THETA = 10000.0

MLA_HEADS = 8
MLA_NOPE = 64
MLA_ROPE = 32
MLA_QK = MLA_NOPE + MLA_ROPE
MLA_V = 64
Q_LORA = 256
KV_LORA = 128

ML_HEADS = 4
ML_DH = 64
ML_W = ML_HEADS * ML_DH
ML_CONV = 5

GLA_HEADS = 4
GLA_DK = 32
GLA_DV = 64
GLA_LR = 16
GLA_TAU = 16.0

N_GROUPS = 4
EXP_PER_GROUP = 8
N_EXPERTS = N_GROUPS * EXP_PER_GROUP
D_EXPERT = 256

LANE = 128
TM = 256
ML_CHUNK = 256
GLA_CHUNK = 128
VMEM_LIMIT = 56 * 1024 * 1024

C_CQ, C_MX, C_MV, C_MO, C_GV, C_GR = 0, 256, 512, 768, 1024, 1280
C_CKV, C_GQ, C_GK, C_SMF, C_SMB = 1536, 1664, 1792, 1920, 2048
D_INP = 2176
SM_GATE = 0
SM_GA = 8
SM_KR = 64
R_GRP = 0
R_EXP = 4
RT_E1 = 0
RT_W1 = 2


def _cparams(sem):
    return pltpu.CompilerParams(dimension_semantics=sem, vmem_limit_bytes=VMEM_LIMIT)


def _silu(x):
    return x * jax.nn.sigmoid(x)


def _log_sigmoid(x):
    return -(jnp.maximum(-x, 0.0) + jnp.log1p(jnp.exp(-jnp.abs(x))))


def _nt_dot(a, b, **kw):
    return lax.dot_general(a, b, (((1,), (1,)), ((), ())), preferred_element_type=F32, **kw)


def _tn_dot(a, b, **kw):
    return lax.dot_general(a, b, (((0,), (0,)), ((), ())), preferred_element_type=F32, **kw)


def _mods_kernel(cc_ref, w_ref, b_ref, o_ref):
    a = _silu(cc_ref[...])
    o_ref[0] = jnp.dot(a, w_ref[0], precision=HIGHEST, preferred_element_type=F32) + b_ref[0]


def _mods(cc, w_mod, b_mod):
    depth, d, d6 = w_mod.shape
    nb = 1536
    return pl.pallas_call(
        _mods_kernel,
        grid=(depth, d6 // nb),
        in_specs=[
            pl.BlockSpec((8, d), lambda l, j: (0, 0)),
            pl.BlockSpec((1, d, nb), lambda l, j: (l, 0, j)),
            pl.BlockSpec((1, 1, nb), lambda l, j: (l, 0, j)),
        ],
        out_specs=pl.BlockSpec((1, 8, nb), lambda l, j: (l, 0, j)),
        out_shape=jax.ShapeDtypeStruct((depth, 8, d6), F32),
        compiler_params=_cparams(("arbitrary", "arbitrary")),
        name="adaln_mods",
    )(cc, w_mod, b_mod.reshape(depth, 1, d6))


def _mod_row(b, i):
    return jnp.where(i == 0, 2, b)


def _inproj_kernel(x_ref, mod_ref, nw_ref, w_ref, o_ref):
    x = x_ref[0]
    y = x * lax.rsqrt(jnp.mean(x * x, axis=-1, keepdims=True) + EPS) * nw_ref[...]
    mod = mod_ref[0]
    h = y * (1.0 + mod[1:2]) + mod[0:1]
    o_ref[0] = jnp.dot(h.astype(BF16), w_ref[...], preferred_element_type=F32)


def _inproj(xs, mods, norm_w, w_in_p):
    b, t, d = xs.shape
    nt = t // TM
    return pl.pallas_call(
        _inproj_kernel,
        grid=(b, nt),
        in_specs=[
            pl.BlockSpec((1, TM, d), lambda bi, i: (bi, i, 0)),
            pl.BlockSpec((1, 6, d), lambda bi, i: (_mod_row(bi, i), 0, 0)),
            pl.BlockSpec((1, d), lambda bi, i: (0, 0)),
            pl.BlockSpec((d, D_INP), lambda bi, i: (0, 0)),
        ],
        out_specs=pl.BlockSpec((1, TM, D_INP), lambda bi, i: (bi, i, 0)),
        out_shape=jax.ShapeDtypeStruct((b, t, D_INP), F32),
        compiler_params=_cparams(("parallel", "parallel")),
        name="in_proj",
    )(xs, mods, norm_w.reshape(1, d), w_in_p)


def _mla_prep_kernel(cq_ref, ckv_ref, sm_ref, qan_ref, wuq_ref, kvan_ref, wuk_ref, wuv_ref,
                     qn_ref, kn_ref, cos_ref, sa_ref, sb_ref, q_ref, k_ref, v_ref):
    cq = cq_ref[0]
    cqn = cq * lax.rsqrt(jnp.mean(cq * cq, axis=-1, keepdims=True) + EPS) * qan_ref[...]
    qall = jnp.dot(cqn.astype(BF16), wuq_ref[...], preferred_element_type=F32)
    ckv = ckv_ref[0]
    ckvn = (ckv * lax.rsqrt(jnp.mean(ckv * ckv, axis=-1, keepdims=True) + EPS)
            * kvan_ref[...]).astype(BF16)
    kall = jnp.dot(ckvn, wuk_ref[...], preferred_element_type=F32)
    vall = jnp.dot(ckvn, wuv_ref[...], preferred_element_type=F32)
    lane = lax.broadcasted_iota(jnp.int32, (1, LANE), 1)
    kr = jnp.where((lane >= SM_KR) & (lane < SM_KR + MLA_ROPE), sm_ref[0], 0.0)
    cos, sa, sb = cos_ref[...], sa_ref[...], sb_ref[...]

    def rope(x):
        return x * cos + pltpu.roll(x, 16, 1) * sa + pltpu.roll(x, LANE - 16, 1) * sb

    def head_norm(x, w):
        return x * lax.rsqrt(jnp.sum(x * x, axis=-1, keepdims=True) * (1.0 / MLA_QK) + EPS) * w

    for h in range(MLA_HEADS):
        sl = slice(h * LANE, (h + 1) * LANE)
        qh = rope(head_norm(qall[:, sl], qn_ref[...]))
        q_ref[0, h] = (qh * Q_SCALE).astype(BF16)
        kh = rope(head_norm(kall[:, sl] + kr, kn_ref[...]))
        k_ref[0, h] = kh.astype(BF16)
        v_ref[0, h] = jnp.where(lane == MLA_V, 1.0, vall[:, sl]).astype(BF16)


def _mla_prep(p, q_a_norm, wuq_p, kv_a_norm, wuk_p, wuv_p, qn_p, kn_p, cos_t, sa_t, sb_t):
    b, t, _ = p.shape
    nt = t // TM
    hw = MLA_HEADS * LANE
    full = lambda shape: pl.BlockSpec(shape, lambda bi, i: (0,) * len(shape))
    tab = pl.BlockSpec((TM, LANE), lambda bi, i: (i, 0))
    out = pl.BlockSpec((1, MLA_HEADS, TM, LANE), lambda bi, i: (bi, 0, i, 0))
    shp = jax.ShapeDtypeStruct((b, MLA_HEADS, t, LANE), BF16)
    return pl.pallas_call(
        _mla_prep_kernel,
        grid=(b, nt),
        in_specs=[
            pl.BlockSpec((1, TM, Q_LORA), lambda bi, i: (bi, i, C_CQ // Q_LORA)),
            pl.BlockSpec((1, TM, KV_LORA), lambda bi, i: (bi, i, C_CKV // KV_LORA)),
            pl.BlockSpec((1, TM, LANE), lambda bi, i: (bi, i, C_SMF // LANE)),
            full((1, Q_LORA)), full((Q_LORA, hw)), full((1, KV_LORA)),
            full((KV_LORA, hw)), full((KV_LORA, hw)), full((1, LANE)), full((1, LANE)),
            tab, tab, tab,
        ],
        out_specs=[out, out, out],
        out_shape=[shp, shp, shp],
        compiler_params=_cparams(("parallel", "parallel")),
        name="mla_prep",
    )(p, p, p, q_a_norm.reshape(1, -1), wuq_p, kv_a_norm.reshape(1, -1), wuk_p, wuv_p,
      qn_p, kn_p, cos_t, sa_t, sb_t)


Q_SCALE = float(MLA_QK ** -0.5 * np.log2(np.e))
ATT_HP = 2
ATT_NQ = 4
ATT_TK = 512


def _softmax_step(q, kb, vb, m, acc):
    s = _nt_dot(q, kb)
    m_new = jnp.max(s, axis=-1, keepdims=True)
    if m is None:
        return m_new, jnp.dot(jnp.exp2((s - m_new).astype(BF16)), vb, preferred_element_type=F32)
    m_new = jnp.maximum(m, m_new)
    p = jnp.exp2((s - m_new).astype(BF16))
    return m_new, acc * jnp.exp2(m - m_new) + jnp.dot(p, vb, preferred_element_type=F32)


def _attn_ctx_kernel(q_ref, k_ref, v_ref, o_ref):
    outs = []
    for h in range(MLA_HEADS):
        _, acc = _softmax_step(q_ref[0, h], k_ref[0, h], v_ref[0, h], None, None)
        outs.append(acc[:, :MLA_V] / acc[:, MLA_V:MLA_V + 1])
    o_ref[0] = jnp.concatenate(outs, axis=-1).astype(BF16)


def _attention_ctx(q, k, v):
    b, h, _, _ = q.shape
    blk = pl.BlockSpec((1, h, TM, LANE), lambda bi: (bi, 0, 0, 0))
    return pl.pallas_call(
        _attn_ctx_kernel,
        grid=(b,),
        in_specs=[blk, blk, blk],
        out_specs=pl.BlockSpec((1, TM, h * MLA_V), lambda bi: (bi, 0, 0)),
        out_shape=jax.ShapeDtypeStruct((b, TM, h * MLA_V), BF16),
        compiler_params=_cparams(("parallel",)),
        name="mla_attention_ctx",
    )(q, k, v)


def _attn_lat_kernel(*refs, n_blk):
    q_refs, (k_ref, v_ref, o_ref, q_buf) = refs[:ATT_NQ], refs[ATT_NQ:]
    init = []
    for hh in range(ATT_HP):
        q_buf[hh] = jnp.concatenate([qr[0, hh] for qr in q_refs], axis=0)
        init += _softmax_step(q_buf[hh], k_ref[0, hh, 0:TM, :], v_ref[0, hh, 0:TM, :], None, None)

    def body(j, carry):
        off = pl.multiple_of(TM + j * ATT_TK, TM)
        new = []
        for hh in range(ATT_HP):
            new += _softmax_step(q_buf[hh], k_ref[0, hh, pl.ds(off, ATT_TK), :],
                                 v_ref[0, hh, pl.ds(off, ATT_TK), :], carry[2 * hh], carry[2 * hh + 1])
        return tuple(new)

    carry = lax.fori_loop(0, n_blk, body, tuple(init))
    outs = [carry[2 * hh + 1][:, :MLA_V] / carry[2 * hh + 1][:, MLA_V:MLA_V + 1] for hh in range(ATT_HP)]
    o_ref[0] = jnp.concatenate(outs, axis=-1).astype(BF16)


def _attention_lat(q, k, v):
    b, h, t, _ = q.shape
    s = t - TM
    tq = ATT_NQ * TM
    assert s % ATT_TK == 0 and s % tq == 0
    kv = pl.BlockSpec((1, ATT_HP, t, LANE), lambda bi, hp, i: (bi, hp, 0, 0))
    qs = [pl.BlockSpec((1, ATT_HP, TM, LANE), lambda bi, hp, i, u=u: (bi, hp, 1 + ATT_NQ * i + u, 0))
          for u in range(ATT_NQ)]
    return pl.pallas_call(
        functools.partial(_attn_lat_kernel, n_blk=s // ATT_TK),
        grid=(b, h // ATT_HP, s // tq),
        in_specs=qs + [kv, kv],
        out_specs=pl.BlockSpec((1, tq, ATT_HP * MLA_V), lambda bi, hp, i: (bi, i, hp)),
        out_shape=jax.ShapeDtypeStruct((b, s, h * MLA_V), BF16),
        scratch_shapes=[pltpu.VMEM((ATT_HP, tq, LANE), BF16)],
        compiler_params=_cparams(("parallel", "parallel", "arbitrary")),
        name="mla_attention",
    )(*([q] * ATT_NQ), k, v)


def _ml_prep_kernel(x_ref, prev_ref, next_ref, cw_ref, cb_ref, wq_ref, wk_ref,
                    xc_ref, q_ref, k_ref, *, n_tiles):
    i = pl.program_id(1)
    x = x_ref[0]
    prev = jnp.where(i <= 1, 0.0, prev_ref[0])
    nxt = jnp.where((i == 0) | (i == n_tiles - 1), 0.0, next_ref[0])
    ext = jnp.concatenate([prev, x, nxt], axis=0)
    n_ext = TM + 16
    cw = cw_ref[...]
    acc = jnp.zeros((TM, ML_W), F32) + cb_ref[...]
    for kk in range(ML_CONV):
        sh = (ML_CONV // 2 - kk) % n_ext
        shifted = ext if sh == 0 else pltpu.roll(ext, sh, 0)
        acc = acc + cw[kk:kk + 1] * shifted[8:8 + TM]
    xc = _silu(acc)
    xc_ref[0] = xc
    xb = xc.astype(BF16)
    q_ref[0] = jnp.dot(xb, wq_ref[...], preferred_element_type=F32).astype(BF16)
    k_ref[0] = jnp.dot(xb, wk_ref[...], preferred_element_type=F32).astype(BF16)


def _ml_prep(p, conv_w8, conv_b, wq_bd, wk_bd):
    b, t, _ = p.shape
    nt = t // TM
    r8 = TM // 8
    full = lambda shape: pl.BlockSpec(shape, lambda bi, i: (0,) * len(shape))
    cb = C_MX // ML_W
    blk = pl.BlockSpec((1, TM, ML_W), lambda bi, i: (bi, i, 0))
    return pl.pallas_call(
        functools.partial(_ml_prep_kernel, n_tiles=nt),
        grid=(b, nt),
        in_specs=[
            pl.BlockSpec((1, TM, ML_W), lambda bi, i: (bi, i, cb)),
            pl.BlockSpec((1, 8, ML_W), lambda bi, i: (bi, jnp.maximum(i * r8 - 1, 0), cb)),
            pl.BlockSpec((1, 8, ML_W), lambda bi, i: (bi, jnp.minimum((i + 1) * r8, nt * r8 - 1), cb)),
            full((8, ML_W)), full((1, ML_W)), full((ML_W, ML_W)), full((ML_W, ML_W)),
        ],
        out_specs=[blk, blk, blk],
        out_shape=[jax.ShapeDtypeStruct((b, t, ML_W), F32),
                   jax.ShapeDtypeStruct((b, t, ML_W), BF16),
                   jax.ShapeDtypeStruct((b, t, ML_W), BF16)],
        compiler_params=_cparams(("parallel", "parallel")),
        name="mlstm_prep",
    )(p, p, p, conv_w8, conv_b.reshape(1, ML_W), wq_bd, wk_bd)


def _scan_chunk(d, step, n_ctx_chunks, n_chunks):
    bwd = jnp.where(step < n_ctx_chunks, n_ctx_chunks - 1 - step, n_chunks - 1 - (step - n_ctx_chunks))
    return jnp.where(d == 0, step, bwd)


def _ml_scan_kernel(q_ref, k_ref, v_ref, sm_ref, gb_ref, h_ref, s_ref, m_ref):
    d = pl.program_id(0)
    L = ML_CHUNK

    @pl.when(pl.program_id(2) == 0)
    def _():
        s_ref[...] = jnp.zeros_like(s_ref)
        m_ref[...] = jnp.zeros_like(m_ref)

    row = lax.broadcasted_iota(jnp.int32, (L, L), 0)
    col = lax.broadcasted_iota(jnp.int32, (L, L), 1)
    mask = (row - col) * (1 - 2 * d) >= 0
    tri = mask.astype(F32)
    lane = lax.broadcasted_iota(jnp.int32, (1, LANE), 1)
    eye8 = (lax.broadcasted_iota(jnp.int32, (8, LANE), 0)
            == lax.broadcasted_iota(jnp.int32, (8, LANE), 1)).astype(F32)

    g = sm_ref[0] + gb_ref[0]
    lf = _log_sigmoid(g)
    bc = jnp.dot(tri, lf, precision=HIGHEST, preferred_element_type=F32)
    g_rows = _nt_dot(eye8, g, precision=HIGHEST)
    bc_rows = _nt_dot(eye8, bc, precision=HIGHEST)

    q = q_ref[0]
    k = k_ref[0]
    v = v_ref[0]
    outs = []
    for pair in range(ML_HEADS // 2):
        sl = slice(pair * LANE, (pair + 1) * LANE)
        q_blk, k_blk, v_blk = q[:, sl], k[:, sl], v[:, sl]
        pair_out = []
        for sub in range(2):
            h = pair * 2 + sub
            head_lanes = (lane >= sub * ML_DH) & (lane < (sub + 1) * ML_DH)
            qh = jnp.where(head_lanes, q_blk, jnp.zeros_like(q_blk))
            vs = v_blk if sub == 0 else pltpu.roll(v_blk, ML_DH, 1)
            v_ext = jnp.where(lane < ML_DH, vs, jnp.where(lane == ML_DH, 1.0, 0.0)).astype(BF16)

            li_c = g[:, SM_GATE + h:SM_GATE + h + 1]
            bc_c = bc[:, SM_GATE + 4 + h:SM_GATE + 5 + h]
            li_r = g_rows[h:h + 1, :]
            bc_r = bc_rows[4 + h:5 + h, :]
            m_st = m_ref[h][0:1, 0:1]

            dmat = jnp.where(mask, bc_c + (li_r - bc_r), -jnp.inf)
            inter = bc_c + m_st
            m_t = jnp.maximum(inter, jnp.max(dmat, axis=-1, keepdims=True))
            e = jnp.exp(dmat - m_t)
            s = (_nt_dot(qh, k_blk) * e).astype(BF16)
            tot = (jnp.dot(s, v_ext, preferred_element_type=F32)
                   + jnp.exp(inter - m_t) * jnp.dot(qh, s_ref[h].astype(BF16),
                                                    preferred_element_type=F32))
            den = tot[:, ML_DH:ML_DH + 1]
            pair_out.append(tot / jnp.maximum(jnp.abs(den), jnp.exp(-m_t)))

            b_end = jnp.where(d == 0, bc_c[L - 1:L], bc_c[0:1])
            g_col = b_end - bc_c + li_c
            m_new = jnp.maximum(b_end + m_st, jnp.max(g_col, axis=0, keepdims=True))
            kw = jnp.where(head_lanes, k_blk.astype(F32) * jnp.exp(g_col - m_new), 0.0).astype(BF16)
            s_ref[h] = jnp.exp(b_end + m_st - m_new) * s_ref[h] + _tn_dot(kw, v_ext)
            m_ref[h] = jnp.broadcast_to(m_new, m_ref.shape[1:])
        outs.append(jnp.where(lane < ML_DH, pair_out[0], pltpu.roll(pair_out[1], ML_DH, 1)))
    h_ref[0, 0] = jnp.concatenate(outs, axis=-1)


def _ml_scan(q, k, p, gate_b_p):
    b, t, _ = q.shape
    nc = t // ML_CHUNK
    chunk = functools.partial(_scan_chunk, n_ctx_chunks=TM // ML_CHUNK, n_chunks=nc)
    blk = lambda cb: pl.BlockSpec((1, ML_CHUNK, ML_W), lambda d, bi, s: (bi, chunk(d, s), cb))
    return pl.pallas_call(
        _ml_scan_kernel,
        grid=(2, b, nc),
        in_specs=[
            blk(0), blk(0), blk(C_MV // ML_W),
            pl.BlockSpec((1, ML_CHUNK, LANE), lambda d, bi, s: (bi, chunk(d, s), C_SMF // LANE + d)),
            pl.BlockSpec((1, 1, LANE), lambda d, bi, s: (d, 0, 0)),
        ],
        out_specs=pl.BlockSpec((1, 1, ML_CHUNK, ML_W), lambda d, bi, s: (d, bi, chunk(d, s), 0)),
        out_shape=jax.ShapeDtypeStruct((2, b, t, ML_W), F32),
        scratch_shapes=[pltpu.VMEM((ML_HEADS, LANE, LANE), F32), pltpu.VMEM((ML_HEADS, 8, LANE), F32)],
        compiler_params=_cparams(("parallel", "parallel", "arbitrary")),
        name="mlstm_scan",
    )(q, k, p, p, gate_b_p)


def _gla_scan_kernel(q_ref, k_ref, v_ref, sm_ref, wa_ref, ba_ref, o_ref, s_ref):
    d = pl.program_id(0)
    L = GLA_CHUNK

    @pl.when(pl.program_id(2) == 0)
    def _():
        s_ref[...] = jnp.zeros_like(s_ref)

    row = lax.broadcasted_iota(jnp.int32, (L, L), 0)
    col = lax.broadcasted_iota(jnp.int32, (L, L), 1)
    mask = (row - col) * (1 - 2 * d) >= 0
    tri = mask.astype(F32)
    lane_k = lax.broadcasted_iota(jnp.int32, (1, GLA_HEADS * GLA_DK), 1)
    lane_v = lax.broadcasted_iota(jnp.int32, (1, GLA_HEADS * GLA_DV), 1)
    eye = (lax.broadcasted_iota(jnp.int32, (LANE, LANE), 0)
           == lax.broadcasted_iota(jnp.int32, (LANE, LANE), 1)).astype(F32)

    pre = jnp.dot(sm_ref[0], wa_ref[0], precision=HIGHEST, preferred_element_type=F32) + ba_ref[0]
    loga = _log_sigmoid(pre) * (1.0 / GLA_TAU)
    bc = jnp.dot(tri, loga, precision=HIGHEST, preferred_element_type=F32)
    ref_row = bc[L // 2 - 1:L // 2]
    b_end = jnp.where(d == 0, bc[L - 1:L], bc[0:1])

    q = q_ref[0] * (GLA_DK ** -0.5)
    k = k_ref[0]
    v = v_ref[0].astype(BF16)
    q_in = (q * jnp.exp(bc - ref_row))
    k_in = (k * jnp.exp(ref_row - bc)).astype(BF16)
    q_st = (q * jnp.exp(bc)).astype(BF16)
    k_st = (k * jnp.exp(b_end - bc)).astype(BF16)

    blockdiag = (lax.broadcasted_iota(jnp.int32, s_ref.shape, 0) // GLA_DK
                 == lax.broadcasted_iota(jnp.int32, s_ref.shape, 1) // GLA_DV)
    s_old = s_ref[...]
    o = jnp.dot(q_st, jnp.where(blockdiag, s_old, 0.0).astype(BF16), preferred_element_type=F32)
    for h in range(GLA_HEADS):
        qh = jnp.where(lane_k // GLA_DK == h, q_in, 0.0).astype(BF16)
        att = jnp.where(mask, _nt_dot(qh, k_in), 0.0).astype(BF16)
        oh = jnp.dot(att, v, preferred_element_type=F32)
        o = o + jnp.where(lane_v // GLA_DV == h, oh, 0.0)
    o_ref[0, 0] = o

    decay_col = jnp.exp(_nt_dot(eye, jnp.broadcast_to(b_end, (8, LANE)), precision=HIGHEST)[:, 0:1])
    s_ref[...] = decay_col * s_old + _tn_dot(k_st, v)


def _gla_scan(p, wa_p, ba_p):
    b, t, _ = p.shape
    nc = t // GLA_CHUNK
    chunk = functools.partial(_scan_chunk, n_ctx_chunks=TM // GLA_CHUNK, n_chunks=nc)
    kw, vw = GLA_HEADS * GLA_DK, GLA_HEADS * GLA_DV
    return pl.pallas_call(
        _gla_scan_kernel,
        grid=(2, b, nc),
        in_specs=[
            pl.BlockSpec((1, GLA_CHUNK, kw), lambda d, bi, s: (bi, chunk(d, s), C_GQ // kw)),
            pl.BlockSpec((1, GLA_CHUNK, kw), lambda d, bi, s: (bi, chunk(d, s), C_GK // kw)),
            pl.BlockSpec((1, GLA_CHUNK, vw), lambda d, bi, s: (bi, chunk(d, s), C_GV // vw)),
            pl.BlockSpec((1, GLA_CHUNK, LANE), lambda d, bi, s: (bi, chunk(d, s), C_SMF // LANE + d)),
            pl.BlockSpec((1, LANE, kw), lambda d, bi, s: (d, 0, 0)),
            pl.BlockSpec((1, 1, kw), lambda d, bi, s: (d, 0, 0)),
        ],
        out_specs=pl.BlockSpec((1, 1, GLA_CHUNK, vw), lambda d, bi, s: (d, bi, chunk(d, s), 0)),
        out_shape=jax.ShapeDtypeStruct((2, b, t, vw), F32),
        scratch_shapes=[pltpu.VMEM((kw, vw), F32)],
        compiler_params=_cparams(("parallel", "parallel", "arbitrary")),
        name="gla_scan",
    )(p, p, p, p, wa_p, ba_p)


def _outproj_kernel(ac_ref, al_ref, mhf_ref, mhb_ref, xc_ref, mo_ref, gof_ref, gob_ref, gr_ref, x_ref, mod_ref,
                    mnw_ref, msk_ref, gnw_ref, wout_ref, n2w_ref, wr_ref, br_ref,
                    xo_ref, h2_ref, route_ref):
    grp64 = (lax.broadcasted_iota(jnp.int32, (ML_W, ML_W), 0) // ML_DH
             == lax.broadcasted_iota(jnp.int32, (ML_W, ML_W), 1) // ML_DH).astype(F32) * (1.0 / ML_DH)

    def head_norm(x, w):
        ms = jnp.dot(x * x, grp64, precision=HIGHEST, preferred_element_type=F32)
        return x * lax.rsqrt(ms + EPS) * w

    m_l = jax.nn.sigmoid(mo_ref[0]) * (head_norm(mhf_ref[0, 0] + mhb_ref[0, 0], mnw_ref[...])
                                       + msk_ref[...] * xc_ref[0])
    g_l = head_norm(gof_ref[0, 0] + gob_ref[0, 0], gnw_ref[...]) * _silu(gr_ref[0])
    na = MLA_HEADS * MLA_V
    a = jnp.where(pl.program_id(1) == 0, ac_ref[0], al_ref[0])
    res = (jnp.dot(a, wout_ref[0:na], preferred_element_type=F32)
           + jnp.dot(m_l.astype(BF16), wout_ref[na:na + ML_W], preferred_element_type=F32)
           + jnp.dot(g_l.astype(BF16), wout_ref[na + ML_W:], preferred_element_type=F32))
    mod = mod_ref[0]
    x = x_ref[0] + mod[2:3] * res
    xo_ref[0] = x
    h2 = (x * lax.rsqrt(jnp.mean(x * x, axis=-1, keepdims=True) + EPS) * n2w_ref[...]
          * (1.0 + mod[4:5]) + mod[3:4])
    h2_ref[0] = h2

    logits = jnp.dot(h2, wr_ref[...], precision=HIGHEST, preferred_element_type=F32) + br_ref[...]
    lane = lax.broadcasted_iota(jnp.int32, (1, LANE), 1)
    lane_f = lane.astype(F32)
    neg = -jnp.inf
    gl = jnp.where(lane < R_EXP, logits, neg)
    gmax = jnp.max(gl, axis=-1, keepdims=True)
    g_w = 1.0 / jnp.sum(jnp.exp(gl - gmax), axis=-1, keepdims=True)
    g_i = jnp.min(jnp.where(gl == gmax, lane_f, float(LANE)), axis=-1, keepdims=True)
    grp_of_lane = ((lane - R_EXP) // EXP_PER_GROUP).astype(F32)
    in_grp = (lane >= R_EXP) & (lane < R_EXP + N_EXPERTS) & (grp_of_lane == g_i)
    el = jnp.where(in_grp, logits, neg)
    m1 = jnp.max(el, axis=-1, keepdims=True)
    i1 = jnp.min(jnp.where(el == m1, lane_f, float(LANE)), axis=-1, keepdims=True)
    el2 = jnp.where(lane_f == i1, neg, el)
    m2 = jnp.max(el2, axis=-1, keepdims=True)
    i2 = jnp.min(jnp.where(el2 == m2, lane_f, float(LANE)), axis=-1, keepdims=True)
    p2 = jnp.exp(m2 - m1)
    w1 = g_w / (1.0 + p2)
    route_ref[0] = jnp.where(lane == RT_E1, i1 - R_EXP, jnp.where(lane == RT_E1 + 1, i2 - R_EXP,
                             jnp.where(lane == RT_W1, w1, jnp.where(lane == RT_W1 + 1, p2 * w1, 0.0))))


def _outproj(a_ctx, a_lat, mh, xconv, p, go, xs, mods, ml_norm_w, ml_skip, gla_norm_w, w_out_b, norm2_w, wr_p, br_p):
    b, t, d = xs.shape
    nt = t // TM
    full = lambda shape: pl.BlockSpec(shape, lambda bi, i: (0,) * len(shape))
    tok = lambda w, cb=0: pl.BlockSpec((1, TM, w), lambda bi, i: (bi, i, cb))
    dirblk = lambda dd: pl.BlockSpec((1, 1, TM, ML_W), lambda bi, i: (dd, bi, i, 0))
    return pl.pallas_call(
        _outproj_kernel,
        grid=(b, nt),
        in_specs=[
            pl.BlockSpec((1, TM, MLA_HEADS * MLA_V), lambda bi, i: (bi, 0, 0)),
            pl.BlockSpec((1, TM, MLA_HEADS * MLA_V), lambda bi, i: (bi, jnp.maximum(i - 1, 0), 0)),
            dirblk(0), dirblk(1), tok(ML_W), tok(ML_W, C_MO // ML_W),
            dirblk(0), dirblk(1), tok(ML_W, C_GR // ML_W), tok(d),
            pl.BlockSpec((1, 6, d), lambda bi, i: (_mod_row(bi, i), 0, 0)),
            full((1, ML_W)), full((1, ML_W)), full((1, ML_W)), full((d, d)), full((1, d)),
            full((d, LANE)), full((1, LANE)),
        ],
        out_specs=[tok(d), tok(d), tok(LANE)],
        out_shape=[jax.ShapeDtypeStruct((b, t, d), F32), jax.ShapeDtypeStruct((b, t, d), F32),
                   jax.ShapeDtypeStruct((b, t, LANE), F32)],
        compiler_params=_cparams(("parallel", "parallel")),
        name="out_proj_router",
    )(a_ctx, a_lat, mh, mh, xconv, p, go, go, p, xs, mods, ml_norm_w.reshape(1, -1), ml_skip.reshape(1, -1),
      gla_norm_w.reshape(1, -1), w_out_b, norm2_w.reshape(1, -1), wr_p, br_p)


def _dispatch(route, n_tiles):
    n = route.shape[0]
    flat = route[:, RT_E1:RT_E1 + 2].astype(jnp.int32).reshape(-1)
    onehot = (flat[:, None] == jnp.arange(N_EXPERTS, dtype=jnp.int32)[None, :]).astype(jnp.int32)
    csum = jnp.cumsum(onehot, axis=0)
    rank = jnp.sum(csum * onehot, axis=1) - 1
    padded = (csum[-1] + TM - 1) // TM * TM
    ends = jnp.cumsum(padded)
    pos = (ends - padded)[flat] + rank
    src = jnp.zeros((n_tiles * TM,), jnp.int32).at[pos].set(jnp.arange(2 * n, dtype=jnp.int32) // 2)
    tile_start = jnp.arange(n_tiles, dtype=jnp.int32) * TM
    tile_exp = jnp.minimum(jnp.sum((ends[None, :] <= tile_start[:, None]).astype(jnp.int32), axis=1),
                           N_EXPERTS - 1)
    tile_on = (tile_start < ends[-1]).astype(jnp.int32)
    pos = pos.reshape(n // TM, TM, 2)
    return (src.reshape(n_tiles, 1, TM), tile_exp, tile_on,
            pos[:, :, 0].reshape(n // TM, 1, TM), pos[:, :, 1].reshape(n // TM, 1, TM))


def _gather_rows(src_hbm, idx_ref, dst, sem):
    def body(j, carry):
        pltpu.make_async_copy(src_hbm.at[pl.ds(idx_ref[0, 0, j], 1)], dst.at[pl.ds(j, 1)], sem).start()
        return carry

    lax.fori_loop(0, TM, body, 0, unroll=8)


def _wait_rows(src_hbm, dst, sem):
    pltpu.make_async_copy(src_hbm.at[pl.ds(0, TM)], dst, sem).wait()


def _experts_kernel(texp_ref, ton_ref, idx_ref, idxn_ref, h2_hbm, wg_ref, wu_ref, wd_ref, y_ref,
                    buf, wgu_b, wd_b, sem):
    r = pl.program_id(0)
    slot = r % 2

    @pl.when(r == 0)
    def _():
        _gather_rows(h2_hbm, idx_ref, buf.at[0], sem.at[0])

    last = pl.num_programs(0) - 1

    @pl.when(jnp.logical_and(r < last, ton_ref[jnp.minimum(r + 1, last)] == 1))
    def _():
        _gather_rows(h2_hbm, idxn_ref, buf.at[1 - slot], sem.at[1 - slot])

    @pl.when(ton_ref[r] == 1)
    def _():
        @pl.when(jnp.logical_or(r == 0, texp_ref[r] != texp_ref[jnp.maximum(r - 1, 0)]))
        def _():
            wgu_b[:, :D_EXPERT] = wg_ref[0, 0].astype(BF16)
            wgu_b[:, D_EXPERT:] = wu_ref[0, 0].astype(BF16)
            wd_b[...] = wd_ref[0, 0].astype(BF16)

        _wait_rows(h2_hbm, buf.at[slot], sem.at[slot])
        gu = jnp.dot(buf[slot].astype(BF16), wgu_b[...], preferred_element_type=F32)
        act = (_silu(gu[:, :D_EXPERT]) * gu[:, D_EXPERT:]).astype(BF16)
        y_ref[...] = jnp.dot(act, wd_b[...], preferred_element_type=F32)

    @pl.when(ton_ref[r] == 0)
    def _():
        y_ref[...] = jnp.zeros_like(y_ref)


def _experts(h2, src, tile_exp, tile_on, w_gate, w_up, w_down, layer):
    n, d = h2.shape
    n_tiles = src.shape[0]
    idx = lambda nxt: pl.BlockSpec((1, 1, TM), lambda r, te, to: (jnp.minimum(r + nxt, n_tiles - 1), 0, 0),
                                   memory_space=pltpu.SMEM)
    wspec = lambda shape: pl.BlockSpec((1, 1) + shape, lambda r, te, to: (layer, te[r], 0, 0))
    return pl.pallas_call(
        _experts_kernel,
        grid_spec=pltpu.PrefetchScalarGridSpec(
            num_scalar_prefetch=2,
            grid=(n_tiles,),
            in_specs=[
                idx(0), idx(1),
                pl.BlockSpec(memory_space=pl.ANY),
                wspec((d, D_EXPERT)), wspec((d, D_EXPERT)), wspec((D_EXPERT, d)),
            ],
            out_specs=pl.BlockSpec((TM, d), lambda r, te, to: (r, 0)),
            scratch_shapes=[pltpu.VMEM((2, TM, d), F32), pltpu.VMEM((d, 2 * D_EXPERT), BF16),
                            pltpu.VMEM((D_EXPERT, d), BF16), pltpu.SemaphoreType.DMA((2,))],
        ),
        out_shape=jax.ShapeDtypeStruct((n_tiles * TM, d), F32),
        compiler_params=_cparams(("arbitrary",)),
        name="moe_experts",
    )(tile_exp, tile_on, src, src, h2, w_gate, w_up, w_down)


def _combine_kernel(p1_ref, p2_ref, p1n_ref, p2n_ref, route_ref, x_ref, mod_ref, y_hbm, o_ref, buf, sem):
    g = pl.program_id(0)
    slot = g % 2

    def gather(pa, pb, s):
        _gather_rows(y_hbm, pa, buf.at[s, 0], sem.at[s])
        _gather_rows(y_hbm, pb, buf.at[s, 1], sem.at[s])

    @pl.when(g == 0)
    def _():
        gather(p1_ref, p2_ref, 0)

    @pl.when(g + 1 < pl.num_programs(0))
    def _():
        gather(p1n_ref, p2n_ref, 1 - slot)

    _wait_rows(y_hbm, buf.at[slot, 0], sem.at[slot])
    _wait_rows(y_hbm, buf.at[slot, 1], sem.at[slot])
    route = route_ref[...]
    moe = route[:, RT_W1:RT_W1 + 1] * buf[slot, 0] + route[:, RT_W1 + 1:RT_W1 + 2] * buf[slot, 1]
    o_ref[...] = x_ref[...] + mod_ref[0][5:6] * moe


def _combine(y, pos1, pos2, route, xs2, mods, nt):
    n, d = xs2.shape
    n_tok_tiles = n // TM
    idx = lambda nxt: pl.BlockSpec((1, 1, TM), lambda g: (jnp.minimum(g + nxt, n_tok_tiles - 1), 0, 0),
                                   memory_space=pltpu.SMEM)
    tok = lambda w: pl.BlockSpec((TM, w), lambda g: (g, 0))
    return pl.pallas_call(
        _combine_kernel,
        grid=(n_tok_tiles,),
        in_specs=[
            idx(0), idx(0), idx(1), idx(1), tok(LANE), tok(d),
            pl.BlockSpec((1, 6, d), lambda g: (_mod_row(g // nt, g % nt), 0, 0)),
            pl.BlockSpec(memory_space=pl.ANY),
        ],
        out_specs=tok(d),
        out_shape=jax.ShapeDtypeStruct((n, d), F32),
        scratch_shapes=[pltpu.VMEM((2, 2, TM, d), F32), pltpu.SemaphoreType.DMA((2,))],
        compiler_params=_cparams(("arbitrary",)),
        name="moe_combine",
    )(pos1, pos2, pos1, pos2, route, xs2, mods, y)


def _moe(h2, route, xs, mods, w_gate, w_up, w_down, layer):
    b, t, d = xs.shape
    n = b * t
    n_tiles = 2 * n // TM + N_EXPERTS
    route2 = route.reshape(n, LANE)
    src, tile_exp, tile_on, pos1, pos2 = _dispatch(route2, n_tiles)
    y = _experts(h2.reshape(n, d), src, tile_exp, tile_on, w_gate, w_up, w_down, layer)
    return _combine(y, pos1, pos2, route2, xs.reshape(n, d), mods, t // TM).reshape(b, t, d)


def _rope_tables(n_ctx, n_lat):
    rows = n_lat // GRID_W
    row = jnp.broadcast_to(jnp.arange(rows, dtype=F32)[:, None], (rows, GRID_W)).reshape(-1)
    col = jnp.broadcast_to(jnp.arange(GRID_W, dtype=F32)[None, :], (rows, GRID_W)).reshape(-1)
    n_freq = MLA_ROPE // 4
    inv = ROPE_THETA ** (-jnp.arange(n_freq, dtype=F32) / n_freq)
    ang = jnp.concatenate([row[:, None] * inv, col[:, None] * inv], axis=-1)
    cos, sin = jnp.cos(ang), jnp.sin(ang)
    half = MLA_ROPE // 2
    z = lambda w: jnp.zeros((n_lat, w), F32)
    o = lambda w: jnp.ones((n_lat, w), F32)
    tail = LANE - MLA_QK
    cos_t = jnp.concatenate([o(MLA_NOPE), cos, cos, o(tail)], axis=-1)
    sa_t = jnp.concatenate([z(MLA_NOPE + half), sin, z(tail)], axis=-1)
    sb_t = jnp.concatenate([z(MLA_NOPE), -sin, z(half + tail)], axis=-1)
    ctx1 = jnp.ones((n_ctx, LANE), F32)
    ctx0 = jnp.zeros((n_ctx, LANE), F32)
    return (jnp.concatenate([ctx1, cos_t], 0), jnp.concatenate([ctx0, sa_t], 0),
            jnp.concatenate([ctx0, sb_t], 0))


def _pad_cols(a, width):
    return jnp.pad(a, [(0, 0)] * (a.ndim - 1) + [(0, width - a.shape[-1])])


def _layer_weights(w_in, w_uq, w_ukv, q_norm_w, k_norm_w, ml_conv_w, ml_wq, ml_wk, ml_gate_b,
                   gla_wa, gla_ba, w_out, w_grp, b_grp, w_erouter, b_erouter):
    d = w_in.shape[0]
    o = np.cumsum((0, Q_LORA, KV_LORA, MLA_ROPE, ML_W, ML_W, ML_W, 4 * ML_HEADS, GLA_HEADS * GLA_DK,
                   GLA_HEADS * GLA_DK, GLA_HEADS * GLA_DV, GLA_HEADS * GLA_DV, 2 * GLA_LR))
    seg = lambda j: w_in[:, o[j]:o[j + 1]]
    cq, ckv, kr, mx, mv, mo, mg, gq, gk, gv, gr, ga = (seg(j) for j in range(12))
    z = lambda w: jnp.zeros((d, w), F32)

    def small(di):
        return jnp.concatenate([mg[:, di * 8:(di + 1) * 8], ga[:, di * GLA_LR:(di + 1) * GLA_LR],
                                z(SM_KR - SM_GA - GLA_LR), kr, z(LANE - SM_KR - MLA_ROPE)], axis=-1)

    w_in_p = jnp.concatenate([cq, mx, mv, mo, gv, gr, ckv, gq, gk, small(0), small(1)], axis=-1).astype(BF16)

    wuq_p = _pad_cols(w_uq.reshape(Q_LORA, MLA_HEADS, MLA_QK), LANE).reshape(Q_LORA, -1).astype(BF16)
    ukv = w_ukv.reshape(KV_LORA, MLA_HEADS, MLA_NOPE + MLA_V)
    wuk_p = _pad_cols(ukv[..., :MLA_NOPE], LANE).reshape(KV_LORA, -1).astype(BF16)
    wuv_p = _pad_cols(ukv[..., MLA_NOPE:], LANE).reshape(KV_LORA, -1).astype(BF16)
    qn_p = _pad_cols(q_norm_w.reshape(1, -1), LANE)
    kn_p = _pad_cols(k_norm_w.reshape(1, -1), LANE)

    conv_w8 = jnp.pad(ml_conv_w, ((0, 8 - ML_CONV), (0, 0)))
    bd = lambda w: jax.scipy.linalg.block_diag(*[w[h] for h in range(ML_HEADS)])
    wq_bd = (bd(ml_wq) * (ML_DH ** -0.5)).astype(BF16)
    wk_bd = bd(ml_wk).astype(BF16)
    gate_b_p = _pad_cols(ml_gate_b.reshape(2, 1, 2 * ML_HEADS), LANE)

    wa_p = jnp.pad(gla_wa, ((0, 0), (SM_GA, LANE - SM_GA - GLA_LR), (0, 0)))
    ba_p = gla_ba.reshape(2, 1, -1)

    wr_p = _pad_cols(jnp.concatenate([w_grp, w_erouter], axis=-1), LANE)
    br_p = _pad_cols(jnp.concatenate([b_grp, b_erouter]).reshape(1, -1), LANE)
    return dict(w_in_p=w_in_p, wuq_p=wuq_p, wuk_p=wuk_p, wuv_p=wuv_p, qn_p=qn_p, kn_p=kn_p,
                conv_w8=conv_w8, wq_bd=wq_bd, wk_bd=wk_bd, gate_b_p=gate_b_p, wa_p=wa_p, ba_p=ba_p,
                w_out_b=w_out.astype(BF16), wr_p=wr_p, br_p=br_p)


def kernel(x, c, ctx, c_ctx, w_mod, b_mod, norm1_w, w_in, q_a_norm, w_uq, kv_a_norm, w_ukv,
           q_norm_w, k_norm_w, ml_conv_w, ml_conv_b, ml_wq, ml_wk, ml_gate_b, ml_norm_w, ml_skip,
           gla_wa, gla_ba, gla_norm_w, w_out, norm2_w, w_grp, b_grp, w_erouter, b_erouter,
           w_gate, w_up, w_down):
    b, s, d = x.shape
    n_ctx = ctx.shape[1]
    depth = w_mod.shape[0]
    assert n_ctx == TM and s % TM == 0 and b == 2

    cc = jnp.concatenate([c, c_ctx[None, :], jnp.zeros((8 - b - 1, d), F32)], axis=0)
    mods_all = _mods(cc, w_mod, b_mod).reshape(depth, 8, 6, d)
    cos_t, sa_t, sb_t = _rope_tables(n_ctx, s)
    xs = jnp.concatenate([ctx, x], axis=1)

    for l in range(depth):
        w = _layer_weights(w_in[l], w_uq[l], w_ukv[l], q_norm_w[l], k_norm_w[l], ml_conv_w[l],
                           ml_wq[l], ml_wk[l], ml_gate_b[l], gla_wa[l], gla_ba[l], w_out[l],
                           w_grp[l], b_grp[l], w_erouter[l], b_erouter[l])
        mods = mods_all[l]
        p = _inproj(xs, mods, norm1_w[l], w["w_in_p"])
        q, k, v = _mla_prep(p, q_a_norm[l], w["wuq_p"], kv_a_norm[l], w["wuk_p"], w["wuv_p"],
                            w["qn_p"], w["kn_p"], cos_t, sa_t, sb_t)
        a_lat = _attention_lat(q, k, v)
        a_ctx = _attention_ctx(q, k, v) if l < depth - 1 else jnp.zeros((b, TM, MLA_HEADS * MLA_V), BF16)
        xconv, mq, mk = _ml_prep(p, w["conv_w8"], ml_conv_b[l], w["wq_bd"], w["wk_bd"])
        mh = _ml_scan(mq, mk, p, w["gate_b_p"])
        go = _gla_scan(p, w["wa_p"], w["ba_p"])
        xs, h2, route = _outproj(a_ctx, a_lat, mh, xconv, p, go, xs, mods, ml_norm_w[l], ml_skip[l], gla_norm_w[l],
                                w["w_out_b"], norm2_w[l], w["wr_p"], w["br_p"])
        xs = _moe(h2, route, xs, mods, w_gate, w_up, w_down, l)
    return xs[:, n_ctx:, :]
```

```python
import functools

import jax
import jax.numpy as jnp
import numpy as np
from jax import lax
from jax.experimental import pallas as pl
from jax.experimental.pallas import tpu as pltpu

F32 = jnp.float32
BF16 = jnp.bfloat16
HIGHEST = lax.Precision.HIGHEST

EPS = 1e-6
GRID_W = 64
ROPE_THETA = 10000.0

MLA_HEADS = 8
MLA_NOPE = 64
MLA_ROPE = 32
MLA_QK = MLA_NOPE + MLA_ROPE
MLA_V = 64
Q_LORA = 256
KV_LORA = 128

ML_HEADS = 4
ML_DH = 64
ML_W = ML_HEADS * ML_DH
ML_CONV = 5

GLA_HEADS = 4
GLA_DK = 32
GLA_DV = 64
GLA_LR = 16
GLA_TAU = 16.0

N_GROUPS = 4
EXP_PER_GROUP = 8
N_EXPERTS = N_GROUPS * EXP_PER_GROUP
D_EXPERT = 256

LANE = 128
TM = 256
ML_CHUNK = 256
GLA_CHUNK = 128
VMEM_LIMIT = 56 * 1024 * 1024

C_CQ, C_MX, C_MV, C_MO, C_GV, C_GR = 0, 256, 512, 768, 1024, 1280
C_CKV, C_GQ, C_GK, C_SMF, C_SMB = 1536, 1664, 1792, 1920, 2048
D_INP = 2176
SM_GATE = 0
SM_GA = 8
SM_KR = 64
R_GRP = 0
R_EXP = 4
RT_E1 = 0
RT_W1 = 2


def _cparams(sem):
    return pltpu.CompilerParams(dimension_semantics=sem, vmem_limit_bytes=VMEM_LIMIT)


def _silu(x):
    return x * jax.nn.sigmoid(x)


def _log_sigmoid(x):
    return -(jnp.maximum(-x, 0.0) + jnp.log1p(jnp.exp(-jnp.abs(x))))


def _nt_dot(a, b, **kw):
    return lax.dot_general(a, b, (((1,), (1,)), ((), ())), preferred_element_type=F32, **kw)


def _tn_dot(a, b, **kw):
    return lax.dot_general(a, b, (((0,), (0,)), ((), ())), preferred_element_type=F32, **kw)


def _mods_kernel(cc_ref, w_ref, b_ref, o_ref):
    a = _silu(cc_ref[...])
    o_ref[0] = jnp.dot(a, w_ref[0], precision=HIGHEST, preferred_element_type=F32) + b_ref[0]


def _mods(cc, w_mod, b_mod):
    depth, d, d6 = w_mod.shape
    nb = 1536
    return pl.pallas_call(
        _mods_kernel,
        grid=(depth, d6 // nb),
        in_specs=[
            pl.BlockSpec((8, d), lambda l, j: (0, 0)),
            pl.BlockSpec((1, d, nb), lambda l, j: (l, 0, j)),
            pl.BlockSpec((1, 1, nb), lambda l, j: (l, 0, j)),
        ],
        out_specs=pl.BlockSpec((1, 8, nb), lambda l, j: (l, 0, j)),
        out_shape=jax.ShapeDtypeStruct((depth, 8, d6), F32),
        compiler_params=_cparams(("arbitrary", "arbitrary")),
        name="adaln_mods",
    )(cc, w_mod, b_mod.reshape(depth, 1, d6))


def _mod_row(b, i):
    return jnp.where(i == 0, 2, b)


def _inproj_kernel(x_ref, mod_ref, nw_ref, w_ref, o_ref):
    x = x_ref[0]
    y = x * lax.rsqrt(jnp.mean(x * x, axis=-1, keepdims=True) + EPS) * nw_ref[...]
    mod = mod_ref[0]
    h = y * (1.0 + mod[1:2]) + mod[0:1]
    o_ref[0] = jnp.dot(h.astype(BF16), w_ref[...], preferred_element_type=F32)


def _inproj(xs, mods, norm_w, w_in_p):
    b, t, d = xs.shape
    nt = t // TM
    return pl.pallas_call(
        _inproj_kernel,
        grid=(b, nt),
        in_specs=[
            pl.BlockSpec((1, TM, d), lambda bi, i: (bi, i, 0)),
            pl.BlockSpec((1, 6, d), lambda bi, i: (_mod_row(bi, i), 0, 0)),
            pl.BlockSpec((1, d), lambda bi, i: (0, 0)),
            pl.BlockSpec((d, D_INP), lambda bi, i: (0, 0)),
        ],
        out_specs=pl.BlockSpec((1, TM, D_INP), lambda bi, i: (bi, i, 0)),
        out_shape=jax.ShapeDtypeStruct((b, t, D_INP), F32),
        compiler_params=_cparams(("parallel", "parallel")),
        name="in_proj",
    )(xs, mods, norm_w.reshape(1, d), w_in_p)


def _mla_prep_kernel(cq_ref, ckv_ref, sm_ref, qan_ref, wuq_ref, kvan_ref, wuk_ref, wuv_ref,
                     qn_ref, kn_ref, cos_ref, sa_ref, sb_ref, q_ref, k_ref, v_ref):
    cq = cq_ref[0]
    cqn = cq * lax.rsqrt(jnp.mean(cq * cq, axis=-1, keepdims=True) + EPS) * qan_ref[...]
    qall = jnp.dot(cqn.astype(BF16), wuq_ref[...], preferred_element_type=F32)
    ckv = ckv_ref[0]
    ckvn = (ckv * lax.rsqrt(jnp.mean(ckv * ckv, axis=-1, keepdims=True) + EPS)
            * kvan_ref[...]).astype(BF16)
    kall = jnp.dot(ckvn, wuk_ref[...], preferred_element_type=F32)
    vall = jnp.dot(ckvn, wuv_ref[...], preferred_element_type=F32)
    lane = lax.broadcasted_iota(jnp.int32, (1, LANE), 1)
    kr = jnp.where((lane >= SM_KR) & (lane < SM_KR + MLA_ROPE), sm_ref[0], 0.0)
    cos, sa, sb = cos_ref[...], sa_ref[...], sb_ref[...]

    def rope(x):
        return x * cos + pltpu.roll(x, 16, 1) * sa + pltpu.roll(x, LANE - 16, 1) * sb

    def head_norm(x, w):
        return x * lax.rsqrt(jnp.sum(x * x, axis=-1, keepdims=True) * (1.0 / MLA_QK) + EPS) * w

    for h in range(MLA_HEADS):
        sl = slice(h * LANE, (h + 1) * LANE)
        qh = rope(head_norm(qall[:, sl], qn_ref[...]))
        q_ref[0, h] = (qh * Q_SCALE).astype(BF16)
        kh = rope(head_norm(kall[:, sl] + kr, kn_ref[...]))
        k_ref[0, h] = jnp.where(lane == ATT_SHIFT_LANE, 1.0, kh).astype(BF16)
        v_ref[0, h] = jnp.where(lane == MLA_V, 1.0, vall[:, sl]).astype(BF16)


def _mla_prep(p, q_a_norm, wuq_p, kv_a_norm, wuk_p, wuv_p, qn_p, kn_p, cos_t, sa_t, sb_t):
    b, t, _ = p.shape
    nt = t // TM
    hw = MLA_HEADS * LANE
    full = lambda shape: pl.BlockSpec(shape, lambda bi, i: (0,) * len(shape))
    tab = pl.BlockSpec((TM, LANE), lambda bi, i: (i, 0))
    out = pl.BlockSpec((1, MLA_HEADS, TM, LANE), lambda bi, i: (bi, 0, i, 0))
    shp = jax.ShapeDtypeStruct((b, MLA_HEADS, t, LANE), BF16)
    return pl.pallas_call(
        _mla_prep_kernel,
        grid=(b, nt),
        in_specs=[
            pl.BlockSpec((1, TM, Q_LORA), lambda bi, i: (bi, i, C_CQ // Q_LORA)),
            pl.BlockSpec((1, TM, KV_LORA), lambda bi, i: (bi, i, C_CKV // KV_LORA)),
            pl.BlockSpec((1, TM, LANE), lambda bi, i: (bi, i, C_SMF // LANE)),
            full((1, Q_LORA)), full((Q_LORA, hw)), full((1, KV_LORA)),
            full((KV_LORA, hw)), full((KV_LORA, hw)), full((1, LANE)), full((1, LANE)),
            tab, tab, tab,
        ],
        out_specs=[out, out, out],
        out_shape=[shp, shp, shp],
        compiler_params=_cparams(("parallel", "parallel")),
        name="mla_prep",
    )(p, p, p, q_a_norm.reshape(1, -1), wuq_p, kv_a_norm.reshape(1, -1), wuk_p, wuv_p,
      qn_p, kn_p, cos_t, sa_t, sb_t)


Q_SCALE = float(MLA_QK ** -0.5 * np.log2(np.e))
ATT_HP = 2
ATT_NQ = 4
ATT_TK = 2048
ATT_SHIFT_LANE = MLA_QK
ATT_SAFE_MAX = 2.0 ** 100


def _softmax_step(q, kb, vb, m, acc):
    s = _nt_dot(q, kb)
    m_new = jnp.max(s, axis=-1, keepdims=True)
    if m is None:
        return m_new, jnp.dot(jnp.exp2((s - m_new).astype(BF16)), vb, preferred_element_type=F32)
    m_new = jnp.maximum(m, m_new)
    p = jnp.exp2((s - m_new).astype(BF16))
    return m_new, acc * jnp.exp2(m - m_new) + jnp.dot(p, vb, preferred_element_type=F32)


def _attn_ctx_kernel(q_ref, k_ref, v_ref, o_ref):
    outs = []
    for h in range(MLA_HEADS):
        _, acc = _softmax_step(q_ref[0, h], k_ref[0, h], v_ref[0, h], None, None)
        outs.append(acc[:, :MLA_V] / acc[:, MLA_V:MLA_V + 1])
    o_ref[0] = jnp.concatenate(outs, axis=-1).astype(BF16)


def _attention_ctx(q, k, v):
    b, h, _, _ = q.shape
    blk = pl.BlockSpec((1, h, TM, LANE), lambda bi: (bi, 0, 0, 0))
    return pl.pallas_call(
        _attn_ctx_kernel,
        grid=(b,),
        in_specs=[blk, blk, blk],
        out_specs=pl.BlockSpec((1, TM, h * MLA_V), lambda bi: (bi, 0, 0)),
        out_shape=jax.ShapeDtypeStruct((b, TM, h * MLA_V), BF16),
        compiler_params=_cparams(("parallel",)),
        name="mla_attention_ctx",
    )(q, k, v)


def _attn_lat_kernel(*refs, n_blk):
    q_refs, (k_ref, v_ref, o_ref, q_buf, qs_buf) = refs[:ATT_NQ], refs[ATT_NQ:]
    lane = lax.broadcasted_iota(jnp.int32, (1, LANE), 1)

    def kv_block(hh, j):
        off = pl.multiple_of(TM + j * ATT_TK, TM)
        return k_ref[0, hh, pl.ds(off, ATT_TK), :], v_ref[0, hh, pl.ds(off, ATT_TK), :]

    def finish(accs):
        outs = [acc[:, :MLA_V] / acc[:, MLA_V:MLA_V + 1] for acc in accs]
        o_ref[0] = jnp.concatenate(outs, axis=-1).astype(BF16)

    accs = []
    for hh in range(ATT_HP):
        q = jnp.concatenate([qr[0, hh] for qr in q_refs], axis=0)
        q_buf[hh] = q
        kb, vb = k_ref[0, hh, 0:TM, :], v_ref[0, hh, 0:TM, :]
        shift = jnp.max(_nt_dot(q, kb), axis=-1, keepdims=True).astype(BF16)
        qs_buf[hh] = jnp.where(lane == ATT_SHIFT_LANE, -shift, q)
        accs.append(jnp.dot(jnp.exp2(_nt_dot(qs_buf[hh], kb).astype(BF16)), vb, preferred_element_type=F32))

    def fast_body(j, accs):
        new = []
        for hh in range(ATT_HP):
            kb, vb = kv_block(hh, j)
            p = jnp.exp2(_nt_dot(qs_buf[hh], kb).astype(BF16))
            new.append(accs[hh] + jnp.dot(p, vb, preferred_element_type=F32))
        return tuple(new)

    accs = lax.fori_loop(0, n_blk, fast_body, tuple(accs))
    bad = sum(jnp.max(jnp.where(jnp.abs(acc) < ATT_SAFE_MAX, 0.0, 1.0)) for acc in accs)

    @pl.when(bad == 0.0)
    def _():
        finish(accs)

    @pl.when(bad != 0.0)
    def _():
        init = []
        for hh in range(ATT_HP):
            init += _softmax_step(q_buf[hh], k_ref[0, hh, 0:TM, :], v_ref[0, hh, 0:TM, :], None, None)

        def body(j, carry):
            new = []
            for hh in range(ATT_HP):
                new += _softmax_step(q_buf[hh], *kv_block(hh, j), carry[2 * hh], carry[2 * hh + 1])
            return tuple(new)

        carry = lax.fori_loop(0, n_blk, body, tuple(init))
        finish(carry[1::2])


def _attention_lat(q, k, v):
    b, h, t, _ = q.shape
    s = t - TM
    tq = ATT_NQ * TM
    assert s % ATT_TK == 0 and s % tq == 0
    kv = pl.BlockSpec((1, ATT_HP, t, LANE), lambda bi, hp, i: (bi, hp, 0, 0))
    qs = [pl.BlockSpec((1, ATT_HP, TM, LANE), lambda bi, hp, i, u=u: (bi, hp, 1 + ATT_NQ * i + u, 0))
          for u in range(ATT_NQ)]
    return pl.pallas_call(
        functools.partial(_attn_lat_kernel, n_blk=s // ATT_TK),
        grid=(b, h // ATT_HP, s // tq),
        in_specs=qs + [kv, kv],
        out_specs=pl.BlockSpec((1, tq, ATT_HP * MLA_V), lambda bi, hp, i: (bi, i, hp)),
        out_shape=jax.ShapeDtypeStruct((b, s, h * MLA_V), BF16),
        scratch_shapes=[pltpu.VMEM((ATT_HP, tq, LANE), BF16), pltpu.VMEM((ATT_HP, tq, LANE), BF16)],
        compiler_params=_cparams(("parallel", "parallel", "arbitrary")),
        name="mla_attention",
    )(*([q] * ATT_NQ), k, v)


def _ml_prep_kernel(x_ref, prev_ref, next_ref, cw_ref, cb_ref, wq_ref, wk_ref,
                    xc_ref, q_ref, k_ref, *, n_tiles):
    i = pl.program_id(1)
    x = x_ref[0]
    prev = jnp.where(i <= 1, 0.0, prev_ref[0])
    nxt = jnp.where((i == 0) | (i == n_tiles - 1), 0.0, next_ref[0])
    ext = jnp.concatenate([prev, x, nxt], axis=0)
    n_ext = TM + 16
    cw = cw_ref[...]
    acc = jnp.zeros((TM, ML_W), F32) + cb_ref[...]
    for kk in range(ML_CONV):
        sh = (ML_CONV // 2 - kk) % n_ext
        shifted = ext if sh == 0 else pltpu.roll(ext, sh, 0)
        acc = acc + cw[kk:kk + 1] * shifted[8:8 + TM]
    xc = _silu(acc)
    xc_ref[0] = xc
    xb = xc.astype(BF16)
    q_ref[0] = jnp.dot(xb, wq_ref[...], preferred_element_type=F32).astype(BF16)
    k_ref[0] = jnp.dot(xb, wk_ref[...], preferred_element_type=F32).astype(BF16)


def _ml_prep(p, conv_w8, conv_b, wq_bd, wk_bd):
    b, t, _ = p.shape
    nt = t // TM
    r8 = TM // 8
    full = lambda shape: pl.BlockSpec(shape, lambda bi, i: (0,) * len(shape))
    cb = C_MX // ML_W
    blk = pl.BlockSpec((1, TM, ML_W), lambda bi, i: (bi, i, 0))
    return pl.pallas_call(
        functools.partial(_ml_prep_kernel, n_tiles=nt),
        grid=(b, nt),
        in_specs=[
            pl.BlockSpec((1, TM, ML_W), lambda bi, i: (bi, i, cb)),
            pl.BlockSpec((1, 8, ML_W), lambda bi, i: (bi, jnp.maximum(i * r8 - 1, 0), cb)),
            pl.BlockSpec((1, 8, ML_W), lambda bi, i: (bi, jnp.minimum((i + 1) * r8, nt * r8 - 1), cb)),
            full((8, ML_W)), full((1, ML_W)), full((ML_W, ML_W)), full((ML_W, ML_W)),
        ],
        out_specs=[blk, blk, blk],
        out_shape=[jax.ShapeDtypeStruct((b, t, ML_W), F32),
                   jax.ShapeDtypeStruct((b, t, ML_W), BF16),
                   jax.ShapeDtypeStruct((b, t, ML_W), BF16)],
        compiler_params=_cparams(("parallel", "parallel")),
        name="mlstm_prep",
    )(p, p, p, conv_w8, conv_b.reshape(1, ML_W), wq_bd, wk_bd)


def _scan_chunk(d, step, n_ctx_chunks, n_chunks):
    bwd = jnp.where(step < n_ctx_chunks, n_ctx_chunks - 1 - step, n_chunks - 1 - (step - n_ctx_chunks))
    return jnp.where(d == 0, step, bwd)


def _ml_chunk(d, q, k, v, g, s_ref, m_ref, base):
    L = ML_CHUNK
    row = lax.broadcasted_iota(jnp.int32, (L, L), 0)
    col = lax.broadcasted_iota(jnp.int32, (L, L), 1)
    mask = col <= row if d == 0 else col >= row
    tri = mask.astype(F32)
    lane = lax.broadcasted_iota(jnp.int32, (1, LANE), 1)
    eye8 = (lax.broadcasted_iota(jnp.int32, (8, LANE), 0)
            == lax.broadcasted_iota(jnp.int32, (8, LANE), 1)).astype(F32)

    lf = _log_sigmoid(g)
    bc = jnp.dot(tri, lf, precision=HIGHEST, preferred_element_type=F32)
    g_rows = _nt_dot(eye8, g, precision=HIGHEST)
    bc_rows = _nt_dot(eye8, bc, precision=HIGHEST)

    outs = []
    for pair in range(ML_HEADS // 2):
        sl = slice(pair * LANE, (pair + 1) * LANE)
        q_blk, k_blk, v_blk = q[:, sl], k[:, sl], v[:, sl]
        pair_out = []
        for sub in range(2):
            h = pair * 2 + sub
            head_lanes = (lane >= sub * ML_DH) & (lane < (sub + 1) * ML_DH)
            qh = jnp.where(head_lanes, q_blk, jnp.zeros_like(q_blk))
            vs = v_blk if sub == 0 else pltpu.roll(v_blk, ML_DH, 1)
            v_ext = jnp.where(lane < ML_DH, vs, jnp.where(lane == ML_DH, 1.0, 0.0)).astype(BF16)

            li_c = g[:, SM_GATE + h:SM_GATE + h + 1]
            bc_c = bc[:, SM_GATE + 4 + h:SM_GATE + 5 + h]
            li_r = g_rows[h:h + 1, :]
            bc_r = bc_rows[4 + h:5 + h, :]
            m_st = m_ref[base + h][0:1, 0:1]

            dmat = jnp.where(mask, bc_c + (li_r - bc_r), -jnp.inf)
            inter = bc_c + m_st
            m_t = jnp.maximum(inter, jnp.max(dmat, axis=-1, keepdims=True))
            e = jnp.exp(dmat - m_t)
            s = (_nt_dot(qh, k_blk) * e).astype(BF16)
            tot = (jnp.dot(s, v_ext, preferred_element_type=F32)
                   + jnp.exp(inter - m_t) * jnp.dot(qh, s_ref[base + h].astype(BF16),
                                                    preferred_element_type=F32))
            den = tot[:, ML_DH:ML_DH + 1]
            pair_out.append(tot / jnp.maximum(jnp.abs(den), jnp.exp(-m_t)))

            b_end = bc_c[L - 1:L] if d == 0 else bc_c[0:1]
            g_col = b_end - bc_c + li_c
            m_new = jnp.maximum(b_end + m_st, jnp.max(g_col, axis=0, keepdims=True))
            kw = jnp.where(head_lanes, k_blk.astype(F32) * jnp.exp(g_col - m_new), 0.0).astype(BF16)
            s_ref[base + h] = jnp.exp(b_end + m_st - m_new) * s_ref[base + h] + _tn_dot(kw, v_ext)
            m_ref[base + h] = jnp.broadcast_to(m_new, m_ref.shape[1:])
        outs.append(jnp.where(lane < ML_DH, pair_out[0], pltpu.roll(pair_out[1], ML_DH, 1)))
    return jnp.concatenate(outs, axis=-1)


def _ml_scan_kernel(qf_ref, kf_ref, vf_ref, smf_ref, qb_ref, kb_ref, vb_ref, smb_ref, gb_ref,
                    hf_ref, hb_ref, s_ref, m_ref):
    @pl.when(pl.program_id(0) == 0)
    def _():
        s_ref[...] = jnp.zeros_like(s_ref)
        m_ref[...] = jnp.zeros_like(m_ref)

    nb = qf_ref.shape[0]
    streams = ((0, qf_ref, kf_ref, vf_ref, smf_ref, hf_ref), (1, qb_ref, kb_ref, vb_ref, smb_ref, hb_ref))
    for d, q_ref, k_ref, v_ref, sm_ref, h_ref in streams:
        for bi in range(nb):
            h_ref[bi] = _ml_chunk(d, q_ref[bi], k_ref[bi], v_ref[bi], sm_ref[bi] + gb_ref[d],
                                  s_ref, m_ref, (d * nb + bi) * ML_HEADS)


def _ml_scan(q, k, p, gate_b_p):
    b, t, _ = q.shape
    nc = t // ML_CHUNK
    chunk = functools.partial(_scan_chunk, n_ctx_chunks=TM // ML_CHUNK, n_chunks=nc)
    blk = lambda d, w, cb: pl.BlockSpec((b, ML_CHUNK, w), lambda s: (0, chunk(d, s), cb))
    stream = lambda d: [blk(d, ML_W, 0), blk(d, ML_W, 0), blk(d, ML_W, C_MV // ML_W),
                        blk(d, LANE, C_SMF // LANE + d)]
    shp = jax.ShapeDtypeStruct((b, t, ML_W), F32)
    return pl.pallas_call(
        _ml_scan_kernel,
        grid=(nc,),
        in_specs=stream(0) + stream(1) + [pl.BlockSpec((2, 1, LANE), lambda s: (0, 0, 0))],
        out_specs=[blk(0, ML_W, 0), blk(1, ML_W, 0)],
        out_shape=[shp, shp],
        scratch_shapes=[pltpu.VMEM((2 * b * ML_HEADS, LANE, LANE), F32),
                        pltpu.VMEM((2 * b * ML_HEADS, 8, LANE), F32)],
        compiler_params=_cparams(("arbitrary",)),
        name="mlstm_scan",
    )(q, k, p, p, q, k, p, p, gate_b_p)


def _gla_chunk(d, q, k, v, sm, wa, ba, s_ref, si):
    L = GLA_CHUNK
    row = lax.broadcasted_iota(jnp.int32, (L, L), 0)
    col = lax.broadcasted_iota(jnp.int32, (L, L), 1)
    mask = col <= row if d == 0 else col >= row
    tri = mask.astype(F32)
    lane_k = lax.broadcasted_iota(jnp.int32, (1, GLA_HEADS * GLA_DK), 1)
    lane_v = lax.broadcasted_iota(jnp.int32, (1, GLA_HEADS * GLA_DV), 1)
    eye = (lax.broadcasted_iota(jnp.int32, (LANE, LANE), 0)
           == lax.broadcasted_iota(jnp.int32, (LANE, LANE), 1)).astype(F32)

    pre = jnp.dot(sm, wa, precision=HIGHEST, preferred_element_type=F32) + ba
    loga = _log_sigmoid(pre) * (1.0 / GLA_TAU)
    bc = jnp.dot(tri, loga, precision=HIGHEST, preferred_element_type=F32)
    ref_row = bc[L // 2 - 1:L // 2]
    b_end = bc[L - 1:L] if d == 0 else bc[0:1]

    q = q * (GLA_DK ** -0.5)
    v = v.astype(BF16)
    q_in = (q * jnp.exp(bc - ref_row))
    k_in = (k * jnp.exp(ref_row - bc)).astype(BF16)
    q_st = (q * jnp.exp(bc)).astype(BF16)
    k_st = (k * jnp.exp(b_end - bc)).astype(BF16)

    blockdiag = (lax.broadcasted_iota(jnp.int32, s_ref.shape[1:], 0) // GLA_DK
                 == lax.broadcasted_iota(jnp.int32, s_ref.shape[1:], 1) // GLA_DV)
    s_old = s_ref[si]
    o = jnp.dot(q_st, jnp.where(blockdiag, s_old, 0.0).astype(BF16), preferred_element_type=F32)
    for h in range(GLA_HEADS):
        qh = jnp.where(lane_k // GLA_DK == h, q_in, 0.0).astype(BF16)
        att = jnp.where(mask, _nt_dot(qh, k_in), 0.0).astype(BF16)
        oh = jnp.dot(att, v, preferred_element_type=F32)
        o = o + jnp.where(lane_v // GLA_DV == h, oh, 0.0)

    decay_col = jnp.exp(_nt_dot(eye, jnp.broadcast_to(b_end, (8, LANE)), precision=HIGHEST)[:, 0:1])
    s_ref[si] = decay_col * s_old + _tn_dot(k_st, v)
    return o


def _gla_scan_kernel(qf_ref, kf_ref, vf_ref, smf_ref, qb_ref, kb_ref, vb_ref, smb_ref, wa_ref, ba_ref,
                     of_ref, ob_ref, s_ref):
    @pl.when(pl.program_id(0) == 0)
    def _():
        s_ref[...] = jnp.zeros_like(s_ref)

    nb = qf_ref.shape[0]
    streams = ((0, qf_ref, kf_ref, vf_ref, smf_ref, of_ref), (1, qb_ref, kb_ref, vb_ref, smb_ref, ob_ref))
    for d, q_ref, k_ref, v_ref, sm_ref, o_ref in streams:
        for bi in range(nb):
            o_ref[bi] = _gla_chunk(d, q_ref[bi], k_ref[bi], v_ref[bi], sm_ref[bi], wa_ref[d], ba_ref[d],
                                   s_ref, d * nb + bi)


def _gla_scan(p, wa_p, ba_p):
    b, t, _ = p.shape
    nc = t // GLA_CHUNK
    chunk = functools.partial(_scan_chunk, n_ctx_chunks=TM // GLA_CHUNK, n_chunks=nc)
    kw, vw = GLA_HEADS * GLA_DK, GLA_HEADS * GLA_DV
    blk = lambda d, w, col: pl.BlockSpec((b, GLA_CHUNK, w), lambda s: (0, chunk(d, s), col // w))
    stream = lambda d: [blk(d, kw, C_GQ), blk(d, kw, C_GK), blk(d, vw, C_GV), blk(d, LANE, C_SMF + d * LANE)]
    shp = jax.ShapeDtypeStruct((b, t, vw), F32)
    return pl.pallas_call(
        _gla_scan_kernel,
        grid=(nc,),
        in_specs=stream(0) + stream(1) + [pl.BlockSpec((2, LANE, kw), lambda s: (0, 0, 0)),
                                          pl.BlockSpec((2, 1, kw), lambda s: (0, 0, 0))],
        out_specs=[blk(0, vw, 0), blk(1, vw, 0)],
        out_shape=[shp, shp],
        scratch_shapes=[pltpu.VMEM((2 * b, kw, vw), F32)],
        compiler_params=_cparams(("arbitrary",)),
        name="gla_scan",
    )(p, p, p, p, p, p, p, p, wa_p, ba_p)


def _outproj_kernel(ac_ref, al_ref, mhf_ref, mhb_ref, xc_ref, mo_ref, gof_ref, gob_ref, gr_ref, x_ref, mod_ref,
                    mnw_ref, msk_ref, gnw_ref, wout_ref, n2w_ref, wr_ref, br_ref,
                    xo_ref, h2_ref, route_ref):
    grp64 = (lax.broadcasted_iota(jnp.int32, (ML_W, ML_W), 0) // ML_DH
             == lax.broadcasted_iota(jnp.int32, (ML_W, ML_W), 1) // ML_DH).astype(F32) * (1.0 / ML_DH)

    def head_norm(x, w):
        ms = jnp.dot(x * x, grp64, precision=HIGHEST, preferred_element_type=F32)
        return x * lax.rsqrt(ms + EPS) * w

    m_l = jax.nn.sigmoid(mo_ref[0]) * (head_norm(mhf_ref[0] + mhb_ref[0], mnw_ref[...])
                                       + msk_ref[...] * xc_ref[0])
    g_l = head_norm(gof_ref[0] + gob_ref[0], gnw_ref[...]) * _silu(gr_ref[0])
    na = MLA_HEADS * MLA_V
    a = jnp.where(pl.program_id(1) == 0, ac_ref[0], al_ref[0])
    res = (jnp.dot(a, wout_ref[0:na], preferred_element_type=F32)
           + jnp.dot(m_l.astype(BF16), wout_ref[na:na + ML_W], preferred_element_type=F32)
           + jnp.dot(g_l.astype(BF16), wout_ref[na + ML_W:], preferred_element_type=F32))
    mod = mod_ref[0]
    x = x_ref[0] + mod[2:3] * res
    xo_ref[0] = x
    h2 = (x * lax.rsqrt(jnp.mean(x * x, axis=-1, keepdims=True) + EPS) * n2w_ref[...]
          * (1.0 + mod[4:5]) + mod[3:4])
    h2_ref[0] = h2

    logits = jnp.dot(h2, wr_ref[...], precision=HIGHEST, preferred_element_type=F32) + br_ref[...]
    lane = lax.broadcasted_iota(jnp.int32, (1, LANE), 1)
    lane_f = lane.astype(F32)
    neg = -jnp.inf
    gl = jnp.where(lane < R_EXP, logits, neg)
    gmax = jnp.max(gl, axis=-1, keepdims=True)
    g_w = 1.0 / jnp.sum(jnp.exp(gl - gmax), axis=-1, keepdims=True)
    g_i = jnp.min(jnp.where(gl == gmax, lane_f, float(LANE)), axis=-1, keepdims=True)
    grp_of_lane = ((lane - R_EXP) // EXP_PER_GROUP).astype(F32)
    in_grp = (lane >= R_EXP) & (lane < R_EXP + N_EXPERTS) & (grp_of_lane == g_i)
    el = jnp.where(in_grp, logits, neg)
    m1 = jnp.max(el, axis=-1, keepdims=True)
    i1 = jnp.min(jnp.where(el == m1, lane_f, float(LANE)), axis=-1, keepdims=True)
    el2 = jnp.where(lane_f == i1, neg, el)
    m2 = jnp.max(el2, axis=-1, keepdims=True)
    i2 = jnp.min(jnp.where(el2 == m2, lane_f, float(LANE)), axis=-1, keepdims=True)
    p2 = jnp.exp(m2 - m1)
    w1 = g_w / (1.0 + p2)
    route_ref[0] = jnp.where(lane == RT_E1, i1 - R_EXP, jnp.where(lane == RT_E1 + 1, i2 - R_EXP,
                             jnp.where(lane == RT_W1, w1, jnp.where(lane == RT_W1 + 1, p2 * w1, 0.0))))


def _outproj(a_ctx, a_lat, mh, xconv, p, go, xs, mods, ml_norm_w, ml_skip, gla_norm_w, w_out_b, norm2_w, wr_p, br_p):
    b, t, d = xs.shape
    nt = t // TM
    full = lambda shape: pl.BlockSpec(shape, lambda bi, i: (0,) * len(shape))
    tok = lambda w, cb=0: pl.BlockSpec((1, TM, w), lambda bi, i: (bi, i, cb))
    dirblk = lambda dd: tok(ML_W)
    return pl.pallas_call(
        _outproj_kernel,
        grid=(b, nt),
        in_specs=[
            pl.BlockSpec((1, TM, MLA_HEADS * MLA_V), lambda bi, i: (bi, 0, 0)),
            pl.BlockSpec((1, TM, MLA_HEADS * MLA_V), lambda bi, i: (bi, jnp.maximum(i - 1, 0), 0)),
            dirblk(0), dirblk(1), tok(ML_W), tok(ML_W, C_MO // ML_W),
            dirblk(0), dirblk(1), tok(ML_W, C_GR // ML_W), tok(d),
            pl.BlockSpec((1, 6, d), lambda bi, i: (_mod_row(bi, i), 0, 0)),
            full((1, ML_W)), full((1, ML_W)), full((1, ML_W)), full((d, d)), full((1, d)),
            full((d, LANE)), full((1, LANE)),
        ],
        out_specs=[tok(d), tok(d), tok(LANE)],
        out_shape=[jax.ShapeDtypeStruct((b, t, d), F32), jax.ShapeDtypeStruct((b, t, d), F32),
                   jax.ShapeDtypeStruct((b, t, LANE), F32)],
        compiler_params=_cparams(("parallel", "parallel")),
        name="out_proj_router",
    )(a_ctx, a_lat, mh[0], mh[1], xconv, p, go[0], go[1], p, xs, mods, ml_norm_w.reshape(1, -1), ml_skip.reshape(1, -1),
      gla_norm_w.reshape(1, -1), w_out_b, norm2_w.reshape(1, -1), wr_p, br_p)


def _dispatch(route, n_tiles):
    n = route.shape[0]
    flat = route[:, RT_E1:RT_E1 + 2].astype(jnp.int32).reshape(-1)
    onehot = (flat[:, None] == jnp.arange(N_EXPERTS, dtype=jnp.int32)[None, :]).astype(jnp.int32)
    csum = jnp.cumsum(onehot, axis=0)
    rank = jnp.sum(csum * onehot, axis=1) - 1
    padded = (csum[-1] + TM - 1) // TM * TM
    ends = jnp.cumsum(padded)
    pos = (ends - padded)[flat] + rank
    src = jnp.zeros((n_tiles * TM,), jnp.int32).at[pos].set(jnp.arange(2 * n, dtype=jnp.int32) // 2)
    tile_start = jnp.arange(n_tiles, dtype=jnp.int32) * TM
    tile_exp = jnp.minimum(jnp.sum((ends[None, :] <= tile_start[:, None]).astype(jnp.int32), axis=1),
                           N_EXPERTS - 1)
    tile_on = (tile_start < ends[-1]).astype(jnp.int32)
    pos = pos.reshape(n // TM, TM, 2)
    return (src.reshape(n_tiles, 1, TM), tile_exp, tile_on,
            pos[:, :, 0].reshape(n // TM, 1, TM), pos[:, :, 1].reshape(n // TM, 1, TM))


def _gather_rows(src_hbm, idx_ref, dst, sem):
    def body(j, carry):
        pltpu.make_async_copy(src_hbm.at[pl.ds(idx_ref[0, 0, j], 1)], dst.at[pl.ds(j, 1)], sem).start()
        return carry

    lax.fori_loop(0, TM, body, 0, unroll=8)


def _wait_rows(src_hbm, dst, sem):
    pltpu.make_async_copy(src_hbm.at[pl.ds(0, TM)], dst, sem).wait()


def _experts_kernel(texp_ref, ton_ref, idx_ref, idxn_ref, h2_hbm, wg_ref, wu_ref, wd_ref, y_ref,
                    buf, wgu_b, wd_b, sem):
    r = pl.program_id(0)
    slot = r % 2

    @pl.when(r == 0)
    def _():
        _gather_rows(h2_hbm, idx_ref, buf.at[0], sem.at[0])

    last = pl.num_programs(0) - 1

    @pl.when(jnp.logical_and(r < last, ton_ref[jnp.minimum(r + 1, last)] == 1))
    def _():
        _gather_rows(h2_hbm, idxn_ref, buf.at[1 - slot], sem.at[1 - slot])

    @pl.when(ton_ref[r] == 1)
    def _():
        @pl.when(jnp.logical_or(r == 0, texp_ref[r] != texp_ref[jnp.maximum(r - 1, 0)]))
        def _():
            wgu_b[:, :D_EXPERT] = wg_ref[0, 0].astype(BF16)
            wgu_b[:, D_EXPERT:] = wu_ref[0, 0].astype(BF16)
            wd_b[...] = wd_ref[0, 0].astype(BF16)

        _wait_rows(h2_hbm, buf.at[slot], sem.at[slot])
        gu = jnp.dot(buf[slot].astype(BF16), wgu_b[...], preferred_element_type=F32)
        act = (_silu(gu[:, :D_EXPERT]) * gu[:, D_EXPERT:]).astype(BF16)
        y_ref[...] = jnp.dot(act, wd_b[...], preferred_element_type=F32)

    @pl.when(ton_ref[r] == 0)
    def _():
        y_ref[...] = jnp.zeros_like(y_ref)


def _experts(h2, src, tile_exp, tile_on, w_gate, w_up, w_down, layer):
    n, d = h2.shape
    n_tiles = src.shape[0]
    idx = lambda nxt: pl.BlockSpec((1, 1, TM), lambda r, te, to: (jnp.minimum(r + nxt, n_tiles - 1), 0, 0),
                                   memory_space=pltpu.SMEM)
    wspec = lambda shape: pl.BlockSpec((1, 1) + shape, lambda r, te, to: (layer, te[r], 0, 0))
    return pl.pallas_call(
        _experts_kernel,
        grid_spec=pltpu.PrefetchScalarGridSpec(
            num_scalar_prefetch=2,
            grid=(n_tiles,),
            in_specs=[
                idx(0), idx(1),
                pl.BlockSpec(memory_space=pl.ANY),
                wspec((d, D_EXPERT)), wspec((d, D_EXPERT)), wspec((D_EXPERT, d)),
            ],
            out_specs=pl.BlockSpec((TM, d), lambda r, te, to: (r, 0)),
            scratch_shapes=[pltpu.VMEM((2, TM, d), F32), pltpu.VMEM((d, 2 * D_EXPERT), BF16),
                            pltpu.VMEM((D_EXPERT, d), BF16), pltpu.SemaphoreType.DMA((2,))],
        ),
        out_shape=jax.ShapeDtypeStruct((n_tiles * TM, d), F32),
        compiler_params=_cparams(("arbitrary",)),
        name="moe_experts",
    )(tile_exp, tile_on, src, src, h2, w_gate, w_up, w_down)


def _combine_kernel(p1_ref, p2_ref, p1n_ref, p2n_ref, route_ref, x_ref, mod_ref, y_hbm, o_ref, buf, sem):
    g = pl.program_id(0)
    slot = g % 2

    def gather(pa, pb, s):
        _gather_rows(y_hbm, pa, buf.at[s, 0], sem.at[s])
        _gather_rows(y_hbm, pb, buf.at[s, 1], sem.at[s])

    @pl.when(g == 0)
    def _():
        gather(p1_ref, p2_ref, 0)

    @pl.when(g + 1 < pl.num_programs(0))
    def _():
        gather(p1n_ref, p2n_ref, 1 - slot)

    _wait_rows(y_hbm, buf.at[slot, 0], sem.at[slot])
    _wait_rows(y_hbm, buf.at[slot, 1], sem.at[slot])
    route = route_ref[...]
    moe = route[:, RT_W1:RT_W1 + 1] * buf[slot, 0] + route[:, RT_W1 + 1:RT_W1 + 2] * buf[slot, 1]
    o_ref[...] = x_ref[...] + mod_ref[0][5:6] * moe


def _combine(y, pos1, pos2, route, xs2, mods, nt):
    n, d = xs2.shape
    n_tok_tiles = n // TM
    idx = lambda nxt: pl.BlockSpec((1, 1, TM), lambda g: (jnp.minimum(g + nxt, n_tok_tiles - 1), 0, 0),
                                   memory_space=pltpu.SMEM)
    tok = lambda w: pl.BlockSpec((TM, w), lambda g: (g, 0))
    return pl.pallas_call(
        _combine_kernel,
        grid=(n_tok_tiles,),
        in_specs=[
            idx(0), idx(0), idx(1), idx(1), tok(LANE), tok(d),
            pl.BlockSpec((1, 6, d), lambda g: (_mod_row(g // nt, g % nt), 0, 0)),
            pl.BlockSpec(memory_space=pl.ANY),
        ],
        out_specs=tok(d),
        out_shape=jax.ShapeDtypeStruct((n, d), F32),
        scratch_shapes=[pltpu.VMEM((2, 2, TM, d), F32), pltpu.SemaphoreType.DMA((2,))],
        compiler_params=_cparams(("arbitrary",)),
        name="moe_combine",
    )(pos1, pos2, pos1, pos2, route, xs2, mods, y)


def _moe(h2, route, xs, mods, w_gate, w_up, w_down, layer):
    b, t, d = xs.shape
    n = b * t
    n_tiles = 2 * n // TM + N_EXPERTS
    route2 = route.reshape(n, LANE)
    src, tile_exp, tile_on, pos1, pos2 = _dispatch(route2, n_tiles)
    y = _experts(h2.reshape(n, d), src, tile_exp, tile_on, w_gate, w_up, w_down, layer)
    return _combine(y, pos1, pos2, route2, xs.reshape(n, d), mods, t // TM).reshape(b, t, d)


def _rope_tables(n_ctx, n_lat):
    rows = n_lat // GRID_W
    row = jnp.broadcast_to(jnp.arange(rows, dtype=F32)[:, None], (rows, GRID_W)).reshape(-1)
    col = jnp.broadcast_to(jnp.arange(GRID_W, dtype=F32)[None, :], (rows, GRID_W)).reshape(-1)
    n_freq = MLA_ROPE // 4
    inv = ROPE_THETA ** (-jnp.arange(n_freq, dtype=F32) / n_freq)
    ang = jnp.concatenate([row[:, None] * inv, col[:, None] * inv], axis=-1)
    cos, sin = jnp.cos(ang), jnp.sin(ang)
    half = MLA_ROPE // 2
    z = lambda w: jnp.zeros((n_lat, w), F32)
    o = lambda w: jnp.ones((n_lat, w), F32)
    tail = LANE - MLA_QK
    cos_t = jnp.concatenate([o(MLA_NOPE), cos, cos, o(tail)], axis=-1)
    sa_t = jnp.concatenate([z(MLA_NOPE + half), sin, z(tail)], axis=-1)
    sb_t = jnp.concatenate([z(MLA_NOPE), -sin, z(half + tail)], axis=-1)
    ctx1 = jnp.ones((n_ctx, LANE), F32)
    ctx0 = jnp.zeros((n_ctx, LANE), F32)
    return (jnp.concatenate([ctx1, cos_t], 0), jnp.concatenate([ctx0, sa_t], 0),
            jnp.concatenate([ctx0, sb_t], 0))


def _pad_cols(a, width):
    return jnp.pad(a, [(0, 0)] * (a.ndim - 1) + [(0, width - a.shape[-1])])


def _layer_weights(w_in, w_uq, w_ukv, q_norm_w, k_norm_w, ml_conv_w, ml_wq, ml_wk, ml_gate_b,
                   gla_wa, gla_ba, w_out, w_grp, b_grp, w_erouter, b_erouter):
    d = w_in.shape[0]
    o = np.cumsum((0, Q_LORA, KV_LORA, MLA_ROPE, ML_W, ML_W, ML_W, 4 * ML_HEADS, GLA_HEADS * GLA_DK,
                   GLA_HEADS * GLA_DK, GLA_HEADS * GLA_DV, GLA_HEADS * GLA_DV, 2 * GLA_LR))
    seg = lambda j: w_in[:, o[j]:o[j + 1]]
    cq, ckv, kr, mx, mv, mo, mg, gq, gk, gv, gr, ga = (seg(j) for j in range(12))
    z = lambda w: jnp.zeros((d, w), F32)

    def small(di):
        return jnp.concatenate([mg[:, di * 8:(di + 1) * 8], ga[:, di * GLA_LR:(di + 1) * GLA_LR],
                                z(SM_KR - SM_GA - GLA_LR), kr, z(LANE - SM_KR - MLA_ROPE)], axis=-1)

    w_in_p = jnp.concatenate([cq, mx, mv, mo, gv, gr, ckv, gq, gk, small(0), small(1)], axis=-1).astype(BF16)

    wuq_p = _pad_cols(w_uq.reshape(Q_LORA, MLA_HEADS, MLA_QK), LANE).reshape(Q_LORA, -1).astype(BF16)
    ukv = w_ukv.reshape(KV_LORA, MLA_HEADS, MLA_NOPE + MLA_V)
    wuk_p = _pad_cols(ukv[..., :MLA_NOPE], LANE).reshape(KV_LORA, -1).astype(BF16)
    wuv_p = _pad_cols(ukv[..., MLA_NOPE:], LANE).reshape(KV_LORA, -1).astype(BF16)
    qn_p = _pad_cols(q_norm_w.reshape(1, -1), LANE)
    kn_p = _pad_cols(k_norm_w.reshape(1, -1), LANE)

    conv_w8 = jnp.pad(ml_conv_w, ((0, 8 - ML_CONV), (0, 0)))
    bd = lambda w: jax.scipy.linalg.block_diag(*[w[h] for h in range(ML_HEADS)])
    wq_bd = (bd(ml_wq) * (ML_DH ** -0.5)).astype(BF16)
    wk_bd = bd(ml_wk).astype(BF16)
    gate_b_p = _pad_cols(ml_gate_b.reshape(2, 1, 2 * ML_HEADS), LANE)

    wa_p = jnp.pad(gla_wa, ((0, 0), (SM_GA, LANE - SM_GA - GLA_LR), (0, 0)))
    ba_p = gla_ba.reshape(2, 1, -1)

    wr_p = _pad_cols(jnp.concatenate([w_grp, w_erouter], axis=-1), LANE)
    br_p = _pad_cols(jnp.concatenate([b_grp, b_erouter]).reshape(1, -1), LANE)
    return dict(w_in_p=w_in_p, wuq_p=wuq_p, wuk_p=wuk_p, wuv_p=wuv_p, qn_p=qn_p, kn_p=kn_p,
                conv_w8=conv_w8, wq_bd=wq_bd, wk_bd=wk_bd, gate_b_p=gate_b_p, wa_p=wa_p, ba_p=ba_p,
                w_out_b=w_out.astype(BF16), wr_p=wr_p, br_p=br_p)


def kernel(x, c, ctx, c_ctx, w_mod, b_mod, norm1_w, w_in, q_a_norm, w_uq, kv_a_norm, w_ukv,
           q_norm_w, k_norm_w, ml_conv_w, ml_conv_b, ml_wq, ml_wk, ml_gate_b, ml_norm_w, ml_skip,
           gla_wa, gla_ba, gla_norm_w, w_out, norm2_w, w_grp, b_grp, w_erouter, b_erouter,
           w_gate, w_up, w_down):
    b, s, d = x.shape
    n_ctx = ctx.shape[1]
    depth = w_mod.shape[0]
    assert n_ctx == TM and s % TM == 0 and b == 2

    cc = jnp.concatenate([c, c_ctx[None, :], jnp.zeros((8 - b - 1, d), F32)], axis=0)
    mods_all = _mods(cc, w_mod, b_mod).reshape(depth, 8, 6, d)
    cos_t, sa_t, sb_t = _rope_tables(n_ctx, s)
    xs = jnp.concatenate([ctx, x], axis=1)

    for l in range(depth):
        w = _layer_weights(w_in[l], w_uq[l], w_ukv[l], q_norm_w[l], k_norm_w[l], ml_conv_w[l],
                           ml_wq[l], ml_wk[l], ml_gate_b[l], gla_wa[l], gla_ba[l], w_out[l],
                           w_grp[l], b_grp[l], w_erouter[l], b_erouter[l])
        mods = mods_all[l]
        p = _inproj(xs, mods, norm1_w[l], w["w_in_p"])
        q, k, v = _mla_prep(p, q_a_norm[l], w["wuq_p"], kv_a_norm[l], w["wuk_p"], w["wuv_p"],
                            w["qn_p"], w["kn_p"], cos_t, sa_t, sb_t)
        a_lat = _attention_lat(q, k, v)
        a_ctx = _attention_ctx(q, k, v) if l < depth - 1 else jnp.zeros((b, TM, MLA_HEADS * MLA_V), BF16)
        xconv, mq, mk = _ml_prep(p, w["conv_w8"], ml_conv_b[l], w["wq_bd"], w["wk_bd"])
        mh = _ml_scan(mq, mk, p, w["gate_b_p"])
        go = _gla_scan(p, w["wa_p"], w["ba_p"])
        xs, h2, route = _outproj(a_ctx, a_lat, mh, xconv, p, go, xs, mods, ml_norm_w[l], ml_skip[l], gla_norm_w[l],
                                w["w_out_b"], norm2_w[l], w["wr_p"], w["br_p"])
        xs = _moe(h2, route, xs, mods, w_gate, w_up, w_down, l)
    return xs[:, n_ctx:, :]
```

```python
import functools

import jax
import jax.numpy as jnp
import numpy as np
from jax import lax
from jax.experimental import pallas as pl
from jax.experimental.pallas import tpu as pltpu

F32 = jnp.float32
BF16 = jnp.bfloat16
HIGHEST = lax.Precision.HIGHEST

EPS = 1e-6
GRID_W = 64
ROPE_THETA = 10000.0

MLA_HEADS = 8
MLA_NOPE = 64
MLA_ROPE = 32
MLA_QK = MLA_NOPE + MLA_ROPE
MLA_V = 64
Q_LORA = 256
KV_LORA = 128

ML_HEADS = 4
ML_DH = 64
ML_W = ML_HEADS * ML_DH
ML_CONV = 5

GLA_HEADS = 4
GLA_DK = 32
GLA_DV = 64
GLA_LR = 16
GLA_TAU = 16.0

N_GROUPS = 4
EXP_PER_GROUP = 8
N_EXPERTS = N_GROUPS * EXP_PER_GROUP
D_EXPERT = 256

LANE = 128
TM = 256
ML_CHUNK = 256
GLA_CHUNK = 128
VMEM_LIMIT = 56 * 1024 * 1024

C_CQ, C_MX, C_MV, C_MO, C_GV, C_GR = 0, 256, 512, 768, 1024, 1280
C_CKV, C_GQ, C_GK, C_SMF, C_SMB = 1536, 1664, 1792, 1920, 2048
D_INP = 2176
SM_GATE = 0
SM_GA = 8
SM_KR = 64
R_GRP = 0
R_EXP = 4
RT_E1 = 0
RT_W1 = 2


def _cparams(sem):
    return pltpu.CompilerParams(dimension_semantics=sem, vmem_limit_bytes=VMEM_LIMIT)


def _silu(x):
    return x * jax.nn.sigmoid(x)


def _log_sigmoid(x):
    return -(jnp.maximum(-x, 0.0) + jnp.log1p(jnp.exp(-jnp.abs(x))))


def _nt_dot(a, b, **kw):
    return lax.dot_general(a, b, (((1,), (1,)), ((), ())), preferred_element_type=F32, **kw)


def _tn_dot(a, b, **kw):
    return lax.dot_general(a, b, (((0,), (0,)), ((), ())), preferred_element_type=F32, **kw)


def _mods_kernel(cc_ref, w_ref, b_ref, o_ref):
    a = _silu(cc_ref[...])
    o_ref[0] = jnp.dot(a, w_ref[0], precision=HIGHEST, preferred_element_type=F32) + b_ref[0]


def _mods(cc, w_mod, b_mod):
    depth, d, d6 = w_mod.shape
    nb = 1536
    return pl.pallas_call(
        _mods_kernel,
        grid=(depth, d6 // nb),
        in_specs=[
            pl.BlockSpec((8, d), lambda l, j: (0, 0)),
            pl.BlockSpec((1, d, nb), lambda l, j: (l, 0, j)),
            pl.BlockSpec((1, 1, nb), lambda l, j: (l, 0, j)),
        ],
        out_specs=pl.BlockSpec((1, 8, nb), lambda l, j: (l, 0, j)),
        out_shape=jax.ShapeDtypeStruct((depth, 8, d6), F32),
        compiler_params=_cparams(("arbitrary", "arbitrary")),
        name="adaln_mods",
    )(cc, w_mod, b_mod.reshape(depth, 1, d6))


def _mod_row(b, i):
    return jnp.where(i == 0, 2, b)


def _inproj_kernel(x_ref, mod_ref, nw_ref, w_ref, o_ref):
    x = x_ref[0]
    y = x * lax.rsqrt(jnp.mean(x * x, axis=-1, keepdims=True) + EPS) * nw_ref[...]
    mod = mod_ref[0]
    h = y * (1.0 + mod[1:2]) + mod[0:1]
    o_ref[0] = jnp.dot(h.astype(BF16), w_ref[...], preferred_element_type=F32)


def _inproj(xs, mods, norm_w, w_in_p):
    b, t, d = xs.shape
    nt = t // TM
    return pl.pallas_call(
        _inproj_kernel,
        grid=(b, nt),
        in_specs=[
            pl.BlockSpec((1, TM, d), lambda bi, i: (bi, i, 0)),
            pl.BlockSpec((1, 6, d), lambda bi, i: (_mod_row(bi, i), 0, 0)),
            pl.BlockSpec((1, d), lambda bi, i: (0, 0)),
            pl.BlockSpec((d, D_INP), lambda bi, i: (0, 0)),
        ],
        out_specs=pl.BlockSpec((1, TM, D_INP), lambda bi, i: (bi, i, 0)),
        out_shape=jax.ShapeDtypeStruct((b, t, D_INP), F32),
        compiler_params=_cparams(("parallel", "parallel")),
        name="in_proj",
    )(xs, mods, norm_w.reshape(1, d), w_in_p)


def _mla_prep_kernel(cq_ref, ckv_ref, sm_ref, qan_ref, wuq_ref, kvan_ref, wuk_ref, wuv_ref,
                     qn_ref, kn_ref, cos_ref, sa_ref, sb_ref, q_ref, k_ref, v_ref):
    cq = cq_ref[0]
    cqn = cq * lax.rsqrt(jnp.mean(cq * cq, axis=-1, keepdims=True) + EPS) * qan_ref[...]
    qall = jnp.dot(cqn.astype(BF16), wuq_ref[...], preferred_element_type=F32)
    ckv = ckv_ref[0]
    ckvn = (ckv * lax.rsqrt(jnp.mean(ckv * ckv, axis=-1, keepdims=True) + EPS)
            * kvan_ref[...]).astype(BF16)
    kall = jnp.dot(ckvn, wuk_ref[...], preferred_element_type=F32)
    vall = jnp.dot(ckvn, wuv_ref[...], preferred_element_type=F32)
    lane = lax.broadcasted_iota(jnp.int32, (1, LANE), 1)
    kr = jnp.where((lane >= SM_KR) & (lane < SM_KR + MLA_ROPE), sm_ref[0], 0.0)
    cos, sa, sb = cos_ref[...], sa_ref[...], sb_ref[...]

    def rope(x):
        return x * cos + pltpu.roll(x, 16, 1) * sa + pltpu.roll(x, LANE - 16, 1) * sb

    def head_norm(x, w):
        return x * lax.rsqrt(jnp.sum(x * x, axis=-1, keepdims=True) * (1.0 / MLA_QK) + EPS) * w

    for h in range(MLA_HEADS):
        sl = slice(h * LANE, (h + 1) * LANE)
        qh = rope(head_norm(qall[:, sl], qn_ref[...]))
        q_ref[0, h] = (qh * Q_SCALE).astype(BF16)
        kh = rope(head_norm(kall[:, sl] + kr, kn_ref[...]))
        k_ref[0, h] = jnp.where(lane == ATT_SHIFT_LANE, 1.0, kh).astype(BF16)
        v_ref[0, h] = jnp.where(lane == MLA_V, 1.0, vall[:, sl]).astype(BF16)


def _mla_prep(p, q_a_norm, wuq_p, kv_a_norm, wuk_p, wuv_p, qn_p, kn_p, cos_t, sa_t, sb_t):
    b, t, _ = p.shape
    nt = t // TM
    hw = MLA_HEADS * LANE
    full = lambda shape: pl.BlockSpec(shape, lambda bi, i: (0,) * len(shape))
    tab = pl.BlockSpec((TM, LANE), lambda bi, i: (i, 0))
    out = pl.BlockSpec((1, MLA_HEADS, TM, LANE), lambda bi, i: (bi, 0, i, 0))
    shp = jax.ShapeDtypeStruct((b, MLA_HEADS, t, LANE), BF16)
    return pl.pallas_call(
        _mla_prep_kernel,
        grid=(b, nt),
        in_specs=[
            pl.BlockSpec((1, TM, Q_LORA), lambda bi, i: (bi, i, C_CQ // Q_LORA)),
            pl.BlockSpec((1, TM, KV_LORA), lambda bi, i: (bi, i, C_CKV // KV_LORA)),
            pl.BlockSpec((1, TM, LANE), lambda bi, i: (bi, i, C_SMF // LANE)),
            full((1, Q_LORA)), full((Q_LORA, hw)), full((1, KV_LORA)),
            full((KV_LORA, hw)), full((KV_LORA, hw)), full((1, LANE)), full((1, LANE)),
            tab, tab, tab,
        ],
        out_specs=[out, out, out],
        out_shape=[shp, shp, shp],
        compiler_params=_cparams(("parallel", "parallel")),
        name="mla_prep",
    )(p, p, p, q_a_norm.reshape(1, -1), wuq_p, kv_a_norm.reshape(1, -1), wuk_p, wuv_p,
      qn_p, kn_p, cos_t, sa_t, sb_t)


Q_SCALE = float(MLA_QK ** -0.5 * np.log2(np.e))
ATT_HP = 2
ATT_NQ = 4
ATT_TK = 2048
ATT_SHIFT_LANE = MLA_QK
ATT_SAFE_MAX = 2.0 ** 100


def _softmax_step(q, kb, vb, m, acc):
    s = _nt_dot(q, kb)
    m_new = jnp.max(s, axis=-1, keepdims=True)
    if m is None:
        return m_new, jnp.dot(jnp.exp2((s - m_new).astype(BF16)), vb, preferred_element_type=F32)
    m_new = jnp.maximum(m, m_new)
    p = jnp.exp2((s - m_new).astype(BF16))
    return m_new, acc * jnp.exp2(m - m_new) + jnp.dot(p, vb, preferred_element_type=F32)


def _attn_ctx_kernel(q_ref, k_ref, v_ref, o_ref):
    outs = []
    for h in range(MLA_HEADS):
        _, acc = _softmax_step(q_ref[0, h], k_ref[0, h], v_ref[0, h], None, None)
        outs.append(acc[:, :MLA_V] / acc[:, MLA_V:MLA_V + 1])
    o_ref[0] = jnp.concatenate(outs, axis=-1).astype(BF16)


def _attention_ctx(q, k, v):
    b, h, _, _ = q.shape
    blk = pl.BlockSpec((1, h, TM, LANE), lambda bi: (bi, 0, 0, 0))
    return pl.pallas_call(
        _attn_ctx_kernel,
        grid=(b,),
        in_specs=[blk, blk, blk],
        out_specs=pl.BlockSpec((1, TM, h * MLA_V), lambda bi: (bi, 0, 0)),
        out_shape=jax.ShapeDtypeStruct((b, TM, h * MLA_V), BF16),
        compiler_params=_cparams(("parallel",)),
        name="mla_attention_ctx",
    )(q, k, v)


def _attn_lat_kernel(*refs, n_blk):
    q_refs, (k_ref, v_ref, o_ref, q_buf, qs_buf) = refs[:ATT_NQ], refs[ATT_NQ:]
    lane = lax.broadcasted_iota(jnp.int32, (1, LANE), 1)

    def kv_block(hh, j):
        off = pl.multiple_of(TM + j * ATT_TK, TM)
        return k_ref[0, hh, pl.ds(off, ATT_TK), :], v_ref[0, hh, pl.ds(off, ATT_TK), :]

    def finish(accs):
        outs = [acc[:, :MLA_V] / acc[:, MLA_V:MLA_V + 1] for acc in accs]
        o_ref[0] = jnp.concatenate(outs, axis=-1).astype(BF16)

    accs = []
    for hh in range(ATT_HP):
        q = jnp.concatenate([qr[0, hh] for qr in q_refs], axis=0)
        q_buf[hh] = q
        kb, vb = k_ref[0, hh, 0:TM, :], v_ref[0, hh, 0:TM, :]
        shift = jnp.max(_nt_dot(q, kb), axis=-1, keepdims=True).astype(BF16)
        qs_buf[hh] = jnp.where(lane == ATT_SHIFT_LANE, -shift, q)
        accs.append(jnp.dot(jnp.exp2(_nt_dot(qs_buf[hh], kb).astype(BF16)), vb, preferred_element_type=F32))

    def fast_body(j, accs):
        new = []
        for hh in range(ATT_HP):
            kb, vb = kv_block(hh, j)
            p = jnp.exp2(_nt_dot(qs_buf[hh], kb).astype(BF16))
            new.append(accs[hh] + jnp.dot(p, vb, preferred_element_type=F32))
        return tuple(new)

    accs = lax.fori_loop(0, n_blk, fast_body, tuple(accs))
    bad = sum(jnp.max(jnp.where(jnp.abs(acc) < ATT_SAFE_MAX, 0.0, 1.0)) for acc in accs)

    @pl.when(bad == 0.0)
    def _():
        finish(accs)

    @pl.when(bad != 0.0)
    def _():
        init = []
        for hh in range(ATT_HP):
            init += _softmax_step(q_buf[hh], k_ref[0, hh, 0:TM, :], v_ref[0, hh, 0:TM, :], None, None)

        def body(j, carry):
            new = []
            for hh in range(ATT_HP):
                new += _softmax_step(q_buf[hh], *kv_block(hh, j), carry[2 * hh], carry[2 * hh + 1])
            return tuple(new)

        carry = lax.fori_loop(0, n_blk, body, tuple(init))
        finish(carry[1::2])


def _attention_lat(q, k, v):
    b, h, t, _ = q.shape
    s = t - TM
    tq = ATT_NQ * TM
    assert s % ATT_TK == 0 and s % tq == 0
    kv = pl.BlockSpec((1, ATT_HP, t, LANE), lambda bi, hp, i: (bi, hp, 0, 0))
    qs = [pl.BlockSpec((1, ATT_HP, TM, LANE), lambda bi, hp, i, u=u: (bi, hp, 1 + ATT_NQ * i + u, 0))
          for u in range(ATT_NQ)]
    return pl.pallas_call(
        functools.partial(_attn_lat_kernel, n_blk=s // ATT_TK),
        grid=(b, h // ATT_HP, s // tq),
        in_specs=qs + [kv, kv],
        out_specs=pl.BlockSpec((1, tq, ATT_HP * MLA_V), lambda bi, hp, i: (bi, i, hp)),
        out_shape=jax.ShapeDtypeStruct((b, s, h * MLA_V), BF16),
        scratch_shapes=[pltpu.VMEM((ATT_HP, tq, LANE), BF16), pltpu.VMEM((ATT_HP, tq, LANE), BF16)],
        compiler_params=_cparams(("parallel", "parallel", "arbitrary")),
        name="mla_attention",
    )(*([q] * ATT_NQ), k, v)


def _ml_prep_kernel(x_ref, prev_ref, next_ref, cw_ref, cb_ref, wq_ref, wk_ref,
                    xc_ref, q_ref, k_ref, *, n_tiles):
    i = pl.program_id(1)
    x = x_ref[0]
    prev = jnp.where(i <= 1, 0.0, prev_ref[0])
    nxt = jnp.where((i == 0) | (i == n_tiles - 1), 0.0, next_ref[0])
    ext = jnp.concatenate([prev, x, nxt], axis=0)
    n_ext = TM + 16
    cw = cw_ref[...]
    acc = jnp.zeros((TM, ML_W), F32) + cb_ref[...]
    for kk in range(ML_CONV):
        sh = (ML_CONV // 2 - kk) % n_ext
        shifted = ext if sh == 0 else pltpu.roll(ext, sh, 0)
        acc = acc + cw[kk:kk + 1] * shifted[8:8 + TM]
    xc = _silu(acc)
    xc_ref[0] = xc
    xb = xc.astype(BF16)
    q_ref[0] = jnp.dot(xb, wq_ref[...], preferred_element_type=F32).astype(BF16)
    k_ref[0] = jnp.dot(xb, wk_ref[...], preferred_element_type=F32).astype(BF16)


def _ml_prep(p, conv_w8, conv_b, wq_bd, wk_bd):
    b, t, _ = p.shape
    nt = t // TM
    r8 = TM // 8
    full = lambda shape: pl.BlockSpec(shape, lambda bi, i: (0,) * len(shape))
    cb = C_MX // ML_W
    blk = pl.BlockSpec((1, TM, ML_W), lambda bi, i: (bi, i, 0))
    return pl.pallas_call(
        functools.partial(_ml_prep_kernel, n_tiles=nt),
        grid=(b, nt),
        in_specs=[
            pl.BlockSpec((1, TM, ML_W), lambda bi, i: (bi, i, cb)),
            pl.BlockSpec((1, 8, ML_W), lambda bi, i: (bi, jnp.maximum(i * r8 - 1, 0), cb)),
            pl.BlockSpec((1, 8, ML_W), lambda bi, i: (bi, jnp.minimum((i + 1) * r8, nt * r8 - 1), cb)),
            full((8, ML_W)), full((1, ML_W)), full((ML_W, ML_W)), full((ML_W, ML_W)),
        ],
        out_specs=[blk, blk, blk],
        out_shape=[jax.ShapeDtypeStruct((b, t, ML_W), F32),
                   jax.ShapeDtypeStruct((b, t, ML_W), BF16),
                   jax.ShapeDtypeStruct((b, t, ML_W), BF16)],
        compiler_params=_cparams(("parallel", "parallel")),
        name="mlstm_prep",
    )(p, p, p, conv_w8, conv_b.reshape(1, ML_W), wq_bd, wk_bd)


def _scan_chunk(d, step, n_ctx_chunks, n_chunks):
    bwd = jnp.where(step < n_ctx_chunks, n_ctx_chunks - 1 - step, n_chunks - 1 - (step - n_ctx_chunks))
    return jnp.where(d == 0, step, bwd)


def _ml_chunk(d, q, k, v, g, s_ref, m_ref, base):
    L = ML_CHUNK
    row = lax.broadcasted_iota(jnp.int32, (L, L), 0)
    col = lax.broadcasted_iota(jnp.int32, (L, L), 1)
    mask = col <= row if d == 0 else col >= row
    tri = mask.astype(F32)
    lane = lax.broadcasted_iota(jnp.int32, (1, LANE), 1)
    eye8 = (lax.broadcasted_iota(jnp.int32, (8, LANE), 0)
            == lax.broadcasted_iota(jnp.int32, (8, LANE), 1)).astype(F32)

    lf = _log_sigmoid(g)
    bc = jnp.dot(tri, lf, precision=HIGHEST, preferred_element_type=F32)
    g_rows = _nt_dot(eye8, g, precision=HIGHEST)
    bc_rows = _nt_dot(eye8, bc, precision=HIGHEST)

    outs = []
    for pair in range(ML_HEADS // 2):
        sl = slice(pair * LANE, (pair + 1) * LANE)
        q_blk, k_blk, v_blk = q[:, sl], k[:, sl], v[:, sl]
        pair_out = []
        for sub in range(2):
            h = pair * 2 + sub
            head_lanes = (lane >= sub * ML_DH) & (lane < (sub + 1) * ML_DH)
            qh = jnp.where(head_lanes, q_blk, jnp.zeros_like(q_blk))
            vs = v_blk if sub == 0 else pltpu.roll(v_blk, ML_DH, 1)
            v_ext = jnp.where(lane < ML_DH, vs, jnp.where(lane == ML_DH, 1.0, 0.0)).astype(BF16)

            li_c = g[:, SM_GATE + h:SM_GATE + h + 1]
            bc_c = bc[:, SM_GATE + 4 + h:SM_GATE + 5 + h]
            li_r = g_rows[h:h + 1, :]
            bc_r = bc_rows[4 + h:5 + h, :]
            m_st = m_ref[base + h][0:1, 0:1]

            dmat = jnp.where(mask, bc_c + (li_r - bc_r), -jnp.inf)
            inter = bc_c + m_st
            m_t = jnp.maximum(inter, jnp.max(dmat, axis=-1, keepdims=True))
            e = jnp.exp(dmat - m_t)
            s = (_nt_dot(qh, k_blk) * e).astype(BF16)
            tot = (jnp.dot(s, v_ext, preferred_element_type=F32)
                   + jnp.exp(inter - m_t) * jnp.dot(qh, s_ref[base + h].astype(BF16),
                                                    preferred_element_type=F32))
            den = tot[:, ML_DH:ML_DH + 1]
            pair_out.append(tot / jnp.maximum(jnp.abs(den), jnp.exp(-m_t)))

            b_end = bc_c[L - 1:L] if d == 0 else bc_c[0:1]
            g_col = b_end - bc_c + li_c
            m_new = jnp.maximum(b_end + m_st, jnp.max(g_col, axis=0, keepdims=True))
            kw = jnp.where(head_lanes, k_blk.astype(F32) * jnp.exp(g_col - m_new), 0.0).astype(BF16)
            s_ref[base + h] = jnp.exp(b_end + m_st - m_new) * s_ref[base + h] + _tn_dot(kw, v_ext)
            m_ref[base + h] = jnp.broadcast_to(m_new, m_ref.shape[1:])
        outs.append(jnp.where(lane < ML_DH, pair_out[0], pltpu.roll(pair_out[1], ML_DH, 1)))
    return jnp.concatenate(outs, axis=-1)


def _ml_scan_kernel(qf_ref, kf_ref, vf_ref, smf_ref, qb_ref, kb_ref, vb_ref, smb_ref, gb_ref,
                    hf_ref, hb_ref, s_ref, m_ref):
    @pl.when(pl.program_id(0) == 0)
    def _():
        s_ref[...] = jnp.zeros_like(s_ref)
        m_ref[...] = jnp.zeros_like(m_ref)

    nb = qf_ref.shape[0]
    streams = ((0, qf_ref, kf_ref, vf_ref, smf_ref, hf_ref), (1, qb_ref, kb_ref, vb_ref, smb_ref, hb_ref))
    for d, q_ref, k_ref, v_ref, sm_ref, h_ref in streams:
        for bi in range(nb):
            h_ref[bi] = _ml_chunk(d, q_ref[bi], k_ref[bi], v_ref[bi], sm_ref[bi] + gb_ref[d],
                                  s_ref, m_ref, (d * nb + bi) * ML_HEADS)


def _ml_scan(q, k, p, gate_b_p):
    b, t, _ = q.shape
    nc = t // ML_CHUNK
    chunk = functools.partial(_scan_chunk, n_ctx_chunks=TM // ML_CHUNK, n_chunks=nc)
    blk = lambda d, w, cb: pl.BlockSpec((b, ML_CHUNK, w), lambda s: (0, chunk(d, s), cb))
    stream = lambda d: [blk(d, ML_W, 0), blk(d, ML_W, 0), blk(d, ML_W, C_MV // ML_W),
                        blk(d, LANE, C_SMF // LANE + d)]
    shp = jax.ShapeDtypeStruct((b, t, ML_W), F32)
    return pl.pallas_call(
        _ml_scan_kernel,
        grid=(nc,),
        in_specs=stream(0) + stream(1) + [pl.BlockSpec((2, 1, LANE), lambda s: (0, 0, 0))],
        out_specs=[blk(0, ML_W, 0), blk(1, ML_W, 0)],
        out_shape=[shp, shp],
        scratch_shapes=[pltpu.VMEM((2 * b * ML_HEADS, LANE, LANE), F32),
                        pltpu.VMEM((2 * b * ML_HEADS, 8, LANE), F32)],
        compiler_params=_cparams(("arbitrary",)),
        name="mlstm_scan",
    )(q, k, p, p, q, k, p, p, gate_b_p)


def _gla_chunk(d, q, k, v, sm, wa, ba, s_ref, si):
    L = GLA_CHUNK
    row = lax.broadcasted_iota(jnp.int32, (L, L), 0)
    col = lax.broadcasted_iota(jnp.int32, (L, L), 1)
    mask = col <= row if d == 0 else col >= row
    tri = mask.astype(F32)
    lane_k = lax.broadcasted_iota(jnp.int32, (1, GLA_HEADS * GLA_DK), 1)
    lane_v = lax.broadcasted_iota(jnp.int32, (1, GLA_HEADS * GLA_DV), 1)
    eye = (lax.broadcasted_iota(jnp.int32, (LANE, LANE), 0)
           == lax.broadcasted_iota(jnp.int32, (LANE, LANE), 1)).astype(F32)

    pre = jnp.dot(sm, wa, precision=HIGHEST, preferred_element_type=F32) + ba
    loga = _log_sigmoid(pre) * (1.0 / GLA_TAU)
    bc = jnp.dot(tri, loga, precision=HIGHEST, preferred_element_type=F32)
    ref_row = bc[L // 2 - 1:L // 2]
    b_end = bc[L - 1:L] if d == 0 else bc[0:1]

    q = q * (GLA_DK ** -0.5)
    v = v.astype(BF16)
    q_in = (q * jnp.exp(bc - ref_row))
    k_in = (k * jnp.exp(ref_row - bc)).astype(BF16)
    q_st = (q * jnp.exp(bc)).astype(BF16)
    k_st = (k * jnp.exp(b_end - bc)).astype(BF16)

    blockdiag = (lax.broadcasted_iota(jnp.int32, s_ref.shape[1:], 0) // GLA_DK
                 == lax.broadcasted_iota(jnp.int32, s_ref.shape[1:], 1) // GLA_DV)
    s_old = s_ref[si]
    o = jnp.dot(q_st, jnp.where(blockdiag, s_old, 0.0).astype(BF16), preferred_element_type=F32)
    for h in range(GLA_HEADS):
        qh = jnp.where(lane_k // GLA_DK == h, q_in, 0.0).astype(BF16)
        att = jnp.where(mask, _nt_dot(qh, k_in), 0.0).astype(BF16)
        oh = jnp.dot(att, v, preferred_element_type=F32)
        o = o + jnp.where(lane_v // GLA_DV == h, oh, 0.0)

    decay_col = jnp.exp(_nt_dot(eye, jnp.broadcast_to(b_end, (8, LANE)), precision=HIGHEST)[:, 0:1])
    s_ref[si] = decay_col * s_old + _tn_dot(k_st, v)
    return o


def _gla_scan_kernel(qf_ref, kf_ref, vf_ref, smf_ref, qb_ref, kb_ref, vb_ref, smb_ref, wa_ref, ba_ref,
                     of_ref, ob_ref, s_ref):
    @pl.when(pl.program_id(0) == 0)
    def _():
        s_ref[...] = jnp.zeros_like(s_ref)

    nb = qf_ref.shape[0]
    streams = ((0, qf_ref, kf_ref, vf_ref, smf_ref, of_ref), (1, qb_ref, kb_ref, vb_ref, smb_ref, ob_ref))
    for d, q_ref, k_ref, v_ref, sm_ref, o_ref in streams:
        for bi in range(nb):
            o_ref[bi] = _gla_chunk(d, q_ref[bi], k_ref[bi], v_ref[bi], sm_ref[bi], wa_ref[d], ba_ref[d],
                                   s_ref, d * nb + bi)


def _gla_scan(p, wa_p, ba_p):
    b, t, _ = p.shape
    nc = t // GLA_CHUNK
    chunk = functools.partial(_scan_chunk, n_ctx_chunks=TM // GLA_CHUNK, n_chunks=nc)
    kw, vw = GLA_HEADS * GLA_DK, GLA_HEADS * GLA_DV
    blk = lambda d, w, col: pl.BlockSpec((b, GLA_CHUNK, w), lambda s: (0, chunk(d, s), col // w))
    stream = lambda d: [blk(d, kw, C_GQ), blk(d, kw, C_GK), blk(d, vw, C_GV), blk(d, LANE, C_SMF + d * LANE)]
    shp = jax.ShapeDtypeStruct((b, t, vw), F32)
    return pl.pallas_call(
        _gla_scan_kernel,
        grid=(nc,),
        in_specs=stream(0) + stream(1) + [pl.BlockSpec((2, LANE, kw), lambda s: (0, 0, 0)),
                                          pl.BlockSpec((2, 1, kw), lambda s: (0, 0, 0))],
        out_specs=[blk(0, vw, 0), blk(1, vw, 0)],
        out_shape=[shp, shp],
        scratch_shapes=[pltpu.VMEM((2 * b, kw, vw), F32)],
        compiler_params=_cparams(("arbitrary",)),
        name="gla_scan",
    )(p, p, p, p, p, p, p, p, wa_p, ba_p)


def _outproj_kernel(ac_ref, al_ref, mhf_ref, mhb_ref, xc_ref, mo_ref, gof_ref, gob_ref, gr_ref, x_ref, mod_ref,
                    mnw_ref, msk_ref, gnw_ref, wout_ref, n2w_ref, wr_ref, br_ref,
                    xo_ref, h2_ref, route_ref):
    grp64 = (lax.broadcasted_iota(jnp.int32, (ML_W, ML_W), 0) // ML_DH
             == lax.broadcasted_iota(jnp.int32, (ML_W, ML_W), 1) // ML_DH).astype(F32) * (1.0 / ML_DH)

    def head_norm(x, w):
        ms = jnp.dot(x * x, grp64, precision=HIGHEST, preferred_element_type=F32)
        return x * lax.rsqrt(ms + EPS) * w

    m_l = jax.nn.sigmoid(mo_ref[0]) * (head_norm(mhf_ref[0] + mhb_ref[0], mnw_ref[...])
                                       + msk_ref[...] * xc_ref[0])
    g_l = head_norm(gof_ref[0] + gob_ref[0], gnw_ref[...]) * _silu(gr_ref[0])
    na = MLA_HEADS * MLA_V
    a = jnp.where(pl.program_id(1) == 0, ac_ref[0], al_ref[0])
    res = (jnp.dot(a, wout_ref[0:na], preferred_element_type=F32)
           + jnp.dot(m_l.astype(BF16), wout_ref[na:na + ML_W], preferred_element_type=F32)
           + jnp.dot(g_l.astype(BF16), wout_ref[na + ML_W:], preferred_element_type=F32))
    mod = mod_ref[0]
    x = x_ref[0] + mod[2:3] * res
    xo_ref[0] = x
    h2 = (x * lax.rsqrt(jnp.mean(x * x, axis=-1, keepdims=True) + EPS) * n2w_ref[...]
          * (1.0 + mod[4:5]) + mod[3:4])
    h2_ref[0] = h2

    logits = jnp.dot(h2, wr_ref[...], precision=HIGHEST, preferred_element_type=F32) + br_ref[...]
    lane = lax.broadcasted_iota(jnp.int32, (1, LANE), 1)
    lane_f = lane.astype(F32)
    neg = -jnp.inf
    gl = jnp.where(lane < R_EXP, logits, neg)
    gmax = jnp.max(gl, axis=-1, keepdims=True)
    g_w = 1.0 / jnp.sum(jnp.exp(gl - gmax), axis=-1, keepdims=True)
    g_i = jnp.min(jnp.where(gl == gmax, lane_f, float(LANE)), axis=-1, keepdims=True)
    grp_of_lane = ((lane - R_EXP) // EXP_PER_GROUP).astype(F32)
    in_grp = (lane >= R_EXP) & (lane < R_EXP + N_EXPERTS) & (grp_of_lane == g_i)
    el = jnp.where(in_grp, logits, neg)
    m1 = jnp.max(el, axis=-1, keepdims=True)
    i1 = jnp.min(jnp.where(el == m1, lane_f, float(LANE)), axis=-1, keepdims=True)
    el2 = jnp.where(lane_f == i1, neg, el)
    m2 = jnp.max(el2, axis=-1, keepdims=True)
    i2 = jnp.min(jnp.where(el2 == m2, lane_f, float(LANE)), axis=-1, keepdims=True)
    p2 = jnp.exp(m2 - m1)
    w1 = g_w / (1.0 + p2)
    route_ref[0] = jnp.where(lane == RT_E1, i1 - R_EXP, jnp.where(lane == RT_E1 + 1, i2 - R_EXP,
                             jnp.where(lane == RT_W1, w1, jnp.where(lane == RT_W1 + 1, p2 * w1, 0.0))))


def _outproj(a_ctx, a_lat, mh, xconv, p, go, xs, mods, ml_norm_w, ml_skip, gla_norm_w, w_out_b, norm2_w, wr_p, br_p):
    b, t, d = xs.shape
    nt = t // TM
    full = lambda shape: pl.BlockSpec(shape, lambda bi, i: (0,) * len(shape))
    tok = lambda w, cb=0: pl.BlockSpec((1, TM, w), lambda bi, i: (bi, i, cb))
    dirblk = lambda dd: tok(ML_W)
    return pl.pallas_call(
        _outproj_kernel,
        grid=(b, nt),
        in_specs=[
            pl.BlockSpec((1, TM, MLA_HEADS * MLA_V), lambda bi, i: (bi, 0, 0)),
            pl.BlockSpec((1, TM, MLA_HEADS * MLA_V), lambda bi, i: (bi, jnp.maximum(i - 1, 0), 0)),
            dirblk(0), dirblk(1), tok(ML_W), tok(ML_W, C_MO // ML_W),
            dirblk(0), dirblk(1), tok(ML_W, C_GR // ML_W), tok(d),
            pl.BlockSpec((1, 6, d), lambda bi, i: (_mod_row(bi, i), 0, 0)),
            full((1, ML_W)), full((1, ML_W)), full((1, ML_W)), full((d, d)), full((1, d)),
            full((d, LANE)), full((1, LANE)),
        ],
        out_specs=[tok(d), tok(d), tok(LANE)],
        out_shape=[jax.ShapeDtypeStruct((b, t, d), F32), jax.ShapeDtypeStruct((b, t, d), F32),
                   jax.ShapeDtypeStruct((b, t, LANE), F32)],
        compiler_params=_cparams(("parallel", "parallel")),
        name="out_proj_router",
    )(a_ctx, a_lat, mh[0], mh[1], xconv, p, go[0], go[1], p, xs, mods, ml_norm_w.reshape(1, -1), ml_skip.reshape(1, -1),
      gla_norm_w.reshape(1, -1), w_out_b, norm2_w.reshape(1, -1), wr_p, br_p)


def _dispatch(route, n_tiles):
    n = route.shape[0]
    flat = route[:, RT_E1:RT_E1 + 2].astype(jnp.int32).reshape(-1)
    onehot = (flat[:, None] == jnp.arange(N_EXPERTS, dtype=jnp.int32)[None, :]).astype(jnp.int32)
    csum = jnp.cumsum(onehot, axis=0)
    rank = jnp.sum(csum * onehot, axis=1) - 1
    padded = (csum[-1] + TM - 1) // TM * TM
    ends = jnp.cumsum(padded)
    pos = (ends - padded)[flat] + rank
    src = jnp.zeros((n_tiles * TM,), jnp.int32).at[pos].set(jnp.arange(2 * n, dtype=jnp.int32) // 2)
    tile_start = jnp.arange(n_tiles, dtype=jnp.int32) * TM
    tile_exp = jnp.minimum(jnp.sum((ends[None, :] <= tile_start[:, None]).astype(jnp.int32), axis=1),
                           N_EXPERTS - 1)
    pos = pos.reshape(n // TM, TM, 2)
    return (src.reshape(n_tiles, 1, TM), tile_exp,
            pos[:, :, 0].reshape(n // TM, 1, TM), pos[:, :, 1].reshape(n // TM, 1, TM))


def _gather_rows(src_hbm, idx_ref, dst, sem):
    def body(j, carry):
        pltpu.make_async_copy(src_hbm.at[pl.ds(idx_ref[0, 0, j], 1)], dst.at[pl.ds(j, 1)], sem).start()
        return carry

    lax.fori_loop(0, TM, body, 0, unroll=8)


def _wait_rows(src_hbm, dst, sem):
    pltpu.make_async_copy(src_hbm.at[pl.ds(0, TM)], dst, sem).wait()


def _issue_rows(src_hbm, idx_ref, dst, sem):
    for j in range(TM):
        pltpu.make_async_copy(src_hbm.at[pl.ds(idx_ref[0, 0, j], 1)], dst.at[pl.ds(j, 1)], sem).start()


def _experts_kernel(texp_ref, idx_a, idx_b, idx_an, h2_hbm, wg_a, wu_a, wd_a, wg_b, wu_b, wd_b, y_ref,
                    buf_a, buf_b, wgu_a, wdn_a, wgu_b, wdn_b, sem):
    g = pl.program_id(0)
    last = pl.num_programs(0) - 1

    @pl.when(g == 0)
    def _():
        _gather_rows(h2_hbm, idx_a, buf_a, sem.at[0])

    def refresh(tile, wg, wu, wd, wgu_s, wdn_s):
        @pl.when(jnp.logical_or(g == 0, texp_ref[tile] != texp_ref[jnp.maximum(tile - 2, 0)]))
        def _():
            wgu_s[:, :D_EXPERT] = wg[0, 0].astype(BF16)
            wgu_s[:, D_EXPERT:] = wu[0, 0].astype(BF16)
            wdn_s[...] = wd[0, 0].astype(BF16)

    refresh(2 * g, wg_a, wu_a, wd_a, wgu_a, wdn_a)
    refresh(2 * g + 1, wg_b, wu_b, wd_b, wgu_b, wdn_b)

    def swiglu(buf, wgu_s, wdn_s):
        gu = jnp.dot(buf[...].astype(BF16), wgu_s[...], preferred_element_type=F32)
        act = (_silu(gu[:, :D_EXPERT]) * gu[:, D_EXPERT:]).astype(BF16)
        return jnp.dot(act, wdn_s[...], preferred_element_type=F32)

    _wait_rows(h2_hbm, buf_a, sem.at[0])
    _issue_rows(h2_hbm, idx_b, buf_b, sem.at[1])
    y_ref[0:TM] = swiglu(buf_a, wgu_a, wdn_a)
    _wait_rows(h2_hbm, buf_b, sem.at[1])
    _issue_rows(h2_hbm, idx_an, buf_a, sem.at[0])
    y_ref[TM:2 * TM] = swiglu(buf_b, wgu_b, wdn_b)

    @pl.when(g == last)
    def _():
        _wait_rows(h2_hbm, buf_a, sem.at[0])


def _experts(h2, src, tile_exp, w_gate, w_up, w_down, layer):
    n, d = h2.shape
    n_tiles = src.shape[0]
    assert n_tiles % 2 == 0
    idx = lambda u: pl.BlockSpec((1, 1, TM), lambda g, te: (jnp.minimum(2 * g + u, n_tiles - 1), 0, 0),
                                 memory_space=pltpu.SMEM)
    wspec = lambda shape, u: pl.BlockSpec((1, 1) + shape, lambda g, te: (layer, te[2 * g + u], 0, 0))
    weights = lambda u: [wspec((d, D_EXPERT), u), wspec((d, D_EXPERT), u), wspec((D_EXPERT, d), u)]
    return pl.pallas_call(
        _experts_kernel,
        grid_spec=pltpu.PrefetchScalarGridSpec(
            num_scalar_prefetch=1,
            grid=(n_tiles // 2,),
            in_specs=[idx(0), idx(1), idx(2), pl.BlockSpec(memory_space=pl.ANY)] + weights(0) + weights(1),
            out_specs=pl.BlockSpec((2 * TM, d), lambda g, te: (g, 0)),
            scratch_shapes=[pltpu.VMEM((TM, d), F32), pltpu.VMEM((TM, d), F32),
                            pltpu.VMEM((d, 2 * D_EXPERT), BF16), pltpu.VMEM((D_EXPERT, d), BF16),
                            pltpu.VMEM((d, 2 * D_EXPERT), BF16), pltpu.VMEM((D_EXPERT, d), BF16),
                            pltpu.SemaphoreType.DMA((2,))],
        ),
        out_shape=jax.ShapeDtypeStruct((n_tiles * TM, d), F32),
        compiler_params=_cparams(("arbitrary",)),
        name="moe_experts",
    )(tile_exp, src, src, src, h2, w_gate, w_up, w_down, w_gate, w_up, w_down)


def _combine_kernel(p1a, p2a, p1b, p2b, p1n, p2n, route_ref, x_ref, moda_ref, modb_ref, y_hbm, o_ref,
                    a1, a2, b1, b2, sem):
    g = pl.program_id(0)

    @pl.when(g == 0)
    def _():
        _gather_rows(y_hbm, p1a, a1, sem.at[0])
        _gather_rows(y_hbm, p2a, a2, sem.at[0])

    def combine(rows, r1, r2, mod_ref):
        route = route_ref[rows]
        moe = route[:, RT_W1:RT_W1 + 1] * r1[...] + route[:, RT_W1 + 1:RT_W1 + 2] * r2[...]
        o_ref[rows] = x_ref[rows] + mod_ref[0][5:6] * moe

    _wait_rows(y_hbm, a1, sem.at[0])
    _wait_rows(y_hbm, a2, sem.at[0])
    _issue_rows(y_hbm, p1b, b1, sem.at[1])
    _issue_rows(y_hbm, p2b, b2, sem.at[1])
    combine(slice(0, TM), a1, a2, moda_ref)
    _wait_rows(y_hbm, b1, sem.at[1])
    _wait_rows(y_hbm, b2, sem.at[1])
    _issue_rows(y_hbm, p1n, a1, sem.at[0])
    _issue_rows(y_hbm, p2n, a2, sem.at[0])
    combine(slice(TM, 2 * TM), b1, b2, modb_ref)

    @pl.when(g == pl.num_programs(0) - 1)
    def _():
        _wait_rows(y_hbm, a1, sem.at[0])
        _wait_rows(y_hbm, a2, sem.at[0])


def _combine(y, pos1, pos2, route, xs2, mods, nt):
    n, d = xs2.shape
    n_tok_tiles = n // TM
    assert n_tok_tiles % 2 == 0
    idx = lambda u: pl.BlockSpec((1, 1, TM), lambda g: (jnp.minimum(2 * g + u, n_tok_tiles - 1), 0, 0),
                                 memory_space=pltpu.SMEM)
    tok = lambda w: pl.BlockSpec((2 * TM, w), lambda g: (g, 0))
    mod = lambda u: pl.BlockSpec((1, 6, d), lambda g: (_mod_row((2 * g + u) // nt, (2 * g + u) % nt), 0, 0))
    rows = pltpu.VMEM((TM, d), F32)
    return pl.pallas_call(
        _combine_kernel,
        grid=(n_tok_tiles // 2,),
        in_specs=[idx(0), idx(0), idx(1), idx(1), idx(2), idx(2), tok(LANE), tok(d), mod(0), mod(1),
                  pl.BlockSpec(memory_space=pl.ANY)],
        out_specs=tok(d),
        out_shape=jax.ShapeDtypeStruct((n, d), F32),
        scratch_shapes=[rows, rows, rows, rows, pltpu.SemaphoreType.DMA((2,))],
        compiler_params=_cparams(("arbitrary",)),
        name="moe_combine",
    )(pos1, pos2, pos1, pos2, pos1, pos2, route, xs2, mods, mods, y)


def _moe(h2, route, xs, mods, w_gate, w_up, w_down, layer):
    b, t, d = xs.shape
    n = b * t
    n_tiles = 2 * n // TM + N_EXPERTS
    route2 = route.reshape(n, LANE)
    src, tile_exp, pos1, pos2 = _dispatch(route2, n_tiles)
    y = _experts(h2.reshape(n, d), src, tile_exp, w_gate, w_up, w_down, layer)
    return _combine(y, pos1, pos2, route2, xs.reshape(n, d), mods, t // TM).reshape(b, t, d)


def _rope_tables(n_ctx, n_lat):
    rows = n_lat // GRID_W
    row = jnp.broadcast_to(jnp.arange(rows, dtype=F32)[:, None], (rows, GRID_W)).reshape(-1)
    col = jnp.broadcast_to(jnp.arange(GRID_W, dtype=F32)[None, :], (rows, GRID_W)).reshape(-1)
    n_freq = MLA_ROPE // 4
    inv = ROPE_THETA ** (-jnp.arange(n_freq, dtype=F32) / n_freq)
    ang = jnp.concatenate([row[:, None] * inv, col[:, None] * inv], axis=-1)
    cos, sin = jnp.cos(ang), jnp.sin(ang)
    half = MLA_ROPE // 2
    z = lambda w: jnp.zeros((n_lat, w), F32)
    o = lambda w: jnp.ones((n_lat, w), F32)
    tail = LANE - MLA_QK
    cos_t = jnp.concatenate([o(MLA_NOPE), cos, cos, o(tail)], axis=-1)
    sa_t = jnp.concatenate([z(MLA_NOPE + half), sin, z(tail)], axis=-1)
    sb_t = jnp.concatenate([z(MLA_NOPE), -sin, z(half + tail)], axis=-1)
    ctx1 = jnp.ones((n_ctx, LANE), F32)
    ctx0 = jnp.zeros((n_ctx, LANE), F32)
    return (jnp.concatenate([ctx1, cos_t], 0), jnp.concatenate([ctx0, sa_t], 0),
            jnp.concatenate([ctx0, sb_t], 0))


def _pad_cols(a, width):
    return jnp.pad(a, [(0, 0)] * (a.ndim - 1) + [(0, width - a.shape[-1])])


def _layer_weights(w_in, w_uq, w_ukv, q_norm_w, k_norm_w, ml_conv_w, ml_wq, ml_wk, ml_gate_b,
                   gla_wa, gla_ba, w_out, w_grp, b_grp, w_erouter, b_erouter):
    d = w_in.shape[0]
    o = np.cumsum((0, Q_LORA, KV_LORA, MLA_ROPE, ML_W, ML_W, ML_W, 4 * ML_HEADS, GLA_HEADS * GLA_DK,
                   GLA_HEADS * GLA_DK, GLA_HEADS * GLA_DV, GLA_HEADS * GLA_DV, 2 * GLA_LR))
    seg = lambda j: w_in[:, o[j]:o[j + 1]]
    cq, ckv, kr, mx, mv, mo, mg, gq, gk, gv, gr, ga = (seg(j) for j in range(12))
    z = lambda w: jnp.zeros((d, w), F32)

    def small(di):
        return jnp.concatenate([mg[:, di * 8:(di + 1) * 8], ga[:, di * GLA_LR:(di + 1) * GLA_LR],
                                z(SM_KR - SM_GA - GLA_LR), kr, z(LANE - SM_KR - MLA_ROPE)], axis=-1)

    w_in_p = jnp.concatenate([cq, mx, mv, mo, gv, gr, ckv, gq, gk, small(0), small(1)], axis=-1).astype(BF16)

    wuq_p = _pad_cols(w_uq.reshape(Q_LORA, MLA_HEADS, MLA_QK), LANE).reshape(Q_LORA, -1).astype(BF16)
    ukv = w_ukv.reshape(KV_LORA, MLA_HEADS, MLA_NOPE + MLA_V)
    wuk_p = _pad_cols(ukv[..., :MLA_NOPE], LANE).reshape(KV_LORA, -1).astype(BF16)
    wuv_p = _pad_cols(ukv[..., MLA_NOPE:], LANE).reshape(KV_LORA, -1).astype(BF16)
    qn_p = _pad_cols(q_norm_w.reshape(1, -1), LANE)
    kn_p = _pad_cols(k_norm_w.reshape(1, -1), LANE)

    conv_w8 = jnp.pad(ml_conv_w, ((0, 8 - ML_CONV), (0, 0)))
    bd = lambda w: jax.scipy.linalg.block_diag(*[w[h] for h in range(ML_HEADS)])
    wq_bd = (bd(ml_wq) * (ML_DH ** -0.5)).astype(BF16)
    wk_bd = bd(ml_wk).astype(BF16)
    gate_b_p = _pad_cols(ml_gate_b.reshape(2, 1, 2 * ML_HEADS), LANE)

    wa_p = jnp.pad(gla_wa, ((0, 0), (SM_GA, LANE - SM_GA - GLA_LR), (0, 0)))
    ba_p = gla_ba.reshape(2, 1, -1)

    wr_p = _pad_cols(jnp.concatenate([w_grp, w_erouter], axis=-1), LANE)
    br_p = _pad_cols(jnp.concatenate([b_grp, b_erouter]).reshape(1, -1), LANE)
    return dict(w_in_p=w_in_p, wuq_p=wuq_p, wuk_p=wuk_p, wuv_p=wuv_p, qn_p=qn_p, kn_p=kn_p,
                conv_w8=conv_w8, wq_bd=wq_bd, wk_bd=wk_bd, gate_b_p=gate_b_p, wa_p=wa_p, ba_p=ba_p,
                w_out_b=w_out.astype(BF16), wr_p=wr_p, br_p=br_p)


def kernel(x, c, ctx, c_ctx, w_mod, b_mod, norm1_w, w_in, q_a_norm, w_uq, kv_a_norm, w_ukv,
           q_norm_w, k_norm_w, ml_conv_w, ml_conv_b, ml_wq, ml_wk, ml_gate_b, ml_norm_w, ml_skip,
           gla_wa, gla_ba, gla_norm_w, w_out, norm2_w, w_grp, b_grp, w_erouter, b_erouter,
           w_gate, w_up, w_down):
    b, s, d = x.shape
    n_ctx = ctx.shape[1]
    depth = w_mod.shape[0]
    assert n_ctx == TM and s % TM == 0 and b == 2

    cc = jnp.concatenate([c, c_ctx[None, :], jnp.zeros((8 - b - 1, d), F32)], axis=0)
    mods_all = _mods(cc, w_mod, b_mod).reshape(depth, 8, 6, d)
    cos_t, sa_t, sb_t = _rope_tables(n_ctx, s)
    xs = jnp.concatenate([ctx, x], axis=1)

    for l in range(depth):
        w = _layer_weights(w_in[l], w_uq[l], w_ukv[l], q_norm_w[l], k_norm_w[l], ml_conv_w[l],
                           ml_wq[l], ml_wk[l], ml_gate_b[l], gla_wa[l], gla_ba[l], w_out[l],
                           w_grp[l], b_grp[l], w_erouter[l], b_erouter[l])
        mods = mods_all[l]
        p = _inproj(xs, mods, norm1_w[l], w["w_in_p"])
        q, k, v = _mla_prep(p, q_a_norm[l], w["wuq_p"], kv_a_norm[l], w["wuk_p"], w["wuv_p"],
                            w["qn_p"], w["kn_p"], cos_t, sa_t, sb_t)
        a_lat = _attention_lat(q, k, v)
        a_ctx = _attention_ctx(q, k, v) if l < depth - 1 else jnp.zeros((b, TM, MLA_HEADS * MLA_V), BF16)
        xconv, mq, mk = _ml_prep(p, w["conv_w8"], ml_conv_b[l], w["wq_bd"], w["wk_bd"])
        mh = _ml_scan(mq, mk, p, w["gate_b_p"])
        go = _gla_scan(p, w["wa_p"], w["ba_p"])
        xs, h2, route = _outproj(a_ctx, a_lat, mh, xconv, p, go, xs, mods, ml_norm_w[l], ml_skip[l], gla_norm_w[l],
                                w["w_out_b"], norm2_w[l], w["wr_p"], w["br_p"])
        xs = _moe(h2, route, xs, mods, w_gate, w_up, w_down, l)
    return xs[:, n_ctx:, :]
```

```python
import functools

import jax
import jax.numpy as jnp
import numpy as np
from jax import lax
from jax.experimental import pallas as pl
from jax.experimental.pallas import tpu as pltpu

F32 = jnp.float32
BF16 = jnp.bfloat16
HIGHEST = lax.Precision.HIGHEST

EPS = 1e-6
GRID_W = 64
ROPE_THETA = 10000.0

MLA_HEADS = 8
MLA_NOPE = 64
MLA_ROPE = 32
MLA_QK = MLA_NOPE + MLA_ROPE
MLA_V = 64
Q_LORA = 256
KV_LORA = 128

ML_HEADS = 4
ML_DH = 64
ML_W = ML_HEADS * ML_DH
ML_CONV = 5

GLA_HEADS = 4
GLA_DK = 32
GLA_DV = 64
GLA_LR = 16
GLA_TAU = 16.0

N_GROUPS = 4
EXP_PER_GROUP = 8
N_EXPERTS = N_GROUPS * EXP_PER_GROUP
D_EXPERT = 256

LANE = 128
TM = 256
ML_CHUNK = 256
GLA_CHUNK = 128
VMEM_LIMIT = 56 * 1024 * 1024

C_CQ, C_MX, C_MV, C_MO, C_GV, C_GR = 0, 256, 512, 768, 1024, 1280
C_CKV, C_GQ, C_GK, C_SMF, C_SMB = 1536, 1664, 1792, 1920, 2048
D_INP = 2176
SM_GATE = 0
SM_GA = 8
SM_KR = 64
R_GRP = 0
R_EXP = 4
RT_E1 = 0
RT_W1 = 2


def _cparams(sem):
    return pltpu.CompilerParams(dimension_semantics=sem, vmem_limit_bytes=VMEM_LIMIT)


def _silu(x):
    return x * jax.nn.sigmoid(x)


def _log_sigmoid(x):
    return -(jnp.maximum(-x, 0.0) + jnp.log1p(jnp.exp(-jnp.abs(x))))


def _nt_dot(a, b, **kw):
    return lax.dot_general(a, b, (((1,), (1,)), ((), ())), preferred_element_type=F32, **kw)


def _tn_dot(a, b, **kw):
    return lax.dot_general(a, b, (((0,), (0,)), ((), ())), preferred_element_type=F32, **kw)


def _mods_kernel(cc_ref, w_ref, b_ref, o_ref):
    a = _silu(cc_ref[...])
    o_ref[0] = jnp.dot(a, w_ref[0], precision=HIGHEST, preferred_element_type=F32) + b_ref[0]


def _mods(cc, w_mod, b_mod):
    depth, d, d6 = w_mod.shape
    nb = 1536
    return pl.pallas_call(
        _mods_kernel,
        grid=(depth, d6 // nb),
        in_specs=[
            pl.BlockSpec((8, d), lambda l, j: (0, 0)),
            pl.BlockSpec((1, d, nb), lambda l, j: (l, 0, j)),
            pl.BlockSpec((1, 1, nb), lambda l, j: (l, 0, j)),
        ],
        out_specs=pl.BlockSpec((1, 8, nb), lambda l, j: (l, 0, j)),
        out_shape=jax.ShapeDtypeStruct((depth, 8, d6), F32),
        compiler_params=_cparams(("arbitrary", "arbitrary")),
        name="adaln_mods",
    )(cc, w_mod, b_mod.reshape(depth, 1, d6))


def _mod_row(b, i):
    return jnp.where(i == 0, 2, b)


def _inproj_kernel(x_ref, mod_ref, nw_ref, w_ref, o_ref):
    x = x_ref[0]
    y = x * lax.rsqrt(jnp.mean(x * x, axis=-1, keepdims=True) + EPS) * nw_ref[...]
    mod = mod_ref[0]
    h = y * (1.0 + mod[1:2]) + mod[0:1]
    o_ref[0] = jnp.dot(h.astype(BF16), w_ref[...], preferred_element_type=F32)


def _inproj(xs, mods, norm_w, w_in_p):
    b, t, d = xs.shape
    nt = t // TM
    return pl.pallas_call(
        _inproj_kernel,
        grid=(b, nt),
        in_specs=[
            pl.BlockSpec((1, TM, d), lambda bi, i: (bi, i, 0)),
            pl.BlockSpec((1, 6, d), lambda bi, i: (_mod_row(bi, i), 0, 0)),
            pl.BlockSpec((1, d), lambda bi, i: (0, 0)),
            pl.BlockSpec((d, D_INP), lambda bi, i: (0, 0)),
        ],
        out_specs=pl.BlockSpec((1, TM, D_INP), lambda bi, i: (bi, i, 0)),
        out_shape=jax.ShapeDtypeStruct((b, t, D_INP), F32),
        compiler_params=_cparams(("parallel", "parallel")),
        name="in_proj",
    )(xs, mods, norm_w.reshape(1, d), w_in_p)


def _mla_prep_kernel(cq_ref, ckv_ref, sm_ref, qan_ref, wuq_ref, kvan_ref, wuk_ref, wuv_ref,
                     qn_ref, kn_ref, cos_ref, sa_ref, sb_ref, q_ref, k_ref, v_ref):
    cq = cq_ref[0]
    cqn = cq * lax.rsqrt(jnp.mean(cq * cq, axis=-1, keepdims=True) + EPS) * qan_ref[...]
    qall = jnp.dot(cqn.astype(BF16), wuq_ref[...], preferred_element_type=F32)
    ckv = ckv_ref[0]
    ckvn = (ckv * lax.rsqrt(jnp.mean(ckv * ckv, axis=-1, keepdims=True) + EPS)
            * kvan_ref[...]).astype(BF16)
    kall = jnp.dot(ckvn, wuk_ref[...], preferred_element_type=F32)
    vall = jnp.dot(ckvn, wuv_ref[...], preferred_element_type=F32)
    lane = lax.broadcasted_iota(jnp.int32, (1, LANE), 1)
    kr = jnp.where((lane >= SM_KR) & (lane < SM_KR + MLA_ROPE), sm_ref[0], 0.0)
    cos, sin = cos_ref[...], sa_ref[...] - sb_ref[...]
    r_i = lax.broadcasted_iota(jnp.int32, (LANE, LANE), 0)
    c_i = lax.broadcasted_iota(jnp.int32, (LANE, LANE), 1)
    half = MLA_ROPE // 2
    first = (c_i >= MLA_NOPE) & (c_i < MLA_NOPE + half)
    second = (c_i >= MLA_NOPE + half) & (c_i < MLA_QK)
    rot = jnp.where(first & (r_i == c_i + half), -1.0,
                    jnp.where(second & (r_i == c_i - half), 1.0, 0.0)).astype(BF16)

    def rope(x):
        return x * cos + jnp.dot(x.astype(BF16), rot, preferred_element_type=F32) * sin

    def head_norm(x, w):
        return x * lax.rsqrt(jnp.sum(x * x, axis=-1, keepdims=True) * (1.0 / MLA_QK) + EPS) * w

    for h in range(MLA_HEADS):
        sl = slice(h * LANE, (h + 1) * LANE)
        qh = rope(head_norm(qall[:, sl], qn_ref[...]))
        q_ref[0, h] = (qh * Q_SCALE).astype(BF16)
        kh = rope(head_norm(kall[:, sl] + kr, kn_ref[...]))
        k_ref[0, h] = jnp.where(lane == ATT_SHIFT_LANE, 1.0, kh).astype(BF16)
        v_ref[0, h] = jnp.where(lane == MLA_V, 1.0, vall[:, sl]).astype(BF16)


def _mla_prep(p, q_a_norm, wuq_p, kv_a_norm, wuk_p, wuv_p, qn_p, kn_p, cos_t, sa_t, sb_t):
    b, t, _ = p.shape
    nt = t // TM
    hw = MLA_HEADS * LANE
    full = lambda shape: pl.BlockSpec(shape, lambda bi, i: (0,) * len(shape))
    tab = pl.BlockSpec((TM, LANE), lambda bi, i: (i, 0))
    out = pl.BlockSpec((1, MLA_HEADS, TM, LANE), lambda bi, i: (bi, 0, i, 0))
    shp = jax.ShapeDtypeStruct((b, MLA_HEADS, t, LANE), BF16)
    return pl.pallas_call(
        _mla_prep_kernel,
        grid=(b, nt),
        in_specs=[
            pl.BlockSpec((1, TM, Q_LORA), lambda bi, i: (bi, i, C_CQ // Q_LORA)),
            pl.BlockSpec((1, TM, KV_LORA), lambda bi, i: (bi, i, C_CKV // KV_LORA)),
            pl.BlockSpec((1, TM, LANE), lambda bi, i: (bi, i, C_SMF // LANE)),
            full((1, Q_LORA)), full((Q_LORA, hw)), full((1, KV_LORA)),
            full((KV_LORA, hw)), full((KV_LORA, hw)), full((1, LANE)), full((1, LANE)),
            tab, tab, tab,
        ],
        out_specs=[out, out, out],
        out_shape=[shp, shp, shp],
        compiler_params=_cparams(("parallel", "parallel")),
        name="mla_prep",
    )(p, p, p, q_a_norm.reshape(1, -1), wuq_p, kv_a_norm.reshape(1, -1), wuk_p, wuv_p,
      qn_p, kn_p, cos_t, sa_t, sb_t)


Q_SCALE = float(MLA_QK ** -0.5 * np.log2(np.e))
ATT_HP = 2
ATT_NQ = 4
ATT_TK = 2048
ATT_SHIFT_LANE = MLA_QK
ATT_SAFE_MAX = 2.0 ** 100


def _softmax_step(q, kb, vb, m, acc):
    s = _nt_dot(q, kb)
    m_new = jnp.max(s, axis=-1, keepdims=True)
    if m is None:
        return m_new, jnp.dot(jnp.exp2((s - m_new).astype(BF16)), vb, preferred_element_type=F32)
    m_new = jnp.maximum(m, m_new)
    p = jnp.exp2((s - m_new).astype(BF16))
    return m_new, acc * jnp.exp2(m - m_new) + jnp.dot(p, vb, preferred_element_type=F32)


def _attn_ctx_kernel(q_ref, k_ref, v_ref, o_ref):
    outs = []
    for h in range(MLA_HEADS):
        _, acc = _softmax_step(q_ref[0, h], k_ref[0, h], v_ref[0, h], None, None)
        outs.append(acc[:, :MLA_V] / acc[:, MLA_V:MLA_V + 1])
    o_ref[0] = jnp.concatenate(outs, axis=-1).astype(BF16)


def _attention_ctx(q, k, v):
    b, h, _, _ = q.shape
    blk = pl.BlockSpec((1, h, TM, LANE), lambda bi: (bi, 0, 0, 0))
    return pl.pallas_call(
        _attn_ctx_kernel,
        grid=(b,),
        in_specs=[blk, blk, blk],
        out_specs=pl.BlockSpec((1, TM, h * MLA_V), lambda bi: (bi, 0, 0)),
        out_shape=jax.ShapeDtypeStruct((b, TM, h * MLA_V), BF16),
        compiler_params=_cparams(("parallel",)),
        name="mla_attention_ctx",
    )(q, k, v)


def _attn_lat_kernel(*refs, n_blk):
    q_refs, (k_ref, v_ref, o_ref, q_buf, qs_buf) = refs[:ATT_NQ], refs[ATT_NQ:]
    lane = lax.broadcasted_iota(jnp.int32, (1, LANE), 1)

    def kv_block(hh, j):
        off = pl.multiple_of(TM + j * ATT_TK, TM)
        return k_ref[0, hh, pl.ds(off, ATT_TK), :], v_ref[0, hh, pl.ds(off, ATT_TK), :]

    def finish(accs):
        outs = [acc[:, :MLA_V] / acc[:, MLA_V:MLA_V + 1] for acc in accs]
        o_ref[0] = jnp.concatenate(outs, axis=-1).astype(BF16)

    accs = []
    for hh in range(ATT_HP):
        q = jnp.concatenate([qr[0, hh] for qr in q_refs], axis=0)
        q_buf[hh] = q
        kb, vb = k_ref[0, hh, 0:TM, :], v_ref[0, hh, 0:TM, :]
        shift = jnp.max(_nt_dot(q, kb), axis=-1, keepdims=True).astype(BF16)
        qs_buf[hh] = jnp.where(lane == ATT_SHIFT_LANE, -shift, q)
        accs.append(jnp.dot(jnp.exp2(_nt_dot(qs_buf[hh], kb).astype(BF16)), vb, preferred_element_type=F32))

    def fast_body(j, accs):
        new = []
        for hh in range(ATT_HP):
            kb, vb = kv_block(hh, j)
            p = jnp.exp2(_nt_dot(qs_buf[hh], kb).astype(BF16))
            new.append(accs[hh] + jnp.dot(p, vb, preferred_element_type=F32))
        return tuple(new)

    accs = lax.fori_loop(0, n_blk, fast_body, tuple(accs))
    bad = sum(jnp.max(jnp.where(jnp.abs(acc) < ATT_SAFE_MAX, 0.0, 1.0)) for acc in accs)

    @pl.when(bad == 0.0)
    def _():
        finish(accs)

    @pl.when(bad != 0.0)
    def _():
        init = []
        for hh in range(ATT_HP):
            init += _softmax_step(q_buf[hh], k_ref[0, hh, 0:TM, :], v_ref[0, hh, 0:TM, :], None, None)

        def body(j, carry):
            new = []
            for hh in range(ATT_HP):
                new += _softmax_step(q_buf[hh], *kv_block(hh, j), carry[2 * hh], carry[2 * hh + 1])
            return tuple(new)

        carry = lax.fori_loop(0, n_blk, body, tuple(init))
        finish(carry[1::2])


def _attention_lat(q, k, v):
    b, h, t, _ = q.shape
    s = t - TM
    tq = ATT_NQ * TM
    assert s % ATT_TK == 0 and s % tq == 0
    kv = pl.BlockSpec((1, ATT_HP, t, LANE), lambda bi, hp, i: (bi, hp, 0, 0))
    qs = [pl.BlockSpec((1, ATT_HP, TM, LANE), lambda bi, hp, i, u=u: (bi, hp, 1 + ATT_NQ * i + u, 0))
          for u in range(ATT_NQ)]
    return pl.pallas_call(
        functools.partial(_attn_lat_kernel, n_blk=s // ATT_TK),
        grid=(b, h // ATT_HP, s // tq),
        in_specs=qs + [kv, kv],
        out_specs=pl.BlockSpec((1, tq, ATT_HP * MLA_V), lambda bi, hp, i: (bi, i, hp)),
        out_shape=jax.ShapeDtypeStruct((b, s, h * MLA_V), BF16),
        scratch_shapes=[pltpu.VMEM((ATT_HP, tq, LANE), BF16), pltpu.VMEM((ATT_HP, tq, LANE), BF16)],
        compiler_params=_cparams(("parallel", "parallel", "arbitrary")),
        name="mla_attention",
    )(*([q] * ATT_NQ), k, v)


def _ml_prep_kernel(x_ref, prev_ref, next_ref, cw_ref, cb_ref, wq_ref, wk_ref,
                    xc_ref, q_ref, k_ref, *, n_tiles):
    i = pl.program_id(1)
    x = x_ref[0]
    prev = jnp.where(i <= 1, 0.0, prev_ref[0])
    nxt = jnp.where((i == 0) | (i == n_tiles - 1), 0.0, next_ref[0])
    ext = jnp.concatenate([prev, x, nxt], axis=0)
    n_ext = TM + 16
    cw = cw_ref[...]
    acc = jnp.zeros((TM, ML_W), F32) + cb_ref[...]
    for kk in range(ML_CONV):
        sh = (ML_CONV // 2 - kk) % n_ext
        shifted = ext if sh == 0 else pltpu.roll(ext, sh, 0)
        acc = acc + cw[kk:kk + 1] * shifted[8:8 + TM]
    xc = _silu(acc)
    xc_ref[0] = xc
    xb = xc.astype(BF16)
    q_ref[0] = jnp.dot(xb, wq_ref[...], preferred_element_type=F32).astype(BF16)
    k_ref[0] = jnp.dot(xb, wk_ref[...], preferred_element_type=F32).astype(BF16)


def _ml_prep(p, conv_w8, conv_b, wq_bd, wk_bd):
    b, t, _ = p.shape
    nt = t // TM
    r8 = TM // 8
    full = lambda shape: pl.BlockSpec(shape, lambda bi, i: (0,) * len(shape))
    cb = C_MX // ML_W
    blk = pl.BlockSpec((1, TM, ML_W), lambda bi, i: (bi, i, 0))
    return pl.pallas_call(
        functools.partial(_ml_prep_kernel, n_tiles=nt),
        grid=(b, nt),
        in_specs=[
            pl.BlockSpec((1, TM, ML_W), lambda bi, i: (bi, i, cb)),
            pl.BlockSpec((1, 8, ML_W), lambda bi, i: (bi, jnp.maximum(i * r8 - 1, 0), cb)),
            pl.BlockSpec((1, 8, ML_W), lambda bi, i: (bi, jnp.minimum((i + 1) * r8, nt * r8 - 1), cb)),
            full((8, ML_W)), full((1, ML_W)), full((ML_W, ML_W)), full((ML_W, ML_W)),
        ],
        out_specs=[blk, blk, blk],
        out_shape=[jax.ShapeDtypeStruct((b, t, ML_W), F32),
                   jax.ShapeDtypeStruct((b, t, ML_W), BF16),
                   jax.ShapeDtypeStruct((b, t, ML_W), BF16)],
        compiler_params=_cparams(("parallel", "parallel")),
        name="mlstm_prep",
    )(p, p, p, conv_w8, conv_b.reshape(1, ML_W), wq_bd, wk_bd)


def _scan_chunk(d, step, n_ctx_chunks, n_chunks):
    bwd = jnp.where(step < n_ctx_chunks, n_ctx_chunks - 1 - step, n_chunks - 1 - (step - n_ctx_chunks))
    return jnp.where(d == 0, step, bwd)


def _ml_chunk(d, q, k, v, g, s_ref, m_ref, base):
    L = ML_CHUNK
    row = lax.broadcasted_iota(jnp.int32, (L, L), 0)
    col = lax.broadcasted_iota(jnp.int32, (L, L), 1)
    mask = col <= row if d == 0 else col >= row
    tri = mask.astype(F32)
    lane = lax.broadcasted_iota(jnp.int32, (1, LANE), 1)
    eye8 = (lax.broadcasted_iota(jnp.int32, (8, LANE), 0)
            == lax.broadcasted_iota(jnp.int32, (8, LANE), 1)).astype(F32)

    lf = _log_sigmoid(g)
    bc = jnp.dot(tri, lf, precision=HIGHEST, preferred_element_type=F32)
    g_rows = _nt_dot(eye8, g, precision=HIGHEST)
    bc_rows = _nt_dot(eye8, bc, precision=HIGHEST)

    outs = []
    for pair in range(ML_HEADS // 2):
        sl = slice(pair * LANE, (pair + 1) * LANE)
        q_blk, k_blk, v_blk = q[:, sl], k[:, sl], v[:, sl]
        pair_out = []
        for sub in range(2):
            h = pair * 2 + sub
            head_lanes = (lane >= sub * ML_DH) & (lane < (sub + 1) * ML_DH)
            qh = jnp.where(head_lanes, q_blk, jnp.zeros_like(q_blk))
            vs = v_blk if sub == 0 else pltpu.roll(v_blk, ML_DH, 1)
            v_ext = jnp.where(lane < ML_DH, vs, jnp.where(lane == ML_DH, 1.0, 0.0)).astype(BF16)

            li_c = g[:, SM_GATE + h:SM_GATE + h + 1]
            bc_c = bc[:, SM_GATE + 4 + h:SM_GATE + 5 + h]
            li_r = g_rows[h:h + 1, :]
            bc_r = bc_rows[4 + h:5 + h, :]
            m_st = m_ref[base + h][0:1, 0:1]

            dmat = jnp.where(mask, bc_c + (li_r - bc_r), -jnp.inf)
            inter = bc_c + m_st
            m_t = jnp.maximum(inter, jnp.max(dmat, axis=-1, keepdims=True))
            e = jnp.exp(dmat - m_t)
            s = (_nt_dot(qh, k_blk) * e).astype(BF16)
            tot = (jnp.dot(s, v_ext, preferred_element_type=F32)
                   + jnp.exp(inter - m_t) * jnp.dot(qh, s_ref[base + h].astype(BF16),
                                                    preferred_element_type=F32))
            den = tot[:, ML_DH:ML_DH + 1]
            pair_out.append(tot / jnp.maximum(jnp.abs(den), jnp.exp(-m_t)))

            b_end = bc_c[L - 1:L] if d == 0 else bc_c[0:1]
            g_col = b_end - bc_c + li_c
            m_new = jnp.maximum(b_end + m_st, jnp.max(g_col, axis=0, keepdims=True))
            kw = jnp.where(head_lanes, k_blk.astype(F32) * jnp.exp(g_col - m_new), 0.0).astype(BF16)
            s_ref[base + h] = jnp.exp(b_end + m_st - m_new) * s_ref[base + h] + _tn_dot(kw, v_ext)
            m_ref[base + h] = jnp.broadcast_to(m_new, m_ref.shape[1:])
        outs.append(jnp.where(lane < ML_DH, pair_out[0], pltpu.roll(pair_out[1], ML_DH, 1)))
    return jnp.concatenate(outs, axis=-1)


def _ml_scan_kernel(qf_ref, kf_ref, vf_ref, smf_ref, qb_ref, kb_ref, vb_ref, smb_ref, gb_ref,
                    hf_ref, hb_ref, s_ref, m_ref):
    @pl.when(pl.program_id(0) == 0)
    def _():
        s_ref[...] = jnp.zeros_like(s_ref)
        m_ref[...] = jnp.zeros_like(m_ref)

    nb = qf_ref.shape[0]
    streams = ((0, qf_ref, kf_ref, vf_ref, smf_ref, hf_ref), (1, qb_ref, kb_ref, vb_ref, smb_ref, hb_ref))
    for d, q_ref, k_ref, v_ref, sm_ref, h_ref in streams:
        for bi in range(nb):
            h_ref[bi] = _ml_chunk(d, q_ref[bi], k_ref[bi], v_ref[bi], sm_ref[bi] + gb_ref[d],
                                  s_ref, m_ref, (d * nb + bi) * ML_HEADS)


def _ml_scan(q, k, p, gate_b_p):
    b, t, _ = q.shape
    nc = t // ML_CHUNK
    chunk = functools.partial(_scan_chunk, n_ctx_chunks=TM // ML_CHUNK, n_chunks=nc)
    blk = lambda d, w, cb: pl.BlockSpec((b, ML_CHUNK, w), lambda s: (0, chunk(d, s), cb))
    stream = lambda d: [blk(d, ML_W, 0), blk(d, ML_W, 0), blk(d, ML_W, C_MV // ML_W),
                        blk(d, LANE, C_SMF // LANE + d)]
    shp = jax.ShapeDtypeStruct((b, t, ML_W), F32)
    return pl.pallas_call(
        _ml_scan_kernel,
        grid=(nc,),
        in_specs=stream(0) + stream(1) + [pl.BlockSpec((2, 1, LANE), lambda s: (0, 0, 0))],
        out_specs=[blk(0, ML_W, 0), blk(1, ML_W, 0)],
        out_shape=[shp, shp],
        scratch_shapes=[pltpu.VMEM((2 * b * ML_HEADS, LANE, LANE), F32),
                        pltpu.VMEM((2 * b * ML_HEADS, 8, LANE), F32)],
        compiler_params=_cparams(("arbitrary",)),
        name="mlstm_scan",
    )(q, k, p, p, q, k, p, p, gate_b_p)


def _gla_chunk(d, q, k, v, sm, wa, ba, s_ref, si):
    L = GLA_CHUNK
    row = lax.broadcasted_iota(jnp.int32, (L, L), 0)
    col = lax.broadcasted_iota(jnp.int32, (L, L), 1)
    mask = col <= row if d == 0 else col >= row
    tri = mask.astype(F32)
    lane_k = lax.broadcasted_iota(jnp.int32, (1, GLA_HEADS * GLA_DK), 1)
    lane_v = lax.broadcasted_iota(jnp.int32, (1, GLA_HEADS * GLA_DV), 1)
    eye = (lax.broadcasted_iota(jnp.int32, (LANE, LANE), 0)
           == lax.broadcasted_iota(jnp.int32, (LANE, LANE), 1)).astype(F32)

    pre = jnp.dot(sm, wa, precision=HIGHEST, preferred_element_type=F32) + ba
    loga = _log_sigmoid(pre) * (1.0 / GLA_TAU)
    bc = jnp.dot(tri, loga, precision=HIGHEST, preferred_element_type=F32)
    ref_row = bc[L // 2 - 1:L // 2]
    b_end = bc[L - 1:L] if d == 0 else bc[0:1]

    q = q * (GLA_DK ** -0.5)
    v = v.astype(BF16)
    q_in = (q * jnp.exp(bc - ref_row))
    k_in = (k * jnp.exp(ref_row - bc)).astype(BF16)
    q_st = (q * jnp.exp(bc)).astype(BF16)
    k_st = (k * jnp.exp(b_end - bc)).astype(BF16)

    blockdiag = (lax.broadcasted_iota(jnp.int32, s_ref.shape[1:], 0) // GLA_DK
                 == lax.broadcasted_iota(jnp.int32, s_ref.shape[1:], 1) // GLA_DV)
    s_old = s_ref[si]
    o = jnp.dot(q_st, jnp.where(blockdiag, s_old, 0.0).astype(BF16), preferred_element_type=F32)
    for h in range(GLA_HEADS):
        qh = jnp.where(lane_k // GLA_DK == h, q_in, 0.0).astype(BF16)
        att = jnp.where(mask, _nt_dot(qh, k_in), 0.0).astype(BF16)
        oh = jnp.dot(att, v, preferred_element_type=F32)
        o = o + jnp.where(lane_v // GLA_DV == h, oh, 0.0)

    decay_col = jnp.exp(_nt_dot(eye, jnp.broadcast_to(b_end, (8, LANE)), precision=HIGHEST)[:, 0:1])
    s_ref[si] = decay_col * s_old + _tn_dot(k_st, v)
    return o


def _gla_scan_kernel(qf_ref, kf_ref, vf_ref, smf_ref, qb_ref, kb_ref, vb_ref, smb_ref, wa_ref, ba_ref,
                     of_ref, ob_ref, s_ref):
    @pl.when(pl.program_id(0) == 0)
    def _():
        s_ref[...] = jnp.zeros_like(s_ref)

    nb = qf_ref.shape[0]
    streams = ((0, qf_ref, kf_ref, vf_ref, smf_ref, of_ref), (1, qb_ref, kb_ref, vb_ref, smb_ref, ob_ref))
    for d, q_ref, k_ref, v_ref, sm_ref, o_ref in streams:
        for bi in range(nb):
            o_ref[bi] = _gla_chunk(d, q_ref[bi], k_ref[bi], v_ref[bi], sm_ref[bi], wa_ref[d], ba_ref[d],
                                   s_ref, d * nb + bi)


def _gla_scan(p, wa_p, ba_p):
    b, t, _ = p.shape
    nc = t // GLA_CHUNK
    chunk = functools.partial(_scan_chunk, n_ctx_chunks=TM // GLA_CHUNK, n_chunks=nc)
    kw, vw = GLA_HEADS * GLA_DK, GLA_HEADS * GLA_DV
    blk = lambda d, w, col: pl.BlockSpec((b, GLA_CHUNK, w), lambda s: (0, chunk(d, s), col // w))
    stream = lambda d: [blk(d, kw, C_GQ), blk(d, kw, C_GK), blk(d, vw, C_GV), blk(d, LANE, C_SMF + d * LANE)]
    shp = jax.ShapeDtypeStruct((b, t, vw), F32)
    return pl.pallas_call(
        _gla_scan_kernel,
        grid=(nc,),
        in_specs=stream(0) + stream(1) + [pl.BlockSpec((2, LANE, kw), lambda s: (0, 0, 0)),
                                          pl.BlockSpec((2, 1, kw), lambda s: (0, 0, 0))],
        out_specs=[blk(0, vw, 0), blk(1, vw, 0)],
        out_shape=[shp, shp],
        scratch_shapes=[pltpu.VMEM((2 * b, kw, vw), F32)],
        compiler_params=_cparams(("arbitrary",)),
        name="gla_scan",
    )(p, p, p, p, p, p, p, p, wa_p, ba_p)


def _outproj_kernel(ac_ref, al_ref, mhf_ref, mhb_ref, xc_ref, mo_ref, gof_ref, gob_ref, gr_ref, x_ref, mod_ref,
                    mnw_ref, msk_ref, gnw_ref, wout_ref, n2w_ref, wr_ref, br_ref,
                    xo_ref, h2_ref, route_ref):
    grp64 = (lax.broadcasted_iota(jnp.int32, (ML_W, ML_W), 0) // ML_DH
             == lax.broadcasted_iota(jnp.int32, (ML_W, ML_W), 1) // ML_DH).astype(F32) * (1.0 / ML_DH)

    def head_norm(x, w):
        ms = jnp.dot(x * x, grp64, precision=HIGHEST, preferred_element_type=F32)
        return x * lax.rsqrt(ms + EPS) * w

    m_l = jax.nn.sigmoid(mo_ref[0]) * (head_norm(mhf_ref[0] + mhb_ref[0], mnw_ref[...])
                                       + msk_ref[...] * xc_ref[0])
    g_l = head_norm(gof_ref[0] + gob_ref[0], gnw_ref[...]) * _silu(gr_ref[0])
    na = MLA_HEADS * MLA_V
    a = jnp.where(pl.program_id(1) == 0, ac_ref[0], al_ref[0])
    res = (jnp.dot(a, wout_ref[0:na], preferred_element_type=F32)
           + jnp.dot(m_l.astype(BF16), wout_ref[na:na + ML_W], preferred_element_type=F32)
           + jnp.dot(g_l.astype(BF16), wout_ref[na + ML_W:], preferred_element_type=F32))
    mod = mod_ref[0]
    x = x_ref[0] + mod[2:3] * res
    xo_ref[0] = x
    h2 = (x * lax.rsqrt(jnp.mean(x * x, axis=-1, keepdims=True) + EPS) * n2w_ref[...]
          * (1.0 + mod[4:5]) + mod[3:4])
    half = h2.shape[-1] // 2
    h2_ref[0] = _pack_bf16_pair(h2[:, :half], h2[:, half:])

    logits = jnp.dot(h2, wr_ref[...], precision=HIGHEST, preferred_element_type=F32) + br_ref[...]
    lane = lax.broadcasted_iota(jnp.int32, (1, LANE), 1)
    lane_f = lane.astype(F32)
    neg = -jnp.inf
    gl = jnp.where(lane < R_EXP, logits, neg)
    gmax = jnp.max(gl, axis=-1, keepdims=True)
    g_w = 1.0 / jnp.sum(jnp.exp(gl - gmax), axis=-1, keepdims=True)
    g_i = jnp.min(jnp.where(gl == gmax, lane_f, float(LANE)), axis=-1, keepdims=True)
    grp_of_lane = ((lane - R_EXP) // EXP_PER_GROUP).astype(F32)
    in_grp = (lane >= R_EXP) & (lane < R_EXP + N_EXPERTS) & (grp_of_lane == g_i)
    el = jnp.where(in_grp, logits, neg)
    m1 = jnp.max(el, axis=-1, keepdims=True)
    i1 = jnp.min(jnp.where(el == m1, lane_f, float(LANE)), axis=-1, keepdims=True)
    el2 = jnp.where(lane_f == i1, neg, el)
    m2 = jnp.max(el2, axis=-1, keepdims=True)
    i2 = jnp.min(jnp.where(el2 == m2, lane_f, float(LANE)), axis=-1, keepdims=True)
    p2 = jnp.exp(m2 - m1)
    w1 = g_w / (1.0 + p2)
    route_ref[0] = jnp.where(lane == RT_E1, i1 - R_EXP, jnp.where(lane == RT_E1 + 1, i2 - R_EXP,
                             jnp.where(lane == RT_W1, w1, jnp.where(lane == RT_W1 + 1, p2 * w1, 0.0))))


def _outproj(a_ctx, a_lat, mh, xconv, p, go, xs, mods, ml_norm_w, ml_skip, gla_norm_w, w_out_b, norm2_w, wr_p, br_p):
    b, t, d = xs.shape
    nt = t // TM
    full = lambda shape: pl.BlockSpec(shape, lambda bi, i: (0,) * len(shape))
    tok = lambda w, cb=0: pl.BlockSpec((1, TM, w), lambda bi, i: (bi, i, cb))
    dirblk = lambda dd: tok(ML_W)
    return pl.pallas_call(
        _outproj_kernel,
        grid=(b, nt),
        in_specs=[
            pl.BlockSpec((1, TM, MLA_HEADS * MLA_V), lambda bi, i: (bi, 0, 0)),
            pl.BlockSpec((1, TM, MLA_HEADS * MLA_V), lambda bi, i: (bi, jnp.maximum(i - 1, 0), 0)),
            dirblk(0), dirblk(1), tok(ML_W), tok(ML_W, C_MO // ML_W),
            dirblk(0), dirblk(1), tok(ML_W, C_GR // ML_W), tok(d),
            pl.BlockSpec((1, 6, d), lambda bi, i: (_mod_row(bi, i), 0, 0)),
            full((1, ML_W)), full((1, ML_W)), full((1, ML_W)), full((d, d)), full((1, d)),
            full((d, LANE)), full((1, LANE)),
        ],
        out_specs=[tok(d), tok(d // 2), tok(LANE)],
        out_shape=[jax.ShapeDtypeStruct((b, t, d), F32), jax.ShapeDtypeStruct((b, t, d // 2), jnp.uint32),
                   jax.ShapeDtypeStruct((b, t, LANE), F32)],
        compiler_params=_cparams(("parallel", "parallel")),
        name="out_proj_router",
    )(a_ctx, a_lat, mh[0], mh[1], xconv, p, go[0], go[1], p, xs, mods, ml_norm_w.reshape(1, -1), ml_skip.reshape(1, -1),
      gla_norm_w.reshape(1, -1), w_out_b, norm2_w.reshape(1, -1), wr_p, br_p)


def _dispatch(route, n_tiles):
    n = route.shape[0]
    flat = route[:, RT_E1:RT_E1 + 2].astype(jnp.int32).reshape(-1)
    onehot = (flat[:, None] == jnp.arange(N_EXPERTS, dtype=jnp.int32)[None, :]).astype(jnp.int32)
    csum = jnp.cumsum(onehot, axis=0)
    rank = jnp.sum(csum * onehot, axis=1) - 1
    padded = (csum[-1] + TM - 1) // TM * TM
    ends = jnp.cumsum(padded)
    pos = (ends - padded)[flat] + rank
    src = jnp.zeros((n_tiles * TM,), jnp.int32).at[pos].set(jnp.arange(2 * n, dtype=jnp.int32) // 2)
    tile_start = jnp.arange(n_tiles, dtype=jnp.int32) * TM
    tile_exp = jnp.minimum(jnp.sum((ends[None, :] <= tile_start[:, None]).astype(jnp.int32), axis=1),
                           N_EXPERTS - 1)
    tile_on = (tile_start < ends[-1]).astype(jnp.int32)
    pos = pos.reshape(n // TM, TM, 2)
    return (src.reshape(n_tiles, 1, TM), tile_exp, tile_on,
            pos[:, :, 0].reshape(n // TM, 1, TM), pos[:, :, 1].reshape(n // TM, 1, TM))


def _gather_rows(src_hbm, idx_ref, dst, sem):
    def body(j, carry):
        pltpu.make_async_copy(src_hbm.at[pl.ds(idx_ref[0, 0, j], 1)], dst.at[pl.ds(j, 1)], sem).start()
        return carry

    lax.fori_loop(0, TM, body, 0, unroll=8)


def _wait_rows(src_hbm, dst, sem):
    pltpu.make_async_copy(src_hbm.at[pl.ds(0, TM)], dst, sem).wait()


def _pack_bf16_pair(lo, hi):
    lo_b = lax.bitcast_convert_type(lo.astype(BF16).astype(F32), jnp.uint32) >> 16
    hi_b = lax.bitcast_convert_type(hi.astype(BF16).astype(F32), jnp.uint32) & jnp.uint32(0xFFFF0000)
    return hi_b | lo_b


def _unpack_bf16_pair(w):
    return (lax.bitcast_convert_type(w << 16, F32),
            lax.bitcast_convert_type(w & jnp.uint32(0xFFFF0000), F32))


def _experts_kernel(texp_ref, ton_ref, idx_ref, idxn_ref, h2_hbm, wg_ref, wu_ref, wd_ref, y_ref,
                    buf, wgu_b, wd_b, sem):
    r = pl.program_id(0)
    slot = r % 2
    half = wgu_b.shape[0] // 2

    @pl.when(r == 0)
    def _():
        _gather_rows(h2_hbm, idx_ref, buf.at[0], sem.at[0])

    last = pl.num_programs(0) - 1

    @pl.when(jnp.logical_and(r < last, ton_ref[jnp.minimum(r + 1, last)] == 1))
    def _():
        _gather_rows(h2_hbm, idxn_ref, buf.at[1 - slot], sem.at[1 - slot])

    @pl.when(ton_ref[r] == 1)
    def _():
        @pl.when(jnp.logical_or(r == 0, texp_ref[r] != texp_ref[jnp.maximum(r - 1, 0)]))
        def _():
            wgu_b[:, :D_EXPERT] = wg_ref[0, 0].astype(BF16)
            wgu_b[:, D_EXPERT:] = wu_ref[0, 0].astype(BF16)
            wd_b[...] = wd_ref[0, 0].astype(BF16)

        _wait_rows(h2_hbm, buf.at[slot], sem.at[slot])
        x_lo, x_hi = _unpack_bf16_pair(buf[slot])
        gu = (jnp.dot(x_lo.astype(BF16), wgu_b[0:half], preferred_element_type=F32)
              + jnp.dot(x_hi.astype(BF16), wgu_b[half:], preferred_element_type=F32))
        act = (_silu(gu[:, :D_EXPERT]) * gu[:, D_EXPERT:]).astype(BF16)
        y = jnp.dot(act, wd_b[...], preferred_element_type=F32)
        y_ref[...] = _pack_bf16_pair(y[:, :half], y[:, half:])

    @pl.when(ton_ref[r] == 0)
    def _():
        y_ref[...] = jnp.zeros_like(y_ref)


def _experts(h2p, src, tile_exp, tile_on, w_gate, w_up, w_down, layer):
    n, dh = h2p.shape
    d = 2 * dh
    n_tiles = src.shape[0]
    idx = lambda nxt: pl.BlockSpec((1, 1, TM), lambda r, te, to: (jnp.minimum(r + nxt, n_tiles - 1), 0, 0),
                                   memory_space=pltpu.SMEM)
    wspec = lambda shape: pl.BlockSpec((1, 1) + shape, lambda r, te, to: (layer, te[r], 0, 0))
    return pl.pallas_call(
        _experts_kernel,
        grid_spec=pltpu.PrefetchScalarGridSpec(
            num_scalar_prefetch=2,
            grid=(n_tiles,),
            in_specs=[
                idx(0), idx(1),
                pl.BlockSpec(memory_space=pl.ANY),
                wspec((d, D_EXPERT)), wspec((d, D_EXPERT)), wspec((D_EXPERT, d)),
            ],
            out_specs=pl.BlockSpec((TM, dh), lambda r, te, to: (r, 0)),
            scratch_shapes=[pltpu.VMEM((2, TM, dh), jnp.uint32), pltpu.VMEM((d, 2 * D_EXPERT), BF16),
                            pltpu.VMEM((D_EXPERT, d), BF16), pltpu.SemaphoreType.DMA((2,))],
        ),
        out_shape=jax.ShapeDtypeStruct((n_tiles * TM, dh), jnp.uint32),
        compiler_params=_cparams(("arbitrary",)),
        name="moe_experts",
    )(tile_exp, tile_on, src, src, h2p, w_gate, w_up, w_down)


def _combine_kernel(p1_ref, p2_ref, p1n_ref, p2n_ref, route_ref, x_ref, mod_ref, y_hbm, o_ref, buf, sem):
    g = pl.program_id(0)
    slot = g % 2
    half = buf.shape[-1]

    def gather(pa, pb, s):
        _gather_rows(y_hbm, pa, buf.at[s, 0], sem.at[s])
        _gather_rows(y_hbm, pb, buf.at[s, 1], sem.at[s])

    @pl.when(g == 0)
    def _():
        gather(p1_ref, p2_ref, 0)

    @pl.when(g + 1 < pl.num_programs(0))
    def _():
        gather(p1n_ref, p2n_ref, 1 - slot)

    _wait_rows(y_hbm, buf.at[slot, 0], sem.at[slot])
    _wait_rows(y_hbm, buf.at[slot, 1], sem.at[slot])
    route = route_ref[...]
    w1, w2 = route[:, RT_W1:RT_W1 + 1], route[:, RT_W1 + 1:RT_W1 + 2]
    gate = mod_ref[0][5:6]
    for part, (y1, y2) in enumerate(zip(_unpack_bf16_pair(buf[slot, 0]), _unpack_bf16_pair(buf[slot, 1]))):
        cols = slice(part * half, (part + 1) * half)
        o_ref[:, cols] = x_ref[:, cols] + gate[:, cols] * (w1 * y1 + w2 * y2)


def _combine(y, pos1, pos2, route, xs2, mods, nt):
    n, d = xs2.shape
    n_tok_tiles = n // TM
    idx = lambda nxt: pl.BlockSpec((1, 1, TM), lambda g: (jnp.minimum(g + nxt, n_tok_tiles - 1), 0, 0),
                                   memory_space=pltpu.SMEM)
    tok = lambda w: pl.BlockSpec((TM, w), lambda g: (g, 0))
    return pl.pallas_call(
        _combine_kernel,
        grid=(n_tok_tiles,),
        in_specs=[
            idx(0), idx(0), idx(1), idx(1), tok(LANE), tok(d),
            pl.BlockSpec((1, 6, d), lambda g: (_mod_row(g // nt, g % nt), 0, 0)),
            pl.BlockSpec(memory_space=pl.ANY),
        ],
        out_specs=tok(d),
        out_shape=jax.ShapeDtypeStruct((n, d), F32),
        scratch_shapes=[pltpu.VMEM((2, 2, TM, d // 2), jnp.uint32), pltpu.SemaphoreType.DMA((2,))],
        compiler_params=_cparams(("arbitrary",)),
        name="moe_combine",
    )(pos1, pos2, pos1, pos2, route, xs2, mods, y)


def _moe(h2p, route, xs, mods, w_gate, w_up, w_down, layer):
    b, t, d = xs.shape
    n = b * t
    n_tiles = 2 * n // TM + N_EXPERTS
    route2 = route.reshape(n, LANE)
    src, tile_exp, tile_on, pos1, pos2 = _dispatch(route2, n_tiles)
    y = _experts(h2p.reshape(n, d // 2), src, tile_exp, tile_on, w_gate, w_up, w_down, layer)
    return _combine(y, pos1, pos2, route2, xs.reshape(n, d), mods, t // TM).reshape(b, t, d)


def _rope_tables(n_ctx, n_lat):
    rows = n_lat // GRID_W
    row = jnp.broadcast_to(jnp.arange(rows, dtype=F32)[:, None], (rows, GRID_W)).reshape(-1)
    col = jnp.broadcast_to(jnp.arange(GRID_W, dtype=F32)[None, :], (rows, GRID_W)).reshape(-1)
    n_freq = MLA_ROPE // 4
    inv = ROPE_THETA ** (-jnp.arange(n_freq, dtype=F32) / n_freq)
    ang = jnp.concatenate([row[:, None] * inv, col[:, None] * inv], axis=-1)
    cos, sin = jnp.cos(ang), jnp.sin(ang)
    half = MLA_ROPE // 2
    z = lambda w: jnp.zeros((n_lat, w), F32)
    o = lambda w: jnp.ones((n_lat, w), F32)
    tail = LANE - MLA_QK
    cos_t = jnp.concatenate([o(MLA_NOPE), cos, cos, o(tail)], axis=-1)
    sa_t = jnp.concatenate([z(MLA_NOPE + half), sin, z(tail)], axis=-1)
    sb_t = jnp.concatenate([z(MLA_NOPE), -sin, z(half + tail)], axis=-1)
    ctx1 = jnp.ones((n_ctx, LANE), F32)
    ctx0 = jnp.zeros((n_ctx, LANE), F32)
    return (jnp.concatenate([ctx1, cos_t], 0), jnp.concatenate([ctx0, sa_t], 0),
            jnp.concatenate([ctx0, sb_t], 0))


def _pad_cols(a, width):
    return jnp.pad(a, [(0, 0)] * (a.ndim - 1) + [(0, width - a.shape[-1])])


def _layer_weights(w_in, w_uq, w_ukv, q_norm_w, k_norm_w, ml_conv_w, ml_wq, ml_wk, ml_gate_b,
                   gla_wa, gla_ba, w_out, w_grp, b_grp, w_erouter, b_erouter):
    d = w_in.shape[0]
    o = np.cumsum((0, Q_LORA, KV_LORA, MLA_ROPE, ML_W, ML_W, ML_W, 4 * ML_HEADS, GLA_HEADS * GLA_DK,
                   GLA_HEADS * GLA_DK, GLA_HEADS * GLA_DV, GLA_HEADS * GLA_DV, 2 * GLA_LR))
    seg = lambda j: w_in[:, o[j]:o[j + 1]]
    cq, ckv, kr, mx, mv, mo, mg, gq, gk, gv, gr, ga = (seg(j) for j in range(12))
    z = lambda w: jnp.zeros((d, w), F32)

    def small(di):
        return jnp.concatenate([mg[:, di * 8:(di + 1) * 8], ga[:, di * GLA_LR:(di + 1) * GLA_LR],
                                z(SM_KR - SM_GA - GLA_LR), kr, z(LANE - SM_KR - MLA_ROPE)], axis=-1)

    w_in_p = jnp.concatenate([cq, mx, mv, mo, gv, gr, ckv, gq, gk, small(0), small(1)], axis=-1).astype(BF16)

    wuq_p = _pad_cols(w_uq.reshape(Q_LORA, MLA_HEADS, MLA_QK), LANE).reshape(Q_LORA, -1).astype(BF16)
    ukv = w_ukv.reshape(KV_LORA, MLA_HEADS, MLA_NOPE + MLA_V)
    wuk_p = _pad_cols(ukv[..., :MLA_NOPE], LANE).reshape(KV_LORA, -1).astype(BF16)
    wuv_p = _pad_cols(ukv[..., MLA_NOPE:], LANE).reshape(KV_LORA, -1).astype(BF16)
    qn_p = _pad_cols(q_norm_w.reshape(1, -1), LANE)
    kn_p = _pad_cols(k_norm_w.reshape(1, -1), LANE)

    conv_w8 = jnp.pad(ml_conv_w, ((0, 8 - ML_CONV), (0, 0)))
    bd = lambda w: jax.scipy.linalg.block_diag(*[w[h] for h in range(ML_HEADS)])
    wq_bd = (bd(ml_wq) * (ML_DH ** -0.5)).astype(BF16)
    wk_bd = bd(ml_wk).astype(BF16)
    gate_b_p = _pad_cols(ml_gate_b.reshape(2, 1, 2 * ML_HEADS), LANE)

    wa_p = jnp.pad(gla_wa, ((0, 0), (SM_GA, LANE - SM_GA - GLA_LR), (0, 0)))
    ba_p = gla_ba.reshape(2, 1, -1)

    wr_p = _pad_cols(jnp.concatenate([w_grp, w_erouter], axis=-1), LANE)
    br_p = _pad_cols(jnp.concatenate([b_grp, b_erouter]).reshape(1, -1), LANE)
    return dict(w_in_p=w_in_p, wuq_p=wuq_p, wuk_p=wuk_p, wuv_p=wuv_p, qn_p=qn_p, kn_p=kn_p,
                conv_w8=conv_w8, wq_bd=wq_bd, wk_bd=wk_bd, gate_b_p=gate_b_p, wa_p=wa_p, ba_p=ba_p,
                w_out_b=w_out.astype(BF16), wr_p=wr_p, br_p=br_p)


def kernel(x, c, ctx, c_ctx, w_mod, b_mod, norm1_w, w_in, q_a_norm, w_uq, kv_a_norm, w_ukv,
           q_norm_w, k_norm_w, ml_conv_w, ml_conv_b, ml_wq, ml_wk, ml_gate_b, ml_norm_w, ml_skip,
           gla_wa, gla_ba, gla_norm_w, w_out, norm2_w, w_grp, b_grp, w_erouter, b_erouter,
           w_gate, w_up, w_down):
    b, s, d = x.shape
    n_ctx = ctx.shape[1]
    depth = w_mod.shape[0]
    assert n_ctx == TM and s % TM == 0 and b == 2

    cc = jnp.concatenate([c, c_ctx[None, :], jnp.zeros((8 - b - 1, d), F32)], axis=0)
    mods_all = _mods(cc, w_mod, b_mod).reshape(depth, 8, 6, d)
    cos_t, sa_t, sb_t = _rope_tables(n_ctx, s)
    xs = jnp.concatenate([ctx, x], axis=1)

    for l in range(depth):
        w = _layer_weights(w_in[l], w_uq[l], w_ukv[l], q_norm_w[l], k_norm_w[l], ml_conv_w[l],
                           ml_wq[l], ml_wk[l], ml_gate_b[l], gla_wa[l], gla_ba[l], w_out[l],
                           w_grp[l], b_grp[l], w_erouter[l], b_erouter[l])
        mods = mods_all[l]
        p = _inproj(xs, mods, norm1_w[l], w["w_in_p"])
        q, k, v = _mla_prep(p, q_a_norm[l], w["wuq_p"], kv_a_norm[l], w["wuk_p"], w["wuv_p"],
                            w["qn_p"], w["kn_p"], cos_t, sa_t, sb_t)
        a_lat = _attention_lat(q, k, v)
        a_ctx = _attention_ctx(q, k, v) if l < depth - 1 else jnp.zeros((b, TM, MLA_HEADS * MLA_V), BF16)
        xconv, mq, mk = _ml_prep(p, w["conv_w8"], ml_conv_b[l], w["wq_bd"], w["wk_bd"])
        mh = _ml_scan(mq, mk, p, w["gate_b_p"])
        go = _gla_scan(p, w["wa_p"], w["ba_p"])
        xs, h2, route = _outproj(a_ctx, a_lat, mh, xconv, p, go, xs, mods, ml_norm_w[l], ml_skip[l], gla_norm_w[l],
                                w["w_out_b"], norm2_w[l], w["wr_p"], w["br_p"])
        xs = _moe(h2, route, xs, mods, w_gate, w_up, w_down, l)
    return xs[:, n_ctx:, :]
```

```python
import functools

import jax
import jax.numpy as jnp
import numpy as np
from jax import lax
from jax.experimental import pallas as pl
from jax.experimental.pallas import tpu as pltpu

F32 = jnp.float32
BF16 = jnp.bfloat16
HIGHEST = lax.Precision.HIGHEST

EPS = 1e-6
GRID_W = 64
ROPE_THETA = 10000.0

MLA_HEADS = 8
MLA_NOPE = 64
MLA_ROPE = 32
MLA_QK = MLA_NOPE + MLA_ROPE
MLA_V = 64
Q_LORA = 256
KV_LORA = 128

ML_HEADS = 4
ML_DH = 64
ML_W = ML_HEADS * ML_DH
ML_CONV = 5

GLA_HEADS = 4
GLA_DK = 32
GLA_DV = 64
GLA_LR = 16
GLA_TAU = 16.0

N_GROUPS = 4
EXP_PER_GROUP = 8
N_EXPERTS = N_GROUPS * EXP_PER_GROUP
D_EXPERT = 256

LANE = 128
TM = 256
ML_CHUNK = 256
GLA_CHUNK = 128
VMEM_LIMIT = 56 * 1024 * 1024

C_CQ, C_MX, C_MV, C_MO, C_GV, C_GR = 0, 256, 512, 768, 1024, 1280
C_CKV, C_GQ, C_GK, C_SMF, C_SMB = 1536, 1664, 1792, 1920, 2048
D_INP = 2176
SM_GATE = 0
SM_GA = 8
SM_KR = 64
R_GRP = 0
R_EXP = 4
RT_E1 = 0
RT_W1 = 2
GATHER_UNROLL = 8


def _cparams(sem):
    return pltpu.CompilerParams(dimension_semantics=sem, vmem_limit_bytes=VMEM_LIMIT)


def _silu(x):
    return x * jax.nn.sigmoid(x)


def _log_sigmoid(x):
    return -(jnp.maximum(-x, 0.0) + jnp.log1p(jnp.exp(-jnp.abs(x))))


def _nt_dot(a, b, **kw):
    return lax.dot_general(a, b, (((1,), (1,)), ((), ())), preferred_element_type=F32, **kw)


def _tn_dot(a, b, **kw):
    return lax.dot_general(a, b, (((0,), (0,)), ((), ())), preferred_element_type=F32, **kw)


def _mods_kernel(cc_ref, w_ref, b_ref, o_ref):
    a = _silu(cc_ref[...])
    o_ref[0] = jnp.dot(a, w_ref[0], precision=HIGHEST, preferred_element_type=F32) + b_ref[0]


def _mods(cc, w_mod, b_mod):
    depth, d, d6 = w_mod.shape
    nb = 1536
    return pl.pallas_call(
        _mods_kernel,
        grid=(depth, d6 // nb),
        in_specs=[
            pl.BlockSpec((8, d), lambda l, j: (0, 0)),
            pl.BlockSpec((1, d, nb), lambda l, j: (l, 0, j)),
            pl.BlockSpec((1, 1, nb), lambda l, j: (l, 0, j)),
        ],
        out_specs=pl.BlockSpec((1, 8, nb), lambda l, j: (l, 0, j)),
        out_shape=jax.ShapeDtypeStruct((depth, 8, d6), F32),
        compiler_params=_cparams(("arbitrary", "arbitrary")),
        name="adaln_mods",
    )(cc, w_mod, b_mod.reshape(depth, 1, d6))


def _mod_row(b, i):
    return jnp.where(i == 0, 2, b)


def _inproj_kernel(x_ref, mod_ref, nw_ref, w_ref, o_ref):
    x = x_ref[0]
    y = x * lax.rsqrt(jnp.mean(x * x, axis=-1, keepdims=True) + EPS) * nw_ref[...]
    mod = mod_ref[0]
    h = y * (1.0 + mod[1:2]) + mod[0:1]
    o_ref[0] = jnp.dot(h.astype(BF16), w_ref[...], preferred_element_type=F32)


def _inproj(xs, mods, norm_w, w_in_p):
    b, t, d = xs.shape
    nt = t // TM
    return pl.pallas_call(
        _inproj_kernel,
        grid=(b, nt),
        in_specs=[
            pl.BlockSpec((1, TM, d), lambda bi, i: (bi, i, 0)),
            pl.BlockSpec((1, 6, d), lambda bi, i: (_mod_row(bi, i), 0, 0)),
            pl.BlockSpec((1, d), lambda bi, i: (0, 0)),
            pl.BlockSpec((d, D_INP), lambda bi, i: (0, 0)),
        ],
        out_specs=pl.BlockSpec((1, TM, D_INP), lambda bi, i: (bi, i, 0)),
        out_shape=jax.ShapeDtypeStruct((b, t, D_INP), F32),
        compiler_params=_cparams(("parallel", "parallel")),
        name="in_proj",
    )(xs, mods, norm_w.reshape(1, d), w_in_p)


def _mla_prep_kernel(cq_ref, ckv_ref, sm_ref, qan_ref, wuq_ref, kvan_ref, wuk_ref, wuv_ref,
                     qn_ref, kn_ref, cos_ref, sa_ref, sb_ref, q_ref, k_ref, v_ref):
    cq = cq_ref[0]
    cqn = cq * lax.rsqrt(jnp.mean(cq * cq, axis=-1, keepdims=True) + EPS) * qan_ref[...]
    qall = jnp.dot(cqn.astype(BF16), wuq_ref[...], preferred_element_type=F32)
    ckv = ckv_ref[0]
    ckvn = (ckv * lax.rsqrt(jnp.mean(ckv * ckv, axis=-1, keepdims=True) + EPS)
            * kvan_ref[...]).astype(BF16)
    kall = jnp.dot(ckvn, wuk_ref[...], preferred_element_type=F32)
    vall = jnp.dot(ckvn, wuv_ref[...], preferred_element_type=F32)
    lane = lax.broadcasted_iota(jnp.int32, (1, LANE), 1)
    kr = jnp.where((lane >= SM_KR) & (lane < SM_KR + MLA_ROPE), sm_ref[0], 0.0)
    cos, sin = cos_ref[...], sa_ref[...] - sb_ref[...]
    r_i = lax.broadcasted_iota(jnp.int32, (LANE, LANE), 0)
    c_i = lax.broadcasted_iota(jnp.int32, (LANE, LANE), 1)
    half = MLA_ROPE // 2
    first = (c_i >= MLA_NOPE) & (c_i < MLA_NOPE + half)
    second = (c_i >= MLA_NOPE + half) & (c_i < MLA_QK)
    rot = jnp.where(first & (r_i == c_i + half), -1.0,
                    jnp.where(second & (r_i == c_i - half), 1.0, 0.0)).astype(BF16)

    def rope(x):
        return x * cos + jnp.dot(x.astype(BF16), rot, preferred_element_type=F32) * sin

    def head_norm(x, w):
        return x * lax.rsqrt(jnp.sum(x * x, axis=-1, keepdims=True) * (1.0 / MLA_QK) + EPS) * w

    for h in range(MLA_HEADS):
        sl = slice(h * LANE, (h + 1) * LANE)
        qh = rope(head_norm(qall[:, sl], qn_ref[...]))
        q_ref[0, h] = (qh * Q_SCALE).astype(BF16)
        kh = rope(head_norm(kall[:, sl] + kr, kn_ref[...]))
        k_ref[0, h] = jnp.where(lane == ATT_SHIFT_LANE, 1.0, kh).astype(BF16)
        v_ref[0, h] = jnp.where(lane == MLA_V, 1.0, vall[:, sl]).astype(BF16)


def _mla_prep(p, q_a_norm, wuq_p, kv_a_norm, wuk_p, wuv_p, qn_p, kn_p, cos_t, sa_t, sb_t):
    b, t, _ = p.shape
    nt = t // TM
    hw = MLA_HEADS * LANE
    full = lambda shape: pl.BlockSpec(shape, lambda bi, i: (0,) * len(shape))
    tab = pl.BlockSpec((TM, LANE), lambda bi, i: (i, 0))
    out = pl.BlockSpec((1, MLA_HEADS, TM, LANE), lambda bi, i: (bi, 0, i, 0))
    shp = jax.ShapeDtypeStruct((b, MLA_HEADS, t, LANE), BF16)
    return pl.pallas_call(
        _mla_prep_kernel,
        grid=(b, nt),
        in_specs=[
            pl.BlockSpec((1, TM, Q_LORA), lambda bi, i: (bi, i, C_CQ // Q_LORA)),
            pl.BlockSpec((1, TM, KV_LORA), lambda bi, i: (bi, i, C_CKV // KV_LORA)),
            pl.BlockSpec((1, TM, LANE), lambda bi, i: (bi, i, C_SMF // LANE)),
            full((1, Q_LORA)), full((Q_LORA, hw)), full((1, KV_LORA)),
            full((KV_LORA, hw)), full((KV_LORA, hw)), full((1, LANE)), full((1, LANE)),
            tab, tab, tab,
        ],
        out_specs=[out, out, out],
        out_shape=[shp, shp, shp],
        compiler_params=_cparams(("parallel", "parallel")),
        name="mla_prep",
    )(p, p, p, q_a_norm.reshape(1, -1), wuq_p, kv_a_norm.reshape(1, -1), wuk_p, wuv_p,
      qn_p, kn_p, cos_t, sa_t, sb_t)


Q_SCALE = float(MLA_QK ** -0.5 * np.log2(np.e))
ATT_HP = 2
ATT_NQ = 4
ATT_TK = 2048
ATT_SHIFT_LANE = MLA_QK
ATT_SAFE_MAX = 2.0 ** 100


def _softmax_step(q, kb, vb, m, acc):
    s = _nt_dot(q, kb)
    m_new = jnp.max(s, axis=-1, keepdims=True)
    if m is None:
        return m_new, jnp.dot(jnp.exp2((s - m_new).astype(BF16)), vb, preferred_element_type=F32)
    m_new = jnp.maximum(m, m_new)
    p = jnp.exp2((s - m_new).astype(BF16))
    return m_new, acc * jnp.exp2(m - m_new) + jnp.dot(p, vb, preferred_element_type=F32)


def _attn_ctx_kernel(q_ref, k_ref, v_ref, o_ref):
    outs = []
    for h in range(MLA_HEADS):
        _, acc = _softmax_step(q_ref[0, h], k_ref[0, h], v_ref[0, h], None, None)
        outs.append(acc[:, :MLA_V] / acc[:, MLA_V:MLA_V + 1])
    o_ref[0] = jnp.concatenate(outs, axis=-1).astype(BF16)


def _attention_ctx(q, k, v):
    b, h, _, _ = q.shape
    blk = pl.BlockSpec((1, h, TM, LANE), lambda bi: (bi, 0, 0, 0))
    return pl.pallas_call(
        _attn_ctx_kernel,
        grid=(b,),
        in_specs=[blk, blk, blk],
        out_specs=pl.BlockSpec((1, TM, h * MLA_V), lambda bi: (bi, 0, 0)),
        out_shape=jax.ShapeDtypeStruct((b, TM, h * MLA_V), BF16),
        compiler_params=_cparams(("parallel",)),
        name="mla_attention_ctx",
    )(q, k, v)


def _attn_lat_kernel(*refs, n_blk):
    q_refs, (k_ref, v_ref, o_ref, q_buf, qs_buf) = refs[:ATT_NQ], refs[ATT_NQ:]
    lane = lax.broadcasted_iota(jnp.int32, (1, LANE), 1)

    def kv_block(hh, j):
        off = pl.multiple_of(TM + j * ATT_TK, TM)
        return k_ref[0, hh, pl.ds(off, ATT_TK), :], v_ref[0, hh, pl.ds(off, ATT_TK), :]

    def finish(accs):
        outs = [acc[:, :MLA_V] / acc[:, MLA_V:MLA_V + 1] for acc in accs]
        o_ref[0] = jnp.concatenate(outs, axis=-1).astype(BF16)

    accs = []
    for hh in range(ATT_HP):
        q = jnp.concatenate([qr[0, hh] for qr in q_refs], axis=0)
        q_buf[hh] = q
        kb, vb = k_ref[0, hh, 0:TM, :], v_ref[0, hh, 0:TM, :]
        shift = jnp.max(_nt_dot(q, kb), axis=-1, keepdims=True).astype(BF16)
        qs_buf[hh] = jnp.where(lane == ATT_SHIFT_LANE, -shift, q)
        accs.append(jnp.dot(jnp.exp2(_nt_dot(qs_buf[hh], kb).astype(BF16)), vb, preferred_element_type=F32))

    def fast_body(j, accs):
        new = []
        for hh in range(ATT_HP):
            kb, vb = kv_block(hh, j)
            p = jnp.exp2(_nt_dot(qs_buf[hh], kb).astype(BF16))
            new.append(accs[hh] + jnp.dot(p, vb, preferred_element_type=F32))
        return tuple(new)

    accs = lax.fori_loop(0, n_blk, fast_body, tuple(accs))
    bad = sum(jnp.max(jnp.where(jnp.abs(acc) < ATT_SAFE_MAX, 0.0, 1.0)) for acc in accs)

    @pl.when(bad == 0.0)
    def _():
        finish(accs)

    @pl.when(bad != 0.0)
    def _():
        init = []
        for hh in range(ATT_HP):
            init += _softmax_step(q_buf[hh], k_ref[0, hh, 0:TM, :], v_ref[0, hh, 0:TM, :], None, None)

        def body(j, carry):
            new = []
            for hh in range(ATT_HP):
                new += _softmax_step(q_buf[hh], *kv_block(hh, j), carry[2 * hh], carry[2 * hh + 1])
            return tuple(new)

        carry = lax.fori_loop(0, n_blk, body, tuple(init))
        finish(carry[1::2])


def _attention_lat(q, k, v):
    b, h, t, _ = q.shape
    s = t - TM
    tq = ATT_NQ * TM
    assert s % ATT_TK == 0 and s % tq == 0
    kv = pl.BlockSpec((1, ATT_HP, t, LANE), lambda bi, hp, i: (bi, hp, 0, 0))
    qs = [pl.BlockSpec((1, ATT_HP, TM, LANE), lambda bi, hp, i, u=u: (bi, hp, 1 + ATT_NQ * i + u, 0))
          for u in range(ATT_NQ)]
    return pl.pallas_call(
        functools.partial(_attn_lat_kernel, n_blk=s // ATT_TK),
        grid=(b, h // ATT_HP, s // tq),
        in_specs=qs + [kv, kv],
        out_specs=pl.BlockSpec((1, tq, ATT_HP * MLA_V), lambda bi, hp, i: (bi, i, hp)),
        out_shape=jax.ShapeDtypeStruct((b, s, h * MLA_V), BF16),
        scratch_shapes=[pltpu.VMEM((ATT_HP, tq, LANE), BF16), pltpu.VMEM((ATT_HP, tq, LANE), BF16)],
        compiler_params=_cparams(("parallel", "parallel", "arbitrary")),
        name="mla_attention",
    )(*([q] * ATT_NQ), k, v)


def _ml_prep_kernel(x_ref, prev_ref, next_ref, cw_ref, cb_ref, wq_ref, wk_ref,
                    xc_ref, q_ref, k_ref, *, n_tiles):
    i = pl.program_id(1)
    x = x_ref[0]
    prev = jnp.where(i <= 1, 0.0, prev_ref[0])
    nxt = jnp.where((i == 0) | (i == n_tiles - 1), 0.0, next_ref[0])
    ext = jnp.concatenate([prev, x, nxt], axis=0)
    n_ext = TM + 16
    cw = cw_ref[...]
    acc = jnp.zeros((TM, ML_W), F32) + cb_ref[...]
    for kk in range(ML_CONV):
        sh = (ML_CONV // 2 - kk) % n_ext
        shifted = ext if sh == 0 else pltpu.roll(ext, sh, 0)
        acc = acc + cw[kk:kk + 1] * shifted[8:8 + TM]
    xc = _silu(acc)
    xc_ref[0] = xc
    xb = xc.astype(BF16)
    q_ref[0] = jnp.dot(xb, wq_ref[...], preferred_element_type=F32).astype(BF16)
    k_ref[0] = jnp.dot(xb, wk_ref[...], preferred_element_type=F32).astype(BF16)


def _ml_prep(p, conv_w8, conv_b, wq_bd, wk_bd):
    b, t, _ = p.shape
    nt = t // TM
    r8 = TM // 8
    full = lambda shape: pl.BlockSpec(shape, lambda bi, i: (0,) * len(shape))
    cb = C_MX // ML_W
    blk = pl.BlockSpec((1, TM, ML_W), lambda bi, i: (bi, i, 0))
    return pl.pallas_call(
        functools.partial(_ml_prep_kernel, n_tiles=nt),
        grid=(b, nt),
        in_specs=[
            pl.BlockSpec((1, TM, ML_W), lambda bi, i: (bi, i, cb)),
            pl.BlockSpec((1, 8, ML_W), lambda bi, i: (bi, jnp.maximum(i * r8 - 1, 0), cb)),
            pl.BlockSpec((1, 8, ML_W), lambda bi, i: (bi, jnp.minimum((i + 1) * r8, nt * r8 - 1), cb)),
            full((8, ML_W)), full((1, ML_W)), full((ML_W, ML_W)), full((ML_W, ML_W)),
        ],
        out_specs=[blk, blk, blk],
        out_shape=[jax.ShapeDtypeStruct((b, t, ML_W), F32),
                   jax.ShapeDtypeStruct((b, t, ML_W), BF16),
                   jax.ShapeDtypeStruct((b, t, ML_W), BF16)],
        compiler_params=_cparams(("parallel", "parallel")),
        name="mlstm_prep",
    )(p, p, p, conv_w8, conv_b.reshape(1, ML_W), wq_bd, wk_bd)


def _scan_chunk(d, step, n_ctx_chunks, n_chunks):
    bwd = jnp.where(step < n_ctx_chunks, n_ctx_chunks - 1 - step, n_chunks - 1 - (step - n_ctx_chunks))
    return jnp.where(d == 0, step, bwd)


def _ml_chunk(d, q, k, v, g, s_ref, m_ref, base):
    L = ML_CHUNK
    row = lax.broadcasted_iota(jnp.int32, (L, L), 0)
    col = lax.broadcasted_iota(jnp.int32, (L, L), 1)
    mask = col <= row if d == 0 else col >= row
    tri = mask.astype(F32)
    lane = lax.broadcasted_iota(jnp.int32, (1, LANE), 1)
    eye8 = (lax.broadcasted_iota(jnp.int32, (8, LANE), 0)
            == lax.broadcasted_iota(jnp.int32, (8, LANE), 1)).astype(F32)

    lf = _log_sigmoid(g)
    bc = jnp.dot(tri, lf, precision=HIGHEST, preferred_element_type=F32)
    g_rows = _nt_dot(eye8, g, precision=HIGHEST)
    bc_rows = _nt_dot(eye8, bc, precision=HIGHEST)

    outs = []
    for pair in range(ML_HEADS // 2):
        sl = slice(pair * LANE, (pair + 1) * LANE)
        q_blk, k_blk, v_blk = q[:, sl], k[:, sl], v[:, sl]
        pair_out = []
        for sub in range(2):
            h = pair * 2 + sub
            head_lanes = (lane >= sub * ML_DH) & (lane < (sub + 1) * ML_DH)
            qh = jnp.where(head_lanes, q_blk, jnp.zeros_like(q_blk))
            vs = v_blk if sub == 0 else pltpu.roll(v_blk, ML_DH, 1)
            v_ext = jnp.where(lane < ML_DH, vs, jnp.where(lane == ML_DH, 1.0, 0.0)).astype(BF16)

            li_c = g[:, SM_GATE + h:SM_GATE + h + 1]
            bc_c = bc[:, SM_GATE + 4 + h:SM_GATE + 5 + h]
            li_r = g_rows[h:h + 1, :]
            bc_r = bc_rows[4 + h:5 + h, :]
            m_st = m_ref[base + h][0:1, 0:1]

            dmat = jnp.where(mask, bc_c + (li_r - bc_r), -jnp.inf)
            inter = bc_c + m_st
            m_t = jnp.maximum(inter, jnp.max(dmat, axis=-1, keepdims=True))
            e = jnp.exp(dmat - m_t)
            s = (_nt_dot(qh, k_blk) * e).astype(BF16)
            tot = (jnp.dot(s, v_ext, preferred_element_type=F32)
                   + jnp.exp(inter - m_t) * jnp.dot(qh, s_ref[base + h].astype(BF16),
                                                    preferred_element_type=F32))
            den = tot[:, ML_DH:ML_DH + 1]
            pair_out.append(tot / jnp.maximum(jnp.abs(den), jnp.exp(-m_t)))

            b_end = bc_c[L - 1:L] if d == 0 else bc_c[0:1]
            g_col = b_end - bc_c + li_c
            m_new = jnp.maximum(b_end + m_st, jnp.max(g_col, axis=0, keepdims=True))
            kw = jnp.where(head_lanes, k_blk.astype(F32) * jnp.exp(g_col - m_new), 0.0).astype(BF16)
            s_ref[base + h] = jnp.exp(b_end + m_st - m_new) * s_ref[base + h] + _tn_dot(kw, v_ext)
            m_ref[base + h] = jnp.broadcast_to(m_new, m_ref.shape[1:])
        outs.append(jnp.where(lane < ML_DH, pair_out[0], pltpu.roll(pair_out[1], ML_DH, 1)))
    return jnp.concatenate(outs, axis=-1)


def _ml_scan_kernel(qf_ref, kf_ref, vf_ref, smf_ref, qb_ref, kb_ref, vb_ref, smb_ref, gb_ref,
                    hf_ref, hb_ref, s_ref, m_ref):
    @pl.when(pl.program_id(0) == 0)
    def _():
        s_ref[...] = jnp.zeros_like(s_ref)
        m_ref[...] = jnp.zeros_like(m_ref)

    nb = qf_ref.shape[0]
    streams = ((0, qf_ref, kf_ref, vf_ref, smf_ref, hf_ref), (1, qb_ref, kb_ref, vb_ref, smb_ref, hb_ref))
    for d, q_ref, k_ref, v_ref, sm_ref, h_ref in streams:
        for bi in range(nb):
            h_ref[bi] = _ml_chunk(d, q_ref[bi], k_ref[bi], v_ref[bi], sm_ref[bi] + gb_ref[d],
                                  s_ref, m_ref, (d * nb + bi) * ML_HEADS)


def _ml_scan(q, k, p, gate_b_p):
    b, t, _ = q.shape
    nc = t // ML_CHUNK
    chunk = functools.partial(_scan_chunk, n_ctx_chunks=TM // ML_CHUNK, n_chunks=nc)
    blk = lambda d, w, cb: pl.BlockSpec((b, ML_CHUNK, w), lambda s: (0, chunk(d, s), cb))
    stream = lambda d: [blk(d, ML_W, 0), blk(d, ML_W, 0), blk(d, ML_W, C_MV // ML_W),
                        blk(d, LANE, C_SMF // LANE + d)]
    shp = jax.ShapeDtypeStruct((b, t, ML_W), F32)
    return pl.pallas_call(
        _ml_scan_kernel,
        grid=(nc,),
        in_specs=stream(0) + stream(1) + [pl.BlockSpec((2, 1, LANE), lambda s: (0, 0, 0))],
        out_specs=[blk(0, ML_W, 0), blk(1, ML_W, 0)],
        out_shape=[shp, shp],
        scratch_shapes=[pltpu.VMEM((2 * b * ML_HEADS, LANE, LANE), F32),
                        pltpu.VMEM((2 * b * ML_HEADS, 8, LANE), F32)],
        compiler_params=_cparams(("arbitrary",)),
        name="mlstm_scan",
    )(q, k, p, p, q, k, p, p, gate_b_p)


def _gla_chunk(d, q, k, v, sm, wa, ba, s_ref, si):
    L = GLA_CHUNK
    row = lax.broadcasted_iota(jnp.int32, (L, L), 0)
    col = lax.broadcasted_iota(jnp.int32, (L, L), 1)
    mask = col <= row if d == 0 else col >= row
    tri = mask.astype(F32)
    lane_k = lax.broadcasted_iota(jnp.int32, (1, GLA_HEADS * GLA_DK), 1)
    lane_v = lax.broadcasted_iota(jnp.int32, (1, GLA_HEADS * GLA_DV), 1)
    eye = (lax.broadcasted_iota(jnp.int32, (LANE, LANE), 0)
           == lax.broadcasted_iota(jnp.int32, (LANE, LANE), 1)).astype(F32)

    pre = jnp.dot(sm, wa, precision=HIGHEST, preferred_element_type=F32) + ba
    loga = _log_sigmoid(pre) * (1.0 / GLA_TAU)
    bc = jnp.dot(tri, loga, precision=HIGHEST, preferred_element_type=F32)
    ref_row = bc[L // 2 - 1:L // 2]
    b_end = bc[L - 1:L] if d == 0 else bc[0:1]

    q = q * (GLA_DK ** -0.5)
    v = v.astype(BF16)
    q_in = (q * jnp.exp(bc - ref_row))
    k_in = (k * jnp.exp(ref_row - bc)).astype(BF16)
    q_st = (q * jnp.exp(bc)).astype(BF16)
    k_st = (k * jnp.exp(b_end - bc)).astype(BF16)

    blockdiag = (lax.broadcasted_iota(jnp.int32, s_ref.shape[1:], 0) // GLA_DK
                 == lax.broadcasted_iota(jnp.int32, s_ref.shape[1:], 1) // GLA_DV)
    s_old = s_ref[si]
    o = jnp.dot(q_st, jnp.where(blockdiag, s_old, 0.0).astype(BF16), preferred_element_type=F32)
    for h in range(GLA_HEADS):
        qh = jnp.where(lane_k // GLA_DK == h, q_in, 0.0).astype(BF16)
        att = jnp.where(mask, _nt_dot(qh, k_in), 0.0).astype(BF16)
        oh = jnp.dot(att, v, preferred_element_type=F32)
        o = o + jnp.where(lane_v // GLA_DV == h, oh, 0.0)

    decay_col = jnp.exp(_nt_dot(eye, jnp.broadcast_to(b_end, (8, LANE)), precision=HIGHEST)[:, 0:1])
    s_ref[si] = decay_col * s_old + _tn_dot(k_st, v)
    return o


def _gla_scan_kernel(qf_ref, kf_ref, vf_ref, smf_ref, qb_ref, kb_ref, vb_ref, smb_ref, wa_ref, ba_ref,
                     of_ref, ob_ref, s_ref):
    @pl.when(pl.program_id(0) == 0)
    def _():
        s_ref[...] = jnp.zeros_like(s_ref)

    nb = qf_ref.shape[0]
    n_sub = TM // GLA_CHUNK
    streams = ((0, qf_ref, kf_ref, vf_ref, smf_ref, of_ref), (1, qb_ref, kb_ref, vb_ref, smb_ref, ob_ref))
    for d, q_ref, k_ref, v_ref, sm_ref, o_ref in streams:
        for bi in range(nb):
            for c in (range(n_sub) if d == 0 else reversed(range(n_sub))):
                rows = slice(c * GLA_CHUNK, (c + 1) * GLA_CHUNK)
                o_ref[bi, rows] = _gla_chunk(d, q_ref[bi, rows], k_ref[bi, rows], v_ref[bi, rows],
                                             sm_ref[bi, rows], wa_ref[d], ba_ref[d], s_ref, d * nb + bi)


def _gla_scan(p, wa_p, ba_p):
    b, t, _ = p.shape
    nc = t // TM
    chunk = functools.partial(_scan_chunk, n_ctx_chunks=1, n_chunks=nc)
    kw, vw = GLA_HEADS * GLA_DK, GLA_HEADS * GLA_DV
    blk = lambda d, w, col: pl.BlockSpec((b, TM, w), lambda s: (0, chunk(d, s), col // w))
    stream = lambda d: [blk(d, kw, C_GQ), blk(d, kw, C_GK), blk(d, vw, C_GV), blk(d, LANE, C_SMF + d * LANE)]
    shp = jax.ShapeDtypeStruct((b, t, vw), F32)
    return pl.pallas_call(
        _gla_scan_kernel,
        grid=(nc,),
        in_specs=stream(0) + stream(1) + [pl.BlockSpec((2, LANE, kw), lambda s: (0, 0, 0)),
                                          pl.BlockSpec((2, 1, kw), lambda s: (0, 0, 0))],
        out_specs=[blk(0, vw, 0), blk(1, vw, 0)],
        out_shape=[shp, shp],
        scratch_shapes=[pltpu.VMEM((2 * b, kw, vw), F32)],
        compiler_params=_cparams(("arbitrary",)),
        name="gla_scan",
    )(p, p, p, p, p, p, p, p, wa_p, ba_p)


OUTPROJ_TILES = 2
OUTPROJ_N_TOK = 11


def _outproj_kernel(*refs, n_tiles):
    n_in = OUTPROJ_TILES * OUTPROJ_N_TOK
    shared = refs[n_in:n_in + 7]
    outs = refs[n_in + 7:]
    for u in range(OUTPROJ_TILES):
        tile = jnp.minimum(OUTPROJ_TILES * pl.program_id(1) + u, n_tiles - 1)
        _outproj_tile(tile == 0, slice(u * TM, (u + 1) * TM),
                      *refs[u * OUTPROJ_N_TOK:(u + 1) * OUTPROJ_N_TOK], *shared, *outs)


def _outproj_tile(is_ctx, rows, ac_ref, al_ref, mhf_ref, mhb_ref, xc_ref, mo_ref, gof_ref, gob_ref, gr_ref, x_ref,
                  mod_ref, mnw_ref, msk_ref, gnw_ref, wout_ref, n2w_ref, wr_ref, br_ref,
                  xo_ref, h2_ref, route_ref):
    grp64 = (lax.broadcasted_iota(jnp.int32, (ML_W, ML_W), 0) // ML_DH
             == lax.broadcasted_iota(jnp.int32, (ML_W, ML_W), 1) // ML_DH).astype(F32) * (1.0 / ML_DH)

    def head_norm(x, w):
        ms = jnp.dot(x * x, grp64, precision=HIGHEST, preferred_element_type=F32)
        return x * lax.rsqrt(ms + EPS) * w

    m_l = jax.nn.sigmoid(mo_ref[0]) * (head_norm(mhf_ref[0] + mhb_ref[0], mnw_ref[...])
                                       + msk_ref[...] * xc_ref[0])
    g_l = head_norm(gof_ref[0] + gob_ref[0], gnw_ref[...]) * _silu(gr_ref[0])
    na = MLA_HEADS * MLA_V
    a = jnp.where(is_ctx, ac_ref[0], al_ref[0])
    res = (jnp.dot(a, wout_ref[0:na], preferred_element_type=F32)
           + jnp.dot(m_l.astype(BF16), wout_ref[na:na + ML_W], preferred_element_type=F32)
           + jnp.dot(g_l.astype(BF16), wout_ref[na + ML_W:], preferred_element_type=F32))
    mod = mod_ref[0]
    x = x_ref[0] + mod[2:3] * res
    xo_ref[0, rows] = x
    h2 = (x * lax.rsqrt(jnp.mean(x * x, axis=-1, keepdims=True) + EPS) * n2w_ref[...]
          * (1.0 + mod[4:5]) + mod[3:4])
    half = h2.shape[-1] // 2
    h2_ref[0, rows] = _pack_bf16_pair(h2[:, :half], h2[:, half:])

    logits = jnp.dot(h2, wr_ref[...], precision=HIGHEST, preferred_element_type=F32) + br_ref[...]
    lane = lax.broadcasted_iota(jnp.int32, (1, LANE), 1)
    lane_f = lane.astype(F32)
    neg = -jnp.inf
    gl = jnp.where(lane < R_EXP, logits, neg)
    gmax = jnp.max(gl, axis=-1, keepdims=True)
    g_w = 1.0 / jnp.sum(jnp.exp(gl - gmax), axis=-1, keepdims=True)
    g_i = jnp.min(jnp.where(gl == gmax, lane_f, float(LANE)), axis=-1, keepdims=True)
    grp_of_lane = ((lane - R_EXP) // EXP_PER_GROUP).astype(F32)
    in_grp = (lane >= R_EXP) & (lane < R_EXP + N_EXPERTS) & (grp_of_lane == g_i)
    el = jnp.where(in_grp, logits, neg)
    m1 = jnp.max(el, axis=-1, keepdims=True)
    i1 = jnp.min(jnp.where(el == m1, lane_f, float(LANE)), axis=-1, keepdims=True)
    el2 = jnp.where(lane_f == i1, neg, el)
    m2 = jnp.max(el2, axis=-1, keepdims=True)
    i2 = jnp.min(jnp.where(el2 == m2, lane_f, float(LANE)), axis=-1, keepdims=True)
    p2 = jnp.exp(m2 - m1)
    w1 = g_w / (1.0 + p2)
    route_ref[0, rows] = jnp.where(lane == RT_E1, i1 - R_EXP, jnp.where(lane == RT_E1 + 1, i2 - R_EXP,
                             jnp.where(lane == RT_W1, w1, jnp.where(lane == RT_W1 + 1, p2 * w1, 0.0))))


def _outproj(a_ctx, a_lat, mh, xconv, p, go, xs, mods, ml_norm_w, ml_skip, gla_norm_w, w_out_b, norm2_w, wr_p, br_p):
    b, t, d = xs.shape
    nt = t // TM
    full = lambda shape: pl.BlockSpec(shape, lambda bi, i: (0,) * len(shape))
    na = MLA_HEADS * MLA_V

    out = lambda w: pl.BlockSpec((1, OUTPROJ_TILES * TM, w), lambda bi, i: (bi, i, 0))

    def tile_specs(u):
        tile = lambda i: jnp.minimum(OUTPROJ_TILES * i + u, nt - 1)
        tok = lambda w, cb=0: pl.BlockSpec((1, TM, w), lambda bi, i: (bi, tile(i), cb))
        ins = [pl.BlockSpec((1, TM, na), lambda bi, i: (bi, 0, 0)),
               pl.BlockSpec((1, TM, na), lambda bi, i: (bi, jnp.maximum(tile(i) - 1, 0), 0)),
               tok(ML_W), tok(ML_W), tok(ML_W), tok(ML_W, C_MO // ML_W),
               tok(ML_W), tok(ML_W), tok(ML_W, C_GR // ML_W), tok(d),
               pl.BlockSpec((1, 6, d), lambda bi, i: (_mod_row(bi, tile(i)), 0, 0))]
        return ins

    specs = [tile_specs(u) for u in range(OUTPROJ_TILES)]
    tok_args = (a_ctx, a_lat, mh[0], mh[1], xconv, p, go[0], go[1], p, xs, mods)
    assert len(tok_args) == OUTPROJ_N_TOK
    shapes = [jax.ShapeDtypeStruct((b, t, d), F32), jax.ShapeDtypeStruct((b, t, d // 2), jnp.uint32),
              jax.ShapeDtypeStruct((b, t, LANE), F32)]
    outs = pl.pallas_call(
        functools.partial(_outproj_kernel, n_tiles=nt),
        grid=(b, pl.cdiv(nt, OUTPROJ_TILES)),
        in_specs=sum(specs, []) + [
            full((1, ML_W)), full((1, ML_W)), full((1, ML_W)), full((d, d)), full((1, d)),
            full((d, LANE)), full((1, LANE))],
        out_specs=[out(d), out(d // 2), out(LANE)],
        out_shape=shapes,
        compiler_params=_cparams(("parallel", "arbitrary")),
        name="out_proj_router",
    )(*(tok_args * OUTPROJ_TILES), ml_norm_w.reshape(1, -1), ml_skip.reshape(1, -1),
      gla_norm_w.reshape(1, -1), w_out_b, norm2_w.reshape(1, -1), wr_p, br_p)
    return outs


def _dispatch(route, n_tiles):
    n = route.shape[0]
    flat = route[:, RT_E1:RT_E1 + 2].astype(jnp.int32).reshape(-1)
    onehot = (flat[:, None] == jnp.arange(N_EXPERTS, dtype=jnp.int32)[None, :]).astype(jnp.int32)
    csum = jnp.cumsum(onehot, axis=0)
    rank = jnp.sum(csum * onehot, axis=1) - 1
    padded = (csum[-1] + TM - 1) // TM * TM
    ends = jnp.cumsum(padded)
    pos = (ends - padded)[flat] + rank
    src = jnp.zeros((n_tiles * TM,), jnp.int32).at[pos].set(jnp.arange(2 * n, dtype=jnp.int32) // 2)
    tile_start = jnp.arange(n_tiles, dtype=jnp.int32) * TM
    tile_exp = jnp.minimum(jnp.sum((ends[None, :] <= tile_start[:, None]).astype(jnp.int32), axis=1),
                           N_EXPERTS - 1)
    tile_on = (tile_start < ends[-1]).astype(jnp.int32)
    pos = pos.reshape(n // TM, TM, 2)
    return (src.reshape(n_tiles, 1, TM), tile_exp, tile_on,
            pos[:, :, 0].reshape(n // TM, 1, TM), pos[:, :, 1].reshape(n // TM, 1, TM))


def _gather_rows(src_hbm, idx_ref, dst, sem):
    def body(jj, carry):
        for u in range(GATHER_UNROLL):
            j = jj * GATHER_UNROLL + u
            pltpu.make_async_copy(src_hbm.at[pl.ds(idx_ref[0, 0, j], 1)], dst.at[pl.ds(j, 1)],
                                  sem).start(priority=u % 2)
        return carry

    lax.fori_loop(0, TM // GATHER_UNROLL, body, 0)


def _wait_rows(src_hbm, dst, sem):
    pltpu.make_async_copy(src_hbm.at[pl.ds(0, TM)], dst, sem).wait()


def _pack_bf16_pair(lo, hi):
    lo_b = lax.bitcast_convert_type(lo.astype(BF16).astype(F32), jnp.uint32) >> 16
    hi_b = lax.bitcast_convert_type(hi.astype(BF16).astype(F32), jnp.uint32) & jnp.uint32(0xFFFF0000)
    return hi_b | lo_b


def _unpack_bf16_pair(w):
    return (lax.bitcast_convert_type(w << 16, F32),
            lax.bitcast_convert_type(w & jnp.uint32(0xFFFF0000), F32))


def _experts_kernel(texp_ref, ton_ref, idx_ref, idxn_ref, h2_hbm, wg_ref, wu_ref, wd_ref, y_ref,
                    buf, wgu_b, wd_b, sem):
    r = pl.program_id(0)
    slot = r % 2
    half = wgu_b.shape[0] // 2

    @pl.when(r == 0)
    def _():
        _gather_rows(h2_hbm, idx_ref, buf.at[0], sem.at[0])

    last = pl.num_programs(0) - 1

    @pl.when(jnp.logical_and(r < last, ton_ref[jnp.minimum(r + 1, last)] == 1))
    def _():
        _gather_rows(h2_hbm, idxn_ref, buf.at[1 - slot], sem.at[1 - slot])

    @pl.when(ton_ref[r] == 1)
    def _():
        @pl.when(jnp.logical_or(r == 0, texp_ref[r] != texp_ref[jnp.maximum(r - 1, 0)]))
        def _():
            wgu_b[:, :D_EXPERT] = wg_ref[0, 0].astype(BF16)
            wgu_b[:, D_EXPERT:] = wu_ref[0, 0].astype(BF16)
            wd_b[...] = wd_ref[0, 0].astype(BF16)

        _wait_rows(h2_hbm, buf.at[slot], sem.at[slot])
        x_lo, x_hi = _unpack_bf16_pair(buf[slot])
        gu = (jnp.dot(x_lo.astype(BF16), wgu_b[0:half], preferred_element_type=F32)
              + jnp.dot(x_hi.astype(BF16), wgu_b[half:], preferred_element_type=F32))
        act = (_silu(gu[:, :D_EXPERT]) * gu[:, D_EXPERT:]).astype(BF16)
        y = jnp.dot(act, wd_b[...], preferred_element_type=F32)
        y_ref[...] = _pack_bf16_pair(y[:, :half], y[:, half:])

    @pl.when(ton_ref[r] == 0)
    def _():
        y_ref[...] = jnp.zeros_like(y_ref)


def _experts(h2p, src, tile_exp, tile_on, w_gate, w_up, w_down, layer):
    n, dh = h2p.shape
    d = 2 * dh
    n_tiles = src.shape[0]
    idx = lambda nxt: pl.BlockSpec((1, 1, TM), lambda r, te, to: (jnp.minimum(r + nxt, n_tiles - 1), 0, 0),
                                   memory_space=pltpu.SMEM)
    wspec = lambda shape: pl.BlockSpec((1, 1) + shape, lambda r, te, to: (layer, te[r], 0, 0))
    return pl.pallas_call(
        _experts_kernel,
        grid_spec=pltpu.PrefetchScalarGridSpec(
            num_scalar_prefetch=2,
            grid=(n_tiles,),
            in_specs=[
                idx(0), idx(1),
                pl.BlockSpec(memory_space=pl.ANY),
                wspec((d, D_EXPERT)), wspec((d, D_EXPERT)), wspec((D_EXPERT, d)),
            ],
            out_specs=pl.BlockSpec((TM, dh), lambda r, te, to: (r, 0)),
            scratch_shapes=[pltpu.VMEM((2, TM, dh), jnp.uint32), pltpu.VMEM((d, 2 * D_EXPERT), BF16),
                            pltpu.VMEM((D_EXPERT, d), BF16), pltpu.SemaphoreType.DMA((2,))],
        ),
        out_shape=jax.ShapeDtypeStruct((n_tiles * TM, dh), jnp.uint32),
        compiler_params=_cparams(("arbitrary",)),
        name="moe_experts",
    )(tile_exp, tile_on, src, src, h2p, w_gate, w_up, w_down)


def _combine_kernel(p1_ref, p2_ref, p1n_ref, p2n_ref, route_ref, x_ref, mod_ref, y_hbm, o_ref, buf, sem):
    g = pl.program_id(0)
    slot = g % 2
    half = buf.shape[-1]

    def gather(pa, pb, s):
        _gather_rows(y_hbm, pa, buf.at[s, 0], sem.at[s])
        _gather_rows(y_hbm, pb, buf.at[s, 1], sem.at[s])

    @pl.when(g == 0)
    def _():
        gather(p1_ref, p2_ref, 0)

    @pl.when(g + 1 < pl.num_programs(0))
    def _():
        gather(p1n_ref, p2n_ref, 1 - slot)

    _wait_rows(y_hbm, buf.at[slot, 0], sem.at[slot])
    _wait_rows(y_hbm, buf.at[slot, 1], sem.at[slot])
    route = route_ref[...]
    w1, w2 = route[:, RT_W1:RT_W1 + 1], route[:, RT_W1 + 1:RT_W1 + 2]
    gate = mod_ref[0][5:6]
    for part, (y1, y2) in enumerate(zip(_unpack_bf16_pair(buf[slot, 0]), _unpack_bf16_pair(buf[slot, 1]))):
        cols = slice(part * half, (part + 1) * half)
        o_ref[:, cols] = x_ref[:, cols] + gate[:, cols] * (w1 * y1 + w2 * y2)


def _combine(y, pos1, pos2, route, xs2, mods, nt):
    n, d = xs2.shape
    n_tok_tiles = n // TM
    idx = lambda nxt: pl.BlockSpec((1, 1, TM), lambda g: (jnp.minimum(g + nxt, n_tok_tiles - 1), 0, 0),
                                   memory_space=pltpu.SMEM)
    tok = lambda w: pl.BlockSpec((TM, w), lambda g: (g, 0))
    return pl.pallas_call(
        _combine_kernel,
        grid=(n_tok_tiles,),
        in_specs=[
            idx(0), idx(0), idx(1), idx(1), tok(LANE), tok(d),
            pl.BlockSpec((1, 6, d), lambda g: (_mod_row(g // nt, g % nt), 0, 0)),
            pl.BlockSpec(memory_space=pl.ANY),
        ],
        out_specs=tok(d),
        out_shape=jax.ShapeDtypeStruct((n, d), F32),
        scratch_shapes=[pltpu.VMEM((2, 2, TM, d // 2), jnp.uint32), pltpu.SemaphoreType.DMA((2,))],
        compiler_params=_cparams(("arbitrary",)),
        name="moe_combine",
    )(pos1, pos2, pos1, pos2, route, xs2, mods, y)


def _moe(h2p, route, xs, mods, w_gate, w_up, w_down, layer):
    b, t, d = xs.shape
    n = b * t
    n_tiles = 2 * n // TM + N_EXPERTS
    route2 = route.reshape(n, LANE)
    src, tile_exp, tile_on, pos1, pos2 = _dispatch(route2, n_tiles)
    y = _experts(h2p.reshape(n, d // 2), src, tile_exp, tile_on, w_gate, w_up, w_down, layer)
    return _combine(y, pos1, pos2, route2, xs.reshape(n, d), mods, t // TM).reshape(b, t, d)


def _rope_tables(n_ctx, n_lat):
    rows = n_lat // GRID_W
    row = jnp.broadcast_to(jnp.arange(rows, dtype=F32)[:, None], (rows, GRID_W)).reshape(-1)
    col = jnp.broadcast_to(jnp.arange(GRID_W, dtype=F32)[None, :], (rows, GRID_W)).reshape(-1)
    n_freq = MLA_ROPE // 4
    inv = ROPE_THETA ** (-jnp.arange(n_freq, dtype=F32) / n_freq)
    ang = jnp.concatenate([row[:, None] * inv, col[:, None] * inv], axis=-1)
    cos, sin = jnp.cos(ang), jnp.sin(ang)
    half = MLA_ROPE // 2
    z = lambda w: jnp.zeros((n_lat, w), F32)
    o = lambda w: jnp.ones((n_lat, w), F32)
    tail = LANE - MLA_QK
    cos_t = jnp.concatenate([o(MLA_NOPE), cos, cos, o(tail)], axis=-1)
    sa_t = jnp.concatenate([z(MLA_NOPE + half), sin, z(tail)], axis=-1)
    sb_t = jnp.concatenate([z(MLA_NOPE), -sin, z(half + tail)], axis=-1)
    ctx1 = jnp.ones((n_ctx, LANE), F32)
    ctx0 = jnp.zeros((n_ctx, LANE), F32)
    return (jnp.concatenate([ctx1, cos_t], 0), jnp.concatenate([ctx0, sa_t], 0),
            jnp.concatenate([ctx0, sb_t], 0))


def _pad_cols(a, width):
    return jnp.pad(a, [(0, 0)] * (a.ndim - 1) + [(0, width - a.shape[-1])])


def _layer_weights(w_in, w_uq, w_ukv, q_norm_w, k_norm_w, ml_conv_w, ml_wq, ml_wk, ml_gate_b,
                   gla_wa, gla_ba, w_out, w_grp, b_grp, w_erouter, b_erouter):
    d = w_in.shape[0]
    o = np.cumsum((0, Q_LORA, KV_LORA, MLA_ROPE, ML_W, ML_W, ML_W, 4 * ML_HEADS, GLA_HEADS * GLA_DK,
                   GLA_HEADS * GLA_DK, GLA_HEADS * GLA_DV, GLA_HEADS * GLA_DV, 2 * GLA_LR))
    seg = lambda j: w_in[:, o[j]:o[j + 1]]
    cq, ckv, kr, mx, mv, mo, mg, gq, gk, gv, gr, ga = (seg(j) for j in range(12))
    z = lambda w: jnp.zeros((d, w), F32)

    def small(di):
        return jnp.concatenate([mg[:, di * 8:(di + 1) * 8], ga[:, di * GLA_LR:(di + 1) * GLA_LR],
                                z(SM_KR - SM_GA - GLA_LR), kr, z(LANE - SM_KR - MLA_ROPE)], axis=-1)

    w_in_p = jnp.concatenate([cq, mx, mv, mo, gv, gr, ckv, gq, gk, small(0), small(1)], axis=-1).astype(BF16)

    wuq_p = _pad_cols(w_uq.reshape(Q_LORA, MLA_HEADS, MLA_QK), LANE).reshape(Q_LORA, -1).astype(BF16)
    ukv = w_ukv.reshape(KV_LORA, MLA_HEADS, MLA_NOPE + MLA_V)
    wuk_p = _pad_cols(ukv[..., :MLA_NOPE], LANE).reshape(KV_LORA, -1).astype(BF16)
    wuv_p = _pad_cols(ukv[..., MLA_NOPE:], LANE).reshape(KV_LORA, -1).astype(BF16)
    qn_p = _pad_cols(q_norm_w.reshape(1, -1), LANE)
    kn_p = _pad_cols(k_norm_w.reshape(1, -1), LANE)

    conv_w8 = jnp.pad(ml_conv_w, ((0, 8 - ML_CONV), (0, 0)))
    bd = lambda w: jax.scipy.linalg.block_diag(*[w[h] for h in range(ML_HEADS)])
    wq_bd = (bd(ml_wq) * (ML_DH ** -0.5)).astype(BF16)
    wk_bd = bd(ml_wk).astype(BF16)
    gate_b_p = _pad_cols(ml_gate_b.reshape(2, 1, 2 * ML_HEADS), LANE)

    wa_p = jnp.pad(gla_wa, ((0, 0), (SM_GA, LANE - SM_GA - GLA_LR), (0, 0)))
    ba_p = gla_ba.reshape(2, 1, -1)

    wr_p = _pad_cols(jnp.concatenate([w_grp, w_erouter], axis=-1), LANE)
    br_p = _pad_cols(jnp.concatenate([b_grp, b_erouter]).reshape(1, -1), LANE)
    return dict(w_in_p=w_in_p, wuq_p=wuq_p, wuk_p=wuk_p, wuv_p=wuv_p, qn_p=qn_p, kn_p=kn_p,
                conv_w8=conv_w8, wq_bd=wq_bd, wk_bd=wk_bd, gate_b_p=gate_b_p, wa_p=wa_p, ba_p=ba_p,
                w_out_b=w_out.astype(BF16), wr_p=wr_p, br_p=br_p)


def kernel(x, c, ctx, c_ctx, w_mod, b_mod, norm1_w, w_in, q_a_norm, w_uq, kv_a_norm, w_ukv,
           q_norm_w, k_norm_w, ml_conv_w, ml_conv_b, ml_wq, ml_wk, ml_gate_b, ml_norm_w, ml_skip,
           gla_wa, gla_ba, gla_norm_w, w_out, norm2_w, w_grp, b_grp, w_erouter, b_erouter,
           w_gate, w_up, w_down):
    b, s, d = x.shape
    n_ctx = ctx.shape[1]
    depth = w_mod.shape[0]
    assert n_ctx == TM and s % TM == 0 and b == 2

    cc = jnp.concatenate([c, c_ctx[None, :], jnp.zeros((8 - b - 1, d), F32)], axis=0)
    mods_all = _mods(cc, w_mod, b_mod).reshape(depth, 8, 6, d)
    cos_t, sa_t, sb_t = _rope_tables(n_ctx, s)
    xs = jnp.concatenate([ctx, x], axis=1)

    for l in range(depth):
        w = _layer_weights(w_in[l], w_uq[l], w_ukv[l], q_norm_w[l], k_norm_w[l], ml_conv_w[l],
                           ml_wq[l], ml_wk[l], ml_gate_b[l], gla_wa[l], gla_ba[l], w_out[l],
                           w_grp[l], b_grp[l], w_erouter[l], b_erouter[l])
        mods = mods_all[l]
        p = _inproj(xs, mods, norm1_w[l], w["w_in_p"])
        q, k, v = _mla_prep(p, q_a_norm[l], w["wuq_p"], kv_a_norm[l], w["wuk_p"], w["wuv_p"],
                            w["qn_p"], w["kn_p"], cos_t, sa_t, sb_t)
        a_lat = _attention_lat(q, k, v)
        a_ctx = _attention_ctx(q, k, v) if l < depth - 1 else jnp.zeros((b, TM, MLA_HEADS * MLA_V), BF16)
        xconv, mq, mk = _ml_prep(p, w["conv_w8"], ml_conv_b[l], w["wq_bd"], w["wk_bd"])
        mh = _ml_scan(mq, mk, p, w["gate_b_p"])
        go = _gla_scan(p, w["wa_p"], w["ba_p"])
        xs, h2, route = _outproj(a_ctx, a_lat, mh, xconv, p, go, xs, mods, ml_norm_w[l], ml_skip[l], gla_norm_w[l],
                                w["w_out_b"], norm2_w[l], w["wr_p"], w["br_p"])
        xs = _moe(h2, route, xs, mods, w_gate, w_up, w_down, l)
    return xs[:, n_ctx:, :]
```

```python
import functools

import jax
import jax.numpy as jnp
import numpy as np
from jax import lax
from jax.experimental import pallas as pl
from jax.experimental.pallas import tpu as pltpu

F32 = jnp.float32
BF16 = jnp.bfloat16
HIGHEST = lax.Precision.HIGHEST

EPS = 1e-6
GRID_W = 64
ROPE_THETA = 10000.0

MLA_HEADS = 8
MLA_NOPE = 64
MLA_ROPE = 32
MLA_QK = MLA_NOPE + MLA_ROPE
MLA_V = 64
Q_LORA = 256
KV_LORA = 128

ML_HEADS = 4
ML_DH = 64
ML_W = ML_HEADS * ML_DH
ML_CONV = 5

GLA_HEADS = 4
GLA_DK = 32
GLA_DV = 64
GLA_LR = 16
GLA_TAU = 16.0

N_GROUPS = 4
EXP_PER_GROUP = 8
N_EXPERTS = N_GROUPS * EXP_PER_GROUP
D_EXPERT = 256

LANE = 128
TM = 256
ML_CHUNK = 256
GLA_CHUNK = 128
VMEM_LIMIT = 56 * 1024 * 1024

C_CQ, C_MX, C_MV, C_MO, C_GV, C_GR = 0, 256, 512, 768, 1024, 1280
C_CKV, C_GQ, C_GK, C_SMF, C_SMB = 1536, 1664, 1792, 1920, 2048
D_INP = 2176
SM_GATE = 0
SM_GA = 8
SM_KR = 64
R_GRP = 0
R_EXP = 4
RT_E1 = 0
RT_W1 = 2
GATHER_UNROLL = 8


def _cparams(sem):
    return pltpu.CompilerParams(dimension_semantics=sem, vmem_limit_bytes=VMEM_LIMIT)


def _silu(x):
    return x * jax.nn.sigmoid(x)


def _log_sigmoid(x):
    return -(jnp.maximum(-x, 0.0) + jnp.log1p(jnp.exp(-jnp.abs(x))))


def _nt_dot(a, b, **kw):
    return lax.dot_general(a, b, (((1,), (1,)), ((), ())), preferred_element_type=F32, **kw)


def _split3(x):
    hi = x.astype(BF16)
    r = x - hi.astype(F32)
    mid = r.astype(BF16)
    return hi, mid, (r - mid.astype(F32)).astype(BF16)


def _dot_sel(sel, x, nt=False, right=False):
    s = sel.astype(BF16)
    dot = _nt_dot if nt else functools.partial(jnp.dot, preferred_element_type=F32)
    return sum((dot(piece, s) if right else dot(s, piece)) for piece in _split3(x))


def _tn_dot(a, b, **kw):
    return lax.dot_general(a, b, (((0,), (0,)), ((), ())), preferred_element_type=F32, **kw)


def _mods_kernel(cc_ref, w_ref, b_ref, o_ref):
    a = _silu(cc_ref[...])
    o_ref[0] = jnp.dot(a, w_ref[0], precision=HIGHEST, preferred_element_type=F32) + b_ref[0]


def _mods(cc, w_mod, b_mod):
    depth, d, d6 = w_mod.shape
    nb = 1536
    return pl.pallas_call(
        _mods_kernel,
        grid=(depth, d6 // nb),
        in_specs=[
            pl.BlockSpec((8, d), lambda l, j: (0, 0)),
            pl.BlockSpec((1, d, nb), lambda l, j: (l, 0, j)),
            pl.BlockSpec((1, 1, nb), lambda l, j: (l, 0, j)),
        ],
        out_specs=pl.BlockSpec((1, 8, nb), lambda l, j: (l, 0, j)),
        out_shape=jax.ShapeDtypeStruct((depth, 8, d6), F32),
        compiler_params=_cparams(("arbitrary", "arbitrary")),
        name="adaln_mods",
    )(cc, w_mod, b_mod.reshape(depth, 1, d6))


def _mod_row(b, i):
    return jnp.where(i == 0, 2, b)


def _inproj_kernel(x_ref, mod_ref, nw_ref, w_ref, o_ref):
    x = x_ref[0]
    y = x * lax.rsqrt(jnp.mean(x * x, axis=-1, keepdims=True) + EPS) * nw_ref[...]
    mod = mod_ref[0]
    h = y * (1.0 + mod[1:2]) + mod[0:1]
    o_ref[0] = jnp.dot(h.astype(BF16), w_ref[...], preferred_element_type=F32)


def _inproj(xs, mods, norm_w, w_in_p):
    b, t, d = xs.shape
    nt = t // TM
    return pl.pallas_call(
        _inproj_kernel,
        grid=(b, nt),
        in_specs=[
            pl.BlockSpec((1, TM, d), lambda bi, i: (bi, i, 0)),
            pl.BlockSpec((1, 6, d), lambda bi, i: (_mod_row(bi, i), 0, 0)),
            pl.BlockSpec((1, d), lambda bi, i: (0, 0)),
            pl.BlockSpec((d, D_INP), lambda bi, i: (0, 0)),
        ],
        out_specs=pl.BlockSpec((1, TM, D_INP), lambda bi, i: (bi, i, 0)),
        out_shape=jax.ShapeDtypeStruct((b, t, D_INP), F32),
        compiler_params=_cparams(("parallel", "parallel")),
        name="in_proj",
    )(xs, mods, norm_w.reshape(1, d), w_in_p)


def _mla_prep_kernel(cq_ref, ckv_ref, sm_ref, qan_ref, wuq_ref, kvan_ref, wuk_ref, wuv_ref,
                     qn_ref, kn_ref, cos_ref, sa_ref, sb_ref, q_ref, k_ref, v_ref):
    cq = cq_ref[0]
    cqn = cq * lax.rsqrt(jnp.mean(cq * cq, axis=-1, keepdims=True) + EPS) * qan_ref[...]
    qall = jnp.dot(cqn.astype(BF16), wuq_ref[...], preferred_element_type=F32)
    ckv = ckv_ref[0]
    ckvn = (ckv * lax.rsqrt(jnp.mean(ckv * ckv, axis=-1, keepdims=True) + EPS)
            * kvan_ref[...]).astype(BF16)
    kall = jnp.dot(ckvn, wuk_ref[...], preferred_element_type=F32)
    vall = jnp.dot(ckvn, wuv_ref[...], preferred_element_type=F32)
    lane = lax.broadcasted_iota(jnp.int32, (1, LANE), 1)
    kr = jnp.where((lane >= SM_KR) & (lane < SM_KR + MLA_ROPE), sm_ref[0], 0.0)
    cos, sin = cos_ref[...], sa_ref[...] - sb_ref[...]
    r_i = lax.broadcasted_iota(jnp.int32, (LANE, LANE), 0)
    c_i = lax.broadcasted_iota(jnp.int32, (LANE, LANE), 1)
    half = MLA_ROPE // 2
    first = (c_i >= MLA_NOPE) & (c_i < MLA_NOPE + half)
    second = (c_i >= MLA_NOPE + half) & (c_i < MLA_QK)
    rot = jnp.where(first & (r_i == c_i + half), -1.0,
                    jnp.where(second & (r_i == c_i - half), 1.0, 0.0)).astype(BF16)

    def rope(x):
        return x * cos + jnp.dot(x.astype(BF16), rot, preferred_element_type=F32) * sin

    def head_norm(x, w):
        return x * lax.rsqrt(jnp.sum(x * x, axis=-1, keepdims=True) * (1.0 / MLA_QK) + EPS) * w

    for h in range(MLA_HEADS):
        sl = slice(h * LANE, (h + 1) * LANE)
        qh = rope(head_norm(qall[:, sl], qn_ref[...]))
        q_ref[0, h] = (qh * Q_SCALE).astype(BF16)
        kh = rope(head_norm(kall[:, sl] + kr, kn_ref[...]))
        k_ref[0, h] = jnp.where(lane == ATT_SHIFT_LANE, 1.0, kh).astype(BF16)
        v_ref[0, h] = jnp.where(lane == MLA_V, 1.0, vall[:, sl]).astype(BF16)


def _mla_prep(p, q_a_norm, wuq_p, kv_a_norm, wuk_p, wuv_p, qn_p, kn_p, cos_t, sa_t, sb_t):
    b, t, _ = p.shape
    nt = t // TM
    hw = MLA_HEADS * LANE
    full = lambda shape: pl.BlockSpec(shape, lambda bi, i: (0,) * len(shape))
    tab = pl.BlockSpec((TM, LANE), lambda bi, i: (i, 0))
    out = pl.BlockSpec((1, MLA_HEADS, TM, LANE), lambda bi, i: (bi, 0, i, 0))
    shp = jax.ShapeDtypeStruct((b, MLA_HEADS, t, LANE), BF16)
    return pl.pallas_call(
        _mla_prep_kernel,
        grid=(b, nt),
        in_specs=[
            pl.BlockSpec((1, TM, Q_LORA), lambda bi, i: (bi, i, C_CQ // Q_LORA)),
            pl.BlockSpec((1, TM, KV_LORA), lambda bi, i: (bi, i, C_CKV // KV_LORA)),
            pl.BlockSpec((1, TM, LANE), lambda bi, i: (bi, i, C_SMF // LANE)),
            full((1, Q_LORA)), full((Q_LORA, hw)), full((1, KV_LORA)),
            full((KV_LORA, hw)), full((KV_LORA, hw)), full((1, LANE)), full((1, LANE)),
            tab, tab, tab,
        ],
        out_specs=[out, out, out],
        out_shape=[shp, shp, shp],
        compiler_params=_cparams(("parallel", "parallel")),
        name="mla_prep",
    )(p, p, p, q_a_norm.reshape(1, -1), wuq_p, kv_a_norm.reshape(1, -1), wuk_p, wuv_p,
      qn_p, kn_p, cos_t, sa_t, sb_t)


Q_SCALE = float(MLA_QK ** -0.5 * np.log2(np.e))
ATT_HP = 2
ATT_NQ = 4
ATT_TK = 2048
ATT_SHIFT_LANE = MLA_QK
ATT_SAFE_MAX = 2.0 ** 100


def _softmax_step(q, kb, vb, m, acc):
    s = _nt_dot(q, kb)
    m_new = jnp.max(s, axis=-1, keepdims=True)
    if m is None:
        return m_new, jnp.dot(jnp.exp2((s - m_new).astype(BF16)), vb, preferred_element_type=F32)
    m_new = jnp.maximum(m, m_new)
    p = jnp.exp2((s - m_new).astype(BF16))
    return m_new, acc * jnp.exp2(m - m_new) + jnp.dot(p, vb, preferred_element_type=F32)


def _attn_ctx_kernel(q_ref, k_ref, v_ref, o_ref):
    outs = []
    for h in range(MLA_HEADS):
        _, acc = _softmax_step(q_ref[0, h], k_ref[0, h], v_ref[0, h], None, None)
        outs.append(acc[:, :MLA_V] / acc[:, MLA_V:MLA_V + 1])
    o_ref[0] = jnp.concatenate(outs, axis=-1).astype(BF16)


def _attention_ctx(q, k, v):
    b, h, _, _ = q.shape
    blk = pl.BlockSpec((1, h, TM, LANE), lambda bi: (bi, 0, 0, 0))
    return pl.pallas_call(
        _attn_ctx_kernel,
        grid=(b,),
        in_specs=[blk, blk, blk],
        out_specs=pl.BlockSpec((1, TM, h * MLA_V), lambda bi: (bi, 0, 0)),
        out_shape=jax.ShapeDtypeStruct((b, TM, h * MLA_V), BF16),
        compiler_params=_cparams(("parallel",)),
        name="mla_attention_ctx",
    )(q, k, v)


def _attn_lat_kernel(*refs, n_blk):
    q_refs, (k_ref, v_ref, o_ref, q_buf, qs_buf) = refs[:ATT_NQ], refs[ATT_NQ:]
    lane = lax.broadcasted_iota(jnp.int32, (1, LANE), 1)

    def kv_block(hh, j):
        off = pl.multiple_of(TM + j * ATT_TK, TM)
        return k_ref[0, hh, pl.ds(off, ATT_TK), :], v_ref[0, hh, pl.ds(off, ATT_TK), :]

    def finish(accs):
        outs = [acc[:, :MLA_V] / acc[:, MLA_V:MLA_V + 1] for acc in accs]
        o_ref[0] = jnp.concatenate(outs, axis=-1).astype(BF16)

    accs = []
    for hh in range(ATT_HP):
        q = jnp.concatenate([qr[0, hh] for qr in q_refs], axis=0)
        q_buf[hh] = q
        kb, vb = k_ref[0, hh, 0:TM, :], v_ref[0, hh, 0:TM, :]
        shift = jnp.max(_nt_dot(q, kb), axis=-1, keepdims=True).astype(BF16)
        qs_buf[hh] = jnp.where(lane == ATT_SHIFT_LANE, -shift, q)
        accs.append(jnp.dot(jnp.exp2(_nt_dot(qs_buf[hh], kb).astype(BF16)), vb, preferred_element_type=F32))

    def fast_body(j, accs):
        new = []
        for hh in range(ATT_HP):
            kb, vb = kv_block(hh, j)
            p = jnp.exp2(_nt_dot(qs_buf[hh], kb).astype(BF16))
            new.append(accs[hh] + jnp.dot(p, vb, preferred_element_type=F32))
        return tuple(new)

    accs = lax.fori_loop(0, n_blk, fast_body, tuple(accs))
    bad = sum(jnp.max(jnp.where(jnp.abs(acc) < ATT_SAFE_MAX, 0.0, 1.0)) for acc in accs)

    @pl.when(bad == 0.0)
    def _():
        finish(accs)

    @pl.when(bad != 0.0)
    def _():
        init = []
        for hh in range(ATT_HP):
            init += _softmax_step(q_buf[hh], k_ref[0, hh, 0:TM, :], v_ref[0, hh, 0:TM, :], None, None)

        def body(j, carry):
            new = []
            for hh in range(ATT_HP):
                new += _softmax_step(q_buf[hh], *kv_block(hh, j), carry[2 * hh], carry[2 * hh + 1])
            return tuple(new)

        carry = lax.fori_loop(0, n_blk, body, tuple(init))
        finish(carry[1::2])


def _attention_lat(q, k, v):
    b, h, t, _ = q.shape
    s = t - TM
    tq = ATT_NQ * TM
    assert s % ATT_TK == 0 and s % tq == 0
    kv = pl.BlockSpec((1, ATT_HP, t, LANE), lambda bi, hp, i: (bi, hp, 0, 0))
    qs = [pl.BlockSpec((1, ATT_HP, TM, LANE), lambda bi, hp, i, u=u: (bi, hp, 1 + ATT_NQ * i + u, 0))
          for u in range(ATT_NQ)]
    return pl.pallas_call(
        functools.partial(_attn_lat_kernel, n_blk=s // ATT_TK),
        grid=(b, h // ATT_HP, s // tq),
        in_specs=qs + [kv, kv],
        out_specs=pl.BlockSpec((1, tq, ATT_HP * MLA_V), lambda bi, hp, i: (bi, i, hp)),
        out_shape=jax.ShapeDtypeStruct((b, s, h * MLA_V), BF16),
        scratch_shapes=[pltpu.VMEM((ATT_HP, tq, LANE), BF16), pltpu.VMEM((ATT_HP, tq, LANE), BF16)],
        compiler_params=_cparams(("parallel", "parallel", "arbitrary")),
        name="mla_attention",
    )(*([q] * ATT_NQ), k, v)


def _ml_prep_kernel(x_ref, prev_ref, next_ref, cw_ref, cb_ref, wq_ref, wk_ref,
                    xc_ref, q_ref, k_ref, *, n_tiles):
    i = pl.program_id(1)
    x = x_ref[0]
    prev = jnp.where(i <= 1, 0.0, prev_ref[0])
    nxt = jnp.where((i == 0) | (i == n_tiles - 1), 0.0, next_ref[0])
    ext = jnp.concatenate([prev, x, nxt], axis=0)
    n_ext = TM + 16
    cw = cw_ref[...]
    acc = jnp.zeros((TM, ML_W), F32) + cb_ref[...]
    for kk in range(ML_CONV):
        sh = (ML_CONV // 2 - kk) % n_ext
        shifted = ext if sh == 0 else pltpu.roll(ext, sh, 0)
        acc = acc + cw[kk:kk + 1] * shifted[8:8 + TM]
    xc = _silu(acc)
    xc_ref[0] = xc
    xb = xc.astype(BF16)
    q_ref[0] = jnp.dot(xb, wq_ref[...], preferred_element_type=F32).astype(BF16)
    k_ref[0] = jnp.dot(xb, wk_ref[...], preferred_element_type=F32).astype(BF16)


def _ml_prep(p, conv_w8, conv_b, wq_bd, wk_bd):
    b, t, _ = p.shape
    nt = t // TM
    r8 = TM // 8
    full = lambda shape: pl.BlockSpec(shape, lambda bi, i: (0,) * len(shape))
    cb = C_MX // ML_W
    blk = pl.BlockSpec((1, TM, ML_W), lambda bi, i: (bi, i, 0))
    return pl.pallas_call(
        functools.partial(_ml_prep_kernel, n_tiles=nt),
        grid=(b, nt),
        in_specs=[
            pl.BlockSpec((1, TM, ML_W), lambda bi, i: (bi, i, cb)),
            pl.BlockSpec((1, 8, ML_W), lambda bi, i: (bi, jnp.maximum(i * r8 - 1, 0), cb)),
            pl.BlockSpec((1, 8, ML_W), lambda bi, i: (bi, jnp.minimum((i + 1) * r8, nt * r8 - 1), cb)),
            full((8, ML_W)), full((1, ML_W)), full((ML_W, ML_W)), full((ML_W, ML_W)),
        ],
        out_specs=[blk, blk, blk],
        out_shape=[jax.ShapeDtypeStruct((b, t, ML_W), F32),
                   jax.ShapeDtypeStruct((b, t, ML_W), BF16),
                   jax.ShapeDtypeStruct((b, t, ML_W), BF16)],
        compiler_params=_cparams(("parallel", "parallel")),
        name="mlstm_prep",
    )(p, p, p, conv_w8, conv_b.reshape(1, ML_W), wq_bd, wk_bd)


def _scan_chunk(d, step, n_ctx_chunks, n_chunks):
    bwd = jnp.where(step < n_ctx_chunks, n_ctx_chunks - 1 - step, n_chunks - 1 - (step - n_ctx_chunks))
    return jnp.where(d == 0, step, bwd)


def _ml_chunk(d, q, k, v, g, s_ref, m_ref, base):
    L = ML_CHUNK
    row = lax.broadcasted_iota(jnp.int32, (L, L), 0)
    col = lax.broadcasted_iota(jnp.int32, (L, L), 1)
    mask = col <= row if d == 0 else col >= row
    tri = mask.astype(F32)
    lane = lax.broadcasted_iota(jnp.int32, (1, LANE), 1)
    eye8 = (lax.broadcasted_iota(jnp.int32, (8, LANE), 0)
            == lax.broadcasted_iota(jnp.int32, (8, LANE), 1)).astype(F32)

    lf = _log_sigmoid(g)
    bc = jnp.dot(tri, lf, precision=HIGHEST, preferred_element_type=F32)
    g_rows = _nt_dot(eye8, g, precision=HIGHEST)
    bc_rows = _nt_dot(eye8, bc, precision=HIGHEST)

    outs = []
    for pair in range(ML_HEADS // 2):
        sl = slice(pair * LANE, (pair + 1) * LANE)
        q_blk, k_blk, v_blk = q[:, sl], k[:, sl], v[:, sl]
        pair_out = []
        for sub in range(2):
            h = pair * 2 + sub
            head_lanes = (lane >= sub * ML_DH) & (lane < (sub + 1) * ML_DH)
            qh = jnp.where(head_lanes, q_blk, jnp.zeros_like(q_blk))
            vs = v_blk if sub == 0 else pltpu.roll(v_blk, ML_DH, 1)
            v_ext = jnp.where(lane < ML_DH, vs, jnp.where(lane == ML_DH, 1.0, 0.0)).astype(BF16)

            li_c = g[:, SM_GATE + h:SM_GATE + h + 1]
            bc_c = bc[:, SM_GATE + 4 + h:SM_GATE + 5 + h]
            li_r = g_rows[h:h + 1, :]
            bc_r = bc_rows[4 + h:5 + h, :]
            m_st = m_ref[base + h][0:1, 0:1]

            dmat = jnp.where(mask, bc_c + (li_r - bc_r), -jnp.inf)
            inter = bc_c + m_st
            m_t = jnp.maximum(inter, jnp.max(dmat, axis=-1, keepdims=True))
            e = jnp.exp(dmat - m_t)
            s = (_nt_dot(qh, k_blk) * e).astype(BF16)
            tot = (jnp.dot(s, v_ext, preferred_element_type=F32)
                   + jnp.exp(inter - m_t) * jnp.dot(qh, s_ref[base + h].astype(BF16),
                                                    preferred_element_type=F32))
            den = tot[:, ML_DH:ML_DH + 1]
            pair_out.append(tot / jnp.maximum(jnp.abs(den), jnp.exp(-m_t)))

            b_end = bc_c[L - 1:L] if d == 0 else bc_c[0:1]
            g_col = b_end - bc_c + li_c
            m_new = jnp.maximum(b_end + m_st, jnp.max(g_col, axis=0, keepdims=True))
            kw = jnp.where(head_lanes, k_blk.astype(F32) * jnp.exp(g_col - m_new), 0.0).astype(BF16)
            s_ref[base + h] = jnp.exp(b_end + m_st - m_new) * s_ref[base + h] + _tn_dot(kw, v_ext)
            m_ref[base + h] = jnp.broadcast_to(m_new, m_ref.shape[1:])
        outs.append(jnp.where(lane < ML_DH, pair_out[0], pltpu.roll(pair_out[1], ML_DH, 1)))
    return jnp.concatenate(outs, axis=-1)


def _ml_scan_kernel(qf_ref, kf_ref, vf_ref, smf_ref, qb_ref, kb_ref, vb_ref, smb_ref, gb_ref,
                    hf_ref, hb_ref, s_ref, m_ref):
    @pl.when(pl.program_id(0) == 0)
    def _():
        s_ref[...] = jnp.zeros_like(s_ref)
        m_ref[...] = jnp.zeros_like(m_ref)

    nb = qf_ref.shape[0]
    streams = ((0, qf_ref, kf_ref, vf_ref, smf_ref, hf_ref), (1, qb_ref, kb_ref, vb_ref, smb_ref, hb_ref))
    for d, q_ref, k_ref, v_ref, sm_ref, h_ref in streams:
        for bi in range(nb):
            h_ref[bi] = _ml_chunk(d, q_ref[bi], k_ref[bi], v_ref[bi], sm_ref[bi] + gb_ref[d],
                                  s_ref, m_ref, (d * nb + bi) * ML_HEADS)


def _ml_scan(q, k, p, gate_b_p):
    b, t, _ = q.shape
    nc = t // ML_CHUNK
    chunk = functools.partial(_scan_chunk, n_ctx_chunks=TM // ML_CHUNK, n_chunks=nc)
    blk = lambda d, w, cb: pl.BlockSpec((b, ML_CHUNK, w), lambda s: (0, chunk(d, s), cb))
    stream = lambda d: [blk(d, ML_W, 0), blk(d, ML_W, 0), blk(d, ML_W, C_MV // ML_W),
                        blk(d, LANE, C_SMF // LANE + d)]
    shp = jax.ShapeDtypeStruct((b, t, ML_W), F32)
    return pl.pallas_call(
        _ml_scan_kernel,
        grid=(nc,),
        in_specs=stream(0) + stream(1) + [pl.BlockSpec((2, 1, LANE), lambda s: (0, 0, 0))],
        out_specs=[blk(0, ML_W, 0), blk(1, ML_W, 0)],
        out_shape=[shp, shp],
        scratch_shapes=[pltpu.VMEM((2 * b * ML_HEADS, LANE, LANE), F32),
                        pltpu.VMEM((2 * b * ML_HEADS, 8, LANE), F32)],
        compiler_params=_cparams(("arbitrary",)),
        name="mlstm_scan",
    )(q, k, p, p, q, k, p, p, gate_b_p)


def _gla_chunk(d, q, k, v, sm, wa, ba, s_ref, si):
    L = GLA_CHUNK
    row = lax.broadcasted_iota(jnp.int32, (L, L), 0)
    col = lax.broadcasted_iota(jnp.int32, (L, L), 1)
    mask = col <= row if d == 0 else col >= row
    tri = mask.astype(F32)
    lane_k = lax.broadcasted_iota(jnp.int32, (1, GLA_HEADS * GLA_DK), 1)
    lane_v = lax.broadcasted_iota(jnp.int32, (1, GLA_HEADS * GLA_DV), 1)
    eye = (lax.broadcasted_iota(jnp.int32, (LANE, LANE), 0)
           == lax.broadcasted_iota(jnp.int32, (LANE, LANE), 1)).astype(F32)

    pre = jnp.dot(sm.astype(BF16), wa.astype(BF16), preferred_element_type=F32) + ba
    loga = _log_sigmoid(pre) * (1.0 / GLA_TAU)
    bc = _dot_sel(tri, loga)
    ref_row = bc[L // 2 - 1:L // 2]
    b_end = bc[L - 1:L] if d == 0 else bc[0:1]

    q = q * (GLA_DK ** -0.5)
    v = v.astype(BF16)
    q_in = (q * jnp.exp(bc - ref_row))
    k_in = (k * jnp.exp(ref_row - bc)).astype(BF16)
    q_st = (q * jnp.exp(bc)).astype(BF16)
    k_st = (k * jnp.exp(b_end - bc)).astype(BF16)

    blockdiag = (lax.broadcasted_iota(jnp.int32, s_ref.shape[1:], 0) // GLA_DK
                 == lax.broadcasted_iota(jnp.int32, s_ref.shape[1:], 1) // GLA_DV)
    s_old = s_ref[si]
    o = jnp.dot(q_st, jnp.where(blockdiag, s_old, 0.0).astype(BF16), preferred_element_type=F32)
    for h in range(GLA_HEADS):
        qh = jnp.where(lane_k // GLA_DK == h, q_in, 0.0).astype(BF16)
        att = jnp.where(mask, _nt_dot(qh, k_in), 0.0).astype(BF16)
        oh = jnp.dot(att, v, preferred_element_type=F32)
        o = o + jnp.where(lane_v // GLA_DV == h, oh, 0.0)

    decay_col = jnp.exp(_dot_sel(eye, jnp.broadcast_to(b_end, (8, LANE)), nt=True)[:, 0:1])
    s_ref[si] = decay_col * s_old + _tn_dot(k_st, v)
    return o


def _gla_scan_kernel(qf_ref, kf_ref, vf_ref, smf_ref, qb_ref, kb_ref, vb_ref, smb_ref, wa_ref, ba_ref,
                     of_ref, ob_ref, s_ref):
    @pl.when(pl.program_id(0) == 0)
    def _():
        s_ref[...] = jnp.zeros_like(s_ref)

    nb = qf_ref.shape[0]
    n_sub = TM // GLA_CHUNK
    streams = ((0, qf_ref, kf_ref, vf_ref, smf_ref, of_ref), (1, qb_ref, kb_ref, vb_ref, smb_ref, ob_ref))
    for d, q_ref, k_ref, v_ref, sm_ref, o_ref in streams:
        for bi in range(nb):
            for c in (range(n_sub) if d == 0 else reversed(range(n_sub))):
                rows = slice(c * GLA_CHUNK, (c + 1) * GLA_CHUNK)
                o_ref[bi, rows] = _gla_chunk(d, q_ref[bi, rows], k_ref[bi, rows], v_ref[bi, rows],
                                             sm_ref[bi, rows], wa_ref[d], ba_ref[d], s_ref, d * nb + bi)


def _gla_scan(p, wa_p, ba_p):
    b, t, _ = p.shape
    nc = t // TM
    chunk = functools.partial(_scan_chunk, n_ctx_chunks=1, n_chunks=nc)
    kw, vw = GLA_HEADS * GLA_DK, GLA_HEADS * GLA_DV
    blk = lambda d, w, col: pl.BlockSpec((b, TM, w), lambda s: (0, chunk(d, s), col // w))
    stream = lambda d: [blk(d, kw, C_GQ), blk(d, kw, C_GK), blk(d, vw, C_GV), blk(d, LANE, C_SMF + d * LANE)]
    shp = jax.ShapeDtypeStruct((b, t, vw), F32)
    return pl.pallas_call(
        _gla_scan_kernel,
        grid=(nc,),
        in_specs=stream(0) + stream(1) + [pl.BlockSpec((2, LANE, kw), lambda s: (0, 0, 0)),
                                          pl.BlockSpec((2, 1, kw), lambda s: (0, 0, 0))],
        out_specs=[blk(0, vw, 0), blk(1, vw, 0)],
        out_shape=[shp, shp],
        scratch_shapes=[pltpu.VMEM((2 * b, kw, vw), F32)],
        compiler_params=_cparams(("arbitrary",)),
        name="gla_scan",
    )(p, p, p, p, p, p, p, p, wa_p, ba_p)


OUTPROJ_TILES = 2
OUTPROJ_N_TOK = 11


def _outproj_kernel(*refs, n_tiles):
    n_in = OUTPROJ_TILES * OUTPROJ_N_TOK
    shared = refs[n_in:n_in + 7]
    outs = refs[n_in + 7:]
    for u in range(OUTPROJ_TILES):
        tile = jnp.minimum(OUTPROJ_TILES * pl.program_id(1) + u, n_tiles - 1)
        _outproj_tile(tile == 0, slice(u * TM, (u + 1) * TM),
                      *refs[u * OUTPROJ_N_TOK:(u + 1) * OUTPROJ_N_TOK], *shared, *outs)


def _outproj_tile(is_ctx, rows, ac_ref, al_ref, mhf_ref, mhb_ref, xc_ref, mo_ref, gof_ref, gob_ref, gr_ref, x_ref,
                  mod_ref, mnw_ref, msk_ref, gnw_ref, wout_ref, n2w_ref, wr_ref, br_ref,
                  xo_ref, h2_ref, route_ref):
    grp64 = (lax.broadcasted_iota(jnp.int32, (ML_W, ML_W), 0) // ML_DH
             == lax.broadcasted_iota(jnp.int32, (ML_W, ML_W), 1) // ML_DH).astype(F32) * (1.0 / ML_DH)

    def head_norm(x, w):
        ms = _dot_sel(grp64, x * x, right=True)
        return x * lax.rsqrt(ms + EPS) * w

    m_l = jax.nn.sigmoid(mo_ref[0]) * (head_norm(mhf_ref[0] + mhb_ref[0], mnw_ref[...])
                                       + msk_ref[...] * xc_ref[0])
    g_l = head_norm(gof_ref[0] + gob_ref[0], gnw_ref[...]) * _silu(gr_ref[0])
    na = MLA_HEADS * MLA_V
    a = jnp.where(is_ctx, ac_ref[0], al_ref[0])
    res = (jnp.dot(a, wout_ref[0:na], preferred_element_type=F32)
           + jnp.dot(m_l.astype(BF16), wout_ref[na:na + ML_W], preferred_element_type=F32)
           + jnp.dot(g_l.astype(BF16), wout_ref[na + ML_W:], preferred_element_type=F32))
    mod = mod_ref[0]
    x = x_ref[0] + mod[2:3] * res
    xo_ref[0, rows] = x
    h2 = (x * lax.rsqrt(jnp.mean(x * x, axis=-1, keepdims=True) + EPS) * n2w_ref[...]
          * (1.0 + mod[4:5]) + mod[3:4])
    half = h2.shape[-1] // 2
    h2_ref[0, rows] = _pack_bf16_pair(h2[:, :half], h2[:, half:])

    logits = jnp.dot(h2, wr_ref[...], precision=HIGHEST, preferred_element_type=F32) + br_ref[...]
    lane = lax.broadcasted_iota(jnp.int32, (1, LANE), 1)
    lane_f = lane.astype(F32)
    neg = -jnp.inf
    gl = jnp.where(lane < R_EXP, logits, neg)
    gmax = jnp.max(gl, axis=-1, keepdims=True)
    g_w = 1.0 / jnp.sum(jnp.exp(gl - gmax), axis=-1, keepdims=True)
    g_i = jnp.min(jnp.where(gl == gmax, lane_f, float(LANE)), axis=-1, keepdims=True)
    grp_of_lane = ((lane - R_EXP) // EXP_PER_GROUP).astype(F32)
    in_grp = (lane >= R_EXP) & (lane < R_EXP + N_EXPERTS) & (grp_of_lane == g_i)
    el = jnp.where(in_grp, logits, neg)
    m1 = jnp.max(el, axis=-1, keepdims=True)
    i1 = jnp.min(jnp.where(el == m1, lane_f, float(LANE)), axis=-1, keepdims=True)
    el2 = jnp.where(lane_f == i1, neg, el)
    m2 = jnp.max(el2, axis=-1, keepdims=True)
    i2 = jnp.min(jnp.where(el2 == m2, lane_f, float(LANE)), axis=-1, keepdims=True)
    p2 = jnp.exp(m2 - m1)
    w1 = g_w / (1.0 + p2)
    route_ref[0, rows] = jnp.where(lane == RT_E1, i1 - R_EXP, jnp.where(lane == RT_E1 + 1, i2 - R_EXP,
                             jnp.where(lane == RT_W1, w1, jnp.where(lane == RT_W1 + 1, p2 * w1, 0.0))))


def _outproj(a_ctx, a_lat, mh, xconv, p, go, xs, mods, ml_norm_w, ml_skip, gla_norm_w, w_out_b, norm2_w, wr_p, br_p):
    b, t, d = xs.shape
    nt = t // TM
    full = lambda shape: pl.BlockSpec(shape, lambda bi, i: (0,) * len(shape))
    na = MLA_HEADS * MLA_V

    out = lambda w: pl.BlockSpec((1, OUTPROJ_TILES * TM, w), lambda bi, i: (bi, i, 0))

    def tile_specs(u):
        tile = lambda i: jnp.minimum(OUTPROJ_TILES * i + u, nt - 1)
        tok = lambda w, cb=0: pl.BlockSpec((1, TM, w), lambda bi, i: (bi, tile(i), cb))
        ins = [pl.BlockSpec((1, TM, na), lambda bi, i: (bi, 0, 0)),
               pl.BlockSpec((1, TM, na), lambda bi, i: (bi, jnp.maximum(tile(i) - 1, 0), 0)),
               tok(ML_W), tok(ML_W), tok(ML_W), tok(ML_W, C_MO // ML_W),
               tok(ML_W), tok(ML_W), tok(ML_W, C_GR // ML_W), tok(d),
               pl.BlockSpec((1, 6, d), lambda bi, i: (_mod_row(bi, tile(i)), 0, 0))]
        return ins

    specs = [tile_specs(u) for u in range(OUTPROJ_TILES)]
    tok_args = (a_ctx, a_lat, mh[0], mh[1], xconv, p, go[0], go[1], p, xs, mods)
    assert len(tok_args) == OUTPROJ_N_TOK
    shapes = [jax.ShapeDtypeStruct((b, t, d), F32), jax.ShapeDtypeStruct((b, t, d // 2), jnp.uint32),
              jax.ShapeDtypeStruct((b, t, LANE), F32)]
    outs = pl.pallas_call(
        functools.partial(_outproj_kernel, n_tiles=nt),
        grid=(b, pl.cdiv(nt, OUTPROJ_TILES)),
        in_specs=sum(specs, []) + [
            full((1, ML_W)), full((1, ML_W)), full((1, ML_W)), full((d, d)), full((1, d)),
            full((d, LANE)), full((1, LANE))],
        out_specs=[out(d), out(d // 2), out(LANE)],
        out_shape=shapes,
        compiler_params=_cparams(("parallel", "arbitrary")),
        name="out_proj_router",
    )(*(tok_args * OUTPROJ_TILES), ml_norm_w.reshape(1, -1), ml_skip.reshape(1, -1),
      gla_norm_w.reshape(1, -1), w_out_b, norm2_w.reshape(1, -1), wr_p, br_p)
    return outs


def _dispatch(route, n_tiles):
    n = route.shape[0]
    flat = route[:, RT_E1:RT_E1 + 2].astype(jnp.int32).reshape(-1)
    onehot = (flat[:, None] == jnp.arange(N_EXPERTS, dtype=jnp.int32)[None, :]).astype(jnp.int32)
    csum = jnp.cumsum(onehot, axis=0)
    rank = jnp.sum(csum * onehot, axis=1) - 1
    padded = (csum[-1] + TM - 1) // TM * TM
    ends = jnp.cumsum(padded)
    pos = (ends - padded)[flat] + rank
    src = jnp.zeros((n_tiles * TM,), jnp.int32).at[pos].set(jnp.arange(2 * n, dtype=jnp.int32) // 2)
    tile_start = jnp.arange(n_tiles, dtype=jnp.int32) * TM
    tile_exp = jnp.minimum(jnp.sum((ends[None, :] <= tile_start[:, None]).astype(jnp.int32), axis=1),
                           N_EXPERTS - 1)
    tile_on = (tile_start < ends[-1]).astype(jnp.int32)
    pos = pos.reshape(n // TM, TM, 2)
    return (src.reshape(n_tiles, 1, TM), tile_exp, tile_on,
            pos[:, :, 0].reshape(n // TM, 1, TM), pos[:, :, 1].reshape(n // TM, 1, TM))


def _gather_rows(src_hbm, idx_ref, dst, sem):
    def body(j, carry):
        pltpu.make_async_copy(src_hbm.at[pl.ds(idx_ref[0, 0, j], 1)], dst.at[pl.ds(j, 1)], sem).start()
        return carry

    lax.fori_loop(0, TM, body, 0, unroll=GATHER_UNROLL)


def _wait_rows(src_hbm, dst, sem):
    pltpu.make_async_copy(src_hbm.at[pl.ds(0, TM)], dst, sem).wait()


def _pack_bf16_pair(lo, hi):
    lo_b = lax.bitcast_convert_type(lo.astype(BF16).astype(F32), jnp.uint32) >> 16
    hi_b = lax.bitcast_convert_type(hi.astype(BF16).astype(F32), jnp.uint32) & jnp.uint32(0xFFFF0000)
    return hi_b | lo_b


def _unpack_bf16_pair(w):
    return (lax.bitcast_convert_type(w << 16, F32),
            lax.bitcast_convert_type(w & jnp.uint32(0xFFFF0000), F32))


def _experts_kernel(texp_ref, ton_ref, idx_ref, idxn_ref, h2_hbm, wg_ref, wu_ref, wd_ref, y_ref,
                    buf, wgu_b, wd_b, sem):
    r = pl.program_id(0)
    slot = r % 2
    half = wgu_b.shape[0] // 2

    @pl.when(r == 0)
    def _():
        _gather_rows(h2_hbm, idx_ref, buf.at[0], sem.at[0])

    last = pl.num_programs(0) - 1

    @pl.when(jnp.logical_and(r < last, ton_ref[jnp.minimum(r + 1, last)] == 1))
    def _():
        _gather_rows(h2_hbm, idxn_ref, buf.at[1 - slot], sem.at[1 - slot])

    @pl.when(ton_ref[r] == 1)
    def _():
        @pl.when(jnp.logical_or(r == 0, texp_ref[r] != texp_ref[jnp.maximum(r - 1, 0)]))
        def _():
            wgu_b[:, :D_EXPERT] = wg_ref[0, 0].astype(BF16)
            wgu_b[:, D_EXPERT:] = wu_ref[0, 0].astype(BF16)
            wd_b[...] = wd_ref[0, 0].astype(BF16)

        _wait_rows(h2_hbm, buf.at[slot], sem.at[slot])
        x_lo, x_hi = _unpack_bf16_pair(buf[slot])
        gu = (jnp.dot(x_lo.astype(BF16), wgu_b[0:half], preferred_element_type=F32)
              + jnp.dot(x_hi.astype(BF16), wgu_b[half:], preferred_element_type=F32))
        act = (_silu(gu[:, :D_EXPERT]) * gu[:, D_EXPERT:]).astype(BF16)
        y = jnp.dot(act, wd_b[...], preferred_element_type=F32)
        y_ref[...] = _pack_bf16_pair(y[:, :half], y[:, half:])

    @pl.when(ton_ref[r] == 0)
    def _():
        y_ref[...] = jnp.zeros_like(y_ref)


def _experts(h2p, src, tile_exp, tile_on, w_gate, w_up, w_down, layer):
    n, dh = h2p.shape
    d = 2 * dh
    n_tiles = src.shape[0]
    idx = lambda nxt: pl.BlockSpec((1, 1, TM), lambda r, te, to: (jnp.minimum(r + nxt, n_tiles - 1), 0, 0),
                                   memory_space=pltpu.SMEM)
    wspec = lambda shape: pl.BlockSpec((1, 1) + shape, lambda r, te, to: (layer, te[r], 0, 0))
    return pl.pallas_call(
        _experts_kernel,
        grid_spec=pltpu.PrefetchScalarGridSpec(
            num_scalar_prefetch=2,
            grid=(n_tiles,),
            in_specs=[
                idx(0), idx(1),
                pl.BlockSpec(memory_space=pl.ANY),
                wspec((d, D_EXPERT)), wspec((d, D_EXPERT)), wspec((D_EXPERT, d)),
            ],
            out_specs=pl.BlockSpec((TM, dh), lambda r, te, to: (r, 0)),
            scratch_shapes=[pltpu.VMEM((2, TM, dh), jnp.uint32), pltpu.VMEM((d, 2 * D_EXPERT), BF16),
                            pltpu.VMEM((D_EXPERT, d), BF16), pltpu.SemaphoreType.DMA((2,))],
        ),
        out_shape=jax.ShapeDtypeStruct((n_tiles * TM, dh), jnp.uint32),
        compiler_params=_cparams(("arbitrary",)),
        name="moe_experts",
    )(tile_exp, tile_on, src, src, h2p, w_gate, w_up, w_down)


def _combine_kernel(p1_ref, p2_ref, p1n_ref, p2n_ref, route_ref, x_ref, mod_ref, y_hbm, o_ref, buf, sem):
    g = pl.program_id(0)
    slot = g % 2
    half = buf.shape[-1]

    def gather(pa, pb, s):
        _gather_rows(y_hbm, pa, buf.at[s, 0], sem.at[s])
        _gather_rows(y_hbm, pb, buf.at[s, 1], sem.at[s])

    @pl.when(g == 0)
    def _():
        gather(p1_ref, p2_ref, 0)

    @pl.when(g + 1 < pl.num_programs(0))
    def _():
        gather(p1n_ref, p2n_ref, 1 - slot)

    _wait_rows(y_hbm, buf.at[slot, 0], sem.at[slot])
    _wait_rows(y_hbm, buf.at[slot, 1], sem.at[slot])
    route = route_ref[...]
    w1, w2 = route[:, RT_W1:RT_W1 + 1], route[:, RT_W1 + 1:RT_W1 + 2]
    gate = mod_ref[0][5:6]
    for part, (y1, y2) in enumerate(zip(_unpack_bf16_pair(buf[slot, 0]), _unpack_bf16_pair(buf[slot, 1]))):
        cols = slice(part * half, (part + 1) * half)
        o_ref[:, cols] = x_ref[:, cols] + gate[:, cols] * (w1 * y1 + w2 * y2)


def _combine(y, pos1, pos2, route, xs2, mods, nt):
    n, d = xs2.shape
    n_tok_tiles = n // TM
    idx = lambda nxt: pl.BlockSpec((1, 1, TM), lambda g: (jnp.minimum(g + nxt, n_tok_tiles - 1), 0, 0),
                                   memory_space=pltpu.SMEM)
    tok = lambda w: pl.BlockSpec((TM, w), lambda g: (g, 0))
    return pl.pallas_call(
        _combine_kernel,
        grid=(n_tok_tiles,),
        in_specs=[
            idx(0), idx(0), idx(1), idx(1), tok(LANE), tok(d),
            pl.BlockSpec((1, 6, d), lambda g: (_mod_row(g // nt, g % nt), 0, 0)),
            pl.BlockSpec(memory_space=pl.ANY),
        ],
        out_specs=tok(d),
        out_shape=jax.ShapeDtypeStruct((n, d), F32),
        scratch_shapes=[pltpu.VMEM((2, 2, TM, d // 2), jnp.uint32), pltpu.SemaphoreType.DMA((2,))],
        compiler_params=_cparams(("arbitrary",)),
        name="moe_combine",
    )(pos1, pos2, pos1, pos2, route, xs2, mods, y)


def _moe(h2p, route, xs, mods, w_gate, w_up, w_down, layer):
    b, t, d = xs.shape
    n = b * t
    n_tiles = 2 * n // TM + N_EXPERTS
    route2 = route.reshape(n, LANE)
    src, tile_exp, tile_on, pos1, pos2 = _dispatch(route2, n_tiles)
    y = _experts(h2p.reshape(n, d // 2), src, tile_exp, tile_on, w_gate, w_up, w_down, layer)
    return _combine(y, pos1, pos2, route2, xs.reshape(n, d), mods, t // TM).reshape(b, t, d)


def _rope_tables(n_ctx, n_lat):
    rows = n_lat // GRID_W
    row = jnp.broadcast_to(jnp.arange(rows, dtype=F32)[:, None], (rows, GRID_W)).reshape(-1)
    col = jnp.broadcast_to(jnp.arange(GRID_W, dtype=F32)[None, :], (rows, GRID_W)).reshape(-1)
    n_freq = MLA_ROPE // 4
    inv = ROPE_THETA ** (-jnp.arange(n_freq, dtype=F32) / n_freq)
    ang = jnp.concatenate([row[:, None] * inv, col[:, None] * inv], axis=-1)
    cos, sin = jnp.cos(ang), jnp.sin(ang)
    half = MLA_ROPE // 2
    z = lambda w: jnp.zeros((n_lat, w), F32)
    o = lambda w: jnp.ones((n_lat, w), F32)
    tail = LANE - MLA_QK
    cos_t = jnp.concatenate([o(MLA_NOPE), cos, cos, o(tail)], axis=-1)
    sa_t = jnp.concatenate([z(MLA_NOPE + half), sin, z(tail)], axis=-1)
    sb_t = jnp.concatenate([z(MLA_NOPE), -sin, z(half + tail)], axis=-1)
    ctx1 = jnp.ones((n_ctx, LANE), F32)
    ctx0 = jnp.zeros((n_ctx, LANE), F32)
    return (jnp.concatenate([ctx1, cos_t], 0), jnp.concatenate([ctx0, sa_t], 0),
            jnp.concatenate([ctx0, sb_t], 0))


def _pad_cols(a, width):
    return jnp.pad(a, [(0, 0)] * (a.ndim - 1) + [(0, width - a.shape[-1])])


def _layer_weights(w_in, w_uq, w_ukv, q_norm_w, k_norm_w, ml_conv_w, ml_wq, ml_wk, ml_gate_b,
                   gla_wa, gla_ba, w_out, w_grp, b_grp, w_erouter, b_erouter):
    d = w_in.shape[0]
    o = np.cumsum((0, Q_LORA, KV_LORA, MLA_ROPE, ML_W, ML_W, ML_W, 4 * ML_HEADS, GLA_HEADS * GLA_DK,
                   GLA_HEADS * GLA_DK, GLA_HEADS * GLA_DV, GLA_HEADS * GLA_DV, 2 * GLA_LR))
    seg = lambda j: w_in[:, o[j]:o[j + 1]]
    cq, ckv, kr, mx, mv, mo, mg, gq, gk, gv, gr, ga = (seg(j) for j in range(12))
    z = lambda w: jnp.zeros((d, w), F32)

    def small(di):
        return jnp.concatenate([mg[:, di * 8:(di + 1) * 8], ga[:, di * GLA_LR:(di + 1) * GLA_LR],
                                z(SM_KR - SM_GA - GLA_LR), kr, z(LANE - SM_KR - MLA_ROPE)], axis=-1)

    w_in_p = jnp.concatenate([cq, mx, mv, mo, gv, gr, ckv, gq, gk, small(0), small(1)], axis=-1).astype(BF16)

    wuq_p = _pad_cols(w_uq.reshape(Q_LORA, MLA_HEADS, MLA_QK), LANE).reshape(Q_LORA, -1).astype(BF16)
    ukv = w_ukv.reshape(KV_LORA, MLA_HEADS, MLA_NOPE + MLA_V)
    wuk_p = _pad_cols(ukv[..., :MLA_NOPE], LANE).reshape(KV_LORA, -1).astype(BF16)
    wuv_p = _pad_cols(ukv[..., MLA_NOPE:], LANE).reshape(KV_LORA, -1).astype(BF16)
    qn_p = _pad_cols(q_norm_w.reshape(1, -1), LANE)
    kn_p = _pad_cols(k_norm_w.reshape(1, -1), LANE)

    conv_w8 = jnp.pad(ml_conv_w, ((0, 8 - ML_CONV), (0, 0)))
    bd = lambda w: jax.scipy.linalg.block_diag(*[w[h] for h in range(ML_HEADS)])
    wq_bd = (bd(ml_wq) * (ML_DH ** -0.5)).astype(BF16)
    wk_bd = bd(ml_wk).astype(BF16)
    gate_b_p = _pad_cols(ml_gate_b.reshape(2, 1, 2 * ML_HEADS), LANE)

    wa_p = jnp.pad(gla_wa, ((0, 0), (SM_GA, LANE - SM_GA - GLA_LR), (0, 0)))
    ba_p = gla_ba.reshape(2, 1, -1)

    wr_p = _pad_cols(jnp.concatenate([w_grp, w_erouter], axis=-1), LANE)
    br_p = _pad_cols(jnp.concatenate([b_grp, b_erouter]).reshape(1, -1), LANE)
    return dict(w_in_p=w_in_p, wuq_p=wuq_p, wuk_p=wuk_p, wuv_p=wuv_p, qn_p=qn_p, kn_p=kn_p,
                conv_w8=conv_w8, wq_bd=wq_bd, wk_bd=wk_bd, gate_b_p=gate_b_p, wa_p=wa_p, ba_p=ba_p,
                w_out_b=w_out.astype(BF16), wr_p=wr_p, br_p=br_p)


def kernel(x, c, ctx, c_ctx, w_mod, b_mod, norm1_w, w_in, q_a_norm, w_uq, kv_a_norm, w_ukv,
           q_norm_w, k_norm_w, ml_conv_w, ml_conv_b, ml_wq, ml_wk, ml_gate_b, ml_norm_w, ml_skip,
           gla_wa, gla_ba, gla_norm_w, w_out, norm2_w, w_grp, b_grp, w_erouter, b_erouter,
           w_gate, w_up, w_down):
    b, s, d = x.shape
    n_ctx = ctx.shape[1]
    depth = w_mod.shape[0]
    assert n_ctx == TM and s % TM == 0 and b == 2

    cc = jnp.concatenate([c, c_ctx[None, :], jnp.zeros((8 - b - 1, d), F32)], axis=0)
    mods_all = _mods(cc, w_mod, b_mod).reshape(depth, 8, 6, d)
    cos_t, sa_t, sb_t = _rope_tables(n_ctx, s)
    xs = jnp.concatenate([ctx, x], axis=1)

    for l in range(depth):
        w = _layer_weights(w_in[l], w_uq[l], w_ukv[l], q_norm_w[l], k_norm_w[l], ml_conv_w[l],
                           ml_wq[l], ml_wk[l], ml_gate_b[l], gla_wa[l], gla_ba[l], w_out[l],
                           w_grp[l], b_grp[l], w_erouter[l], b_erouter[l])
        mods = mods_all[l]
        p = _inproj(xs, mods, norm1_w[l], w["w_in_p"])
        q, k, v = _mla_prep(p, q_a_norm[l], w["wuq_p"], kv_a_norm[l], w["wuk_p"], w["wuv_p"],
                            w["qn_p"], w["kn_p"], cos_t, sa_t, sb_t)
        a_lat = _attention_lat(q, k, v)
        a_ctx = _attention_ctx(q, k, v) if l < depth - 1 else jnp.zeros((b, TM, MLA_HEADS * MLA_V), BF16)
        xconv, mq, mk = _ml_prep(p, w["conv_w8"], ml_conv_b[l], w["wq_bd"], w["wk_bd"])
        mh = _ml_scan(mq, mk, p, w["gate_b_p"])
        go = _gla_scan(p, w["wa_p"], w["ba_p"])
        xs, h2, route = _outproj(a_ctx, a_lat, mh, xconv, p, go, xs, mods, ml_norm_w[l], ml_skip[l], gla_norm_w[l],
                                w["w_out_b"], norm2_w[l], w["wr_p"], w["br_p"])
        xs = _moe(h2, route, xs, mods, w_gate, w_up, w_down, l)
    return xs[:, n_ctx:, :]
```

```python
import functools

import jax
import jax.numpy as jnp
import numpy as np
from jax import lax
from jax.experimental import pallas as pl
from jax.experimental.pallas import tpu as pltpu

F32 = jnp.float32
BF16 = jnp.bfloat16
HIGHEST = lax.Precision.HIGHEST

EPS = 1e-6
GRID_W = 64
ROPE_THETA = 10000.0

MLA_HEADS = 8
MLA_NOPE = 64
MLA_ROPE = 32
MLA_QK = MLA_NOPE + MLA_ROPE
MLA_V = 64
Q_LORA = 256
KV_LORA = 128

ML_HEADS = 4
ML_DH = 64
ML_W = ML_HEADS * ML_DH
ML_CONV = 5

GLA_HEADS = 4
GLA_DK = 32
GLA_DV = 64
GLA_LR = 16
GLA_TAU = 16.0

N_GROUPS = 4
EXP_PER_GROUP = 8
N_EXPERTS = N_GROUPS * EXP_PER_GROUP
D_EXPERT = 256

LANE = 128
TM = 256
ML_CHUNK = 256
GLA_CHUNK = 128
VMEM_LIMIT = 56 * 1024 * 1024

C_CQ, C_MX, C_MV, C_MO, C_GV, C_GR = 0, 256, 512, 768, 1024, 1280
C_CKV, C_GQ, C_GK, C_SMF, C_SMB = 1536, 1664, 1792, 1920, 2048
D_INP = 2176
SM_GATE = 0
SM_GA = 8
SM_KR = 64
R_GRP = 0
R_EXP = 4
RT_E1 = 0
RT_W1 = 2
GATHER_UNROLL = 8


def _cparams(sem):
    return pltpu.CompilerParams(dimension_semantics=sem, vmem_limit_bytes=VMEM_LIMIT)


def _silu(x):
    return x * jax.nn.sigmoid(x)


def _log_sigmoid(x):
    return -(jnp.maximum(-x, 0.0) + jnp.log1p(jnp.exp(-jnp.abs(x))))


def _nt_dot(a, b, **kw):
    return lax.dot_general(a, b, (((1,), (1,)), ((), ())), preferred_element_type=F32, **kw)


def _split3(x):
    hi = x.astype(BF16)
    r = x - hi.astype(F32)
    mid = r.astype(BF16)
    return hi, mid, (r - mid.astype(F32)).astype(BF16)


def _dot_sel(sel, x, nt=False, right=False):
    s = sel.astype(BF16)
    dot = _nt_dot if nt else functools.partial(jnp.dot, preferred_element_type=F32)
    return sum((dot(piece, s) if right else dot(s, piece)) for piece in _split3(x))


def _tn_dot(a, b, **kw):
    return lax.dot_general(a, b, (((0,), (0,)), ((), ())), preferred_element_type=F32, **kw)


def _mods_kernel(cc_ref, w_ref, b_ref, o_ref):
    a = _silu(cc_ref[...])
    o_ref[0] = jnp.dot(a, w_ref[0], precision=HIGHEST, preferred_element_type=F32) + b_ref[0]


def _mods(cc, w_mod, b_mod):
    depth, d, d6 = w_mod.shape
    nb = 1536
    return pl.pallas_call(
        _mods_kernel,
        grid=(depth, d6 // nb),
        in_specs=[
            pl.BlockSpec((8, d), lambda l, j: (0, 0)),
            pl.BlockSpec((1, d, nb), lambda l, j: (l, 0, j)),
            pl.BlockSpec((1, 1, nb), lambda l, j: (l, 0, j)),
        ],
        out_specs=pl.BlockSpec((1, 8, nb), lambda l, j: (l, 0, j)),
        out_shape=jax.ShapeDtypeStruct((depth, 8, d6), F32),
        compiler_params=_cparams(("arbitrary", "arbitrary")),
        name="adaln_mods",
    )(cc, w_mod, b_mod.reshape(depth, 1, d6))


def _mod_row(b, i):
    return jnp.where(i == 0, 2, b)


def _inproj_kernel(x_ref, mod_ref, nw_ref, w_ref, o_ref):
    x = x_ref[0]
    y = x * lax.rsqrt(jnp.mean(x * x, axis=-1, keepdims=True) + EPS) * nw_ref[...]
    mod = mod_ref[0]
    h = y * (1.0 + mod[1:2]) + mod[0:1]
    o_ref[0] = jnp.dot(h.astype(BF16), w_ref[...], preferred_element_type=F32)


def _inproj(xs, mods, norm_w, w_in_p):
    b, t, d = xs.shape
    nt = t // TM
    return pl.pallas_call(
        _inproj_kernel,
        grid=(b, nt),
        in_specs=[
            pl.BlockSpec((1, TM, d), lambda bi, i: (bi, i, 0)),
            pl.BlockSpec((1, 6, d), lambda bi, i: (_mod_row(bi, i), 0, 0)),
            pl.BlockSpec((1, d), lambda bi, i: (0, 0)),
            pl.BlockSpec((d, D_INP), lambda bi, i: (0, 0)),
        ],
        out_specs=pl.BlockSpec((1, TM, D_INP), lambda bi, i: (bi, i, 0)),
        out_shape=jax.ShapeDtypeStruct((b, t, D_INP), F32),
        compiler_params=_cparams(("parallel", "parallel")),
        name="in_proj",
    )(xs, mods, norm_w.reshape(1, d), w_in_p)


def _mla_prep_kernel(cq_ref, ckv_ref, sm_ref, qan_ref, wuq_ref, kvan_ref, wuk_ref, wuv_ref,
                     qn_ref, kn_ref, cos_ref, sa_ref, sb_ref, q_ref, k_ref, v_ref):
    cq = cq_ref[0]
    cqn = cq * lax.rsqrt(jnp.mean(cq * cq, axis=-1, keepdims=True) + EPS) * qan_ref[...]
    qall = jnp.dot(cqn.astype(BF16), wuq_ref[...], preferred_element_type=F32)
    ckv = ckv_ref[0]
    ckvn = (ckv * lax.rsqrt(jnp.mean(ckv * ckv, axis=-1, keepdims=True) + EPS)
            * kvan_ref[...]).astype(BF16)
    kall = jnp.dot(ckvn, wuk_ref[...], preferred_element_type=F32)
    vall = jnp.dot(ckvn, wuv_ref[...], preferred_element_type=F32)
    lane = lax.broadcasted_iota(jnp.int32, (1, LANE), 1)
    kr = jnp.where((lane >= SM_KR) & (lane < SM_KR + MLA_ROPE), sm_ref[0], 0.0)
    cos, sin = cos_ref[...], sa_ref[...] - sb_ref[...]
    r_i = lax.broadcasted_iota(jnp.int32, (LANE, LANE), 0)
    c_i = lax.broadcasted_iota(jnp.int32, (LANE, LANE), 1)
    half = MLA_ROPE // 2
    first = (c_i >= MLA_NOPE) & (c_i < MLA_NOPE + half)
    second = (c_i >= MLA_NOPE + half) & (c_i < MLA_QK)
    rot = jnp.where(first & (r_i == c_i + half), -1.0,
                    jnp.where(second & (r_i == c_i - half), 1.0, 0.0)).astype(BF16)

    def rope(x):
        return x * cos + jnp.dot(x.astype(BF16), rot, preferred_element_type=F32) * sin

    def head_norm(x, w):
        return x * lax.rsqrt(jnp.sum(x * x, axis=-1, keepdims=True) * (1.0 / MLA_QK) + EPS) * w

    for h in range(MLA_HEADS):
        sl = slice(h * LANE, (h + 1) * LANE)
        qh = rope(head_norm(qall[:, sl], qn_ref[...]))
        q_ref[0, h] = (qh * Q_SCALE).astype(BF16)
        kh = rope(head_norm(kall[:, sl] + kr, kn_ref[...]))
        k_ref[0, h] = jnp.where(lane == ATT_SHIFT_LANE, 1.0, kh).astype(BF16)
        v_ref[0, h] = jnp.where(lane == MLA_V, 1.0, vall[:, sl]).astype(BF16)


def _mla_prep(p, q_a_norm, wuq_p, kv_a_norm, wuk_p, wuv_p, qn_p, kn_p, cos_t, sa_t, sb_t):
    b, t, _ = p.shape
    nt = t // TM
    hw = MLA_HEADS * LANE
    full = lambda shape: pl.BlockSpec(shape, lambda bi, i: (0,) * len(shape))
    tab = pl.BlockSpec((TM, LANE), lambda bi, i: (i, 0))
    out = pl.BlockSpec((1, MLA_HEADS, TM, LANE), lambda bi, i: (bi, 0, i, 0))
    shp = jax.ShapeDtypeStruct((b, MLA_HEADS, t, LANE), BF16)
    return pl.pallas_call(
        _mla_prep_kernel,
        grid=(b, nt),
        in_specs=[
            pl.BlockSpec((1, TM, Q_LORA), lambda bi, i: (bi, i, C_CQ // Q_LORA)),
            pl.BlockSpec((1, TM, KV_LORA), lambda bi, i: (bi, i, C_CKV // KV_LORA)),
            pl.BlockSpec((1, TM, LANE), lambda bi, i: (bi, i, C_SMF // LANE)),
            full((1, Q_LORA)), full((Q_LORA, hw)), full((1, KV_LORA)),
            full((KV_LORA, hw)), full((KV_LORA, hw)), full((1, LANE)), full((1, LANE)),
            tab, tab, tab,
        ],
        out_specs=[out, out, out],
        out_shape=[shp, shp, shp],
        compiler_params=_cparams(("parallel", "parallel")),
        name="mla_prep",
    )(p, p, p, q_a_norm.reshape(1, -1), wuq_p, kv_a_norm.reshape(1, -1), wuk_p, wuv_p,
      qn_p, kn_p, cos_t, sa_t, sb_t)


Q_SCALE = float(MLA_QK ** -0.5 * np.log2(np.e))
ATT_HP = 2
ATT_NQ = 4
ATT_TK = 2048
ATT_SHIFT_LANE = MLA_QK
ATT_SAFE_MAX = 2.0 ** 100


def _softmax_step(q, kb, vb, m, acc):
    s = _nt_dot(q, kb)
    m_new = jnp.max(s, axis=-1, keepdims=True)
    if m is None:
        return m_new, jnp.dot(jnp.exp2((s - m_new).astype(BF16)), vb, preferred_element_type=F32)
    m_new = jnp.maximum(m, m_new)
    p = jnp.exp2((s - m_new).astype(BF16))
    return m_new, acc * jnp.exp2(m - m_new) + jnp.dot(p, vb, preferred_element_type=F32)


def _attn_ctx_kernel(q_ref, k_ref, v_ref, o_ref):
    outs = []
    for h in range(MLA_HEADS):
        _, acc = _softmax_step(q_ref[0, h], k_ref[0, h], v_ref[0, h], None, None)
        outs.append(acc[:, :MLA_V] / acc[:, MLA_V:MLA_V + 1])
    o_ref[0] = jnp.concatenate(outs, axis=-1).astype(BF16)


def _attention_ctx(q, k, v):
    b, h, _, _ = q.shape
    blk = pl.BlockSpec((1, h, TM, LANE), lambda bi: (bi, 0, 0, 0))
    return pl.pallas_call(
        _attn_ctx_kernel,
        grid=(b,),
        in_specs=[blk, blk, blk],
        out_specs=pl.BlockSpec((1, TM, h * MLA_V), lambda bi: (bi, 0, 0)),
        out_shape=jax.ShapeDtypeStruct((b, TM, h * MLA_V), BF16),
        compiler_params=_cparams(("parallel",)),
        name="mla_attention_ctx",
    )(q, k, v)


def _attn_lat_kernel(*refs, n_blk):
    q_refs, (k_ref, v_ref, o_ref, q_buf, qs_buf) = refs[:ATT_NQ], refs[ATT_NQ:]
    lane = lax.broadcasted_iota(jnp.int32, (1, LANE), 1)

    def kv_block(hh, j):
        off = pl.multiple_of(TM + j * ATT_TK, TM)
        return k_ref[0, hh, pl.ds(off, ATT_TK), :], v_ref[0, hh, pl.ds(off, ATT_TK), :]

    def finish(accs):
        outs = [acc[:, :MLA_V] / acc[:, MLA_V:MLA_V + 1] for acc in accs]
        o_ref[0] = jnp.concatenate(outs, axis=-1).astype(BF16)

    accs = []
    for hh in range(ATT_HP):
        q = jnp.concatenate([qr[0, hh] for qr in q_refs], axis=0)
        q_buf[hh] = q
        kb, vb = k_ref[0, hh, 0:TM, :], v_ref[0, hh, 0:TM, :]
        shift = jnp.max(_nt_dot(q, kb), axis=-1, keepdims=True).astype(BF16)
        qs_buf[hh] = jnp.where(lane == ATT_SHIFT_LANE, -shift, q)
        accs.append(jnp.dot(jnp.exp2(_nt_dot(qs_buf[hh], kb).astype(BF16)), vb, preferred_element_type=F32))

    def fast_body(j, accs):
        new = []
        for hh in range(ATT_HP):
            kb, vb = kv_block(hh, j)
            p = jnp.exp2(_nt_dot(qs_buf[hh], kb).astype(BF16))
            new.append(accs[hh] + jnp.dot(p, vb, preferred_element_type=F32))
        return tuple(new)

    accs = lax.fori_loop(0, n_blk, fast_body, tuple(accs))
    bad = sum(jnp.max(jnp.where(jnp.abs(acc) < ATT_SAFE_MAX, 0.0, 1.0)) for acc in accs)

    @pl.when(bad == 0.0)
    def _():
        finish(accs)

    @pl.when(bad != 0.0)
    def _():
        init = []
        for hh in range(ATT_HP):
            init += _softmax_step(q_buf[hh], k_ref[0, hh, 0:TM, :], v_ref[0, hh, 0:TM, :], None, None)

        def body(j, carry):
            new = []
            for hh in range(ATT_HP):
                new += _softmax_step(q_buf[hh], *kv_block(hh, j), carry[2 * hh], carry[2 * hh + 1])
            return tuple(new)

        carry = lax.fori_loop(0, n_blk, body, tuple(init))
        finish(carry[1::2])


def _attention_lat(q, k, v):
    b, h, t, _ = q.shape
    s = t - TM
    tq = ATT_NQ * TM
    assert s % ATT_TK == 0 and s % tq == 0
    kv = pl.BlockSpec((1, ATT_HP, t, LANE), lambda bi, hp, i: (bi, hp, 0, 0))
    qs = [pl.BlockSpec((1, ATT_HP, TM, LANE), lambda bi, hp, i, u=u: (bi, hp, 1 + ATT_NQ * i + u, 0))
          for u in range(ATT_NQ)]
    return pl.pallas_call(
        functools.partial(_attn_lat_kernel, n_blk=s // ATT_TK),
        grid=(b, h // ATT_HP, s // tq),
        in_specs=qs + [kv, kv],
        out_specs=pl.BlockSpec((1, tq, ATT_HP * MLA_V), lambda bi, hp, i: (bi, i, hp)),
        out_shape=jax.ShapeDtypeStruct((b, s, h * MLA_V), BF16),
        scratch_shapes=[pltpu.VMEM((ATT_HP, tq, LANE), BF16), pltpu.VMEM((ATT_HP, tq, LANE), BF16)],
        compiler_params=_cparams(("parallel", "parallel", "arbitrary")),
        name="mla_attention",
    )(*([q] * ATT_NQ), k, v)


def _ml_prep_kernel(x_ref, prev_ref, next_ref, cw_ref, cb_ref, wq_ref, wk_ref,
                    xc_ref, q_ref, k_ref, *, n_tiles):
    i = pl.program_id(1)
    x = x_ref[0]
    prev = jnp.where(i <= 1, 0.0, prev_ref[0])
    nxt = jnp.where((i == 0) | (i == n_tiles - 1), 0.0, next_ref[0])
    ext = jnp.concatenate([prev, x, nxt], axis=0)
    n_ext = TM + 16
    cw = cw_ref[...]
    acc = jnp.zeros((TM, ML_W), F32) + cb_ref[...]
    for kk in range(ML_CONV):
        sh = (ML_CONV // 2 - kk) % n_ext
        shifted = ext if sh == 0 else pltpu.roll(ext, sh, 0)
        acc = acc + cw[kk:kk + 1] * shifted[8:8 + TM]
    xc = _silu(acc)
    xc_ref[0] = xc
    xb = xc.astype(BF16)
    q_ref[0] = jnp.dot(xb, wq_ref[...], preferred_element_type=F32).astype(BF16)
    k_ref[0] = jnp.dot(xb, wk_ref[...], preferred_element_type=F32).astype(BF16)


def _ml_prep(p, conv_w8, conv_b, wq_bd, wk_bd):
    b, t, _ = p.shape
    nt = t // TM
    r8 = TM // 8
    full = lambda shape: pl.BlockSpec(shape, lambda bi, i: (0,) * len(shape))
    cb = C_MX // ML_W
    blk = pl.BlockSpec((1, TM, ML_W), lambda bi, i: (bi, i, 0))
    return pl.pallas_call(
        functools.partial(_ml_prep_kernel, n_tiles=nt),
        grid=(b, nt),
        in_specs=[
            pl.BlockSpec((1, TM, ML_W), lambda bi, i: (bi, i, cb)),
            pl.BlockSpec((1, 8, ML_W), lambda bi, i: (bi, jnp.maximum(i * r8 - 1, 0), cb)),
            pl.BlockSpec((1, 8, ML_W), lambda bi, i: (bi, jnp.minimum((i + 1) * r8, nt * r8 - 1), cb)),
            full((8, ML_W)), full((1, ML_W)), full((ML_W, ML_W)), full((ML_W, ML_W)),
        ],
        out_specs=[blk, blk, blk],
        out_shape=[jax.ShapeDtypeStruct((b, t, ML_W), F32),
                   jax.ShapeDtypeStruct((b, t, ML_W), BF16),
                   jax.ShapeDtypeStruct((b, t, ML_W), BF16)],
        compiler_params=_cparams(("parallel", "parallel")),
        name="mlstm_prep",
    )(p, p, p, conv_w8, conv_b.reshape(1, ML_W), wq_bd, wk_bd)


def _scan_chunk(d, step, n_ctx_chunks, n_chunks):
    bwd = jnp.where(step < n_ctx_chunks, n_ctx_chunks - 1 - step, n_chunks - 1 - (step - n_ctx_chunks))
    return jnp.where(d == 0, step, bwd)


def _ml_chunk(d, q, k, v, g, s_ref, m_ref, base):
    L = ML_CHUNK
    row = lax.broadcasted_iota(jnp.int32, (L, L), 0)
    col = lax.broadcasted_iota(jnp.int32, (L, L), 1)
    mask = col <= row if d == 0 else col >= row
    tri = mask.astype(F32)
    lane = lax.broadcasted_iota(jnp.int32, (1, LANE), 1)
    eye8 = (lax.broadcasted_iota(jnp.int32, (8, LANE), 0)
            == lax.broadcasted_iota(jnp.int32, (8, LANE), 1)).astype(F32)

    lf = _log_sigmoid(g)
    bc = jnp.dot(tri, lf, precision=HIGHEST, preferred_element_type=F32)
    g_rows = _nt_dot(eye8, g, precision=HIGHEST)
    bc_rows = _nt_dot(eye8, bc, precision=HIGHEST)

    outs = []
    for pair in range(ML_HEADS // 2):
        sl = slice(pair * LANE, (pair + 1) * LANE)
        q_blk, k_blk, v_blk = q[:, sl], k[:, sl], v[:, sl]
        pair_out = []
        for sub in range(2):
            h = pair * 2 + sub
            head_lanes = (lane >= sub * ML_DH) & (lane < (sub + 1) * ML_DH)
            qh = jnp.where(head_lanes, q_blk, jnp.zeros_like(q_blk))
            vs = v_blk if sub == 0 else pltpu.roll(v_blk, ML_DH, 1)
            v_ext = jnp.where(lane < ML_DH, vs, jnp.where(lane == ML_DH, 1.0, 0.0)).astype(BF16)

            li_c = g[:, SM_GATE + h:SM_GATE + h + 1]
            bc_c = bc[:, SM_GATE + 4 + h:SM_GATE + 5 + h]
            li_r = g_rows[h:h + 1, :]
            bc_r = bc_rows[4 + h:5 + h, :]
            m_st = m_ref[base + h][0:1, 0:1]

            dmat = jnp.where(mask, bc_c + (li_r - bc_r), -jnp.inf)
            inter = bc_c + m_st
            m_t = jnp.maximum(inter, jnp.max(dmat, axis=-1, keepdims=True))
            e = jnp.exp(dmat - m_t)
            s = (_nt_dot(qh, k_blk) * e).astype(BF16)
            tot = (jnp.dot(s, v_ext, preferred_element_type=F32)
                   + jnp.exp(inter - m_t) * jnp.dot(qh, s_ref[base + h].astype(BF16),
                                                    preferred_element_type=F32))
            den = tot[:, ML_DH:ML_DH + 1]
            pair_out.append(tot / jnp.maximum(jnp.abs(den), jnp.exp(-m_t)))

            b_end = bc_c[L - 1:L] if d == 0 else bc_c[0:1]
            g_col = b_end - bc_c + li_c
            m_new = jnp.maximum(b_end + m_st, jnp.max(g_col, axis=0, keepdims=True))
            kw = jnp.where(head_lanes, k_blk.astype(F32) * jnp.exp(g_col - m_new), 0.0).astype(BF16)
            s_ref[base + h] = jnp.exp(b_end + m_st - m_new) * s_ref[base + h] + _tn_dot(kw, v_ext)
            m_ref[base + h] = jnp.broadcast_to(m_new, m_ref.shape[1:])
        outs.append(jnp.where(lane < ML_DH, pair_out[0], pltpu.roll(pair_out[1], ML_DH, 1)))
    return jnp.concatenate(outs, axis=-1)


def _ml_scan_kernel(qf_ref, kf_ref, vf_ref, smf_ref, qb_ref, kb_ref, vb_ref, smb_ref, gb_ref,
                    hf_ref, hb_ref, s_ref, m_ref):
    @pl.when(pl.program_id(0) == 0)
    def _():
        s_ref[...] = jnp.zeros_like(s_ref)
        m_ref[...] = jnp.zeros_like(m_ref)

    nb = qf_ref.shape[0]
    streams = ((0, qf_ref, kf_ref, vf_ref, smf_ref, hf_ref), (1, qb_ref, kb_ref, vb_ref, smb_ref, hb_ref))
    for d, q_ref, k_ref, v_ref, sm_ref, h_ref in streams:
        for bi in range(nb):
            h_ref[bi] = _ml_chunk(d, q_ref[bi], k_ref[bi], v_ref[bi], sm_ref[bi] + gb_ref[d],
                                  s_ref, m_ref, (d * nb + bi) * ML_HEADS)


def _ml_scan(q, k, p, gate_b_p):
    b, t, _ = q.shape
    nc = t // ML_CHUNK
    chunk = functools.partial(_scan_chunk, n_ctx_chunks=TM // ML_CHUNK, n_chunks=nc)
    blk = lambda d, w, cb: pl.BlockSpec((b, ML_CHUNK, w), lambda s: (0, chunk(d, s), cb))
    stream = lambda d: [blk(d, ML_W, 0), blk(d, ML_W, 0), blk(d, ML_W, C_MV // ML_W),
                        blk(d, LANE, C_SMF // LANE + d)]
    shp = jax.ShapeDtypeStruct((b, t, ML_W), F32)
    return pl.pallas_call(
        _ml_scan_kernel,
        grid=(nc,),
        in_specs=stream(0) + stream(1) + [pl.BlockSpec((2, 1, LANE), lambda s: (0, 0, 0))],
        out_specs=[blk(0, ML_W, 0), blk(1, ML_W, 0)],
        out_shape=[shp, shp],
        scratch_shapes=[pltpu.VMEM((2 * b * ML_HEADS, LANE, LANE), F32),
                        pltpu.VMEM((2 * b * ML_HEADS, 8, LANE), F32)],
        compiler_params=_cparams(("arbitrary",)),
        name="mlstm_scan",
    )(q, k, p, p, q, k, p, p, gate_b_p)


def _gla_chunk(d, q, k, v, sm, wa, ba, s_ref, si):
    L = GLA_CHUNK
    row = lax.broadcasted_iota(jnp.int32, (L, L), 0)
    col = lax.broadcasted_iota(jnp.int32, (L, L), 1)
    mask = col <= row if d == 0 else col >= row
    tri = mask.astype(F32)
    lane_k = lax.broadcasted_iota(jnp.int32, (1, GLA_HEADS * GLA_DK), 1)
    lane_v = lax.broadcasted_iota(jnp.int32, (1, GLA_HEADS * GLA_DV), 1)
    eye = (lax.broadcasted_iota(jnp.int32, (LANE, LANE), 0)
           == lax.broadcasted_iota(jnp.int32, (LANE, LANE), 1)).astype(F32)

    pre = jnp.dot(sm.astype(BF16), wa.astype(BF16), preferred_element_type=F32) + ba
    loga = _log_sigmoid(pre) * (1.0 / GLA_TAU)
    bc = _dot_sel(tri, loga)
    ref_row = bc[L // 2 - 1:L // 2]
    b_end = bc[L - 1:L] if d == 0 else bc[0:1]

    q = q * (GLA_DK ** -0.5)
    v = v.astype(BF16)
    q_in = (q * jnp.exp(bc - ref_row))
    k_in = (k * jnp.exp(ref_row - bc)).astype(BF16)
    q_st = (q * jnp.exp(bc)).astype(BF16)
    k_st = (k * jnp.exp(b_end - bc)).astype(BF16)

    blockdiag = (lax.broadcasted_iota(jnp.int32, s_ref.shape[1:], 0) // GLA_DK
                 == lax.broadcasted_iota(jnp.int32, s_ref.shape[1:], 1) // GLA_DV)
    s_old = s_ref[si]
    o = jnp.dot(q_st, jnp.where(blockdiag, s_old, 0.0).astype(BF16), preferred_element_type=F32)
    for h in range(GLA_HEADS):
        qh = jnp.where(lane_k // GLA_DK == h, q_in, 0.0).astype(BF16)
        att = jnp.where(mask, _nt_dot(qh, k_in), 0.0).astype(BF16)
        oh = jnp.dot(att, v, preferred_element_type=F32)
        o = o + jnp.where(lane_v // GLA_DV == h, oh, 0.0)

    decay_col = jnp.exp(_dot_sel(eye, jnp.broadcast_to(b_end, (8, LANE)), nt=True)[:, 0:1])
    s_ref[si] = decay_col * s_old + _tn_dot(k_st, v)
    return o


def _gla_scan_kernel(qf_ref, kf_ref, vf_ref, smf_ref, qb_ref, kb_ref, vb_ref, smb_ref, wa_ref, ba_ref,
                     of_ref, ob_ref, s_ref):
    @pl.when(pl.program_id(0) == 0)
    def _():
        s_ref[...] = jnp.zeros_like(s_ref)

    nb = qf_ref.shape[0]
    n_sub = TM // GLA_CHUNK
    streams = ((0, qf_ref, kf_ref, vf_ref, smf_ref, of_ref), (1, qb_ref, kb_ref, vb_ref, smb_ref, ob_ref))
    for d, q_ref, k_ref, v_ref, sm_ref, o_ref in streams:
        for bi in range(nb):
            for c in (range(n_sub) if d == 0 else reversed(range(n_sub))):
                rows = slice(c * GLA_CHUNK, (c + 1) * GLA_CHUNK)
                o_ref[bi, rows] = _gla_chunk(d, q_ref[bi, rows], k_ref[bi, rows], v_ref[bi, rows],
                                             sm_ref[bi, rows], wa_ref[d], ba_ref[d], s_ref, d * nb + bi)


def _gla_scan(p, wa_p, ba_p):
    b, t, _ = p.shape
    nc = t // TM
    chunk = functools.partial(_scan_chunk, n_ctx_chunks=1, n_chunks=nc)
    kw, vw = GLA_HEADS * GLA_DK, GLA_HEADS * GLA_DV
    blk = lambda d, w, col: pl.BlockSpec((b, TM, w), lambda s: (0, chunk(d, s), col // w))
    stream = lambda d: [blk(d, kw, C_GQ), blk(d, kw, C_GK), blk(d, vw, C_GV), blk(d, LANE, C_SMF + d * LANE)]
    shp = jax.ShapeDtypeStruct((b, t, vw), F32)
    return pl.pallas_call(
        _gla_scan_kernel,
        grid=(nc,),
        in_specs=stream(0) + stream(1) + [pl.BlockSpec((2, LANE, kw), lambda s: (0, 0, 0)),
                                          pl.BlockSpec((2, 1, kw), lambda s: (0, 0, 0))],
        out_specs=[blk(0, vw, 0), blk(1, vw, 0)],
        out_shape=[shp, shp],
        scratch_shapes=[pltpu.VMEM((2 * b, kw, vw), F32)],
        compiler_params=_cparams(("arbitrary",)),
        name="gla_scan",
    )(p, p, p, p, p, p, p, p, wa_p, ba_p)


OUTPROJ_TILES = 2
OUTPROJ_N_TOK = 11


def _outproj_kernel(*refs, n_tiles):
    n_in = OUTPROJ_TILES * OUTPROJ_N_TOK
    shared = refs[n_in:n_in + 7]
    outs = refs[n_in + 7:]
    for u in range(OUTPROJ_TILES):
        tile = jnp.minimum(OUTPROJ_TILES * pl.program_id(1) + u, n_tiles - 1)
        _outproj_tile(tile == 0, slice(u * TM, (u + 1) * TM),
                      *refs[u * OUTPROJ_N_TOK:(u + 1) * OUTPROJ_N_TOK], *shared, *outs)


def _outproj_tile(is_ctx, rows, ac_ref, al_ref, mhf_ref, mhb_ref, xc_ref, mo_ref, gof_ref, gob_ref, gr_ref, x_ref,
                  mod_ref, mnw_ref, msk_ref, gnw_ref, wout_ref, n2w_ref, wr_ref, br_ref,
                  xo_ref, h2_ref, route_ref):
    grp64 = (lax.broadcasted_iota(jnp.int32, (ML_W, ML_W), 0) // ML_DH
             == lax.broadcasted_iota(jnp.int32, (ML_W, ML_W), 1) // ML_DH).astype(F32) * (1.0 / ML_DH)

    def head_norm(x, w):
        ms = _dot_sel(grp64, x * x, right=True)
        return x * lax.rsqrt(ms + EPS) * w

    m_l = jax.nn.sigmoid(mo_ref[0]) * (head_norm(mhf_ref[0] + mhb_ref[0], mnw_ref[...])
                                       + msk_ref[...] * xc_ref[0])
    g_l = head_norm(gof_ref[0] + gob_ref[0], gnw_ref[...]) * _silu(gr_ref[0])
    na = MLA_HEADS * MLA_V
    a = jnp.where(is_ctx, ac_ref[0], al_ref[0])
    res = (jnp.dot(a, wout_ref[0:na], preferred_element_type=F32)
           + jnp.dot(m_l.astype(BF16), wout_ref[na:na + ML_W], preferred_element_type=F32)
           + jnp.dot(g_l.astype(BF16), wout_ref[na + ML_W:], preferred_element_type=F32))
    mod = mod_ref[0]
    x = x_ref[0] + mod[2:3] * res
    xo_ref[0, rows] = x
    h2 = (x * lax.rsqrt(jnp.mean(x * x, axis=-1, keepdims=True) + EPS) * n2w_ref[...]
          * (1.0 + mod[4:5]) + mod[3:4])
    half = h2.shape[-1] // 2
    h2_ref[0, rows] = _pack_bf16_pair(h2[:, :half], h2[:, half:])

    logits = jnp.dot(h2, wr_ref[...], precision=HIGHEST, preferred_element_type=F32) + br_ref[...]
    lane = lax.broadcasted_iota(jnp.int32, (1, LANE), 1)
    lane_f = lane.astype(F32)
    neg = -jnp.inf
    gl = jnp.where(lane < R_EXP, logits, neg)
    gmax = jnp.max(gl, axis=-1, keepdims=True)
    g_w = 1.0 / jnp.sum(jnp.exp(gl - gmax), axis=-1, keepdims=True)
    g_i = jnp.min(jnp.where(gl == gmax, lane_f, float(LANE)), axis=-1, keepdims=True)
    grp_of_lane = ((lane - R_EXP) // EXP_PER_GROUP).astype(F32)
    in_grp = (lane >= R_EXP) & (lane < R_EXP + N_EXPERTS) & (grp_of_lane == g_i)
    el = jnp.where(in_grp, logits, neg)
    m1 = jnp.max(el, axis=-1, keepdims=True)
    i1 = jnp.min(jnp.where(el == m1, lane_f, float(LANE)), axis=-1, keepdims=True)
    el2 = jnp.where(lane_f == i1, neg, el)
    m2 = jnp.max(el2, axis=-1, keepdims=True)
    i2 = jnp.min(jnp.where(el2 == m2, lane_f, float(LANE)), axis=-1, keepdims=True)
    p2 = jnp.exp(m2 - m1)
    w1 = g_w / (1.0 + p2)
    route_ref[0, rows] = jnp.where(lane == RT_E1, i1 - R_EXP, jnp.where(lane == RT_E1 + 1, i2 - R_EXP,
                             jnp.where(lane == RT_W1, w1, jnp.where(lane == RT_W1 + 1, p2 * w1, 0.0))))


def _outproj(a_ctx, a_lat, mh, xconv, p, go, xs, mods, ml_norm_w, ml_skip, gla_norm_w, w_out_b, norm2_w, wr_p, br_p):
    b, t, d = xs.shape
    nt = t // TM
    full = lambda shape: pl.BlockSpec(shape, lambda bi, i: (0,) * len(shape))
    na = MLA_HEADS * MLA_V

    out = lambda w: pl.BlockSpec((1, OUTPROJ_TILES * TM, w), lambda bi, i: (bi, i, 0))

    def tile_specs(u):
        tile = lambda i: jnp.minimum(OUTPROJ_TILES * i + u, nt - 1)
        tok = lambda w, cb=0: pl.BlockSpec((1, TM, w), lambda bi, i: (bi, tile(i), cb))
        ins = [pl.BlockSpec((1, TM, na), lambda bi, i: (bi, 0, 0)),
               pl.BlockSpec((1, TM, na), lambda bi, i: (bi, jnp.maximum(tile(i) - 1, 0), 0)),
               tok(ML_W), tok(ML_W), tok(ML_W), tok(ML_W, C_MO // ML_W),
               tok(ML_W), tok(ML_W), tok(ML_W, C_GR // ML_W), tok(d),
               pl.BlockSpec((1, 6, d), lambda bi, i: (_mod_row(bi, tile(i)), 0, 0))]
        return ins

    specs = [tile_specs(u) for u in range(OUTPROJ_TILES)]
    tok_args = (a_ctx, a_lat, mh[0], mh[1], xconv, p, go[0], go[1], p, xs, mods)
    assert len(tok_args) == OUTPROJ_N_TOK
    shapes = [jax.ShapeDtypeStruct((b, t, d), F32), jax.ShapeDtypeStruct((b, t, d // 2), jnp.uint32),
              jax.ShapeDtypeStruct((b, t, LANE), F32)]
    outs = pl.pallas_call(
        functools.partial(_outproj_kernel, n_tiles=nt),
        grid=(b, pl.cdiv(nt, OUTPROJ_TILES)),
        in_specs=sum(specs, []) + [
            full((1, ML_W)), full((1, ML_W)), full((1, ML_W)), full((d, d)), full((1, d)),
            full((d, LANE)), full((1, LANE))],
        out_specs=[out(d), out(d // 2), out(LANE)],
        out_shape=shapes,
        compiler_params=_cparams(("parallel", "arbitrary")),
        name="out_proj_router",
    )(*(tok_args * OUTPROJ_TILES), ml_norm_w.reshape(1, -1), ml_skip.reshape(1, -1),
      gla_norm_w.reshape(1, -1), w_out_b, norm2_w.reshape(1, -1), wr_p, br_p)
    return outs


def _dispatch(route, n_tiles):
    n = route.shape[0]
    flat = route[:, RT_E1:RT_E1 + 2].astype(jnp.int32).reshape(-1)
    onehot = (flat[:, None] == jnp.arange(N_EXPERTS, dtype=jnp.int32)[None, :]).astype(jnp.int32)
    csum = jnp.cumsum(onehot, axis=0)
    rank = jnp.sum(csum * onehot, axis=1) - 1
    padded = (csum[-1] + TM - 1) // TM * TM
    ends = jnp.cumsum(padded)
    pos = (ends - padded)[flat] + rank
    tile_start = jnp.arange(n_tiles, dtype=jnp.int32) * TM
    tile_exp = jnp.minimum(jnp.sum((ends[None, :] <= tile_start[:, None]).astype(jnp.int32), axis=1),
                           N_EXPERTS - 1)
    tile_on = (tile_start < ends[-1]).astype(jnp.int32)
    pos = pos.reshape(n // TM, TM, 2)
    return (tile_exp, tile_on,
            pos[:, :, 0].reshape(n // TM, 1, TM), pos[:, :, 1].reshape(n // TM, 1, TM))


def _gather_rows(src_hbm, idx_ref, dst, sem):
    def body(j, carry):
        pltpu.make_async_copy(src_hbm.at[pl.ds(idx_ref[0, 0, j], 1)], dst.at[pl.ds(j, 1)], sem).start()
        return carry

    lax.fori_loop(0, TM, body, 0, unroll=GATHER_UNROLL)


def _wait_rows(src_hbm, dst, sem):
    pltpu.make_async_copy(src_hbm.at[pl.ds(0, TM)], dst, sem).wait()


def _pack_bf16_pair(lo, hi):
    lo_b = lax.bitcast_convert_type(lo.astype(BF16).astype(F32), jnp.uint32) >> 16
    hi_b = lax.bitcast_convert_type(hi.astype(BF16).astype(F32), jnp.uint32) & jnp.uint32(0xFFFF0000)
    return hi_b | lo_b


def _unpack_bf16_pair(w):
    return (lax.bitcast_convert_type(w << 16, F32),
            lax.bitcast_convert_type(w & jnp.uint32(0xFFFF0000), F32))


def _scatter_kernel(p1_ref, p2_ref, h2_ref, init_hbm, xs_hbm, sem):
    del init_hbm

    def body(j, carry):
        row = h2_ref.at[pl.ds(j, 1)]
        pltpu.make_async_copy(row, xs_hbm.at[pl.ds(p1_ref[0, 0, j], 1)], sem).start()
        pltpu.make_async_copy(row, xs_hbm.at[pl.ds(p2_ref[0, 0, j], 1)], sem).start()
        return carry

    lax.fori_loop(0, TM, body, 0, unroll=GATHER_UNROLL)
    for _ in range(2):
        pltpu.make_async_copy(h2_ref, xs_hbm.at[pl.ds(0, TM)], sem).wait()


def _scatter_rows(h2p, pos1, pos2, n_rows):
    n, dh = h2p.shape
    idx = pl.BlockSpec((1, 1, TM), lambda g: (g, 0, 0), memory_space=pltpu.SMEM)
    return pl.pallas_call(
        _scatter_kernel,
        grid=(n // TM,),
        in_specs=[idx, idx, pl.BlockSpec((TM, dh), lambda g: (g, 0)), pl.BlockSpec(memory_space=pl.ANY)],
        out_specs=pl.BlockSpec(memory_space=pl.ANY),
        out_shape=jax.ShapeDtypeStruct((n_rows, dh), jnp.uint32),
        scratch_shapes=[pltpu.SemaphoreType.DMA(())],
        input_output_aliases={3: 0},
        compiler_params=_cparams(("arbitrary",)),
        name="moe_scatter",
    )(pos1, pos2, h2p, jnp.zeros((n_rows, dh), jnp.uint32))


def _experts_kernel(texp_ref, ton_ref, x_ref, wg_ref, wu_ref, wd_ref, y_ref, wgu_b, wd_b):
    r = pl.program_id(0)
    half = wgu_b.shape[0] // 2

    @pl.when(ton_ref[r] == 1)
    def _():
        @pl.when(jnp.logical_or(r == 0, texp_ref[r] != texp_ref[jnp.maximum(r - 1, 0)]))
        def _():
            wgu_b[:, :D_EXPERT] = wg_ref[0, 0].astype(BF16)
            wgu_b[:, D_EXPERT:] = wu_ref[0, 0].astype(BF16)
            wd_b[...] = wd_ref[0, 0].astype(BF16)

        x_lo, x_hi = _unpack_bf16_pair(x_ref[...])
        gu = (jnp.dot(x_lo.astype(BF16), wgu_b[0:half], preferred_element_type=F32)
              + jnp.dot(x_hi.astype(BF16), wgu_b[half:], preferred_element_type=F32))
        act = (_silu(gu[:, :D_EXPERT]) * gu[:, D_EXPERT:]).astype(BF16)
        y = jnp.dot(act, wd_b[...], preferred_element_type=F32)
        y_ref[...] = _pack_bf16_pair(y[:, :half], y[:, half:])

    @pl.when(ton_ref[r] == 0)
    def _():
        y_ref[...] = jnp.zeros_like(y_ref)


def _experts(xs, tile_exp, tile_on, w_gate, w_up, w_down, layer):
    n_rows, dh = xs.shape
    d = 2 * dh
    wspec = lambda shape: pl.BlockSpec((1, 1) + shape, lambda r, te, to: (layer, te[r], 0, 0))
    rows = pl.BlockSpec((TM, dh), lambda r, te, to: (r, 0))
    return pl.pallas_call(
        _experts_kernel,
        grid_spec=pltpu.PrefetchScalarGridSpec(
            num_scalar_prefetch=2,
            grid=(n_rows // TM,),
            in_specs=[rows, wspec((d, D_EXPERT)), wspec((d, D_EXPERT)), wspec((D_EXPERT, d))],
            out_specs=rows,
            scratch_shapes=[pltpu.VMEM((d, 2 * D_EXPERT), BF16), pltpu.VMEM((D_EXPERT, d), BF16)],
        ),
        out_shape=jax.ShapeDtypeStruct((n_rows, dh), jnp.uint32),
        compiler_params=_cparams(("arbitrary",)),
        name="moe_experts",
    )(tile_exp, tile_on, xs, w_gate, w_up, w_down)


def _combine_kernel(p1_ref, p2_ref, p1n_ref, p2n_ref, route_ref, x_ref, mod_ref, y_hbm, o_ref, buf, sem):
    g = pl.program_id(0)
    slot = g % 2
    half = buf.shape[-1]

    def gather(pa, pb, s):
        _gather_rows(y_hbm, pa, buf.at[s, 0], sem.at[s])
        _gather_rows(y_hbm, pb, buf.at[s, 1], sem.at[s])

    @pl.when(g == 0)
    def _():
        gather(p1_ref, p2_ref, 0)

    @pl.when(g + 1 < pl.num_programs(0))
    def _():
        gather(p1n_ref, p2n_ref, 1 - slot)

    _wait_rows(y_hbm, buf.at[slot, 0], sem.at[slot])
    _wait_rows(y_hbm, buf.at[slot, 1], sem.at[slot])
    route = route_ref[...]
    w1, w2 = route[:, RT_W1:RT_W1 + 1], route[:, RT_W1 + 1:RT_W1 + 2]
    gate = mod_ref[0][5:6]
    for part, (y1, y2) in enumerate(zip(_unpack_bf16_pair(buf[slot, 0]), _unpack_bf16_pair(buf[slot, 1]))):
        cols = slice(part * half, (part + 1) * half)
        o_ref[:, cols] = x_ref[:, cols] + gate[:, cols] * (w1 * y1 + w2 * y2)


def _combine(y, pos1, pos2, route, xs2, mods, nt):
    n, d = xs2.shape
    n_tok_tiles = n // TM
    idx = lambda nxt: pl.BlockSpec((1, 1, TM), lambda g: (jnp.minimum(g + nxt, n_tok_tiles - 1), 0, 0),
                                   memory_space=pltpu.SMEM)
    tok = lambda w: pl.BlockSpec((TM, w), lambda g: (g, 0))
    return pl.pallas_call(
        _combine_kernel,
        grid=(n_tok_tiles,),
        in_specs=[
            idx(0), idx(0), idx(1), idx(1), tok(LANE), tok(d),
            pl.BlockSpec((1, 6, d), lambda g: (_mod_row(g // nt, g % nt), 0, 0)),
            pl.BlockSpec(memory_space=pl.ANY),
        ],
        out_specs=tok(d),
        out_shape=jax.ShapeDtypeStruct((n, d), F32),
        scratch_shapes=[pltpu.VMEM((2, 2, TM, d // 2), jnp.uint32), pltpu.SemaphoreType.DMA((2,))],
        compiler_params=_cparams(("arbitrary",)),
        name="moe_combine",
    )(pos1, pos2, pos1, pos2, route, xs2, mods, y)


def _moe(h2p, route, xs, mods, w_gate, w_up, w_down, layer):
    b, t, d = xs.shape
    n = b * t
    n_tiles = 2 * n // TM + N_EXPERTS
    route2 = route.reshape(n, LANE)
    tile_exp, tile_on, pos1, pos2 = _dispatch(route2, n_tiles)
    xs_sorted = _scatter_rows(h2p.reshape(n, d // 2), pos1, pos2, n_tiles * TM)
    y = _experts(xs_sorted, tile_exp, tile_on, w_gate, w_up, w_down, layer)
    return _combine(y, pos1, pos2, route2, xs.reshape(n, d), mods, t // TM).reshape(b, t, d)


def _rope_tables(n_ctx, n_lat):
    rows = n_lat // GRID_W
    row = jnp.broadcast_to(jnp.arange(rows, dtype=F32)[:, None], (rows, GRID_W)).reshape(-1)
    col = jnp.broadcast_to(jnp.arange(GRID_W, dtype=F32)[None, :], (rows, GRID_W)).reshape(-1)
    n_freq = MLA_ROPE // 4
    inv = ROPE_THETA ** (-jnp.arange(n_freq, dtype=F32) / n_freq)
    ang = jnp.concatenate([row[:, None] * inv, col[:, None] * inv], axis=-1)
    cos, sin = jnp.cos(ang), jnp.sin(ang)
    half = MLA_ROPE // 2
    z = lambda w: jnp.zeros((n_lat, w), F32)
    o = lambda w: jnp.ones((n_lat, w), F32)
    tail = LANE - MLA_QK
    cos_t = jnp.concatenate([o(MLA_NOPE), cos, cos, o(tail)], axis=-1)
    sa_t = jnp.concatenate([z(MLA_NOPE + half), sin, z(tail)], axis=-1)
    sb_t = jnp.concatenate([z(MLA_NOPE), -sin, z(half + tail)], axis=-1)
    ctx1 = jnp.ones((n_ctx, LANE), F32)
    ctx0 = jnp.zeros((n_ctx, LANE), F32)
    return (jnp.concatenate([ctx1, cos_t], 0), jnp.concatenate([ctx0, sa_t], 0),
            jnp.concatenate([ctx0, sb_t], 0))


def _pad_cols(a, width):
    return jnp.pad(a, [(0, 0)] * (a.ndim - 1) + [(0, width - a.shape[-1])])


def _layer_weights(w_in, w_uq, w_ukv, q_norm_w, k_norm_w, ml_conv_w, ml_wq, ml_wk, ml_gate_b,
                   gla_wa, gla_ba, w_out, w_grp, b_grp, w_erouter, b_erouter):
    d = w_in.shape[0]
    o = np.cumsum((0, Q_LORA, KV_LORA, MLA_ROPE, ML_W, ML_W, ML_W, 4 * ML_HEADS, GLA_HEADS * GLA_DK,
                   GLA_HEADS * GLA_DK, GLA_HEADS * GLA_DV, GLA_HEADS * GLA_DV, 2 * GLA_LR))
    seg = lambda j: w_in[:, o[j]:o[j + 1]]
    cq, ckv, kr, mx, mv, mo, mg, gq, gk, gv, gr, ga = (seg(j) for j in range(12))
    z = lambda w: jnp.zeros((d, w), F32)

    def small(di):
        return jnp.concatenate([mg[:, di * 8:(di + 1) * 8], ga[:, di * GLA_LR:(di + 1) * GLA_LR],
                                z(SM_KR - SM_GA - GLA_LR), kr, z(LANE - SM_KR - MLA_ROPE)], axis=-1)

    w_in_p = jnp.concatenate([cq, mx, mv, mo, gv, gr, ckv, gq, gk, small(0), small(1)], axis=-1).astype(BF16)

    wuq_p = _pad_cols(w_uq.reshape(Q_LORA, MLA_HEADS, MLA_QK), LANE).reshape(Q_LORA, -1).astype(BF16)
    ukv = w_ukv.reshape(KV_LORA, MLA_HEADS, MLA_NOPE + MLA_V)
    wuk_p = _pad_cols(ukv[..., :MLA_NOPE], LANE).reshape(KV_LORA, -1).astype(BF16)
    wuv_p = _pad_cols(ukv[..., MLA_NOPE:], LANE).reshape(KV_LORA, -1).astype(BF16)
    qn_p = _pad_cols(q_norm_w.reshape(1, -1), LANE)
    kn_p = _pad_cols(k_norm_w.reshape(1, -1), LANE)

    conv_w8 = jnp.pad(ml_conv_w, ((0, 8 - ML_CONV), (0, 0)))
    bd = lambda w: jax.scipy.linalg.block_diag(*[w[h] for h in range(ML_HEADS)])
    wq_bd = (bd(ml_wq) * (ML_DH ** -0.5)).astype(BF16)
    wk_bd = bd(ml_wk).astype(BF16)
    gate_b_p = _pad_cols(ml_gate_b.reshape(2, 1, 2 * ML_HEADS), LANE)

    wa_p = jnp.pad(gla_wa, ((0, 0), (SM_GA, LANE - SM_GA - GLA_LR), (0, 0)))
    ba_p = gla_ba.reshape(2, 1, -1)

    wr_p = _pad_cols(jnp.concatenate([w_grp, w_erouter], axis=-1), LANE)
    br_p = _pad_cols(jnp.concatenate([b_grp, b_erouter]).reshape(1, -1), LANE)
    return dict(w_in_p=w_in_p, wuq_p=wuq_p, wuk_p=wuk_p, wuv_p=wuv_p, qn_p=qn_p, kn_p=kn_p,
                conv_w8=conv_w8, wq_bd=wq_bd, wk_bd=wk_bd, gate_b_p=gate_b_p, wa_p=wa_p, ba_p=ba_p,
                w_out_b=w_out.astype(BF16), wr_p=wr_p, br_p=br_p)


def kernel(x, c, ctx, c_ctx, w_mod, b_mod, norm1_w, w_in, q_a_norm, w_uq, kv_a_norm, w_ukv,
           q_norm_w, k_norm_w, ml_conv_w, ml_conv_b, ml_wq, ml_wk, ml_gate_b, ml_norm_w, ml_skip,
           gla_wa, gla_ba, gla_norm_w, w_out, norm2_w, w_grp, b_grp, w_erouter, b_erouter,
           w_gate, w_up, w_down):
    b, s, d = x.shape
    n_ctx = ctx.shape[1]
    depth = w_mod.shape[0]
    assert n_ctx == TM and s % TM == 0 and b == 2

    cc = jnp.concatenate([c, c_ctx[None, :], jnp.zeros((8 - b - 1, d), F32)], axis=0)
    mods_all = _mods(cc, w_mod, b_mod).reshape(depth, 8, 6, d)
    cos_t, sa_t, sb_t = _rope_tables(n_ctx, s)
    xs = jnp.concatenate([ctx, x], axis=1)

    for l in range(depth):
        w = _layer_weights(w_in[l], w_uq[l], w_ukv[l], q_norm_w[l], k_norm_w[l], ml_conv_w[l],
                           ml_wq[l], ml_wk[l], ml_gate_b[l], gla_wa[l], gla_ba[l], w_out[l],
                           w_grp[l], b_grp[l], w_erouter[l], b_erouter[l])
        mods = mods_all[l]
        p = _inproj(xs, mods, norm1_w[l], w["w_in_p"])
        q, k, v = _mla_prep(p, q_a_norm[l], w["wuq_p"], kv_a_norm[l], w["wuk_p"], w["wuv_p"],
                            w["qn_p"], w["kn_p"], cos_t, sa_t, sb_t)
        a_lat = _attention_lat(q, k, v)
        a_ctx = _attention_ctx(q, k, v) if l < depth - 1 else jnp.zeros((b, TM, MLA_HEADS * MLA_V), BF16)
        xconv, mq, mk = _ml_prep(p, w["conv_w8"], ml_conv_b[l], w["wq_bd"], w["wk_bd"])
        mh = _ml_scan(mq, mk, p, w["gate_b_p"])
        go = _gla_scan(p, w["wa_p"], w["ba_p"])
        xs, h2, route = _outproj(a_ctx, a_lat, mh, xconv, p, go, xs, mods, ml_norm_w[l], ml_skip[l], gla_norm_w[l],
                                w["w_out_b"], norm2_w[l], w["wr_p"], w["br_p"])
        xs = _moe(h2, route, xs, mods, w_gate, w_up, w_down, l)
    return xs[:, n_ctx:, :]
```

```python
import functools

import jax
import jax.numpy as jnp
import numpy as np
from jax import lax
from jax.experimental import pallas as pl
from jax.experimental.pallas import tpu as pltpu

F32 = jnp.float32
BF16 = jnp.bfloat16
HIGHEST = lax.Precision.HIGHEST

EPS = 1e-6
GRID_W = 64
ROPE_THETA = 10000.0

MLA_HEADS = 8
MLA_NOPE = 64
MLA_ROPE = 32
MLA_QK = MLA_NOPE + MLA_ROPE
MLA_V = 64
Q_LORA = 256
KV_LORA = 128

ML_HEADS = 4
ML_DH = 64
ML_W = ML_HEADS * ML_DH
ML_CONV = 5

GLA_HEADS = 4
GLA_DK = 32
GLA_DV = 64
GLA_LR = 16
GLA_TAU = 16.0

N_GROUPS = 4
EXP_PER_GROUP = 8
N_EXPERTS = N_GROUPS * EXP_PER_GROUP
D_EXPERT = 256

LANE = 128
TM = 256
ML_CHUNK = 256
GLA_CHUNK = 128
VMEM_LIMIT = 56 * 1024 * 1024

C_CQ, C_MX, C_MV, C_MO, C_GV, C_GR = 0, 256, 512, 768, 1024, 1280
C_CKV, C_GQ, C_GK, C_SMF, C_SMB = 1536, 1664, 1792, 1920, 2048
D_INP = 2176
SM_GATE = 0
SM_GA = 8
SM_KR = 64
R_GRP = 0
R_EXP = 4
RT_E1 = 0
RT_W1 = 2
GATHER_UNROLL = 8


def _cparams(sem):
    return pltpu.CompilerParams(dimension_semantics=sem, vmem_limit_bytes=VMEM_LIMIT)


def _silu(x):
    return x * jax.nn.sigmoid(x)


def _log_sigmoid(x):
    return -(jnp.maximum(-x, 0.0) + jnp.log1p(jnp.exp(-jnp.abs(x))))


def _nt_dot(a, b, **kw):
    return lax.dot_general(a, b, (((1,), (1,)), ((), ())), preferred_element_type=F32, **kw)


def _split3(x):
    hi = x.astype(BF16)
    r = x - hi.astype(F32)
    mid = r.astype(BF16)
    return hi, mid, (r - mid.astype(F32)).astype(BF16)


def _dot_sel(sel, x, nt=False, right=False):
    s = sel.astype(BF16)
    dot = _nt_dot if nt else functools.partial(jnp.dot, preferred_element_type=F32)
    return sum((dot(piece, s) if right else dot(s, piece)) for piece in _split3(x))


def _tn_dot(a, b, **kw):
    return lax.dot_general(a, b, (((0,), (0,)), ((), ())), preferred_element_type=F32, **kw)


def _mods_kernel(cc_ref, w_ref, b_ref, o_ref):
    a = _silu(cc_ref[...])
    o_ref[0] = jnp.dot(a, w_ref[0], precision=HIGHEST, preferred_element_type=F32) + b_ref[0]


def _mods(cc, w_mod, b_mod):
    depth, d, d6 = w_mod.shape
    nb = 1536
    return pl.pallas_call(
        _mods_kernel,
        grid=(depth, d6 // nb),
        in_specs=[
            pl.BlockSpec((8, d), lambda l, j: (0, 0)),
            pl.BlockSpec((1, d, nb), lambda l, j: (l, 0, j)),
            pl.BlockSpec((1, 1, nb), lambda l, j: (l, 0, j)),
        ],
        out_specs=pl.BlockSpec((1, 8, nb), lambda l, j: (l, 0, j)),
        out_shape=jax.ShapeDtypeStruct((depth, 8, d6), F32),
        compiler_params=_cparams(("arbitrary", "arbitrary")),
        name="adaln_mods",
    )(cc, w_mod, b_mod.reshape(depth, 1, d6))


def _mod_row(b, i):
    return jnp.where(i == 0, 2, b)


def _inproj_kernel(x_ref, mod_ref, nw_ref, w_ref, o_ref):
    x = x_ref[0]
    y = x * lax.rsqrt(jnp.mean(x * x, axis=-1, keepdims=True) + EPS) * nw_ref[...]
    mod = mod_ref[0]
    h = y * (1.0 + mod[1:2]) + mod[0:1]
    o_ref[0] = jnp.dot(h.astype(BF16), w_ref[...], preferred_element_type=F32)


def _inproj(xs, mods, norm_w, w_in_p):
    b, t, d = xs.shape
    nt = t // TM
    return pl.pallas_call(
        _inproj_kernel,
        grid=(b, nt),
        in_specs=[
            pl.BlockSpec((1, TM, d), lambda bi, i: (bi, i, 0)),
            pl.BlockSpec((1, 6, d), lambda bi, i: (_mod_row(bi, i), 0, 0)),
            pl.BlockSpec((1, d), lambda bi, i: (0, 0)),
            pl.BlockSpec((d, D_INP), lambda bi, i: (0, 0)),
        ],
        out_specs=pl.BlockSpec((1, TM, D_INP), lambda bi, i: (bi, i, 0)),
        out_shape=jax.ShapeDtypeStruct((b, t, D_INP), F32),
        compiler_params=_cparams(("parallel", "parallel")),
        name="in_proj",
    )(xs, mods, norm_w.reshape(1, d), w_in_p)


def _mla_prep_kernel(cq_ref, ckv_ref, sm_ref, qan_ref, wuq_ref, kvan_ref, wuk_ref, wuv_ref,
                     qn_ref, kn_ref, cos_ref, sa_ref, sb_ref, q_ref, k_ref, v_ref):
    cq = cq_ref[0]
    cqn = cq * lax.rsqrt(jnp.mean(cq * cq, axis=-1, keepdims=True) + EPS) * qan_ref[...]
    qall = jnp.dot(cqn.astype(BF16), wuq_ref[...], preferred_element_type=F32)
    ckv = ckv_ref[0]
    ckvn = (ckv * lax.rsqrt(jnp.mean(ckv * ckv, axis=-1, keepdims=True) + EPS)
            * kvan_ref[...]).astype(BF16)
    kall = jnp.dot(ckvn, wuk_ref[...], preferred_element_type=F32)
    vall = jnp.dot(ckvn, wuv_ref[...], preferred_element_type=F32)
    lane = lax.broadcasted_iota(jnp.int32, (1, LANE), 1)
    kr = jnp.where((lane >= SM_KR) & (lane < SM_KR + MLA_ROPE), sm_ref[0], 0.0)
    cos, sin = cos_ref[...], sa_ref[...] - sb_ref[...]
    r_i = lax.broadcasted_iota(jnp.int32, (LANE, LANE), 0)
    c_i = lax.broadcasted_iota(jnp.int32, (LANE, LANE), 1)
    half = MLA_ROPE // 2
    first = (c_i >= MLA_NOPE) & (c_i < MLA_NOPE + half)
    second = (c_i >= MLA_NOPE + half) & (c_i < MLA_QK)
    rot = jnp.where(first & (r_i == c_i + half), -1.0,
                    jnp.where(second & (r_i == c_i - half), 1.0, 0.0)).astype(BF16)

    def rope(x):
        return x * cos + jnp.dot(x.astype(BF16), rot, preferred_element_type=F32) * sin

    def head_norm(x, w):
        return x * lax.rsqrt(jnp.sum(x * x, axis=-1, keepdims=True) * (1.0 / MLA_QK) + EPS) * w

    for h in range(MLA_HEADS):
        sl = slice(h * LANE, (h + 1) * LANE)
        qh = rope(head_norm(qall[:, sl], qn_ref[...]))
        q_ref[0, h] = (qh * Q_SCALE).astype(BF16)
        kh = rope(head_norm(kall[:, sl] + kr, kn_ref[...]))
        k_ref[0, h] = jnp.where(lane == ATT_SHIFT_LANE, 1.0, kh).astype(BF16)
        v_ref[0, h] = jnp.where(lane == MLA_V, 1.0, vall[:, sl]).astype(BF16)


def _mla_prep(p, q_a_norm, wuq_p, kv_a_norm, wuk_p, wuv_p, qn_p, kn_p, cos_t, sa_t, sb_t):
    b, t, _ = p.shape
    nt = t // TM
    hw = MLA_HEADS * LANE
    full = lambda shape: pl.BlockSpec(shape, lambda bi, i: (0,) * len(shape))
    tab = pl.BlockSpec((TM, LANE), lambda bi, i: (i, 0))
    out = pl.BlockSpec((1, MLA_HEADS, TM, LANE), lambda bi, i: (bi, 0, i, 0))
    shp = jax.ShapeDtypeStruct((b, MLA_HEADS, t, LANE), BF16)
    return pl.pallas_call(
        _mla_prep_kernel,
        grid=(b, nt),
        in_specs=[
            pl.BlockSpec((1, TM, Q_LORA), lambda bi, i: (bi, i, C_CQ // Q_LORA)),
            pl.BlockSpec((1, TM, KV_LORA), lambda bi, i: (bi, i, C_CKV // KV_LORA)),
            pl.BlockSpec((1, TM, LANE), lambda bi, i: (bi, i, C_SMF // LANE)),
            full((1, Q_LORA)), full((Q_LORA, hw)), full((1, KV_LORA)),
            full((KV_LORA, hw)), full((KV_LORA, hw)), full((1, LANE)), full((1, LANE)),
            tab, tab, tab,
        ],
        out_specs=[out, out, out],
        out_shape=[shp, shp, shp],
        compiler_params=_cparams(("parallel", "parallel")),
        name="mla_prep",
    )(p, p, p, q_a_norm.reshape(1, -1), wuq_p, kv_a_norm.reshape(1, -1), wuk_p, wuv_p,
      qn_p, kn_p, cos_t, sa_t, sb_t)


Q_SCALE = float(MLA_QK ** -0.5 * np.log2(np.e))
ATT_HP = 2
ATT_NQ = 4
ATT_TK = 2048
ATT_SHIFT_LANE = MLA_QK
ATT_SAFE_MAX = 2.0 ** 100


def _softmax_step(q, kb, vb, m, acc):
    s = _nt_dot(q, kb)
    m_new = jnp.max(s, axis=-1, keepdims=True)
    if m is None:
        return m_new, jnp.dot(jnp.exp2((s - m_new).astype(BF16)), vb, preferred_element_type=F32)
    m_new = jnp.maximum(m, m_new)
    p = jnp.exp2((s - m_new).astype(BF16))
    return m_new, acc * jnp.exp2(m - m_new) + jnp.dot(p, vb, preferred_element_type=F32)


def _attn_ctx_kernel(q_ref, k_ref, v_ref, o_ref):
    outs = []
    for h in range(MLA_HEADS):
        _, acc = _softmax_step(q_ref[0, h], k_ref[0, h], v_ref[0, h], None, None)
        outs.append(acc[:, :MLA_V] / acc[:, MLA_V:MLA_V + 1])
    o_ref[0] = jnp.concatenate(outs, axis=-1).astype(BF16)


def _attention_ctx(q, k, v):
    b, h, _, _ = q.shape
    blk = pl.BlockSpec((1, h, TM, LANE), lambda bi: (bi, 0, 0, 0))
    return pl.pallas_call(
        _attn_ctx_kernel,
        grid=(b,),
        in_specs=[blk, blk, blk],
        out_specs=pl.BlockSpec((1, TM, h * MLA_V), lambda bi: (bi, 0, 0)),
        out_shape=jax.ShapeDtypeStruct((b, TM, h * MLA_V), BF16),
        compiler_params=_cparams(("parallel",)),
        name="mla_attention_ctx",
    )(q, k, v)


def _attn_lat_kernel(*refs, n_blk):
    q_refs, (k_ref, v_ref, o_ref, q_buf, qs_buf) = refs[:ATT_NQ], refs[ATT_NQ:]
    lane = lax.broadcasted_iota(jnp.int32, (1, LANE), 1)

    def kv_block(hh, j):
        off = pl.multiple_of(TM + j * ATT_TK, TM)
        return k_ref[0, hh, pl.ds(off, ATT_TK), :], v_ref[0, hh, pl.ds(off, ATT_TK), :]

    def finish(accs):
        outs = [acc[:, :MLA_V] / acc[:, MLA_V:MLA_V + 1] for acc in accs]
        o_ref[0] = jnp.concatenate(outs, axis=-1).astype(BF16)

    accs = []
    for hh in range(ATT_HP):
        q = jnp.concatenate([qr[0, hh] for qr in q_refs], axis=0)
        q_buf[hh] = q
        kb, vb = k_ref[0, hh, 0:TM, :], v_ref[0, hh, 0:TM, :]
        shift = jnp.max(_nt_dot(q, kb), axis=-1, keepdims=True).astype(BF16)
        qs_buf[hh] = jnp.where(lane == ATT_SHIFT_LANE, -shift, q)
        accs.append(jnp.dot(jnp.exp2(_nt_dot(qs_buf[hh], kb).astype(BF16)), vb, preferred_element_type=F32))

    def fast_body(j, accs):
        new = []
        for hh in range(ATT_HP):
            kb, vb = kv_block(hh, j)
            p = jnp.exp2(_nt_dot(qs_buf[hh], kb).astype(BF16))
            new.append(accs[hh] + jnp.dot(p, vb, preferred_element_type=F32))
        return tuple(new)

    accs = lax.fori_loop(0, n_blk, fast_body, tuple(accs))
    bad = sum(jnp.max(jnp.where(jnp.abs(acc) < ATT_SAFE_MAX, 0.0, 1.0)) for acc in accs)

    @pl.when(bad == 0.0)
    def _():
        finish(accs)

    @pl.when(bad != 0.0)
    def _():
        init = []
        for hh in range(ATT_HP):
            init += _softmax_step(q_buf[hh], k_ref[0, hh, 0:TM, :], v_ref[0, hh, 0:TM, :], None, None)

        def body(j, carry):
            new = []
            for hh in range(ATT_HP):
                new += _softmax_step(q_buf[hh], *kv_block(hh, j), carry[2 * hh], carry[2 * hh + 1])
            return tuple(new)

        carry = lax.fori_loop(0, n_blk, body, tuple(init))
        finish(carry[1::2])


def _attention_lat(q, k, v):
    b, h, t, _ = q.shape
    s = t - TM
    tq = ATT_NQ * TM
    assert s % ATT_TK == 0 and s % tq == 0
    kv = pl.BlockSpec((1, ATT_HP, t, LANE), lambda bi, hp, i: (bi, hp, 0, 0))
    qs = [pl.BlockSpec((1, ATT_HP, TM, LANE), lambda bi, hp, i, u=u: (bi, hp, 1 + ATT_NQ * i + u, 0))
          for u in range(ATT_NQ)]
    return pl.pallas_call(
        functools.partial(_attn_lat_kernel, n_blk=s // ATT_TK),
        grid=(b, h // ATT_HP, s // tq),
        in_specs=qs + [kv, kv],
        out_specs=pl.BlockSpec((1, tq, ATT_HP * MLA_V), lambda bi, hp, i: (bi, i, hp)),
        out_shape=jax.ShapeDtypeStruct((b, s, h * MLA_V), BF16),
        scratch_shapes=[pltpu.VMEM((ATT_HP, tq, LANE), BF16), pltpu.VMEM((ATT_HP, tq, LANE), BF16)],
        compiler_params=_cparams(("parallel", "parallel", "arbitrary")),
        name="mla_attention",
    )(*([q] * ATT_NQ), k, v)


def _ml_prep_kernel(x_ref, prev_ref, next_ref, cw_ref, cb_ref, wq_ref, wk_ref,
                    xc_ref, q_ref, k_ref, *, n_tiles):
    i = pl.program_id(1)
    x = x_ref[0]
    prev = jnp.where(i <= 1, 0.0, prev_ref[0])
    nxt = jnp.where((i == 0) | (i == n_tiles - 1), 0.0, next_ref[0])
    ext = jnp.concatenate([prev, x, nxt], axis=0)
    n_ext = TM + 16
    cw = cw_ref[...]
    acc = jnp.zeros((TM, ML_W), F32) + cb_ref[...]
    for kk in range(ML_CONV):
        sh = (ML_CONV // 2 - kk) % n_ext
        shifted = ext if sh == 0 else pltpu.roll(ext, sh, 0)
        acc = acc + cw[kk:kk + 1] * shifted[8:8 + TM]
    xc = _silu(acc)
    xc_ref[0] = xc
    xb = xc.astype(BF16)
    q_ref[0] = jnp.dot(xb, wq_ref[...], preferred_element_type=F32).astype(BF16)
    k_ref[0] = jnp.dot(xb, wk_ref[...], preferred_element_type=F32).astype(BF16)


def _ml_prep(p, conv_w8, conv_b, wq_bd, wk_bd):
    b, t, _ = p.shape
    nt = t // TM
    r8 = TM // 8
    full = lambda shape: pl.BlockSpec(shape, lambda bi, i: (0,) * len(shape))
    cb = C_MX // ML_W
    blk = pl.BlockSpec((1, TM, ML_W), lambda bi, i: (bi, i, 0))
    return pl.pallas_call(
        functools.partial(_ml_prep_kernel, n_tiles=nt),
        grid=(b, nt),
        in_specs=[
            pl.BlockSpec((1, TM, ML_W), lambda bi, i: (bi, i, cb)),
            pl.BlockSpec((1, 8, ML_W), lambda bi, i: (bi, jnp.maximum(i * r8 - 1, 0), cb)),
            pl.BlockSpec((1, 8, ML_W), lambda bi, i: (bi, jnp.minimum((i + 1) * r8, nt * r8 - 1), cb)),
            full((8, ML_W)), full((1, ML_W)), full((ML_W, ML_W)), full((ML_W, ML_W)),
        ],
        out_specs=[blk, blk, blk],
        out_shape=[jax.ShapeDtypeStruct((b, t, ML_W), F32),
                   jax.ShapeDtypeStruct((b, t, ML_W), BF16),
                   jax.ShapeDtypeStruct((b, t, ML_W), BF16)],
        compiler_params=_cparams(("parallel", "parallel")),
        name="mlstm_prep",
    )(p, p, p, conv_w8, conv_b.reshape(1, ML_W), wq_bd, wk_bd)


def _scan_chunk(d, step, n_ctx_chunks, n_chunks):
    bwd = jnp.where(step < n_ctx_chunks, n_ctx_chunks - 1 - step, n_chunks - 1 - (step - n_ctx_chunks))
    return jnp.where(d == 0, step, bwd)


def _ml_chunk(d, q, k, v, g, s_ref, m_ref, base):
    L = ML_CHUNK
    row = lax.broadcasted_iota(jnp.int32, (L, L), 0)
    col = lax.broadcasted_iota(jnp.int32, (L, L), 1)
    mask = col <= row if d == 0 else col >= row
    tri = mask.astype(F32)
    lane = lax.broadcasted_iota(jnp.int32, (1, LANE), 1)
    eye8 = (lax.broadcasted_iota(jnp.int32, (8, LANE), 0)
            == lax.broadcasted_iota(jnp.int32, (8, LANE), 1)).astype(F32)

    lf = _log_sigmoid(g)
    bc = jnp.dot(tri, lf, precision=HIGHEST, preferred_element_type=F32)
    g_rows = _nt_dot(eye8, g, precision=HIGHEST)
    bc = pltpu.roll(bc, LANE - ML_HEADS, 1)
    bc_rows = _nt_dot(eye8, bc, precision=HIGHEST)

    outs = []
    for pair in range(ML_HEADS // 2):
        sl = slice(pair * LANE, (pair + 1) * LANE)
        q_blk, k_blk, v_blk = q[:, sl], k[:, sl], v[:, sl]
        pair_out = []
        for sub in range(2):
            h = pair * 2 + sub
            head_lanes = (lane >= sub * ML_DH) & (lane < (sub + 1) * ML_DH)
            qh = jnp.where(head_lanes, q_blk, jnp.zeros_like(q_blk))
            vs = v_blk if sub == 0 else pltpu.roll(v_blk, ML_DH, 1)
            v_ext = jnp.where(lane < ML_DH, vs, jnp.where(lane == ML_DH, 1.0, 0.0)).astype(BF16)

            li_c = g[:, SM_GATE + h:SM_GATE + h + 1]
            bc_c = bc[:, SM_GATE + h:SM_GATE + h + 1]
            li_r = g_rows[h:h + 1, :]
            bc_r = bc_rows[h:h + 1, :]
            m_st = m_ref[base + h][0:1, 0:1]

            c_row = jnp.where(mask, li_r - bc_r, -jnp.inf)
            m_rel = jnp.maximum(m_st, jnp.max(c_row, axis=-1, keepdims=True))
            m_t = bc_c + m_rel
            e = jnp.exp(c_row - m_rel)
            s = (_nt_dot(qh, k_blk) * e).astype(BF16)
            tot = (jnp.dot(s, v_ext, preferred_element_type=F32)
                   + jnp.exp(m_st - m_rel) * jnp.dot(qh, s_ref[base + h].astype(BF16),
                                                     preferred_element_type=F32))
            den = tot[:, ML_DH:ML_DH + 1]
            pair_out.append(tot / jnp.maximum(jnp.abs(den), jnp.exp(-m_t)))

            b_end = bc_c[L - 1:L] if d == 0 else bc_c[0:1]
            g_col = b_end - bc_c + li_c
            m_new = jnp.maximum(b_end + m_st, jnp.max(g_col, axis=0, keepdims=True))
            kw = jnp.where(head_lanes, k_blk.astype(F32) * jnp.exp(g_col - m_new), 0.0).astype(BF16)
            s_ref[base + h] = jnp.exp(b_end + m_st - m_new) * s_ref[base + h] + _tn_dot(kw, v_ext)
            m_ref[base + h] = jnp.broadcast_to(m_new, m_ref.shape[1:])
        outs.append(jnp.where(lane < ML_DH, pair_out[0], pltpu.roll(pair_out[1], ML_DH, 1)))
    return jnp.concatenate(outs, axis=-1)


def _ml_scan_kernel(qf_ref, kf_ref, vf_ref, smf_ref, qb_ref, kb_ref, vb_ref, smb_ref, gb_ref,
                    hf_ref, hb_ref, s_ref, m_ref):
    @pl.when(pl.program_id(0) == 0)
    def _():
        s_ref[...] = jnp.zeros_like(s_ref)
        m_ref[...] = jnp.zeros_like(m_ref)

    nb = qf_ref.shape[0]
    streams = ((0, qf_ref, kf_ref, vf_ref, smf_ref, hf_ref), (1, qb_ref, kb_ref, vb_ref, smb_ref, hb_ref))
    for d, q_ref, k_ref, v_ref, sm_ref, h_ref in streams:
        for bi in range(nb):
            h_ref[bi] = _ml_chunk(d, q_ref[bi], k_ref[bi], v_ref[bi], sm_ref[bi] + gb_ref[d],
                                  s_ref, m_ref, (d * nb + bi) * ML_HEADS)


def _ml_scan(q, k, p, gate_b_p):
    b, t, _ = q.shape
    nc = t // ML_CHUNK
    chunk = functools.partial(_scan_chunk, n_ctx_chunks=TM // ML_CHUNK, n_chunks=nc)
    blk = lambda d, w, cb: pl.BlockSpec((b, ML_CHUNK, w), lambda s: (0, chunk(d, s), cb))
    stream = lambda d: [blk(d, ML_W, 0), blk(d, ML_W, 0), blk(d, ML_W, C_MV // ML_W),
                        blk(d, LANE, C_SMF // LANE + d)]
    shp = jax.ShapeDtypeStruct((b, t, ML_W), F32)
    return pl.pallas_call(
        _ml_scan_kernel,
        grid=(nc,),
        in_specs=stream(0) + stream(1) + [pl.BlockSpec((2, 1, LANE), lambda s: (0, 0, 0))],
        out_specs=[blk(0, ML_W, 0), blk(1, ML_W, 0)],
        out_shape=[shp, shp],
        scratch_shapes=[pltpu.VMEM((2 * b * ML_HEADS, LANE, LANE), F32),
                        pltpu.VMEM((2 * b * ML_HEADS, 8, LANE), F32)],
        compiler_params=_cparams(("arbitrary",)),
        name="mlstm_scan",
    )(q, k, p, p, q, k, p, p, gate_b_p)


def _gla_chunk(d, q, k, v, sm, wa, ba, s_ref, si):
    L = GLA_CHUNK
    row = lax.broadcasted_iota(jnp.int32, (L, L), 0)
    col = lax.broadcasted_iota(jnp.int32, (L, L), 1)
    mask = col <= row if d == 0 else col >= row
    tri = mask.astype(F32)
    lane_k = lax.broadcasted_iota(jnp.int32, (1, GLA_HEADS * GLA_DK), 1)
    lane_v = lax.broadcasted_iota(jnp.int32, (1, GLA_HEADS * GLA_DV), 1)
    eye = (lax.broadcasted_iota(jnp.int32, (LANE, LANE), 0)
           == lax.broadcasted_iota(jnp.int32, (LANE, LANE), 1)).astype(F32)

    pre = jnp.dot(sm.astype(BF16), wa.astype(BF16), preferred_element_type=F32) + ba
    loga = _log_sigmoid(pre) * (1.0 / GLA_TAU)
    bc = _dot_sel(tri, loga)
    ref_row = bc[L // 2 - 1:L // 2]
    b_end = bc[L - 1:L] if d == 0 else bc[0:1]

    q = q * (GLA_DK ** -0.5)
    v = v.astype(BF16)
    q_in = (q * jnp.exp(bc - ref_row))
    k_in = (k * jnp.exp(ref_row - bc)).astype(BF16)
    q_st = (q * jnp.exp(bc)).astype(BF16)
    k_st = (k * jnp.exp(b_end - bc)).astype(BF16)

    blockdiag = (lax.broadcasted_iota(jnp.int32, s_ref.shape[1:], 0) // GLA_DK
                 == lax.broadcasted_iota(jnp.int32, s_ref.shape[1:], 1) // GLA_DV)
    s_old = s_ref[si]
    o = jnp.dot(q_st, jnp.where(blockdiag, s_old, 0.0).astype(BF16), preferred_element_type=F32)
    for h in range(GLA_HEADS):
        qh = jnp.where(lane_k // GLA_DK == h, q_in, 0.0).astype(BF16)
        att = jnp.where(mask, _nt_dot(qh, k_in), 0.0).astype(BF16)
        oh = jnp.dot(att, v, preferred_element_type=F32)
        o = o + jnp.where(lane_v // GLA_DV == h, oh, 0.0)

    decay_col = jnp.exp(_dot_sel(eye, jnp.broadcast_to(b_end, (8, LANE)), nt=True)[:, 0:1])
    s_ref[si] = decay_col * s_old + _tn_dot(k_st, v)
    return o


def _gla_scan_kernel(qf_ref, kf_ref, vf_ref, smf_ref, qb_ref, kb_ref, vb_ref, smb_ref, wa_ref, ba_ref,
                     of_ref, ob_ref, s_ref):
    @pl.when(pl.program_id(0) == 0)
    def _():
        s_ref[...] = jnp.zeros_like(s_ref)

    nb = qf_ref.shape[0]
    n_sub = TM // GLA_CHUNK
    streams = ((0, qf_ref, kf_ref, vf_ref, smf_ref, of_ref), (1, qb_ref, kb_ref, vb_ref, smb_ref, ob_ref))
    for d, q_ref, k_ref, v_ref, sm_ref, o_ref in streams:
        for bi in range(nb):
            for c in (range(n_sub) if d == 0 else reversed(range(n_sub))):
                rows = slice(c * GLA_CHUNK, (c + 1) * GLA_CHUNK)
                o_ref[bi, rows] = _gla_chunk(d, q_ref[bi, rows], k_ref[bi, rows], v_ref[bi, rows],
                                             sm_ref[bi, rows], wa_ref[d], ba_ref[d], s_ref, d * nb + bi)


def _gla_scan(p, wa_p, ba_p):
    b, t, _ = p.shape
    nc = t // TM
    chunk = functools.partial(_scan_chunk, n_ctx_chunks=1, n_chunks=nc)
    kw, vw = GLA_HEADS * GLA_DK, GLA_HEADS * GLA_DV
    blk = lambda d, w, col: pl.BlockSpec((b, TM, w), lambda s: (0, chunk(d, s), col // w))
    stream = lambda d: [blk(d, kw, C_GQ), blk(d, kw, C_GK), blk(d, vw, C_GV), blk(d, LANE, C_SMF + d * LANE)]
    shp = jax.ShapeDtypeStruct((b, t, vw), F32)
    return pl.pallas_call(
        _gla_scan_kernel,
        grid=(nc,),
        in_specs=stream(0) + stream(1) + [pl.BlockSpec((2, LANE, kw), lambda s: (0, 0, 0)),
                                          pl.BlockSpec((2, 1, kw), lambda s: (0, 0, 0))],
        out_specs=[blk(0, vw, 0), blk(1, vw, 0)],
        out_shape=[shp, shp],
        scratch_shapes=[pltpu.VMEM((2 * b, kw, vw), F32)],
        compiler_params=_cparams(("arbitrary",)),
        name="gla_scan",
    )(p, p, p, p, p, p, p, p, wa_p, ba_p)


OUTPROJ_TILES = 2
OUTPROJ_N_TOK = 11


def _outproj_kernel(*refs, n_tiles):
    n_in = OUTPROJ_TILES * OUTPROJ_N_TOK
    shared = refs[n_in:n_in + 7]
    outs = refs[n_in + 7:]
    for u in range(OUTPROJ_TILES):
        tile = jnp.minimum(OUTPROJ_TILES * pl.program_id(1) + u, n_tiles - 1)
        _outproj_tile(tile == 0, slice(u * TM, (u + 1) * TM),
                      *refs[u * OUTPROJ_N_TOK:(u + 1) * OUTPROJ_N_TOK], *shared, *outs)


def _outproj_tile(is_ctx, rows, ac_ref, al_ref, mhf_ref, mhb_ref, xc_ref, mo_ref, gof_ref, gob_ref, gr_ref, x_ref,
                  mod_ref, mnw_ref, msk_ref, gnw_ref, wout_ref, n2w_ref, wr_ref, br_ref,
                  xo_ref, h2_ref, route_ref):
    grp64 = (lax.broadcasted_iota(jnp.int32, (ML_W, ML_W), 0) // ML_DH
             == lax.broadcasted_iota(jnp.int32, (ML_W, ML_W), 1) // ML_DH).astype(F32) * (1.0 / ML_DH)

    def head_norm(x, w):
        ms = _dot_sel(grp64, x * x, right=True)
        return x * lax.rsqrt(ms + EPS) * w

    m_l = jax.nn.sigmoid(mo_ref[0]) * (head_norm(mhf_ref[0] + mhb_ref[0], mnw_ref[...])
                                       + msk_ref[...] * xc_ref[0])
    g_l = head_norm(gof_ref[0] + gob_ref[0], gnw_ref[...]) * _silu(gr_ref[0])
    na = MLA_HEADS * MLA_V
    a = jnp.where(is_ctx, ac_ref[0], al_ref[0])
    res = (jnp.dot(a, wout_ref[0:na], preferred_element_type=F32)
           + jnp.dot(m_l.astype(BF16), wout_ref[na:na + ML_W], preferred_element_type=F32)
           + jnp.dot(g_l.astype(BF16), wout_ref[na + ML_W:], preferred_element_type=F32))
    mod = mod_ref[0]
    x = x_ref[0] + mod[2:3] * res
    xo_ref[0, rows] = x
    h2 = (x * lax.rsqrt(jnp.mean(x * x, axis=-1, keepdims=True) + EPS) * n2w_ref[...]
          * (1.0 + mod[4:5]) + mod[3:4])
    half = h2.shape[-1] // 2
    h2_ref[0, rows] = _pack_bf16_pair(h2[:, :half], h2[:, half:])

    h_hi, h_lo, _ = _split3(h2)
    w_hi, w_lo, _ = _split3(wr_ref[...])
    dot = functools.partial(jnp.dot, preferred_element_type=F32)
    logits = dot(h_hi, w_hi) + dot(h_hi, w_lo) + dot(h_lo, w_hi) + br_ref[...]
    lane = lax.broadcasted_iota(jnp.int32, (1, LANE), 1)
    lane_f = lane.astype(F32)
    neg = -jnp.inf
    gl = jnp.where(lane < R_EXP, logits, neg)
    gmax = jnp.max(gl, axis=-1, keepdims=True)
    g_w = 1.0 / jnp.sum(jnp.exp(gl - gmax), axis=-1, keepdims=True)
    g_i = jnp.min(jnp.where(gl == gmax, lane_f, float(LANE)), axis=-1, keepdims=True)
    grp_of_lane = ((lane - R_EXP) // EXP_PER_GROUP).astype(F32)
    in_grp = (lane >= R_EXP) & (lane < R_EXP + N_EXPERTS) & (grp_of_lane == g_i)
    el = jnp.where(in_grp, logits, neg)
    m1 = jnp.max(el, axis=-1, keepdims=True)
    i1 = jnp.min(jnp.where(el == m1, lane_f, float(LANE)), axis=-1, keepdims=True)
    el2 = jnp.where(lane_f == i1, neg, el)
    m2 = jnp.max(el2, axis=-1, keepdims=True)
    i2 = jnp.min(jnp.where(el2 == m2, lane_f, float(LANE)), axis=-1, keepdims=True)
    p2 = jnp.exp(m2 - m1)
    w1 = g_w / (1.0 + p2)
    route_ref[0, rows] = jnp.where(lane == RT_E1, i1 - R_EXP, jnp.where(lane == RT_E1 + 1, i2 - R_EXP,
                             jnp.where(lane == RT_W1, w1, jnp.where(lane == RT_W1 + 1, p2 * w1, 0.0))))


def _outproj(a_ctx, a_lat, mh, xconv, p, go, xs, mods, ml_norm_w, ml_skip, gla_norm_w, w_out_b, norm2_w, wr_p, br_p):
    b, t, d = xs.shape
    nt = t // TM
    full = lambda shape: pl.BlockSpec(shape, lambda bi, i: (0,) * len(shape))
    na = MLA_HEADS * MLA_V

    out = lambda w: pl.BlockSpec((1, OUTPROJ_TILES * TM, w), lambda bi, i: (bi, i, 0))

    def tile_specs(u):
        tile = lambda i: jnp.minimum(OUTPROJ_TILES * i + u, nt - 1)
        tok = lambda w, cb=0: pl.BlockSpec((1, TM, w), lambda bi, i: (bi, tile(i), cb))
        ins = [pl.BlockSpec((1, TM, na), lambda bi, i: (bi, 0, 0)),
               pl.BlockSpec((1, TM, na), lambda bi, i: (bi, jnp.maximum(tile(i) - 1, 0), 0)),
               tok(ML_W), tok(ML_W), tok(ML_W), tok(ML_W, C_MO // ML_W),
               tok(ML_W), tok(ML_W), tok(ML_W, C_GR // ML_W), tok(d),
               pl.BlockSpec((1, 6, d), lambda bi, i: (_mod_row(bi, tile(i)), 0, 0))]
        return ins

    specs = [tile_specs(u) for u in range(OUTPROJ_TILES)]
    tok_args = (a_ctx, a_lat, mh[0], mh[1], xconv, p, go[0], go[1], p, xs, mods)
    assert len(tok_args) == OUTPROJ_N_TOK
    shapes = [jax.ShapeDtypeStruct((b, t, d), F32), jax.ShapeDtypeStruct((b, t, d // 2), jnp.uint32),
              jax.ShapeDtypeStruct((b, t, LANE), F32)]
    outs = pl.pallas_call(
        functools.partial(_outproj_kernel, n_tiles=nt),
        grid=(b, pl.cdiv(nt, OUTPROJ_TILES)),
        in_specs=sum(specs, []) + [
            full((1, ML_W)), full((1, ML_W)), full((1, ML_W)), full((d, d)), full((1, d)),
            full((d, LANE)), full((1, LANE))],
        out_specs=[out(d), out(d // 2), out(LANE)],
        out_shape=shapes,
        compiler_params=_cparams(("parallel", "arbitrary")),
        name="out_proj_router",
    )(*(tok_args * OUTPROJ_TILES), ml_norm_w.reshape(1, -1), ml_skip.reshape(1, -1),
      gla_norm_w.reshape(1, -1), w_out_b, norm2_w.reshape(1, -1), wr_p, br_p)
    return outs


def _dispatch(route, n_tiles):
    n = route.shape[0]
    flat = route[:, RT_E1:RT_E1 + 2].astype(jnp.int32).reshape(-1)
    onehot = (flat[:, None] == jnp.arange(N_EXPERTS, dtype=jnp.int32)[None, :]).astype(jnp.int32)
    csum = jnp.cumsum(onehot, axis=0)
    rank = jnp.sum(csum * onehot, axis=1) - 1
    padded = (csum[-1] + TM - 1) // TM * TM
    ends = jnp.cumsum(padded)
    pos = (ends - padded)[flat] + rank
    tile_start = jnp.arange(n_tiles, dtype=jnp.int32) * TM
    tile_exp = jnp.minimum(jnp.sum((ends[None, :] <= tile_start[:, None]).astype(jnp.int32), axis=1),
                           N_EXPERTS - 1)
    tile_on = (tile_start < ends[-1]).astype(jnp.int32)
    pos = pos.reshape(n // TM, TM, 2)
    return (tile_exp, tile_on,
            pos[:, :, 0].reshape(n // TM, 1, TM), pos[:, :, 1].reshape(n // TM, 1, TM))


def _gather_rows(src_hbm, idx_ref, dst, sem):
    def body(j, carry):
        pltpu.make_async_copy(src_hbm.at[pl.ds(idx_ref[0, 0, j], 1)], dst.at[pl.ds(j, 1)], sem).start()
        return carry

    lax.fori_loop(0, TM, body, 0, unroll=GATHER_UNROLL)


def _wait_rows(src_hbm, dst, sem):
    pltpu.make_async_copy(src_hbm.at[pl.ds(0, TM)], dst, sem).wait()


def _pack_bf16_pair(lo, hi):
    lo_b = lax.bitcast_convert_type(lo.astype(BF16).astype(F32), jnp.uint32) >> 16
    hi_b = lax.bitcast_convert_type(hi.astype(BF16).astype(F32), jnp.uint32) & jnp.uint32(0xFFFF0000)
    return hi_b | lo_b


def _unpack_bf16_pair(w):
    return (lax.bitcast_convert_type(w << 16, F32),
            lax.bitcast_convert_type(w & jnp.uint32(0xFFFF0000), F32))


def _scatter_kernel(p1_ref, p2_ref, h2_ref, init_hbm, xs_hbm, sem):
    del init_hbm

    def body(j, carry):
        row = h2_ref.at[pl.ds(j, 1)]
        pltpu.make_async_copy(row, xs_hbm.at[pl.ds(p1_ref[0, 0, j], 1)], sem).start()
        pltpu.make_async_copy(row, xs_hbm.at[pl.ds(p2_ref[0, 0, j], 1)], sem).start()
        return carry

    lax.fori_loop(0, TM, body, 0, unroll=GATHER_UNROLL)
    for _ in range(2):
        pltpu.make_async_copy(h2_ref, xs_hbm.at[pl.ds(0, TM)], sem).wait()


def _scatter_rows(h2p, pos1, pos2, n_rows):
    n, dh = h2p.shape
    idx = pl.BlockSpec((1, 1, TM), lambda g: (g, 0, 0), memory_space=pltpu.SMEM)
    return pl.pallas_call(
        _scatter_kernel,
        grid=(n // TM,),
        in_specs=[idx, idx, pl.BlockSpec((TM, dh), lambda g: (g, 0)), pl.BlockSpec(memory_space=pl.ANY)],
        out_specs=pl.BlockSpec(memory_space=pl.ANY),
        out_shape=jax.ShapeDtypeStruct((n_rows, dh), jnp.uint32),
        scratch_shapes=[pltpu.SemaphoreType.DMA(())],
        input_output_aliases={3: 0},
        compiler_params=_cparams(("arbitrary",)),
        name="moe_scatter",
    )(pos1, pos2, h2p, jnp.zeros((n_rows, dh), jnp.uint32))


def _experts_kernel(texp_ref, ton_ref, x_ref, wg_ref, wu_ref, wd_ref, y_ref, wgu_b, wd_b):
    r = pl.program_id(0)
    half = wgu_b.shape[0] // 2

    @pl.when(ton_ref[r] == 1)
    def _():
        @pl.when(jnp.logical_or(r == 0, texp_ref[r] != texp_ref[jnp.maximum(r - 1, 0)]))
        def _():
            wgu_b[:, :D_EXPERT] = wg_ref[0, 0].astype(BF16)
            wgu_b[:, D_EXPERT:] = wu_ref[0, 0].astype(BF16)
            wd_b[...] = wd_ref[0, 0].astype(BF16)

        x_lo, x_hi = _unpack_bf16_pair(x_ref[...])
        gu = (jnp.dot(x_lo.astype(BF16), wgu_b[0:half], preferred_element_type=F32)
              + jnp.dot(x_hi.astype(BF16), wgu_b[half:], preferred_element_type=F32))
        act = (_silu(gu[:, :D_EXPERT]) * gu[:, D_EXPERT:]).astype(BF16)
        y = jnp.dot(act, wd_b[...], preferred_element_type=F32)
        y_ref[...] = _pack_bf16_pair(y[:, :half], y[:, half:])

    @pl.when(ton_ref[r] == 0)
    def _():
        y_ref[...] = jnp.zeros_like(y_ref)


def _experts(xs, tile_exp, tile_on, w_gate, w_up, w_down, layer):
    n_rows, dh = xs.shape
    d = 2 * dh
    wspec = lambda shape: pl.BlockSpec((1, 1) + shape, lambda r, te, to: (layer, te[r], 0, 0))
    rows = pl.BlockSpec((TM, dh), lambda r, te, to: (r, 0))
    return pl.pallas_call(
        _experts_kernel,
        grid_spec=pltpu.PrefetchScalarGridSpec(
            num_scalar_prefetch=2,
            grid=(n_rows // TM,),
            in_specs=[rows, wspec((d, D_EXPERT)), wspec((d, D_EXPERT)), wspec((D_EXPERT, d))],
            out_specs=rows,
            scratch_shapes=[pltpu.VMEM((d, 2 * D_EXPERT), BF16), pltpu.VMEM((D_EXPERT, d), BF16)],
        ),
        out_shape=jax.ShapeDtypeStruct((n_rows, dh), jnp.uint32),
        compiler_params=_cparams(("arbitrary",)),
        name="moe_experts",
    )(tile_exp, tile_on, xs, w_gate, w_up, w_down)


def _combine_kernel(p1_ref, p2_ref, p1n_ref, p2n_ref, route_ref, x_ref, mod_ref, y_hbm, o_ref, buf, sem):
    g = pl.program_id(0)
    slot = g % 2
    half = buf.shape[-1]

    def gather(pa, pb, s):
        _gather_rows(y_hbm, pa, buf.at[s, 0], sem.at[s])
        _gather_rows(y_hbm, pb, buf.at[s, 1], sem.at[s])

    @pl.when(g == 0)
    def _():
        gather(p1_ref, p2_ref, 0)

    @pl.when(g + 1 < pl.num_programs(0))
    def _():
        gather(p1n_ref, p2n_ref, 1 - slot)

    _wait_rows(y_hbm, buf.at[slot, 0], sem.at[slot])
    _wait_rows(y_hbm, buf.at[slot, 1], sem.at[slot])
    route = route_ref[...]
    w1, w2 = route[:, RT_W1:RT_W1 + 1], route[:, RT_W1 + 1:RT_W1 + 2]
    gate = mod_ref[0][5:6]
    for part, (y1, y2) in enumerate(zip(_unpack_bf16_pair(buf[slot, 0]), _unpack_bf16_pair(buf[slot, 1]))):
        cols = slice(part * half, (part + 1) * half)
        o_ref[:, cols] = x_ref[:, cols] + gate[:, cols] * (w1 * y1 + w2 * y2)


def _combine(y, pos1, pos2, route, xs2, mods, nt):
    n, d = xs2.shape
    n_tok_tiles = n // TM
    idx = lambda nxt: pl.BlockSpec((1, 1, TM), lambda g: (jnp.minimum(g + nxt, n_tok_tiles - 1), 0, 0),
                                   memory_space=pltpu.SMEM)
    tok = lambda w: pl.BlockSpec((TM, w), lambda g: (g, 0))
    return pl.pallas_call(
        _combine_kernel,
        grid=(n_tok_tiles,),
        in_specs=[
            idx(0), idx(0), idx(1), idx(1), tok(LANE), tok(d),
            pl.BlockSpec((1, 6, d), lambda g: (_mod_row(g // nt, g % nt), 0, 0)),
            pl.BlockSpec(memory_space=pl.ANY),
        ],
        out_specs=tok(d),
        out_shape=jax.ShapeDtypeStruct((n, d), F32),
        scratch_shapes=[pltpu.VMEM((2, 2, TM, d // 2), jnp.uint32), pltpu.SemaphoreType.DMA((2,))],
        compiler_params=_cparams(("arbitrary",)),
        name="moe_combine",
    )(pos1, pos2, pos1, pos2, route, xs2, mods, y)


def _moe(h2p, route, xs, mods, w_gate, w_up, w_down, layer):
    b, t, d = xs.shape
    n = b * t
    n_tiles = 2 * n // TM + N_EXPERTS
    route2 = route.reshape(n, LANE)
    tile_exp, tile_on, pos1, pos2 = _dispatch(route2, n_tiles)
    xs_sorted = _scatter_rows(h2p.reshape(n, d // 2), pos1, pos2, n_tiles * TM)
    y = _experts(xs_sorted, tile_exp, tile_on, w_gate, w_up, w_down, layer)
    return _combine(y, pos1, pos2, route2, xs.reshape(n, d), mods, t // TM).reshape(b, t, d)


def _rope_tables(n_ctx, n_lat):
    rows = n_lat // GRID_W
    row = jnp.broadcast_to(jnp.arange(rows, dtype=F32)[:, None], (rows, GRID_W)).reshape(-1)
    col = jnp.broadcast_to(jnp.arange(GRID_W, dtype=F32)[None, :], (rows, GRID_W)).reshape(-1)
    n_freq = MLA_ROPE // 4
    inv = ROPE_THETA ** (-jnp.arange(n_freq, dtype=F32) / n_freq)
    ang = jnp.concatenate([row[:, None] * inv, col[:, None] * inv], axis=-1)
    cos, sin = jnp.cos(ang), jnp.sin(ang)
    half = MLA_ROPE // 2
    z = lambda w: jnp.zeros((n_lat, w), F32)
    o = lambda w: jnp.ones((n_lat, w), F32)
    tail = LANE - MLA_QK
    cos_t = jnp.concatenate([o(MLA_NOPE), cos, cos, o(tail)], axis=-1)
    sa_t = jnp.concatenate([z(MLA_NOPE + half), sin, z(tail)], axis=-1)
    sb_t = jnp.concatenate([z(MLA_NOPE), -sin, z(half + tail)], axis=-1)
    ctx1 = jnp.ones((n_ctx, LANE), F32)
    ctx0 = jnp.zeros((n_ctx, LANE), F32)
    return (jnp.concatenate([ctx1, cos_t], 0), jnp.concatenate([ctx0, sa_t], 0),
            jnp.concatenate([ctx0, sb_t], 0))


def _pad_cols(a, width):
    return jnp.pad(a, [(0, 0)] * (a.ndim - 1) + [(0, width - a.shape[-1])])


def _layer_weights(w_in, w_uq, w_ukv, q_norm_w, k_norm_w, ml_conv_w, ml_wq, ml_wk, ml_gate_b,
                   gla_wa, gla_ba, w_out, w_grp, b_grp, w_erouter, b_erouter):
    d = w_in.shape[0]
    o = np.cumsum((0, Q_LORA, KV_LORA, MLA_ROPE, ML_W, ML_W, ML_W, 4 * ML_HEADS, GLA_HEADS * GLA_DK,
                   GLA_HEADS * GLA_DK, GLA_HEADS * GLA_DV, GLA_HEADS * GLA_DV, 2 * GLA_LR))
    seg = lambda j: w_in[:, o[j]:o[j + 1]]
    cq, ckv, kr, mx, mv, mo, mg, gq, gk, gv, gr, ga = (seg(j) for j in range(12))
    z = lambda w: jnp.zeros((d, w), F32)

    def small(di):
        return jnp.concatenate([mg[:, di * 8:(di + 1) * 8], ga[:, di * GLA_LR:(di + 1) * GLA_LR],
                                z(SM_KR - SM_GA - GLA_LR), kr, z(LANE - SM_KR - MLA_ROPE)], axis=-1)

    w_in_p = jnp.concatenate([cq, mx, mv, mo, gv, gr, ckv, gq, gk, small(0), small(1)], axis=-1).astype(BF16)

    wuq_p = _pad_cols(w_uq.reshape(Q_LORA, MLA_HEADS, MLA_QK), LANE).reshape(Q_LORA, -1).astype(BF16)
    ukv = w_ukv.reshape(KV_LORA, MLA_HEADS, MLA_NOPE + MLA_V)
    wuk_p = _pad_cols(ukv[..., :MLA_NOPE], LANE).reshape(KV_LORA, -1).astype(BF16)
    wuv_p = _pad_cols(ukv[..., MLA_NOPE:], LANE).reshape(KV_LORA, -1).astype(BF16)
    qn_p = _pad_cols(q_norm_w.reshape(1, -1), LANE)
    kn_p = _pad_cols(k_norm_w.reshape(1, -1), LANE)

    conv_w8 = jnp.pad(ml_conv_w, ((0, 8 - ML_CONV), (0, 0)))
    bd = lambda w: jax.scipy.linalg.block_diag(*[w[h] for h in range(ML_HEADS)])
    wq_bd = (bd(ml_wq) * (ML_DH ** -0.5)).astype(BF16)
    wk_bd = bd(ml_wk).astype(BF16)
    gate_b_p = _pad_cols(ml_gate_b.reshape(2, 1, 2 * ML_HEADS), LANE)

    wa_p = jnp.pad(gla_wa, ((0, 0), (SM_GA, LANE - SM_GA - GLA_LR), (0, 0)))
    ba_p = gla_ba.reshape(2, 1, -1)

    wr_p = _pad_cols(jnp.concatenate([w_grp, w_erouter], axis=-1), LANE)
    br_p = _pad_cols(jnp.concatenate([b_grp, b_erouter]).reshape(1, -1), LANE)
    return dict(w_in_p=w_in_p, wuq_p=wuq_p, wuk_p=wuk_p, wuv_p=wuv_p, qn_p=qn_p, kn_p=kn_p,
                conv_w8=conv_w8, wq_bd=wq_bd, wk_bd=wk_bd, gate_b_p=gate_b_p, wa_p=wa_p, ba_p=ba_p,
                w_out_b=w_out.astype(BF16), wr_p=wr_p, br_p=br_p)


def kernel(x, c, ctx, c_ctx, w_mod, b_mod, norm1_w, w_in, q_a_norm, w_uq, kv_a_norm, w_ukv,
           q_norm_w, k_norm_w, ml_conv_w, ml_conv_b, ml_wq, ml_wk, ml_gate_b, ml_norm_w, ml_skip,
           gla_wa, gla_ba, gla_norm_w, w_out, norm2_w, w_grp, b_grp, w_erouter, b_erouter,
           w_gate, w_up, w_down):
    b, s, d = x.shape
    n_ctx = ctx.shape[1]
    depth = w_mod.shape[0]
    assert n_ctx == TM and s % TM == 0 and b == 2

    cc = jnp.concatenate([c, c_ctx[None, :], jnp.zeros((8 - b - 1, d), F32)], axis=0)
    mods_all = _mods(cc, w_mod, b_mod).reshape(depth, 8, 6, d)
    cos_t, sa_t, sb_t = _rope_tables(n_ctx, s)
    xs = jnp.concatenate([ctx, x], axis=1)

    for l in range(depth):
        w = _layer_weights(w_in[l], w_uq[l], w_ukv[l], q_norm_w[l], k_norm_w[l], ml_conv_w[l],
                           ml_wq[l], ml_wk[l], ml_gate_b[l], gla_wa[l], gla_ba[l], w_out[l],
                           w_grp[l], b_grp[l], w_erouter[l], b_erouter[l])
        mods = mods_all[l]
        p = _inproj(xs, mods, norm1_w[l], w["w_in_p"])
        q, k, v = _mla_prep(p, q_a_norm[l], w["wuq_p"], kv_a_norm[l], w["wuk_p"], w["wuv_p"],
                            w["qn_p"], w["kn_p"], cos_t, sa_t, sb_t)
        a_lat = _attention_lat(q, k, v)
        a_ctx = _attention_ctx(q, k, v) if l < depth - 1 else jnp.zeros((b, TM, MLA_HEADS * MLA_V), BF16)
        xconv, mq, mk = _ml_prep(p, w["conv_w8"], ml_conv_b[l], w["wq_bd"], w["wk_bd"])
        mh = _ml_scan(mq, mk, p, w["gate_b_p"])
        go = _gla_scan(p, w["wa_p"], w["ba_p"])
        xs, h2, route = _outproj(a_ctx, a_lat, mh, xconv, p, go, xs, mods, ml_norm_w[l], ml_skip[l], gla_norm_w[l],
                                w["w_out_b"], norm2_w[l], w["wr_p"], w["br_p"])
        xs = _moe(h2, route, xs, mods, w_gate, w_up, w_down, l)
    return xs[:, n_ctx:, :]
```

```python
import functools

import jax
import jax.numpy as jnp
import numpy as np
from jax import lax
from jax.experimental import pallas as pl
from jax.experimental.pallas import tpu as pltpu

F32 = jnp.float32
BF16 = jnp.bfloat16
HIGHEST = lax.Precision.HIGHEST

EPS = 1e-6
GRID_W = 64
ROPE_THETA = 10000.0

MLA_HEADS = 8
MLA_NOPE = 64
MLA_ROPE = 32
MLA_QK = MLA_NOPE + MLA_ROPE
MLA_V = 64
Q_LORA = 256
KV_LORA = 128

ML_HEADS = 4
ML_DH = 64
ML_W = ML_HEADS * ML_DH
ML_CONV = 5

GLA_HEADS = 4
GLA_DK = 32
GLA_DV = 64
GLA_LR = 16
GLA_TAU = 16.0

N_GROUPS = 4
EXP_PER_GROUP = 8
N_EXPERTS = N_GROUPS * EXP_PER_GROUP
D_EXPERT = 256

LANE = 128
TM = 256
ML_CHUNK = 256
GLA_CHUNK = 128
GLA_SAFE_SPAN = 80.0
VMEM_LIMIT = 56 * 1024 * 1024

C_CQ, C_MX, C_MV, C_MO, C_GV, C_GR = 0, 256, 512, 768, 1024, 1280
C_CKV, C_GQ, C_GK, C_SMF, C_SMB = 1536, 1664, 1792, 1920, 2048
D_INP = 2176
SM_GATE = 0
SM_GA = 8
SM_KR = 64
R_GRP = 0
R_EXP = 4
RT_E1 = 0
RT_W1 = 2
GATHER_UNROLL = 8


def _cparams(sem):
    return pltpu.CompilerParams(dimension_semantics=sem, vmem_limit_bytes=VMEM_LIMIT)


def _silu(x):
    return x * jax.nn.sigmoid(x)


def _log_sigmoid(x):
    return -(jnp.maximum(-x, 0.0) + jnp.log1p(jnp.exp(-jnp.abs(x))))


def _nt_dot(a, b, **kw):
    return lax.dot_general(a, b, (((1,), (1,)), ((), ())), preferred_element_type=F32, **kw)


def _split3(x):
    hi = x.astype(BF16)
    r = x - hi.astype(F32)
    mid = r.astype(BF16)
    return hi, mid, (r - mid.astype(F32)).astype(BF16)


def _dot_sel(sel, x, nt=False, right=False):
    s = sel.astype(BF16)
    dot = _nt_dot if nt else functools.partial(jnp.dot, preferred_element_type=F32)
    return sum((dot(piece, s) if right else dot(s, piece)) for piece in _split3(x))


def _tn_dot(a, b, **kw):
    return lax.dot_general(a, b, (((0,), (0,)), ((), ())), preferred_element_type=F32, **kw)


def _mods_kernel(cc_ref, w_ref, b_ref, o_ref):
    a = _silu(cc_ref[...])
    o_ref[0] = jnp.dot(a, w_ref[0], precision=HIGHEST, preferred_element_type=F32) + b_ref[0]


def _mods(cc, w_mod, b_mod):
    depth, d, d6 = w_mod.shape
    nb = 1536
    return pl.pallas_call(
        _mods_kernel,
        grid=(depth, d6 // nb),
        in_specs=[
            pl.BlockSpec((8, d), lambda l, j: (0, 0)),
            pl.BlockSpec((1, d, nb), lambda l, j: (l, 0, j)),
            pl.BlockSpec((1, 1, nb), lambda l, j: (l, 0, j)),
        ],
        out_specs=pl.BlockSpec((1, 8, nb), lambda l, j: (l, 0, j)),
        out_shape=jax.ShapeDtypeStruct((depth, 8, d6), F32),
        compiler_params=_cparams(("arbitrary", "arbitrary")),
        name="adaln_mods",
    )(cc, w_mod, b_mod.reshape(depth, 1, d6))


def _mod_row(b, i):
    return jnp.where(i == 0, 2, b)


def _inproj_kernel(x_ref, mod_ref, nw_ref, w_ref, o_ref):
    x = x_ref[0]
    y = x * lax.rsqrt(jnp.mean(x * x, axis=-1, keepdims=True) + EPS) * nw_ref[...]
    mod = mod_ref[0]
    h = y * (1.0 + mod[1:2]) + mod[0:1]
    o_ref[0] = jnp.dot(h.astype(BF16), w_ref[...], preferred_element_type=F32)


def _inproj(xs, mods, norm_w, w_in_p):
    b, t, d = xs.shape
    nt = t // TM
    return pl.pallas_call(
        _inproj_kernel,
        grid=(b, nt),
        in_specs=[
            pl.BlockSpec((1, TM, d), lambda bi, i: (bi, i, 0)),
            pl.BlockSpec((1, 6, d), lambda bi, i: (_mod_row(bi, i), 0, 0)),
            pl.BlockSpec((1, d), lambda bi, i: (0, 0)),
            pl.BlockSpec((d, D_INP), lambda bi, i: (0, 0)),
        ],
        out_specs=pl.BlockSpec((1, TM, D_INP), lambda bi, i: (bi, i, 0)),
        out_shape=jax.ShapeDtypeStruct((b, t, D_INP), F32),
        compiler_params=_cparams(("parallel", "parallel")),
        name="in_proj",
    )(xs, mods, norm_w.reshape(1, d), w_in_p)


def _mla_prep_kernel(cq_ref, ckv_ref, sm_ref, qan_ref, wuq_ref, kvan_ref, wuk_ref, wuv_ref,
                     qn_ref, kn_ref, cos_ref, sa_ref, sb_ref, q_ref, k_ref, v_ref):
    cq = cq_ref[0]
    cqn = cq * lax.rsqrt(jnp.mean(cq * cq, axis=-1, keepdims=True) + EPS) * qan_ref[...]
    qall = jnp.dot(cqn.astype(BF16), wuq_ref[...], preferred_element_type=F32)
    ckv = ckv_ref[0]
    ckvn = (ckv * lax.rsqrt(jnp.mean(ckv * ckv, axis=-1, keepdims=True) + EPS)
            * kvan_ref[...]).astype(BF16)
    kall = jnp.dot(ckvn, wuk_ref[...], preferred_element_type=F32)
    vall = jnp.dot(ckvn, wuv_ref[...], preferred_element_type=F32)
    lane = lax.broadcasted_iota(jnp.int32, (1, LANE), 1)
    kr = jnp.where((lane >= SM_KR) & (lane < SM_KR + MLA_ROPE), sm_ref[0], 0.0)
    cos, sin = cos_ref[...], sa_ref[...] - sb_ref[...]
    r_i = lax.broadcasted_iota(jnp.int32, (LANE, LANE), 0)
    c_i = lax.broadcasted_iota(jnp.int32, (LANE, LANE), 1)
    half = MLA_ROPE // 2
    first = (c_i >= MLA_NOPE) & (c_i < MLA_NOPE + half)
    second = (c_i >= MLA_NOPE + half) & (c_i < MLA_QK)
    rot = jnp.where(first & (r_i == c_i + half), -1.0,
                    jnp.where(second & (r_i == c_i - half), 1.0, 0.0)).astype(BF16)

    def rope(x):
        return x * cos + jnp.dot(x.astype(BF16), rot, preferred_element_type=F32) * sin

    def head_norm(x, w):
        return x * lax.rsqrt(jnp.sum(x * x, axis=-1, keepdims=True) * (1.0 / MLA_QK) + EPS) * w

    for h in range(MLA_HEADS):
        sl = slice(h * LANE, (h + 1) * LANE)
        qh = rope(head_norm(qall[:, sl], qn_ref[...]))
        q_ref[0, h] = (qh * Q_SCALE).astype(BF16)
        kh = rope(head_norm(kall[:, sl] + kr, kn_ref[...]))
        k_ref[0, h] = jnp.where(lane == ATT_SHIFT_LANE, 1.0, kh).astype(BF16)
        v_ref[0, h] = jnp.where(lane == MLA_V, 1.0, vall[:, sl]).astype(BF16)


def _mla_prep(p, q_a_norm, wuq_p, kv_a_norm, wuk_p, wuv_p, qn_p, kn_p, cos_t, sa_t, sb_t):
    b, t, _ = p.shape
    nt = t // TM
    hw = MLA_HEADS * LANE
    full = lambda shape: pl.BlockSpec(shape, lambda bi, i: (0,) * len(shape))
    tab = pl.BlockSpec((TM, LANE), lambda bi, i: (i, 0))
    out = pl.BlockSpec((1, MLA_HEADS, TM, LANE), lambda bi, i: (bi, 0, i, 0))
    shp = jax.ShapeDtypeStruct((b, MLA_HEADS, t, LANE), BF16)
    return pl.pallas_call(
        _mla_prep_kernel,
        grid=(b, nt),
        in_specs=[
            pl.BlockSpec((1, TM, Q_LORA), lambda bi, i: (bi, i, C_CQ // Q_LORA)),
            pl.BlockSpec((1, TM, KV_LORA), lambda bi, i: (bi, i, C_CKV // KV_LORA)),
            pl.BlockSpec((1, TM, LANE), lambda bi, i: (bi, i, C_SMF // LANE)),
            full((1, Q_LORA)), full((Q_LORA, hw)), full((1, KV_LORA)),
            full((KV_LORA, hw)), full((KV_LORA, hw)), full((1, LANE)), full((1, LANE)),
            tab, tab, tab,
        ],
        out_specs=[out, out, out],
        out_shape=[shp, shp, shp],
        compiler_params=_cparams(("parallel", "parallel")),
        name="mla_prep",
    )(p, p, p, q_a_norm.reshape(1, -1), wuq_p, kv_a_norm.reshape(1, -1), wuk_p, wuv_p,
      qn_p, kn_p, cos_t, sa_t, sb_t)


Q_SCALE = float(MLA_QK ** -0.5 * np.log2(np.e))
ATT_HP = 2
ATT_NQ = 4
ATT_TK = 2048
ATT_SHIFT_LANE = MLA_QK
ATT_SAFE_MAX = 2.0 ** 100


def _softmax_step(q, kb, vb, m, acc):
    s = _nt_dot(q, kb)
    m_new = jnp.max(s, axis=-1, keepdims=True)
    if m is None:
        return m_new, jnp.dot(jnp.exp2((s - m_new).astype(BF16)), vb, preferred_element_type=F32)
    m_new = jnp.maximum(m, m_new)
    p = jnp.exp2((s - m_new).astype(BF16))
    return m_new, acc * jnp.exp2(m - m_new) + jnp.dot(p, vb, preferred_element_type=F32)


def _attn_ctx_kernel(q_ref, k_ref, v_ref, o_ref):
    outs = []
    for h in range(MLA_HEADS):
        _, acc = _softmax_step(q_ref[0, h], k_ref[0, h], v_ref[0, h], None, None)
        outs.append(acc[:, :MLA_V] / acc[:, MLA_V:MLA_V + 1])
    o_ref[0] = jnp.concatenate(outs, axis=-1).astype(BF16)


def _attention_ctx(q, k, v):
    b, h, _, _ = q.shape
    blk = pl.BlockSpec((1, h, TM, LANE), lambda bi: (bi, 0, 0, 0))
    return pl.pallas_call(
        _attn_ctx_kernel,
        grid=(b,),
        in_specs=[blk, blk, blk],
        out_specs=pl.BlockSpec((1, TM, h * MLA_V), lambda bi: (bi, 0, 0)),
        out_shape=jax.ShapeDtypeStruct((b, TM, h * MLA_V), BF16),
        compiler_params=_cparams(("parallel",)),
        name="mla_attention_ctx",
    )(q, k, v)


def _attn_lat_kernel(*refs, n_blk):
    q_refs, (k_ref, v_ref, o_ref, q_buf, qs_buf) = refs[:ATT_NQ], refs[ATT_NQ:]
    lane = lax.broadcasted_iota(jnp.int32, (1, LANE), 1)

    def kv_block(hh, j):
        off = pl.multiple_of(TM + j * ATT_TK, TM)
        return k_ref[0, hh, pl.ds(off, ATT_TK), :], v_ref[0, hh, pl.ds(off, ATT_TK), :]

    def finish(accs):
        outs = [acc[:, :MLA_V] / acc[:, MLA_V:MLA_V + 1] for acc in accs]
        o_ref[0] = jnp.concatenate(outs, axis=-1).astype(BF16)

    accs = []
    for hh in range(ATT_HP):
        q = jnp.concatenate([qr[0, hh] for qr in q_refs], axis=0)
        q_buf[hh] = q
        kb, vb = k_ref[0, hh, 0:TM, :], v_ref[0, hh, 0:TM, :]
        shift = jnp.max(_nt_dot(q, kb), axis=-1, keepdims=True).astype(BF16)
        qs_buf[hh] = jnp.where(lane == ATT_SHIFT_LANE, -shift, q)
        accs.append(jnp.dot(jnp.exp2(_nt_dot(qs_buf[hh], kb).astype(BF16)), vb, preferred_element_type=F32))

    def fast_body(j, accs):
        new = []
        for hh in range(ATT_HP):
            kb, vb = kv_block(hh, j)
            p = jnp.exp2(_nt_dot(qs_buf[hh], kb).astype(BF16))
            new.append(accs[hh] + jnp.dot(p, vb, preferred_element_type=F32))
        return tuple(new)

    accs = lax.fori_loop(0, n_blk, fast_body, tuple(accs))
    bad = sum(jnp.max(jnp.where(jnp.abs(acc) < ATT_SAFE_MAX, 0.0, 1.0)) for acc in accs)

    @pl.when(bad == 0.0)
    def _():
        finish(accs)

    @pl.when(bad != 0.0)
    def _():
        init = []
        for hh in range(ATT_HP):
            init += _softmax_step(q_buf[hh], k_ref[0, hh, 0:TM, :], v_ref[0, hh, 0:TM, :], None, None)

        def body(j, carry):
            new = []
            for hh in range(ATT_HP):
                new += _softmax_step(q_buf[hh], *kv_block(hh, j), carry[2 * hh], carry[2 * hh + 1])
            return tuple(new)

        carry = lax.fori_loop(0, n_blk, body, tuple(init))
        finish(carry[1::2])


def _attention_lat(q, k, v):
    b, h, t, _ = q.shape
    s = t - TM
    tq = ATT_NQ * TM
    assert s % ATT_TK == 0 and s % tq == 0
    kv = pl.BlockSpec((1, ATT_HP, t, LANE), lambda bi, hp, i: (bi, hp, 0, 0))
    qs = [pl.BlockSpec((1, ATT_HP, TM, LANE), lambda bi, hp, i, u=u: (bi, hp, 1 + ATT_NQ * i + u, 0))
          for u in range(ATT_NQ)]
    return pl.pallas_call(
        functools.partial(_attn_lat_kernel, n_blk=s // ATT_TK),
        grid=(b, h // ATT_HP, s // tq),
        in_specs=qs + [kv, kv],
        out_specs=pl.BlockSpec((1, tq, ATT_HP * MLA_V), lambda bi, hp, i: (bi, i, hp)),
        out_shape=jax.ShapeDtypeStruct((b, s, h * MLA_V), BF16),
        scratch_shapes=[pltpu.VMEM((ATT_HP, tq, LANE), BF16), pltpu.VMEM((ATT_HP, tq, LANE), BF16)],
        compiler_params=_cparams(("parallel", "parallel", "arbitrary")),
        name="mla_attention",
    )(*([q] * ATT_NQ), k, v)


def _ml_prep_kernel(x_ref, prev_ref, next_ref, cw_ref, cb_ref, wq_ref, wk_ref,
                    xc_ref, q_ref, k_ref, *, n_tiles):
    i = pl.program_id(1)
    x = x_ref[0]
    prev = jnp.where(i <= 1, 0.0, prev_ref[0])
    nxt = jnp.where((i == 0) | (i == n_tiles - 1), 0.0, next_ref[0])
    ext = jnp.concatenate([prev, x, nxt], axis=0)
    n_ext = TM + 16
    cw = cw_ref[...]
    acc = jnp.zeros((TM, ML_W), F32) + cb_ref[...]
    for kk in range(ML_CONV):
        sh = (ML_CONV // 2 - kk) % n_ext
        shifted = ext if sh == 0 else pltpu.roll(ext, sh, 0)
        acc = acc + cw[kk:kk + 1] * shifted[8:8 + TM]
    xc = _silu(acc)
    xc_ref[0] = xc
    xb = xc.astype(BF16)
    q_ref[0] = jnp.dot(xb, wq_ref[...], preferred_element_type=F32).astype(BF16)
    k_ref[0] = jnp.dot(xb, wk_ref[...], preferred_element_type=F32).astype(BF16)


def _ml_prep(p, conv_w8, conv_b, wq_bd, wk_bd):
    b, t, _ = p.shape
    nt = t // TM
    r8 = TM // 8
    full = lambda shape: pl.BlockSpec(shape, lambda bi, i: (0,) * len(shape))
    cb = C_MX // ML_W
    blk = pl.BlockSpec((1, TM, ML_W), lambda bi, i: (bi, i, 0))
    return pl.pallas_call(
        functools.partial(_ml_prep_kernel, n_tiles=nt),
        grid=(b, nt),
        in_specs=[
            pl.BlockSpec((1, TM, ML_W), lambda bi, i: (bi, i, cb)),
            pl.BlockSpec((1, 8, ML_W), lambda bi, i: (bi, jnp.maximum(i * r8 - 1, 0), cb)),
            pl.BlockSpec((1, 8, ML_W), lambda bi, i: (bi, jnp.minimum((i + 1) * r8, nt * r8 - 1), cb)),
            full((8, ML_W)), full((1, ML_W)), full((ML_W, ML_W)), full((ML_W, ML_W)),
        ],
        out_specs=[blk, blk, blk],
        out_shape=[jax.ShapeDtypeStruct((b, t, ML_W), F32),
                   jax.ShapeDtypeStruct((b, t, ML_W), BF16),
                   jax.ShapeDtypeStruct((b, t, ML_W), BF16)],
        compiler_params=_cparams(("parallel", "parallel")),
        name="mlstm_prep",
    )(p, p, p, conv_w8, conv_b.reshape(1, ML_W), wq_bd, wk_bd)


def _scan_chunk(d, step, n_ctx_chunks, n_chunks):
    bwd = jnp.where(step < n_ctx_chunks, n_ctx_chunks - 1 - step, n_chunks - 1 - (step - n_ctx_chunks))
    return jnp.where(d == 0, step, bwd)


def _ml_chunk(d, q, k, v, g, s_ref, m_ref, base):
    L = ML_CHUNK
    row = lax.broadcasted_iota(jnp.int32, (L, L), 0)
    col = lax.broadcasted_iota(jnp.int32, (L, L), 1)
    mask = col <= row if d == 0 else col >= row
    tri = mask.astype(F32)
    lane = lax.broadcasted_iota(jnp.int32, (1, LANE), 1)
    eye8 = (lax.broadcasted_iota(jnp.int32, (8, LANE), 0)
            == lax.broadcasted_iota(jnp.int32, (8, LANE), 1)).astype(F32)

    lf = _log_sigmoid(g)
    bc = jnp.dot(tri, lf, precision=HIGHEST, preferred_element_type=F32)
    g_rows = _nt_dot(eye8, g, precision=HIGHEST)
    bc = pltpu.roll(bc, LANE - ML_HEADS, 1)
    bc_rows = _nt_dot(eye8, bc, precision=HIGHEST)

    outs = []
    for pair in range(ML_HEADS // 2):
        sl = slice(pair * LANE, (pair + 1) * LANE)
        q_blk, k_blk, v_blk = q[:, sl], k[:, sl], v[:, sl]
        pair_out = []
        for sub in range(2):
            h = pair * 2 + sub
            head_lanes = (lane >= sub * ML_DH) & (lane < (sub + 1) * ML_DH)
            qh = jnp.where(head_lanes, q_blk, jnp.zeros_like(q_blk))
            vs = v_blk if sub == 0 else pltpu.roll(v_blk, ML_DH, 1)
            v_ext = jnp.where(lane < ML_DH, vs, jnp.where(lane == ML_DH, 1.0, 0.0)).astype(BF16)

            li_c = g[:, SM_GATE + h:SM_GATE + h + 1]
            bc_c = bc[:, SM_GATE + h:SM_GATE + h + 1]
            li_r = g_rows[h:h + 1, :]
            bc_r = bc_rows[h:h + 1, :]
            m_st = m_ref[base + h][0:1, 0:1]

            c_row = jnp.where(mask, li_r - bc_r, -jnp.inf)
            m_rel = jnp.maximum(m_st, jnp.max(c_row, axis=-1, keepdims=True))
            m_t = bc_c + m_rel
            e = jnp.exp(c_row - m_rel)
            s = (_nt_dot(qh, k_blk) * e).astype(BF16)
            tot = (jnp.dot(s, v_ext, preferred_element_type=F32)
                   + jnp.exp(m_st - m_rel) * jnp.dot(qh, s_ref[base + h].astype(BF16),
                                                     preferred_element_type=F32))
            den = tot[:, ML_DH:ML_DH + 1]
            pair_out.append(tot / jnp.maximum(jnp.abs(den), jnp.exp(-m_t)))

            b_end = bc_c[L - 1:L] if d == 0 else bc_c[0:1]
            g_col = b_end - bc_c + li_c
            m_new = jnp.maximum(b_end + m_st, jnp.max(g_col, axis=0, keepdims=True))
            kw = jnp.where(head_lanes, k_blk.astype(F32) * jnp.exp(g_col - m_new), 0.0).astype(BF16)
            s_ref[base + h] = jnp.exp(b_end + m_st - m_new) * s_ref[base + h] + _tn_dot(kw, v_ext)
            m_ref[base + h] = jnp.broadcast_to(m_new, m_ref.shape[1:])
        outs.append(jnp.where(lane < ML_DH, pair_out[0], pltpu.roll(pair_out[1], ML_DH, 1)))
    return jnp.concatenate(outs, axis=-1)


def _ml_scan_kernel(qf_ref, kf_ref, vf_ref, smf_ref, qb_ref, kb_ref, vb_ref, smb_ref, gb_ref,
                    hf_ref, hb_ref, s_ref, m_ref):
    @pl.when(pl.program_id(0) == 0)
    def _():
        s_ref[...] = jnp.zeros_like(s_ref)
        m_ref[...] = jnp.zeros_like(m_ref)

    nb = qf_ref.shape[0]
    streams = ((0, qf_ref, kf_ref, vf_ref, smf_ref, hf_ref), (1, qb_ref, kb_ref, vb_ref, smb_ref, hb_ref))
    for d, q_ref, k_ref, v_ref, sm_ref, h_ref in streams:
        for bi in range(nb):
            h_ref[bi] = _ml_chunk(d, q_ref[bi], k_ref[bi], v_ref[bi], sm_ref[bi] + gb_ref[d],
                                  s_ref, m_ref, (d * nb + bi) * ML_HEADS)


def _ml_scan(q, k, p, gate_b_p):
    b, t, _ = q.shape
    nc = t // ML_CHUNK
    chunk = functools.partial(_scan_chunk, n_ctx_chunks=TM // ML_CHUNK, n_chunks=nc)
    blk = lambda d, w, cb: pl.BlockSpec((b, ML_CHUNK, w), lambda s: (0, chunk(d, s), cb))
    stream = lambda d: [blk(d, ML_W, 0), blk(d, ML_W, 0), blk(d, ML_W, C_MV // ML_W),
                        blk(d, LANE, C_SMF // LANE + d)]
    shp = jax.ShapeDtypeStruct((b, t, ML_W), F32)
    return pl.pallas_call(
        _ml_scan_kernel,
        grid=(nc,),
        in_specs=stream(0) + stream(1) + [pl.BlockSpec((2, 1, LANE), lambda s: (0, 0, 0))],
        out_specs=[blk(0, ML_W, 0), blk(1, ML_W, 0)],
        out_shape=[shp, shp],
        scratch_shapes=[pltpu.VMEM((2 * b * ML_HEADS, LANE, LANE), F32),
                        pltpu.VMEM((2 * b * ML_HEADS, 8, LANE), F32)],
        compiler_params=_cparams(("arbitrary",)),
        name="mlstm_scan",
    )(q, k, p, p, q, k, p, p, gate_b_p)


def _gla_chunk(d, q_ref, k_ref, v_ref, sm_ref, o_ref, bi, r0, wa, ba, s_ref, si, la_ref):
    L = GLA_CHUNK
    rows = slice(r0, r0 + L)
    row = lax.broadcasted_iota(jnp.int32, (L, L), 0)
    col = lax.broadcasted_iota(jnp.int32, (L, L), 1)
    mask = col <= row if d == 0 else col >= row
    tri = mask.astype(F32)
    lane_k = lax.broadcasted_iota(jnp.int32, (1, GLA_HEADS * GLA_DK), 1)
    lane_v = lax.broadcasted_iota(jnp.int32, (1, GLA_HEADS * GLA_DV), 1)
    eye = (lax.broadcasted_iota(jnp.int32, (LANE, LANE), 0)
           == lax.broadcasted_iota(jnp.int32, (LANE, LANE), 1)).astype(F32)
    blockdiag = (lax.broadcasted_iota(jnp.int32, s_ref.shape[1:], 0) // GLA_DK
                 == lax.broadcasted_iota(jnp.int32, s_ref.shape[1:], 1) // GLA_DV)
    scale = GLA_DK ** -0.5

    pre = jnp.dot(sm_ref[bi, rows].astype(BF16), wa.astype(BF16), preferred_element_type=F32) + ba
    loga = _log_sigmoid(pre) * (1.0 / GLA_TAU)
    bc = _dot_sel(tri, loga)
    ref_row = bc[L // 2 - 1:L // 2]
    in_range = jnp.max(jnp.abs(bc - ref_row)) < GLA_SAFE_SPAN

    @pl.when(in_range)
    def _():
        b_end = bc[L - 1:L] if d == 0 else bc[0:1]
        q = q_ref[bi, rows] * scale
        k = k_ref[bi, rows]
        v = v_ref[bi, rows].astype(BF16)
        q_in = (q * jnp.exp(bc - ref_row))
        k_in = (k * jnp.exp(ref_row - bc)).astype(BF16)
        q_st = (q * jnp.exp(bc)).astype(BF16)
        k_st = (k * jnp.exp(b_end - bc)).astype(BF16)
        s_old = s_ref[si]
        o = jnp.dot(q_st, jnp.where(blockdiag, s_old, 0.0).astype(BF16), preferred_element_type=F32)
        for h in range(GLA_HEADS):
            qh = jnp.where(lane_k // GLA_DK == h, q_in, 0.0).astype(BF16)
            att = jnp.where(mask, _nt_dot(qh, k_in), 0.0).astype(BF16)
            oh = jnp.dot(att, v, preferred_element_type=F32)
            o = o + jnp.where(lane_v // GLA_DV == h, oh, 0.0)
        o_ref[bi, rows] = o
        decay_col = jnp.exp(_dot_sel(eye, jnp.broadcast_to(b_end, (8, LANE)), nt=True)[:, 0:1])
        s_ref[si] = decay_col * s_old + _tn_dot(k_st, v)

    @pl.when(jnp.logical_not(in_range))
    def _():
        la_ref[...] = loga
        sub = lax.broadcasted_iota(jnp.int32, (8, LANE), 0)

        def body(i, carry):
            t = i if d == 0 else L - 1 - i
            la_t = la_ref[pl.ds(t, 1), :]
            k_t = k_ref[bi, pl.ds(r0 + t, 1), :]
            q_t = q_ref[bi, pl.ds(r0 + t, 1), :] * scale
            v_t = v_ref[bi, pl.ds(r0 + t, 1), :]
            stacked = jnp.where(sub == 0, la_t, jnp.where(sub == 1, k_t, 0.0))
            cols = _dot_sel(eye, stacked, nt=True)
            s_new = jnp.exp(cols[:, 0:1]) * s_ref[si] + cols[:, 1:2] * v_t
            s_ref[si] = s_new
            o_t = jnp.dot(jnp.broadcast_to(q_t, (8, LANE)), jnp.where(blockdiag, s_new, 0.0),
                          precision=HIGHEST, preferred_element_type=F32)
            o_ref[bi, pl.ds(r0 + t, 1), :] = o_t[0:1]
            return carry

        lax.fori_loop(0, L, body, 0)


def _gla_scan_kernel(qf_ref, kf_ref, vf_ref, smf_ref, qb_ref, kb_ref, vb_ref, smb_ref, wa_ref, ba_ref,
                     of_ref, ob_ref, s_ref, la_ref):
    @pl.when(pl.program_id(0) == 0)
    def _():
        s_ref[...] = jnp.zeros_like(s_ref)

    nb = qf_ref.shape[0]
    n_sub = TM // GLA_CHUNK
    streams = ((0, qf_ref, kf_ref, vf_ref, smf_ref, of_ref), (1, qb_ref, kb_ref, vb_ref, smb_ref, ob_ref))
    for d, q_ref, k_ref, v_ref, sm_ref, o_ref in streams:
        for bi in range(nb):
            for c in (range(n_sub) if d == 0 else reversed(range(n_sub))):
                _gla_chunk(d, q_ref, k_ref, v_ref, sm_ref, o_ref, bi, c * GLA_CHUNK, wa_ref[d], ba_ref[d],
                           s_ref, d * nb + bi, la_ref)


def _gla_scan(p, wa_p, ba_p):
    b, t, _ = p.shape
    nc = t // TM
    chunk = functools.partial(_scan_chunk, n_ctx_chunks=1, n_chunks=nc)
    kw, vw = GLA_HEADS * GLA_DK, GLA_HEADS * GLA_DV
    blk = lambda d, w, col: pl.BlockSpec((b, TM, w), lambda s: (0, chunk(d, s), col // w))
    stream = lambda d: [blk(d, kw, C_GQ), blk(d, kw, C_GK), blk(d, vw, C_GV), blk(d, LANE, C_SMF + d * LANE)]
    shp = jax.ShapeDtypeStruct((b, t, vw), F32)
    return pl.pallas_call(
        _gla_scan_kernel,
        grid=(nc,),
        in_specs=stream(0) + stream(1) + [pl.BlockSpec((2, LANE, kw), lambda s: (0, 0, 0)),
                                          pl.BlockSpec((2, 1, kw), lambda s: (0, 0, 0))],
        out_specs=[blk(0, vw, 0), blk(1, vw, 0)],
        out_shape=[shp, shp],
        scratch_shapes=[pltpu.VMEM((2 * b, kw, vw), F32), pltpu.VMEM((GLA_CHUNK, kw), F32)],
        compiler_params=_cparams(("arbitrary",)),
        name="gla_scan",
    )(p, p, p, p, p, p, p, p, wa_p, ba_p)


OUTPROJ_TILES = 2
OUTPROJ_N_TOK = 11


def _outproj_kernel(*refs, n_tiles):
    n_in = OUTPROJ_TILES * OUTPROJ_N_TOK
    shared = refs[n_in:n_in + 7]
    outs = refs[n_in + 7:]
    for u in range(OUTPROJ_TILES):
        tile = jnp.minimum(OUTPROJ_TILES * pl.program_id(1) + u, n_tiles - 1)
        _outproj_tile(tile == 0, slice(u * TM, (u + 1) * TM),
                      *refs[u * OUTPROJ_N_TOK:(u + 1) * OUTPROJ_N_TOK], *shared, *outs)


def _outproj_tile(is_ctx, rows, ac_ref, al_ref, mhf_ref, mhb_ref, xc_ref, mo_ref, gof_ref, gob_ref, gr_ref, x_ref,
                  mod_ref, mnw_ref, msk_ref, gnw_ref, wout_ref, n2w_ref, wr_ref, br_ref,
                  xo_ref, h2_ref, route_ref):
    grp64 = (lax.broadcasted_iota(jnp.int32, (ML_W, ML_W), 0) // ML_DH
             == lax.broadcasted_iota(jnp.int32, (ML_W, ML_W), 1) // ML_DH).astype(F32) * (1.0 / ML_DH)

    def head_norm(x, w):
        ms = _dot_sel(grp64, x * x, right=True)
        return x * lax.rsqrt(ms + EPS) * w

    m_l = jax.nn.sigmoid(mo_ref[0]) * (head_norm(mhf_ref[0] + mhb_ref[0], mnw_ref[...])
                                       + msk_ref[...] * xc_ref[0])
    g_l = head_norm(gof_ref[0] + gob_ref[0], gnw_ref[...]) * _silu(gr_ref[0])
    na = MLA_HEADS * MLA_V
    a = jnp.where(is_ctx, ac_ref[0], al_ref[0])
    res = (jnp.dot(a, wout_ref[0:na], preferred_element_type=F32)
           + jnp.dot(m_l.astype(BF16), wout_ref[na:na + ML_W], preferred_element_type=F32)
           + jnp.dot(g_l.astype(BF16), wout_ref[na + ML_W:], preferred_element_type=F32))
    mod = mod_ref[0]
    x = x_ref[0] + mod[2:3] * res
    xo_ref[0, rows] = x
    h2 = (x * lax.rsqrt(jnp.mean(x * x, axis=-1, keepdims=True) + EPS) * n2w_ref[...]
          * (1.0 + mod[4:5]) + mod[3:4])
    half = h2.shape[-1] // 2
    h2_ref[0, rows] = _pack_bf16_pair(h2[:, :half], h2[:, half:])

    h_hi, h_lo, _ = _split3(h2)
    w_hi, w_lo, _ = _split3(wr_ref[...])
    dot = functools.partial(jnp.dot, preferred_element_type=F32)
    logits = dot(h_hi, w_hi) + dot(h_hi, w_lo) + dot(h_lo, w_hi) + br_ref[...]
    lane = lax.broadcasted_iota(jnp.int32, (1, LANE), 1)
    lane_f = lane.astype(F32)
    neg = -jnp.inf
    gl = jnp.where(lane < R_EXP, logits, neg)
    gmax = jnp.max(gl, axis=-1, keepdims=True)
    g_w = 1.0 / jnp.sum(jnp.exp(gl - gmax), axis=-1, keepdims=True)
    g_i = jnp.min(jnp.where(gl == gmax, lane_f, float(LANE)), axis=-1, keepdims=True)
    grp_of_lane = ((lane - R_EXP) // EXP_PER_GROUP).astype(F32)
    in_grp = (lane >= R_EXP) & (lane < R_EXP + N_EXPERTS) & (grp_of_lane == g_i)
    el = jnp.where(in_grp, logits, neg)
    m1 = jnp.max(el, axis=-1, keepdims=True)
    i1 = jnp.min(jnp.where(el == m1, lane_f, float(LANE)), axis=-1, keepdims=True)
    el2 = jnp.where(lane_f == i1, neg, el)
    m2 = jnp.max(el2, axis=-1, keepdims=True)
    i2 = jnp.min(jnp.where(el2 == m2, lane_f, float(LANE)), axis=-1, keepdims=True)
    p2 = jnp.exp(m2 - m1)
    w1 = g_w / (1.0 + p2)
    route_ref[0, rows] = jnp.where(lane == RT_E1, i1 - R_EXP, jnp.where(lane == RT_E1 + 1, i2 - R_EXP,
                             jnp.where(lane == RT_W1, w1, jnp.where(lane == RT_W1 + 1, p2 * w1, 0.0))))


def _outproj(a_ctx, a_lat, mh, xconv, p, go, xs, mods, ml_norm_w, ml_skip, gla_norm_w, w_out_b, norm2_w, wr_p, br_p):
    b, t, d = xs.shape
    nt = t // TM
    full = lambda shape: pl.BlockSpec(shape, lambda bi, i: (0,) * len(shape))
    na = MLA_HEADS * MLA_V

    out = lambda w: pl.BlockSpec((1, OUTPROJ_TILES * TM, w), lambda bi, i: (bi, i, 0))

    def tile_specs(u):
        tile = lambda i: jnp.minimum(OUTPROJ_TILES * i + u, nt - 1)
        tok = lambda w, cb=0: pl.BlockSpec((1, TM, w), lambda bi, i: (bi, tile(i), cb))
        ins = [pl.BlockSpec((1, TM, na), lambda bi, i: (bi, 0, 0)),
               pl.BlockSpec((1, TM, na), lambda bi, i: (bi, jnp.maximum(tile(i) - 1, 0), 0)),
               tok(ML_W), tok(ML_W), tok(ML_W), tok(ML_W, C_MO // ML_W),
               tok(ML_W), tok(ML_W), tok(ML_W, C_GR // ML_W), tok(d),
               pl.BlockSpec((1, 6, d), lambda bi, i: (_mod_row(bi, tile(i)), 0, 0))]
        return ins

    specs = [tile_specs(u) for u in range(OUTPROJ_TILES)]
    tok_args = (a_ctx, a_lat, mh[0], mh[1], xconv, p, go[0], go[1], p, xs, mods)
    assert len(tok_args) == OUTPROJ_N_TOK
    shapes = [jax.ShapeDtypeStruct((b, t, d), F32), jax.ShapeDtypeStruct((b, t, d // 2), jnp.uint32),
              jax.ShapeDtypeStruct((b, t, LANE), F32)]
    outs = pl.pallas_call(
        functools.partial(_outproj_kernel, n_tiles=nt),
        grid=(b, pl.cdiv(nt, OUTPROJ_TILES)),
        in_specs=sum(specs, []) + [
            full((1, ML_W)), full((1, ML_W)), full((1, ML_W)), full((d, d)), full((1, d)),
            full((d, LANE)), full((1, LANE))],
        out_specs=[out(d), out(d // 2), out(LANE)],
        out_shape=shapes,
        compiler_params=_cparams(("parallel", "arbitrary")),
        name="out_proj_router",
    )(*(tok_args * OUTPROJ_TILES), ml_norm_w.reshape(1, -1), ml_skip.reshape(1, -1),
      gla_norm_w.reshape(1, -1), w_out_b, norm2_w.reshape(1, -1), wr_p, br_p)
    return outs


def _dispatch(route, n_tiles):
    n = route.shape[0]
    flat = route[:, RT_E1:RT_E1 + 2].astype(jnp.int32).reshape(-1)
    onehot = (flat[:, None] == jnp.arange(N_EXPERTS, dtype=jnp.int32)[None, :]).astype(jnp.int32)
    csum = jnp.cumsum(onehot, axis=0)
    rank = jnp.sum(csum * onehot, axis=1) - 1
    padded = (csum[-1] + TM - 1) // TM * TM
    ends = jnp.cumsum(padded)
    pos = (ends - padded)[flat] + rank
    tile_start = jnp.arange(n_tiles, dtype=jnp.int32) * TM
    tile_exp = jnp.minimum(jnp.sum((ends[None, :] <= tile_start[:, None]).astype(jnp.int32), axis=1),
                           N_EXPERTS - 1)
    tile_on = (tile_start < ends[-1]).astype(jnp.int32)
    pos = pos.reshape(n // TM, TM, 2)
    return (tile_exp, tile_on,
            pos[:, :, 0].reshape(n // TM, 1, TM), pos[:, :, 1].reshape(n // TM, 1, TM))


def _gather_rows(src_hbm, idx_ref, dst, sem):
    def body(j, carry):
        pltpu.make_async_copy(src_hbm.at[pl.ds(idx_ref[0, 0, j], 1)], dst.at[pl.ds(j, 1)], sem).start()
        return carry

    lax.fori_loop(0, TM, body, 0, unroll=GATHER_UNROLL)


def _wait_rows(src_hbm, dst, sem):
    pltpu.make_async_copy(src_hbm.at[pl.ds(0, TM)], dst, sem).wait()


def _pack_bf16_pair(lo, hi):
    lo_b = lax.bitcast_convert_type(lo.astype(BF16).astype(F32), jnp.uint32) >> 16
    hi_b = lax.bitcast_convert_type(hi.astype(BF16).astype(F32), jnp.uint32) & jnp.uint32(0xFFFF0000)
    return hi_b | lo_b


def _unpack_bf16_pair(w):
    return (lax.bitcast_convert_type(w << 16, F32),
            lax.bitcast_convert_type(w & jnp.uint32(0xFFFF0000), F32))


def _scatter_kernel(p1_ref, p2_ref, h2_ref, init_hbm, xs_hbm, sem):
    del init_hbm

    def body(j, carry):
        row = h2_ref.at[pl.ds(j, 1)]
        pltpu.make_async_copy(row, xs_hbm.at[pl.ds(p1_ref[0, 0, j], 1)], sem).start()
        pltpu.make_async_copy(row, xs_hbm.at[pl.ds(p2_ref[0, 0, j], 1)], sem).start()
        return carry

    lax.fori_loop(0, TM, body, 0, unroll=GATHER_UNROLL)
    for _ in range(2):
        pltpu.make_async_copy(h2_ref, xs_hbm.at[pl.ds(0, TM)], sem).wait()


def _scatter_rows(h2p, pos1, pos2, n_rows):
    n, dh = h2p.shape
    idx = pl.BlockSpec((1, 1, TM), lambda g: (g, 0, 0), memory_space=pltpu.SMEM)
    return pl.pallas_call(
        _scatter_kernel,
        grid=(n // TM,),
        in_specs=[idx, idx, pl.BlockSpec((TM, dh), lambda g: (g, 0)), pl.BlockSpec(memory_space=pl.ANY)],
        out_specs=pl.BlockSpec(memory_space=pl.ANY),
        out_shape=jax.ShapeDtypeStruct((n_rows, dh), jnp.uint32),
        scratch_shapes=[pltpu.SemaphoreType.DMA(())],
        input_output_aliases={3: 0},
        compiler_params=_cparams(("arbitrary",)),
        name="moe_scatter",
    )(pos1, pos2, h2p, jnp.zeros((n_rows, dh), jnp.uint32))


def _experts_kernel(texp_ref, ton_ref, x_ref, wg_ref, wu_ref, wd_ref, y_ref, wgu_b, wd_b):
    r = pl.program_id(0)
    half = wgu_b.shape[0] // 2

    @pl.when(ton_ref[r] == 1)
    def _():
        @pl.when(jnp.logical_or(r == 0, texp_ref[r] != texp_ref[jnp.maximum(r - 1, 0)]))
        def _():
            wgu_b[:, :D_EXPERT] = wg_ref[0, 0].astype(BF16)
            wgu_b[:, D_EXPERT:] = wu_ref[0, 0].astype(BF16)
            wd_b[...] = wd_ref[0, 0].astype(BF16)

        x_lo, x_hi = _unpack_bf16_pair(x_ref[...])
        gu = (jnp.dot(x_lo.astype(BF16), wgu_b[0:half], preferred_element_type=F32)
              + jnp.dot(x_hi.astype(BF16), wgu_b[half:], preferred_element_type=F32))
        act = (_silu(gu[:, :D_EXPERT]) * gu[:, D_EXPERT:]).astype(BF16)
        y = jnp.dot(act, wd_b[...], preferred_element_type=F32)
        y_ref[...] = _pack_bf16_pair(y[:, :half], y[:, half:])

    @pl.when(ton_ref[r] == 0)
    def _():
        y_ref[...] = jnp.zeros_like(y_ref)


def _experts(xs, tile_exp, tile_on, w_gate, w_up, w_down, layer):
    n_rows, dh = xs.shape
    d = 2 * dh
    wspec = lambda shape: pl.BlockSpec((1, 1) + shape, lambda r, te, to: (layer, te[r], 0, 0))
    rows = pl.BlockSpec((TM, dh), lambda r, te, to: (r, 0))
    return pl.pallas_call(
        _experts_kernel,
        grid_spec=pltpu.PrefetchScalarGridSpec(
            num_scalar_prefetch=2,
            grid=(n_rows // TM,),
            in_specs=[rows, wspec((d, D_EXPERT)), wspec((d, D_EXPERT)), wspec((D_EXPERT, d))],
            out_specs=rows,
            scratch_shapes=[pltpu.VMEM((d, 2 * D_EXPERT), BF16), pltpu.VMEM((D_EXPERT, d), BF16)],
        ),
        out_shape=jax.ShapeDtypeStruct((n_rows, dh), jnp.uint32),
        compiler_params=_cparams(("arbitrary",)),
        name="moe_experts",
    )(tile_exp, tile_on, xs, w_gate, w_up, w_down)


def _combine_kernel(p1_ref, p2_ref, p1n_ref, p2n_ref, route_ref, x_ref, mod_ref, y_hbm, o_ref, buf, sem):
    g = pl.program_id(0)
    slot = g % 2
    half = buf.shape[-1]

    def gather(pa, pb, s):
        _gather_rows(y_hbm, pa, buf.at[s, 0], sem.at[s])
        _gather_rows(y_hbm, pb, buf.at[s, 1], sem.at[s])

    @pl.when(g == 0)
    def _():
        gather(p1_ref, p2_ref, 0)

    @pl.when(g + 1 < pl.num_programs(0))
    def _():
        gather(p1n_ref, p2n_ref, 1 - slot)

    _wait_rows(y_hbm, buf.at[slot, 0], sem.at[slot])
    _wait_rows(y_hbm, buf.at[slot, 1], sem.at[slot])
    route = route_ref[...]
    w1, w2 = route[:, RT_W1:RT_W1 + 1], route[:, RT_W1 + 1:RT_W1 + 2]
    gate = mod_ref[0][5:6]
    for part, (y1, y2) in enumerate(zip(_unpack_bf16_pair(buf[slot, 0]), _unpack_bf16_pair(buf[slot, 1]))):
        cols = slice(part * half, (part + 1) * half)
        o_ref[:, cols] = x_ref[:, cols] + gate[:, cols] * (w1 * y1 + w2 * y2)


def _combine(y, pos1, pos2, route, xs2, mods, nt):
    n, d = xs2.shape
    n_tok_tiles = n // TM
    idx = lambda nxt: pl.BlockSpec((1, 1, TM), lambda g: (jnp.minimum(g + nxt, n_tok_tiles - 1), 0, 0),
                                   memory_space=pltpu.SMEM)
    tok = lambda w: pl.BlockSpec((TM, w), lambda g: (g, 0))
    return pl.pallas_call(
        _combine_kernel,
        grid=(n_tok_tiles,),
        in_specs=[
            idx(0), idx(0), idx(1), idx(1), tok(LANE), tok(d),
            pl.BlockSpec((1, 6, d), lambda g: (_mod_row(g // nt, g % nt), 0, 0)),
            pl.BlockSpec(memory_space=pl.ANY),
        ],
        out_specs=tok(d),
        out_shape=jax.ShapeDtypeStruct((n, d), F32),
        scratch_shapes=[pltpu.VMEM((2, 2, TM, d // 2), jnp.uint32), pltpu.SemaphoreType.DMA((2,))],
        compiler_params=_cparams(("arbitrary",)),
        name="moe_combine",
    )(pos1, pos2, pos1, pos2, route, xs2, mods, y)


def _moe(h2p, route, xs, mods, w_gate, w_up, w_down, layer):
    b, t, d = xs.shape
    n = b * t
    n_tiles = 2 * n // TM + N_EXPERTS
    route2 = route.reshape(n, LANE)
    tile_exp, tile_on, pos1, pos2 = _dispatch(route2, n_tiles)
    xs_sorted = _scatter_rows(h2p.reshape(n, d // 2), pos1, pos2, n_tiles * TM)
    y = _experts(xs_sorted, tile_exp, tile_on, w_gate, w_up, w_down, layer)
    return _combine(y, pos1, pos2, route2, xs.reshape(n, d), mods, t // TM).reshape(b, t, d)


def _rope_tables(n_ctx, n_lat):
    rows = n_lat // GRID_W
    row = jnp.broadcast_to(jnp.arange(rows, dtype=F32)[:, None], (rows, GRID_W)).reshape(-1)
    col = jnp.broadcast_to(jnp.arange(GRID_W, dtype=F32)[None, :], (rows, GRID_W)).reshape(-1)
    n_freq = MLA_ROPE // 4
    inv = ROPE_THETA ** (-jnp.arange(n_freq, dtype=F32) / n_freq)
    ang = jnp.concatenate([row[:, None] * inv, col[:, None] * inv], axis=-1)
    cos, sin = jnp.cos(ang), jnp.sin(ang)
    half = MLA_ROPE // 2
    z = lambda w: jnp.zeros((n_lat, w), F32)
    o = lambda w: jnp.ones((n_lat, w), F32)
    tail = LANE - MLA_QK
    cos_t = jnp.concatenate([o(MLA_NOPE), cos, cos, o(tail)], axis=-1)
    sa_t = jnp.concatenate([z(MLA_NOPE + half), sin, z(tail)], axis=-1)
    sb_t = jnp.concatenate([z(MLA_NOPE), -sin, z(half + tail)], axis=-1)
    ctx1 = jnp.ones((n_ctx, LANE), F32)
    ctx0 = jnp.zeros((n_ctx, LANE), F32)
    return (jnp.concatenate([ctx1, cos_t], 0), jnp.concatenate([ctx0, sa_t], 0),
            jnp.concatenate([ctx0, sb_t], 0))


def _pad_cols(a, width):
    return jnp.pad(a, [(0, 0)] * (a.ndim - 1) + [(0, width - a.shape[-1])])


def _layer_weights(w_in, w_uq, w_ukv, q_norm_w, k_norm_w, ml_conv_w, ml_wq, ml_wk, ml_gate_b,
                   gla_wa, gla_ba, w_out, w_grp, b_grp, w_erouter, b_erouter):
    d = w_in.shape[0]
    o = np.cumsum((0, Q_LORA, KV_LORA, MLA_ROPE, ML_W, ML_W, ML_W, 4 * ML_HEADS, GLA_HEADS * GLA_DK,
                   GLA_HEADS * GLA_DK, GLA_HEADS * GLA_DV, GLA_HEADS * GLA_DV, 2 * GLA_LR))
    seg = lambda j: w_in[:, o[j]:o[j + 1]]
    cq, ckv, kr, mx, mv, mo, mg, gq, gk, gv, gr, ga = (seg(j) for j in range(12))
    z = lambda w: jnp.zeros((d, w), F32)

    def small(di):
        return jnp.concatenate([mg[:, di * 8:(di + 1) * 8], ga[:, di * GLA_LR:(di + 1) * GLA_LR],
                                z(SM_KR - SM_GA - GLA_LR), kr, z(LANE - SM_KR - MLA_ROPE)], axis=-1)

    w_in_p = jnp.concatenate([cq, mx, mv, mo, gv, gr, ckv, gq, gk, small(0), small(1)], axis=-1).astype(BF16)

    wuq_p = _pad_cols(w_uq.reshape(Q_LORA, MLA_HEADS, MLA_QK), LANE).reshape(Q_LORA, -1).astype(BF16)
    ukv = w_ukv.reshape(KV_LORA, MLA_HEADS, MLA_NOPE + MLA_V)
    wuk_p = _pad_cols(ukv[..., :MLA_NOPE], LANE).reshape(KV_LORA, -1).astype(BF16)
    wuv_p = _pad_cols(ukv[..., MLA_NOPE:], LANE).reshape(KV_LORA, -1).astype(BF16)
    qn_p = _pad_cols(q_norm_w.reshape(1, -1), LANE)
    kn_p = _pad_cols(k_norm_w.reshape(1, -1), LANE)

    conv_w8 = jnp.pad(ml_conv_w, ((0, 8 - ML_CONV), (0, 0)))
    bd = lambda w: jax.scipy.linalg.block_diag(*[w[h] for h in range(ML_HEADS)])
    wq_bd = (bd(ml_wq) * (ML_DH ** -0.5)).astype(BF16)
    wk_bd = bd(ml_wk).astype(BF16)
    gate_b_p = _pad_cols(ml_gate_b.reshape(2, 1, 2 * ML_HEADS), LANE)

    wa_p = jnp.pad(gla_wa, ((0, 0), (SM_GA, LANE - SM_GA - GLA_LR), (0, 0)))
    ba_p = gla_ba.reshape(2, 1, -1)

    wr_p = _pad_cols(jnp.concatenate([w_grp, w_erouter], axis=-1), LANE)
    br_p = _pad_cols(jnp.concatenate([b_grp, b_erouter]).reshape(1, -1), LANE)
    return dict(w_in_p=w_in_p, wuq_p=wuq_p, wuk_p=wuk_p, wuv_p=wuv_p, qn_p=qn_p, kn_p=kn_p,
                conv_w8=conv_w8, wq_bd=wq_bd, wk_bd=wk_bd, gate_b_p=gate_b_p, wa_p=wa_p, ba_p=ba_p,
                w_out_b=w_out.astype(BF16), wr_p=wr_p, br_p=br_p)


def kernel(x, c, ctx, c_ctx, w_mod, b_mod, norm1_w, w_in, q_a_norm, w_uq, kv_a_norm, w_ukv,
           q_norm_w, k_norm_w, ml_conv_w, ml_conv_b, ml_wq, ml_wk, ml_gate_b, ml_norm_w, ml_skip,
           gla_wa, gla_ba, gla_norm_w, w_out, norm2_w, w_grp, b_grp, w_erouter, b_erouter,
           w_gate, w_up, w_down):
    b, s, d = x.shape
    n_ctx = ctx.shape[1]
    depth = w_mod.shape[0]
    assert n_ctx == TM and s % TM == 0 and b == 2

    cc = jnp.concatenate([c, c_ctx[None, :], jnp.zeros((8 - b - 1, d), F32)], axis=0)
    mods_all = _mods(cc, w_mod, b_mod).reshape(depth, 8, 6, d)
    cos_t, sa_t, sb_t = _rope_tables(n_ctx, s)
    xs = jnp.concatenate([ctx, x], axis=1)

    for l in range(depth):
        w = _layer_weights(w_in[l], w_uq[l], w_ukv[l], q_norm_w[l], k_norm_w[l], ml_conv_w[l],
                           ml_wq[l], ml_wk[l], ml_gate_b[l], gla_wa[l], gla_ba[l], w_out[l],
                           w_grp[l], b_grp[l], w_erouter[l], b_erouter[l])
        mods = mods_all[l]
        p = _inproj(xs, mods, norm1_w[l], w["w_in_p"])
        q, k, v = _mla_prep(p, q_a_norm[l], w["wuq_p"], kv_a_norm[l], w["wuk_p"], w["wuv_p"],
                            w["qn_p"], w["kn_p"], cos_t, sa_t, sb_t)
        a_lat = _attention_lat(q, k, v)
        a_ctx = _attention_ctx(q, k, v) if l < depth - 1 else jnp.zeros((b, TM, MLA_HEADS * MLA_V), BF16)
        xconv, mq, mk = _ml_prep(p, w["conv_w8"], ml_conv_b[l], w["wq_bd"], w["wk_bd"])
        mh = _ml_scan(mq, mk, p, w["gate_b_p"])
        go = _gla_scan(p, w["wa_p"], w["ba_p"])
        xs, h2, route = _outproj(a_ctx, a_lat, mh, xconv, p, go, xs, mods, ml_norm_w[l], ml_skip[l], gla_norm_w[l],
                                w["w_out_b"], norm2_w[l], w["wr_p"], w["br_p"])
        xs = _moe(h2, route, xs, mods, w_gate, w_up, w_down, l)
    return xs[:, n_ctx:, :]
```

```python
import functools

import jax
import jax.numpy as jnp
import numpy as np
from jax import lax
from jax.experimental import pallas as pl
from jax.experimental.pallas import tpu as pltpu

F32 = jnp.float32
BF16 = jnp.bfloat16
HIGHEST = lax.Precision.HIGHEST

EPS = 1e-6
GRID_W = 64
ROPE_THETA = 10000.0

MLA_HEADS = 8
MLA_NOPE = 64
MLA_ROPE = 32
MLA_QK = MLA_NOPE + MLA_ROPE
MLA_V = 64
Q_LORA = 256
KV_LORA = 128

ML_HEADS = 4
ML_DH = 64
ML_W = ML_HEADS * ML_DH
ML_CONV = 5

GLA_HEADS = 4
GLA_DK = 32
GLA_DV = 64
GLA_LR = 16
GLA_TAU = 16.0

N_GROUPS = 4
EXP_PER_GROUP = 8
N_EXPERTS = N_GROUPS * EXP_PER_GROUP
D_EXPERT = 256

LANE = 128
TM = 256
ML_CHUNK = 256
GLA_CHUNK = 128
GLA_SAFE_SPAN = 80.0
VMEM_LIMIT = 56 * 1024 * 1024

C_CQ, C_MX, C_MV, C_MO, C_GV, C_GR = 0, 256, 512, 768, 1024, 1280
C_CKV, C_GQ, C_GK, C_SMF, C_SMB = 1536, 1664, 1792, 1920, 2048
D_INP = 2176
SM_GATE = 0
SM_GA = 8
SM_KR = 64
R_GRP = 0
R_EXP = 4
RT_E1 = 0
RT_W1 = 2
GATHER_UNROLL = 8


def _cparams(sem):
    return pltpu.CompilerParams(dimension_semantics=sem, vmem_limit_bytes=VMEM_LIMIT)


def _silu(x):
    return x * jax.nn.sigmoid(x)


def _log_sigmoid(x):
    return -(jnp.maximum(-x, 0.0) + jnp.log1p(jnp.exp(-jnp.abs(x))))


def _nt_dot(a, b, **kw):
    return lax.dot_general(a, b, (((1,), (1,)), ((), ())), preferred_element_type=F32, **kw)


def _split3(x):
    hi = x.astype(BF16)
    r = x - hi.astype(F32)
    mid = r.astype(BF16)
    return hi, mid, (r - mid.astype(F32)).astype(BF16)


def _dot_sel(sel, x, nt=False, right=False):
    s = sel.astype(BF16)
    dot = _nt_dot if nt else functools.partial(jnp.dot, preferred_element_type=F32)
    return sum((dot(piece, s) if right else dot(s, piece)) for piece in _split3(x))


def _tn_dot(a, b, **kw):
    return lax.dot_general(a, b, (((0,), (0,)), ((), ())), preferred_element_type=F32, **kw)


def _mods_kernel(cc_ref, w_ref, b_ref, o_ref):
    a = _silu(cc_ref[...])
    o_ref[0] = jnp.dot(a, w_ref[0], precision=HIGHEST, preferred_element_type=F32) + b_ref[0]


def _mods(cc, w_mod, b_mod):
    depth, d, d6 = w_mod.shape
    nb = 1536
    return pl.pallas_call(
        _mods_kernel,
        grid=(depth, d6 // nb),
        in_specs=[
            pl.BlockSpec((8, d), lambda l, j: (0, 0)),
            pl.BlockSpec((1, d, nb), lambda l, j: (l, 0, j)),
            pl.BlockSpec((1, 1, nb), lambda l, j: (l, 0, j)),
        ],
        out_specs=pl.BlockSpec((1, 8, nb), lambda l, j: (l, 0, j)),
        out_shape=jax.ShapeDtypeStruct((depth, 8, d6), F32),
        compiler_params=_cparams(("arbitrary", "arbitrary")),
        name="adaln_mods",
    )(cc, w_mod, b_mod.reshape(depth, 1, d6))


def _mod_row(b, i):
    return jnp.where(i == 0, 2, b)


def _inproj_kernel(x_ref, mod_ref, nw_ref, w_ref, o_ref):
    x = x_ref[0]
    y = x * lax.rsqrt(jnp.mean(x * x, axis=-1, keepdims=True) + EPS) * nw_ref[...]
    mod = mod_ref[0]
    h = y * (1.0 + mod[1:2]) + mod[0:1]
    o_ref[0] = jnp.dot(h.astype(BF16), w_ref[...], preferred_element_type=F32)


def _inproj(xs, mods, norm_w, w_in_p):
    b, t, d = xs.shape
    nt = t // TM
    return pl.pallas_call(
        _inproj_kernel,
        grid=(b, nt),
        in_specs=[
            pl.BlockSpec((1, TM, d), lambda bi, i: (bi, i, 0)),
            pl.BlockSpec((1, 6, d), lambda bi, i: (_mod_row(bi, i), 0, 0)),
            pl.BlockSpec((1, d), lambda bi, i: (0, 0)),
            pl.BlockSpec((d, D_INP), lambda bi, i: (0, 0)),
        ],
        out_specs=pl.BlockSpec((1, TM, D_INP), lambda bi, i: (bi, i, 0)),
        out_shape=jax.ShapeDtypeStruct((b, t, D_INP), F32),
        compiler_params=_cparams(("parallel", "parallel")),
        name="in_proj",
    )(xs, mods, norm_w.reshape(1, d), w_in_p)


def _mla_prep_kernel(cq_ref, ckv_ref, sm_ref, qan_ref, wuq_ref, kvan_ref, wuk_ref, wuv_ref,
                     qn_ref, kn_ref, cos_ref, sa_ref, sb_ref, q_ref, k_ref, v_ref):
    cq = cq_ref[0]
    cqn = cq * lax.rsqrt(jnp.mean(cq * cq, axis=-1, keepdims=True) + EPS) * qan_ref[...]
    qall = jnp.dot(cqn.astype(BF16), wuq_ref[...], preferred_element_type=F32)
    ckv = ckv_ref[0]
    ckvn = (ckv * lax.rsqrt(jnp.mean(ckv * ckv, axis=-1, keepdims=True) + EPS)
            * kvan_ref[...]).astype(BF16)
    kall = jnp.dot(ckvn, wuk_ref[...], preferred_element_type=F32)
    vall = jnp.dot(ckvn, wuv_ref[...], preferred_element_type=F32)
    lane = lax.broadcasted_iota(jnp.int32, (1, LANE), 1)
    kr = jnp.where((lane >= SM_KR) & (lane < SM_KR + MLA_ROPE), sm_ref[0], 0.0)
    cos, sin = cos_ref[...], sa_ref[...] - sb_ref[...]
    r_i = lax.broadcasted_iota(jnp.int32, (LANE, LANE), 0)
    c_i = lax.broadcasted_iota(jnp.int32, (LANE, LANE), 1)
    half = MLA_ROPE // 2
    first = (c_i >= MLA_NOPE) & (c_i < MLA_NOPE + half)
    second = (c_i >= MLA_NOPE + half) & (c_i < MLA_QK)
    rot = jnp.where(first & (r_i == c_i + half), -1.0,
                    jnp.where(second & (r_i == c_i - half), 1.0, 0.0)).astype(BF16)

    def rope(x):
        return x * cos + jnp.dot(x.astype(BF16), rot, preferred_element_type=F32) * sin

    def head_norm(x, w):
        return x * lax.rsqrt(jnp.sum(x * x, axis=-1, keepdims=True) * (1.0 / MLA_QK) + EPS) * w

    for h in range(MLA_HEADS):
        sl = slice(h * LANE, (h + 1) * LANE)
        qh = rope(head_norm(qall[:, sl], qn_ref[...]))
        q_ref[0, h] = (qh * Q_SCALE).astype(BF16)
        kh = rope(head_norm(kall[:, sl] + kr, kn_ref[...]))
        k_ref[0, h] = jnp.where(lane == ATT_SHIFT_LANE, 1.0, kh).astype(BF16)
        v_ref[0, h] = jnp.where(lane == MLA_V, 1.0, vall[:, sl]).astype(BF16)


def _mla_prep(p, q_a_norm, wuq_p, kv_a_norm, wuk_p, wuv_p, qn_p, kn_p, cos_t, sa_t, sb_t):
    b, t, _ = p.shape
    nt = t // TM
    hw = MLA_HEADS * LANE
    full = lambda shape: pl.BlockSpec(shape, lambda bi, i: (0,) * len(shape))
    tab = pl.BlockSpec((TM, LANE), lambda bi, i: (i, 0))
    out = pl.BlockSpec((1, MLA_HEADS, TM, LANE), lambda bi, i: (bi, 0, i, 0))
    shp = jax.ShapeDtypeStruct((b, MLA_HEADS, t, LANE), BF16)
    return pl.pallas_call(
        _mla_prep_kernel,
        grid=(b, nt),
        in_specs=[
            pl.BlockSpec((1, TM, Q_LORA), lambda bi, i: (bi, i, C_CQ // Q_LORA)),
            pl.BlockSpec((1, TM, KV_LORA), lambda bi, i: (bi, i, C_CKV // KV_LORA)),
            pl.BlockSpec((1, TM, LANE), lambda bi, i: (bi, i, C_SMF // LANE)),
            full((1, Q_LORA)), full((Q_LORA, hw)), full((1, KV_LORA)),
            full((KV_LORA, hw)), full((KV_LORA, hw)), full((1, LANE)), full((1, LANE)),
            tab, tab, tab,
        ],
        out_specs=[out, out, out],
        out_shape=[shp, shp, shp],
        compiler_params=_cparams(("parallel", "parallel")),
        name="mla_prep",
    )(p, p, p, q_a_norm.reshape(1, -1), wuq_p, kv_a_norm.reshape(1, -1), wuk_p, wuv_p,
      qn_p, kn_p, cos_t, sa_t, sb_t)


Q_SCALE = float(MLA_QK ** -0.5 * np.log2(np.e))
ATT_HP = 2
ATT_NQ = 4
ATT_TK = 2048
ATT_SHIFT_LANE = MLA_QK
ATT_SAFE_MAX = 2.0 ** 100


def _softmax_step(q, kb, vb, m, acc):
    s = _nt_dot(q, kb)
    m_new = jnp.max(s, axis=-1, keepdims=True)
    if m is None:
        return m_new, jnp.dot(jnp.exp2((s - m_new).astype(BF16)), vb, preferred_element_type=F32)
    m_new = jnp.maximum(m, m_new)
    p = jnp.exp2((s - m_new).astype(BF16))
    return m_new, acc * jnp.exp2(m - m_new) + jnp.dot(p, vb, preferred_element_type=F32)


def _attn_ctx_kernel(q_ref, k_ref, v_ref, o_ref):
    outs = []
    for h in range(MLA_HEADS):
        _, acc = _softmax_step(q_ref[0, h], k_ref[0, h], v_ref[0, h], None, None)
        outs.append(acc[:, :MLA_V] / acc[:, MLA_V:MLA_V + 1])
    o_ref[0] = jnp.concatenate(outs, axis=-1).astype(BF16)


def _attention_ctx(q, k, v):
    b, h, _, _ = q.shape
    blk = pl.BlockSpec((1, h, TM, LANE), lambda bi: (bi, 0, 0, 0))
    return pl.pallas_call(
        _attn_ctx_kernel,
        grid=(b,),
        in_specs=[blk, blk, blk],
        out_specs=pl.BlockSpec((1, TM, h * MLA_V), lambda bi: (bi, 0, 0)),
        out_shape=jax.ShapeDtypeStruct((b, TM, h * MLA_V), BF16),
        compiler_params=_cparams(("parallel",)),
        name="mla_attention_ctx",
    )(q, k, v)


def _attn_lat_kernel(*refs, n_blk):
    q_refs, (k_ref, v_ref, o_ref, q_buf, qs_buf) = refs[:ATT_NQ], refs[ATT_NQ:]
    lane = lax.broadcasted_iota(jnp.int32, (1, LANE), 1)

    def kv_block(hh, j):
        off = pl.multiple_of(TM + j * ATT_TK, TM)
        return k_ref[0, hh, pl.ds(off, ATT_TK), :], v_ref[0, hh, pl.ds(off, ATT_TK), :]

    def finish(accs):
        outs = [acc[:, :MLA_V] / acc[:, MLA_V:MLA_V + 1] for acc in accs]
        o_ref[0] = jnp.concatenate(outs, axis=-1).astype(BF16)

    accs = []
    for hh in range(ATT_HP):
        q = jnp.concatenate([qr[0, hh] for qr in q_refs], axis=0)
        q_buf[hh] = q
        kb, vb = k_ref[0, hh, 0:TM, :], v_ref[0, hh, 0:TM, :]
        shift = jnp.max(_nt_dot(q, kb), axis=-1, keepdims=True).astype(BF16)
        qs_buf[hh] = jnp.where(lane == ATT_SHIFT_LANE, -shift, q)
        accs.append(jnp.dot(jnp.exp2(_nt_dot(qs_buf[hh], kb).astype(BF16)), vb, preferred_element_type=F32))

    def fast_body(j, accs):
        new = []
        for hh in range(ATT_HP):
            kb, vb = kv_block(hh, j)
            p = jnp.exp2(_nt_dot(qs_buf[hh], kb).astype(BF16))
            new.append(accs[hh] + jnp.dot(p, vb, preferred_element_type=F32))
        return tuple(new)

    accs = lax.fori_loop(0, n_blk, fast_body, tuple(accs))
    bad = sum(jnp.max(jnp.where(jnp.abs(acc) < ATT_SAFE_MAX, 0.0, 1.0)) for acc in accs)

    @pl.when(bad == 0.0)
    def _():
        finish(accs)

    @pl.when(bad != 0.0)
    def _():
        init = []
        for hh in range(ATT_HP):
            init += _softmax_step(q_buf[hh], k_ref[0, hh, 0:TM, :], v_ref[0, hh, 0:TM, :], None, None)

        def body(j, carry):
            new = []
            for hh in range(ATT_HP):
                new += _softmax_step(q_buf[hh], *kv_block(hh, j), carry[2 * hh], carry[2 * hh + 1])
            return tuple(new)

        carry = lax.fori_loop(0, n_blk, body, tuple(init))
        finish(carry[1::2])


def _attention_lat(q, k, v):
    b, h, t, _ = q.shape
    s = t - TM
    tq = ATT_NQ * TM
    assert s % ATT_TK == 0 and s % tq == 0
    kv = pl.BlockSpec((1, ATT_HP, t, LANE), lambda bi, hp, i: (bi, hp, 0, 0))
    qs = [pl.BlockSpec((1, ATT_HP, TM, LANE), lambda bi, hp, i, u=u: (bi, hp, 1 + ATT_NQ * i + u, 0))
          for u in range(ATT_NQ)]
    return pl.pallas_call(
        functools.partial(_attn_lat_kernel, n_blk=s // ATT_TK),
        grid=(b, h // ATT_HP, s // tq),
        in_specs=qs + [kv, kv],
        out_specs=pl.BlockSpec((1, tq, ATT_HP * MLA_V), lambda bi, hp, i: (bi, i, hp)),
        out_shape=jax.ShapeDtypeStruct((b, s, h * MLA_V), BF16),
        scratch_shapes=[pltpu.VMEM((ATT_HP, tq, LANE), BF16), pltpu.VMEM((ATT_HP, tq, LANE), BF16)],
        compiler_params=_cparams(("parallel", "parallel", "arbitrary")),
        name="mla_attention",
    )(*([q] * ATT_NQ), k, v)


def _ml_prep_kernel(x_ref, prev_ref, next_ref, cw_ref, cb_ref, wq_ref, wk_ref,
                    xc_ref, q_ref, k_ref, *, n_tiles):
    i = pl.program_id(1)
    x = x_ref[0]
    prev = jnp.where(i <= 1, 0.0, prev_ref[0])
    nxt = jnp.where((i == 0) | (i == n_tiles - 1), 0.0, next_ref[0])
    ext = jnp.concatenate([prev, x, nxt], axis=0)
    n_ext = TM + 16
    cw = cw_ref[...]
    acc = jnp.zeros((TM, ML_W), F32) + cb_ref[...]
    for kk in range(ML_CONV):
        sh = (ML_CONV // 2 - kk) % n_ext
        shifted = ext if sh == 0 else pltpu.roll(ext, sh, 0)
        acc = acc + cw[kk:kk + 1] * shifted[8:8 + TM]
    xc = _silu(acc)
    xc_ref[0] = xc
    xb = xc.astype(BF16)
    q_ref[0] = jnp.dot(xb, wq_ref[...], preferred_element_type=F32).astype(BF16)
    k_ref[0] = jnp.dot(xb, wk_ref[...], preferred_element_type=F32).astype(BF16)


def _ml_prep(p, conv_w8, conv_b, wq_bd, wk_bd):
    b, t, _ = p.shape
    nt = t // TM
    r8 = TM // 8
    full = lambda shape: pl.BlockSpec(shape, lambda bi, i: (0,) * len(shape))
    cb = C_MX // ML_W
    blk = pl.BlockSpec((1, TM, ML_W), lambda bi, i: (bi, i, 0))
    return pl.pallas_call(
        functools.partial(_ml_prep_kernel, n_tiles=nt),
        grid=(b, nt),
        in_specs=[
            pl.BlockSpec((1, TM, ML_W), lambda bi, i: (bi, i, cb)),
            pl.BlockSpec((1, 8, ML_W), lambda bi, i: (bi, jnp.maximum(i * r8 - 1, 0), cb)),
            pl.BlockSpec((1, 8, ML_W), lambda bi, i: (bi, jnp.minimum((i + 1) * r8, nt * r8 - 1), cb)),
            full((8, ML_W)), full((1, ML_W)), full((ML_W, ML_W)), full((ML_W, ML_W)),
        ],
        out_specs=[blk, blk, blk],
        out_shape=[jax.ShapeDtypeStruct((b, t, ML_W), F32),
                   jax.ShapeDtypeStruct((b, t, ML_W), BF16),
                   jax.ShapeDtypeStruct((b, t, ML_W), BF16)],
        compiler_params=_cparams(("parallel", "parallel")),
        name="mlstm_prep",
    )(p, p, p, conv_w8, conv_b.reshape(1, ML_W), wq_bd, wk_bd)


def _scan_chunk(d, step, n_ctx_chunks, n_chunks):
    bwd = jnp.where(step < n_ctx_chunks, n_ctx_chunks - 1 - step, n_chunks - 1 - (step - n_ctx_chunks))
    return jnp.where(d == 0, step, bwd)


def _ml_chunk(d, q, k, v, g, s_ref, m_ref, base):
    L = ML_CHUNK
    row = lax.broadcasted_iota(jnp.int32, (L, L), 0)
    col = lax.broadcasted_iota(jnp.int32, (L, L), 1)
    mask = col <= row if d == 0 else col >= row
    tri = mask.astype(F32)
    lane = lax.broadcasted_iota(jnp.int32, (1, LANE), 1)
    eye8 = (lax.broadcasted_iota(jnp.int32, (8, LANE), 0)
            == lax.broadcasted_iota(jnp.int32, (8, LANE), 1)).astype(F32)

    lf = _log_sigmoid(g)
    bc = jnp.dot(tri, lf, precision=HIGHEST, preferred_element_type=F32)
    g_rows = _nt_dot(eye8, g, precision=HIGHEST)
    bc = pltpu.roll(bc, LANE - ML_HEADS, 1)
    bc_rows = _nt_dot(eye8, bc, precision=HIGHEST)

    outs = []
    for pair in range(ML_HEADS // 2):
        sl = slice(pair * LANE, (pair + 1) * LANE)
        q_blk, k_blk, v_blk = q[:, sl], k[:, sl], v[:, sl]
        pair_out = []
        for sub in range(2):
            h = pair * 2 + sub
            head_lanes = (lane >= sub * ML_DH) & (lane < (sub + 1) * ML_DH)
            qh = jnp.where(head_lanes, q_blk, jnp.zeros_like(q_blk))
            vs = v_blk if sub == 0 else pltpu.roll(v_blk, ML_DH, 1)
            v_ext = jnp.where(lane < ML_DH, vs, jnp.where(lane == ML_DH, 1.0, 0.0)).astype(BF16)

            li_c = g[:, SM_GATE + h:SM_GATE + h + 1]
            bc_c = bc[:, SM_GATE + h:SM_GATE + h + 1]
            li_r = g_rows[h:h + 1, :]
            bc_r = bc_rows[h:h + 1, :]
            m_st = m_ref[base + h][0:1, 0:1]

            c_row = jnp.where(mask, li_r - bc_r, -jnp.inf)
            m_rel = jnp.maximum(m_st, jnp.max(c_row, axis=-1, keepdims=True))
            m_t = bc_c + m_rel
            e = jnp.exp(c_row - m_rel)
            s = (_nt_dot(qh, k_blk) * e).astype(BF16)
            tot = (jnp.dot(s, v_ext, preferred_element_type=F32)
                   + jnp.exp(m_st - m_rel) * jnp.dot(qh, s_ref[base + h].astype(BF16),
                                                     preferred_element_type=F32))
            den = tot[:, ML_DH:ML_DH + 1]
            pair_out.append(tot / jnp.maximum(jnp.abs(den), jnp.exp(-m_t)))

            b_end = bc_c[L - 1:L] if d == 0 else bc_c[0:1]
            g_col = b_end - bc_c + li_c
            m_new = jnp.maximum(b_end + m_st, jnp.max(g_col, axis=0, keepdims=True))
            kw = jnp.where(head_lanes, k_blk.astype(F32) * jnp.exp(g_col - m_new), 0.0).astype(BF16)
            s_ref[base + h] = jnp.exp(b_end + m_st - m_new) * s_ref[base + h] + _tn_dot(kw, v_ext)
            m_ref[base + h] = jnp.broadcast_to(m_new, m_ref.shape[1:])
        outs.append(jnp.where(lane < ML_DH, pair_out[0], pltpu.roll(pair_out[1], ML_DH, 1)))
    return jnp.concatenate(outs, axis=-1)


def _ml_scan_kernel(qf_ref, kf_ref, vf_ref, smf_ref, qb_ref, kb_ref, vb_ref, smb_ref, gb_ref,
                    hf_ref, hb_ref, s_ref, m_ref):
    @pl.when(pl.program_id(0) == 0)
    def _():
        s_ref[...] = jnp.zeros_like(s_ref)
        m_ref[...] = jnp.zeros_like(m_ref)

    nb = qf_ref.shape[0]
    streams = ((0, qf_ref, kf_ref, vf_ref, smf_ref, hf_ref), (1, qb_ref, kb_ref, vb_ref, smb_ref, hb_ref))
    for d, q_ref, k_ref, v_ref, sm_ref, h_ref in streams:
        for bi in range(nb):
            h_ref[bi] = _ml_chunk(d, q_ref[bi], k_ref[bi], v_ref[bi], sm_ref[bi] + gb_ref[d],
                                  s_ref, m_ref, (d * nb + bi) * ML_HEADS)


def _ml_scan(q, k, p, gate_b_p):
    b, t, _ = q.shape
    nc = t // ML_CHUNK
    chunk = functools.partial(_scan_chunk, n_ctx_chunks=TM // ML_CHUNK, n_chunks=nc)
    blk = lambda d, w, cb: pl.BlockSpec((b, ML_CHUNK, w), lambda s: (0, chunk(d, s), cb))
    stream = lambda d: [blk(d, ML_W, 0), blk(d, ML_W, 0), blk(d, ML_W, C_MV // ML_W),
                        blk(d, LANE, C_SMF // LANE + d)]
    shp = jax.ShapeDtypeStruct((b, t, ML_W), F32)
    return pl.pallas_call(
        _ml_scan_kernel,
        grid=(nc,),
        in_specs=stream(0) + stream(1) + [pl.BlockSpec((2, 1, LANE), lambda s: (0, 0, 0))],
        out_specs=[blk(0, ML_W, 0), blk(1, ML_W, 0)],
        out_shape=[shp, shp],
        scratch_shapes=[pltpu.VMEM((2 * b * ML_HEADS, LANE, LANE), F32),
                        pltpu.VMEM((2 * b * ML_HEADS, 8, LANE), F32)],
        compiler_params=_cparams(("arbitrary",)),
        name="mlstm_scan",
    )(q, k, p, p, q, k, p, p, gate_b_p)


def _gla_mask(d):
    row = lax.broadcasted_iota(jnp.int32, (GLA_CHUNK, GLA_CHUNK), 0)
    col = lax.broadcasted_iota(jnp.int32, (GLA_CHUNK, GLA_CHUNK), 1)
    return col <= row if d == 0 else col >= row


def _gla_gates(d, sm, wa, ba):
    pre = jnp.dot(sm.astype(BF16), wa.astype(BF16), preferred_element_type=F32) + ba
    loga = _log_sigmoid(pre) * (1.0 / GLA_TAU)
    bc = _dot_sel(_gla_mask(d).astype(F32), loga)
    return loga, bc, jnp.max(jnp.abs(bc - bc[GLA_CHUNK // 2 - 1:GLA_CHUNK // 2]))


def _gla_chunk(d, q_ref, k_ref, v_ref, o_ref, bi, r0, loga, bc, s_ref, si, la_ref, token_form):
    L = GLA_CHUNK
    rows = slice(r0, r0 + L)
    mask = _gla_mask(d)
    lane_k = lax.broadcasted_iota(jnp.int32, (1, GLA_HEADS * GLA_DK), 1)
    lane_v = lax.broadcasted_iota(jnp.int32, (1, GLA_HEADS * GLA_DV), 1)
    eye = (lax.broadcasted_iota(jnp.int32, (LANE, LANE), 0)
           == lax.broadcasted_iota(jnp.int32, (LANE, LANE), 1)).astype(F32)
    blockdiag = (lax.broadcasted_iota(jnp.int32, s_ref.shape[1:], 0) // GLA_DK
                 == lax.broadcasted_iota(jnp.int32, s_ref.shape[1:], 1) // GLA_DV)
    scale = GLA_DK ** -0.5

    ref_row = bc[L // 2 - 1:L // 2]

    if not token_form:
        b_end = bc[L - 1:L] if d == 0 else bc[0:1]
        q = q_ref[bi, rows] * scale
        k = k_ref[bi, rows]
        v = v_ref[bi, rows].astype(BF16)
        q_in = (q * jnp.exp(bc - ref_row))
        k_in = (k * jnp.exp(ref_row - bc)).astype(BF16)
        q_st = (q * jnp.exp(bc)).astype(BF16)
        k_st = (k * jnp.exp(b_end - bc)).astype(BF16)
        s_old = s_ref[si]
        o = jnp.dot(q_st, jnp.where(blockdiag, s_old, 0.0).astype(BF16), preferred_element_type=F32)
        for h in range(GLA_HEADS):
            qh = jnp.where(lane_k // GLA_DK == h, q_in, 0.0).astype(BF16)
            att = jnp.where(mask, _nt_dot(qh, k_in), 0.0).astype(BF16)
            oh = jnp.dot(att, v, preferred_element_type=F32)
            o = o + jnp.where(lane_v // GLA_DV == h, oh, 0.0)
        o_ref[bi, rows] = o
        decay_col = jnp.exp(_dot_sel(eye, jnp.broadcast_to(b_end, (8, LANE)), nt=True)[:, 0:1])
        s_ref[si] = decay_col * s_old + _tn_dot(k_st, v)

    else:
        la_ref[...] = loga
        sub = lax.broadcasted_iota(jnp.int32, (8, LANE), 0)

        def body(i, carry):
            t = i if d == 0 else L - 1 - i
            la_t = la_ref[pl.ds(t, 1), :]
            k_t = k_ref[bi, pl.ds(r0 + t, 1), :]
            q_t = q_ref[bi, pl.ds(r0 + t, 1), :] * scale
            v_t = v_ref[bi, pl.ds(r0 + t, 1), :]
            stacked = jnp.where(sub == 0, la_t, jnp.where(sub == 1, k_t, 0.0))
            cols = _dot_sel(eye, stacked, nt=True)
            s_new = jnp.exp(cols[:, 0:1]) * s_ref[si] + cols[:, 1:2] * v_t
            s_ref[si] = s_new
            o_t = jnp.dot(jnp.broadcast_to(q_t, (8, LANE)), jnp.where(blockdiag, s_new, 0.0),
                          precision=HIGHEST, preferred_element_type=F32)
            o_ref[bi, pl.ds(r0 + t, 1), :] = o_t[0:1]
            return carry

        lax.fori_loop(0, L, body, 0)


def _gla_scan_kernel(qf_ref, kf_ref, vf_ref, smf_ref, qb_ref, kb_ref, vb_ref, smb_ref, wa_ref, ba_ref,
                     of_ref, ob_ref, s_ref, la_ref):
    @pl.when(pl.program_id(0) == 0)
    def _():
        s_ref[...] = jnp.zeros_like(s_ref)

    nb = qf_ref.shape[0]
    n_sub = TM // GLA_CHUNK
    streams = ((0, qf_ref, kf_ref, vf_ref, smf_ref, of_ref), (1, qb_ref, kb_ref, vb_ref, smb_ref, ob_ref))
    chunks, span = [], 0.0
    for d, q_ref, k_ref, v_ref, sm_ref, o_ref in streams:
        for bi in range(nb):
            for c in (range(n_sub) if d == 0 else reversed(range(n_sub))):
                r0 = c * GLA_CHUNK
                loga, bc, chunk_span = _gla_gates(d, sm_ref[bi, r0:r0 + GLA_CHUNK], wa_ref[d], ba_ref[d])
                span = jnp.maximum(span, chunk_span)
                chunks.append((d, q_ref, k_ref, v_ref, o_ref, bi, r0, loga, bc, s_ref, d * nb + bi, la_ref))

    in_range = span < GLA_SAFE_SPAN
    for token_form, pred in ((False, in_range), (True, jnp.logical_not(in_range))):
        @pl.when(pred)
        def _(token_form=token_form):
            for chunk in chunks:
                _gla_chunk(*chunk, token_form)


def _gla_scan(p, wa_p, ba_p):
    b, t, _ = p.shape
    nc = t // TM
    chunk = functools.partial(_scan_chunk, n_ctx_chunks=1, n_chunks=nc)
    kw, vw = GLA_HEADS * GLA_DK, GLA_HEADS * GLA_DV
    blk = lambda d, w, col: pl.BlockSpec((b, TM, w), lambda s: (0, chunk(d, s), col // w))
    stream = lambda d: [blk(d, kw, C_GQ), blk(d, kw, C_GK), blk(d, vw, C_GV), blk(d, LANE, C_SMF + d * LANE)]
    shp = jax.ShapeDtypeStruct((b, t, vw), F32)
    return pl.pallas_call(
        _gla_scan_kernel,
        grid=(nc,),
        in_specs=stream(0) + stream(1) + [pl.BlockSpec((2, LANE, kw), lambda s: (0, 0, 0)),
                                          pl.BlockSpec((2, 1, kw), lambda s: (0, 0, 0))],
        out_specs=[blk(0, vw, 0), blk(1, vw, 0)],
        out_shape=[shp, shp],
        scratch_shapes=[pltpu.VMEM((2 * b, kw, vw), F32), pltpu.VMEM((GLA_CHUNK, kw), F32)],
        compiler_params=_cparams(("arbitrary",)),
        name="gla_scan",
    )(p, p, p, p, p, p, p, p, wa_p, ba_p)


OUTPROJ_TILES = 2
OUTPROJ_N_TOK = 11


def _outproj_kernel(*refs, n_tiles):
    n_in = OUTPROJ_TILES * OUTPROJ_N_TOK
    shared = refs[n_in:n_in + 7]
    outs = refs[n_in + 7:]
    for u in range(OUTPROJ_TILES):
        tile = jnp.minimum(OUTPROJ_TILES * pl.program_id(1) + u, n_tiles - 1)
        _outproj_tile(tile == 0, slice(u * TM, (u + 1) * TM),
                      *refs[u * OUTPROJ_N_TOK:(u + 1) * OUTPROJ_N_TOK], *shared, *outs)


def _outproj_tile(is_ctx, rows, ac_ref, al_ref, mhf_ref, mhb_ref, xc_ref, mo_ref, gof_ref, gob_ref, gr_ref, x_ref,
                  mod_ref, mnw_ref, msk_ref, gnw_ref, wout_ref, n2w_ref, wr_ref, br_ref,
                  xo_ref, h2_ref, route_ref):
    grp64 = (lax.broadcasted_iota(jnp.int32, (ML_W, ML_W), 0) // ML_DH
             == lax.broadcasted_iota(jnp.int32, (ML_W, ML_W), 1) // ML_DH).astype(F32) * (1.0 / ML_DH)

    def head_norm(x, w):
        ms = _dot_sel(grp64, x * x, right=True)
        return x * lax.rsqrt(ms + EPS) * w

    m_l = jax.nn.sigmoid(mo_ref[0]) * (head_norm(mhf_ref[0] + mhb_ref[0], mnw_ref[...])
                                       + msk_ref[...] * xc_ref[0])
    g_l = head_norm(gof_ref[0] + gob_ref[0], gnw_ref[...]) * _silu(gr_ref[0])
    na = MLA_HEADS * MLA_V
    a = jnp.where(is_ctx, ac_ref[0], al_ref[0])
    res = (jnp.dot(a, wout_ref[0:na], preferred_element_type=F32)
           + jnp.dot(m_l.astype(BF16), wout_ref[na:na + ML_W], preferred_element_type=F32)
           + jnp.dot(g_l.astype(BF16), wout_ref[na + ML_W:], preferred_element_type=F32))
    mod = mod_ref[0]
    x = x_ref[0] + mod[2:3] * res
    xo_ref[0, rows] = x
    h2 = (x * lax.rsqrt(jnp.mean(x * x, axis=-1, keepdims=True) + EPS) * n2w_ref[...]
          * (1.0 + mod[4:5]) + mod[3:4])
    half = h2.shape[-1] // 2
    h2_ref[0, rows] = _pack_bf16_pair(h2[:, :half], h2[:, half:])

    h_hi, h_lo, _ = _split3(h2)
    w_hi, w_lo, _ = _split3(wr_ref[...])
    dot = functools.partial(jnp.dot, preferred_element_type=F32)
    logits = dot(h_hi, w_hi) + dot(h_hi, w_lo) + dot(h_lo, w_hi) + br_ref[...]
    lane = lax.broadcasted_iota(jnp.int32, (1, LANE), 1)
    lane_f = lane.astype(F32)
    neg = -jnp.inf
    gl = jnp.where(lane < R_EXP, logits, neg)
    gmax = jnp.max(gl, axis=-1, keepdims=True)
    g_w = 1.0 / jnp.sum(jnp.exp(gl - gmax), axis=-1, keepdims=True)
    g_i = jnp.min(jnp.where(gl == gmax, lane_f, float(LANE)), axis=-1, keepdims=True)
    grp_of_lane = ((lane - R_EXP) // EXP_PER_GROUP).astype(F32)
    in_grp = (lane >= R_EXP) & (lane < R_EXP + N_EXPERTS) & (grp_of_lane == g_i)
    el = jnp.where(in_grp, logits, neg)
    m1 = jnp.max(el, axis=-1, keepdims=True)
    i1 = jnp.min(jnp.where(el == m1, lane_f, float(LANE)), axis=-1, keepdims=True)
    el2 = jnp.where(lane_f == i1, neg, el)
    m2 = jnp.max(el2, axis=-1, keepdims=True)
    i2 = jnp.min(jnp.where(el2 == m2, lane_f, float(LANE)), axis=-1, keepdims=True)
    p2 = jnp.exp(m2 - m1)
    w1 = g_w / (1.0 + p2)
    route_ref[0, rows] = jnp.where(lane == RT_E1, i1 - R_EXP, jnp.where(lane == RT_E1 + 1, i2 - R_EXP,
                             jnp.where(lane == RT_W1, w1, jnp.where(lane == RT_W1 + 1, p2 * w1, 0.0))))


def _outproj(a_ctx, a_lat, mh, xconv, p, go, xs, mods, ml_norm_w, ml_skip, gla_norm_w, w_out_b, norm2_w, wr_p, br_p):
    b, t, d = xs.shape
    nt = t // TM
    full = lambda shape: pl.BlockSpec(shape, lambda bi, i: (0,) * len(shape))
    na = MLA_HEADS * MLA_V

    out = lambda w: pl.BlockSpec((1, OUTPROJ_TILES * TM, w), lambda bi, i: (bi, i, 0))

    def tile_specs(u):
        tile = lambda i: jnp.minimum(OUTPROJ_TILES * i + u, nt - 1)
        tok = lambda w, cb=0: pl.BlockSpec((1, TM, w), lambda bi, i: (bi, tile(i), cb))
        ins = [pl.BlockSpec((1, TM, na), lambda bi, i: (bi, 0, 0)),
               pl.BlockSpec((1, TM, na), lambda bi, i: (bi, jnp.maximum(tile(i) - 1, 0), 0)),
               tok(ML_W), tok(ML_W), tok(ML_W), tok(ML_W, C_MO // ML_W),
               tok(ML_W), tok(ML_W), tok(ML_W, C_GR // ML_W), tok(d),
               pl.BlockSpec((1, 6, d), lambda bi, i: (_mod_row(bi, tile(i)), 0, 0))]
        return ins

    specs = [tile_specs(u) for u in range(OUTPROJ_TILES)]
    tok_args = (a_ctx, a_lat, mh[0], mh[1], xconv, p, go[0], go[1], p, xs, mods)
    assert len(tok_args) == OUTPROJ_N_TOK
    shapes = [jax.ShapeDtypeStruct((b, t, d), F32), jax.ShapeDtypeStruct((b, t, d // 2), jnp.uint32),
              jax.ShapeDtypeStruct((b, t, LANE), F32)]
    outs = pl.pallas_call(
        functools.partial(_outproj_kernel, n_tiles=nt),
        grid=(b, pl.cdiv(nt, OUTPROJ_TILES)),
        in_specs=sum(specs, []) + [
            full((1, ML_W)), full((1, ML_W)), full((1, ML_W)), full((d, d)), full((1, d)),
            full((d, LANE)), full((1, LANE))],
        out_specs=[out(d), out(d // 2), out(LANE)],
        out_shape=shapes,
        compiler_params=_cparams(("parallel", "arbitrary")),
        name="out_proj_router",
    )(*(tok_args * OUTPROJ_TILES), ml_norm_w.reshape(1, -1), ml_skip.reshape(1, -1),
      gla_norm_w.reshape(1, -1), w_out_b, norm2_w.reshape(1, -1), wr_p, br_p)
    return outs


def _dispatch(route, n_tiles):
    n = route.shape[0]
    flat = route[:, RT_E1:RT_E1 + 2].astype(jnp.int32).reshape(-1)
    onehot = (flat[:, None] == jnp.arange(N_EXPERTS, dtype=jnp.int32)[None, :]).astype(jnp.int32)
    csum = jnp.cumsum(onehot, axis=0)
    rank = jnp.sum(csum * onehot, axis=1) - 1
    padded = (csum[-1] + TM - 1) // TM * TM
    ends = jnp.cumsum(padded)
    pos = (ends - padded)[flat] + rank
    tile_start = jnp.arange(n_tiles, dtype=jnp.int32) * TM
    tile_exp = jnp.minimum(jnp.sum((ends[None, :] <= tile_start[:, None]).astype(jnp.int32), axis=1),
                           N_EXPERTS - 1)
    tile_on = (tile_start < ends[-1]).astype(jnp.int32)
    pos = pos.reshape(n // TM, TM, 2)
    return (tile_exp, tile_on,
            pos[:, :, 0].reshape(n // TM, 1, TM), pos[:, :, 1].reshape(n // TM, 1, TM))


def _gather_rows(src_hbm, idx_ref, dst, sem):
    def body(j, carry):
        pltpu.make_async_copy(src_hbm.at[pl.ds(idx_ref[0, 0, j], 1)], dst.at[pl.ds(j, 1)], sem).start()
        return carry

    lax.fori_loop(0, TM, body, 0, unroll=GATHER_UNROLL)


def _wait_rows(src_hbm, dst, sem):
    pltpu.make_async_copy(src_hbm.at[pl.ds(0, TM)], dst, sem).wait()


def _pack_bf16_pair(lo, hi):
    lo_b = lax.bitcast_convert_type(lo.astype(BF16).astype(F32), jnp.uint32) >> 16
    hi_b = lax.bitcast_convert_type(hi.astype(BF16).astype(F32), jnp.uint32) & jnp.uint32(0xFFFF0000)
    return hi_b | lo_b


def _unpack_bf16_pair(w):
    return (lax.bitcast_convert_type(w << 16, F32),
            lax.bitcast_convert_type(w & jnp.uint32(0xFFFF0000), F32))


def _scatter_kernel(p1_ref, p2_ref, h2_ref, init_hbm, xs_hbm, sem):
    del init_hbm

    def body(j, carry):
        row = h2_ref.at[pl.ds(j, 1)]
        pltpu.make_async_copy(row, xs_hbm.at[pl.ds(p1_ref[0, 0, j], 1)], sem).start()
        pltpu.make_async_copy(row, xs_hbm.at[pl.ds(p2_ref[0, 0, j], 1)], sem).start()
        return carry

    lax.fori_loop(0, TM, body, 0, unroll=GATHER_UNROLL)
    for _ in range(2):
        pltpu.make_async_copy(h2_ref, xs_hbm.at[pl.ds(0, TM)], sem).wait()


def _scatter_rows(h2p, pos1, pos2, n_rows):
    n, dh = h2p.shape
    idx = pl.BlockSpec((1, 1, TM), lambda g: (g, 0, 0), memory_space=pltpu.SMEM)
    return pl.pallas_call(
        _scatter_kernel,
        grid=(n // TM,),
        in_specs=[idx, idx, pl.BlockSpec((TM, dh), lambda g: (g, 0)), pl.BlockSpec(memory_space=pl.ANY)],
        out_specs=pl.BlockSpec(memory_space=pl.ANY),
        out_shape=jax.ShapeDtypeStruct((n_rows, dh), jnp.uint32),
        scratch_shapes=[pltpu.SemaphoreType.DMA(())],
        input_output_aliases={3: 0},
        compiler_params=_cparams(("arbitrary",)),
        name="moe_scatter",
    )(pos1, pos2, h2p, jnp.zeros((n_rows, dh), jnp.uint32))


def _experts_kernel(texp_ref, ton_ref, x_ref, wg_ref, wu_ref, wd_ref, y_ref, wgu_b, wd_b):
    r = pl.program_id(0)
    half = wgu_b.shape[0] // 2

    @pl.when(ton_ref[r] == 1)
    def _():
        @pl.when(jnp.logical_or(r == 0, texp_ref[r] != texp_ref[jnp.maximum(r - 1, 0)]))
        def _():
            wgu_b[:, :D_EXPERT] = wg_ref[0, 0].astype(BF16)
            wgu_b[:, D_EXPERT:] = wu_ref[0, 0].astype(BF16)
            wd_b[...] = wd_ref[0, 0].astype(BF16)

        x_lo, x_hi = _unpack_bf16_pair(x_ref[...])
        gu = (jnp.dot(x_lo.astype(BF16), wgu_b[0:half], preferred_element_type=F32)
              + jnp.dot(x_hi.astype(BF16), wgu_b[half:], preferred_element_type=F32))
        act = (_silu(gu[:, :D_EXPERT]) * gu[:, D_EXPERT:]).astype(BF16)
        y = jnp.dot(act, wd_b[...], preferred_element_type=F32)
        y_ref[...] = _pack_bf16_pair(y[:, :half], y[:, half:])

    @pl.when(ton_ref[r] == 0)
    def _():
        y_ref[...] = jnp.zeros_like(y_ref)


def _experts(xs, tile_exp, tile_on, w_gate, w_up, w_down, layer):
    n_rows, dh = xs.shape
    d = 2 * dh
    wspec = lambda shape: pl.BlockSpec((1, 1) + shape, lambda r, te, to: (layer, te[r], 0, 0))
    rows = pl.BlockSpec((TM, dh), lambda r, te, to: (r, 0))
    return pl.pallas_call(
        _experts_kernel,
        grid_spec=pltpu.PrefetchScalarGridSpec(
            num_scalar_prefetch=2,
            grid=(n_rows // TM,),
            in_specs=[rows, wspec((d, D_EXPERT)), wspec((d, D_EXPERT)), wspec((D_EXPERT, d))],
            out_specs=rows,
            scratch_shapes=[pltpu.VMEM((d, 2 * D_EXPERT), BF16), pltpu.VMEM((D_EXPERT, d), BF16)],
        ),
        out_shape=jax.ShapeDtypeStruct((n_rows, dh), jnp.uint32),
        compiler_params=_cparams(("arbitrary",)),
        name="moe_experts",
    )(tile_exp, tile_on, xs, w_gate, w_up, w_down)


def _combine_kernel(p1_ref, p2_ref, p1n_ref, p2n_ref, route_ref, x_ref, mod_ref, y_hbm, o_ref, buf, sem):
    g = pl.program_id(0)
    slot = g % 2
    half = buf.shape[-1]

    def gather(pa, pb, s):
        _gather_rows(y_hbm, pa, buf.at[s, 0], sem.at[s])
        _gather_rows(y_hbm, pb, buf.at[s, 1], sem.at[s])

    @pl.when(g == 0)
    def _():
        gather(p1_ref, p2_ref, 0)

    @pl.when(g + 1 < pl.num_programs(0))
    def _():
        gather(p1n_ref, p2n_ref, 1 - slot)

    _wait_rows(y_hbm, buf.at[slot, 0], sem.at[slot])
    _wait_rows(y_hbm, buf.at[slot, 1], sem.at[slot])
    route = route_ref[...]
    w1, w2 = route[:, RT_W1:RT_W1 + 1], route[:, RT_W1 + 1:RT_W1 + 2]
    gate = mod_ref[0][5:6]
    for part, (y1, y2) in enumerate(zip(_unpack_bf16_pair(buf[slot, 0]), _unpack_bf16_pair(buf[slot, 1]))):
        cols = slice(part * half, (part + 1) * half)
        o_ref[:, cols] = x_ref[:, cols] + gate[:, cols] * (w1 * y1 + w2 * y2)


def _combine(y, pos1, pos2, route, xs2, mods, nt):
    n, d = xs2.shape
    n_tok_tiles = n // TM
    idx = lambda nxt: pl.BlockSpec((1, 1, TM), lambda g: (jnp.minimum(g + nxt, n_tok_tiles - 1), 0, 0),
                                   memory_space=pltpu.SMEM)
    tok = lambda w: pl.BlockSpec((TM, w), lambda g: (g, 0))
    return pl.pallas_call(
        _combine_kernel,
        grid=(n_tok_tiles,),
        in_specs=[
            idx(0), idx(0), idx(1), idx(1), tok(LANE), tok(d),
            pl.BlockSpec((1, 6, d), lambda g: (_mod_row(g // nt, g % nt), 0, 0)),
            pl.BlockSpec(memory_space=pl.ANY),
        ],
        out_specs=tok(d),
        out_shape=jax.ShapeDtypeStruct((n, d), F32),
        scratch_shapes=[pltpu.VMEM((2, 2, TM, d // 2), jnp.uint32), pltpu.SemaphoreType.DMA((2,))],
        compiler_params=_cparams(("arbitrary",)),
        name="moe_combine",
    )(pos1, pos2, pos1, pos2, route, xs2, mods, y)


def _moe(h2p, route, xs, mods, w_gate, w_up, w_down, layer):
    b, t, d = xs.shape
    n = b * t
    n_tiles = 2 * n // TM + N_EXPERTS
    route2 = route.reshape(n, LANE)
    tile_exp, tile_on, pos1, pos2 = _dispatch(route2, n_tiles)
    xs_sorted = _scatter_rows(h2p.reshape(n, d // 2), pos1, pos2, n_tiles * TM)
    y = _experts(xs_sorted, tile_exp, tile_on, w_gate, w_up, w_down, layer)
    return _combine(y, pos1, pos2, route2, xs.reshape(n, d), mods, t // TM).reshape(b, t, d)


def _rope_tables(n_ctx, n_lat):
    rows = n_lat // GRID_W
    row = jnp.broadcast_to(jnp.arange(rows, dtype=F32)[:, None], (rows, GRID_W)).reshape(-1)
    col = jnp.broadcast_to(jnp.arange(GRID_W, dtype=F32)[None, :], (rows, GRID_W)).reshape(-1)
    n_freq = MLA_ROPE // 4
    inv = ROPE_THETA ** (-jnp.arange(n_freq, dtype=F32) / n_freq)
    ang = jnp.concatenate([row[:, None] * inv, col[:, None] * inv], axis=-1)
    cos, sin = jnp.cos(ang), jnp.sin(ang)
    half = MLA_ROPE // 2
    z = lambda w: jnp.zeros((n_lat, w), F32)
    o = lambda w: jnp.ones((n_lat, w), F32)
    tail = LANE - MLA_QK
    cos_t = jnp.concatenate([o(MLA_NOPE), cos, cos, o(tail)], axis=-1)
    sa_t = jnp.concatenate([z(MLA_NOPE + half), sin, z(tail)], axis=-1)
    sb_t = jnp.concatenate([z(MLA_NOPE), -sin, z(half + tail)], axis=-1)
    ctx1 = jnp.ones((n_ctx, LANE), F32)
    ctx0 = jnp.zeros((n_ctx, LANE), F32)
    return (jnp.concatenate([ctx1, cos_t], 0), jnp.concatenate([ctx0, sa_t], 0),
            jnp.concatenate([ctx0, sb_t], 0))


def _pad_cols(a, width):
    return jnp.pad(a, [(0, 0)] * (a.ndim - 1) + [(0, width - a.shape[-1])])


def _layer_weights(w_in, w_uq, w_ukv, q_norm_w, k_norm_w, ml_conv_w, ml_wq, ml_wk, ml_gate_b,
                   gla_wa, gla_ba, w_out, w_grp, b_grp, w_erouter, b_erouter):
    d = w_in.shape[0]
    o = np.cumsum((0, Q_LORA, KV_LORA, MLA_ROPE, ML_W, ML_W, ML_W, 4 * ML_HEADS, GLA_HEADS * GLA_DK,
                   GLA_HEADS * GLA_DK, GLA_HEADS * GLA_DV, GLA_HEADS * GLA_DV, 2 * GLA_LR))
    seg = lambda j: w_in[:, o[j]:o[j + 1]]
    cq, ckv, kr, mx, mv, mo, mg, gq, gk, gv, gr, ga = (seg(j) for j in range(12))
    z = lambda w: jnp.zeros((d, w), F32)

    def small(di):
        return jnp.concatenate([mg[:, di * 8:(di + 1) * 8], ga[:, di * GLA_LR:(di + 1) * GLA_LR],
                                z(SM_KR - SM_GA - GLA_LR), kr, z(LANE - SM_KR - MLA_ROPE)], axis=-1)

    w_in_p = jnp.concatenate([cq, mx, mv, mo, gv, gr, ckv, gq, gk, small(0), small(1)], axis=-1).astype(BF16)

    wuq_p = _pad_cols(w_uq.reshape(Q_LORA, MLA_HEADS, MLA_QK), LANE).reshape(Q_LORA, -1).astype(BF16)
    ukv = w_ukv.reshape(KV_LORA, MLA_HEADS, MLA_NOPE + MLA_V)
    wuk_p = _pad_cols(ukv[..., :MLA_NOPE], LANE).reshape(KV_LORA, -1).astype(BF16)
    wuv_p = _pad_cols(ukv[..., MLA_NOPE:], LANE).reshape(KV_LORA, -1).astype(BF16)
    qn_p = _pad_cols(q_norm_w.reshape(1, -1), LANE)
    kn_p = _pad_cols(k_norm_w.reshape(1, -1), LANE)

    conv_w8 = jnp.pad(ml_conv_w, ((0, 8 - ML_CONV), (0, 0)))
    bd = lambda w: jax.scipy.linalg.block_diag(*[w[h] for h in range(ML_HEADS)])
    wq_bd = (bd(ml_wq) * (ML_DH ** -0.5)).astype(BF16)
    wk_bd = bd(ml_wk).astype(BF16)
    gate_b_p = _pad_cols(ml_gate_b.reshape(2, 1, 2 * ML_HEADS), LANE)

    wa_p = jnp.pad(gla_wa, ((0, 0), (SM_GA, LANE - SM_GA - GLA_LR), (0, 0)))
    ba_p = gla_ba.reshape(2, 1, -1)

    wr_p = _pad_cols(jnp.concatenate([w_grp, w_erouter], axis=-1), LANE)
    br_p = _pad_cols(jnp.concatenate([b_grp, b_erouter]).reshape(1, -1), LANE)
    return dict(w_in_p=w_in_p, wuq_p=wuq_p, wuk_p=wuk_p, wuv_p=wuv_p, qn_p=qn_p, kn_p=kn_p,
                conv_w8=conv_w8, wq_bd=wq_bd, wk_bd=wk_bd, gate_b_p=gate_b_p, wa_p=wa_p, ba_p=ba_p,
                w_out_b=w_out.astype(BF16), wr_p=wr_p, br_p=br_p)


def kernel(x, c, ctx, c_ctx, w_mod, b_mod, norm1_w, w_in, q_a_norm, w_uq, kv_a_norm, w_ukv,
           q_norm_w, k_norm_w, ml_conv_w, ml_conv_b, ml_wq, ml_wk, ml_gate_b, ml_norm_w, ml_skip,
           gla_wa, gla_ba, gla_norm_w, w_out, norm2_w, w_grp, b_grp, w_erouter, b_erouter,
           w_gate, w_up, w_down):
    b, s, d = x.shape
    n_ctx = ctx.shape[1]
    depth = w_mod.shape[0]
    assert n_ctx == TM and s % TM == 0 and b == 2

    cc = jnp.concatenate([c, c_ctx[None, :], jnp.zeros((8 - b - 1, d), F32)], axis=0)
    mods_all = _mods(cc, w_mod, b_mod).reshape(depth, 8, 6, d)
    cos_t, sa_t, sb_t = _rope_tables(n_ctx, s)
    xs = jnp.concatenate([ctx, x], axis=1)

    for l in range(depth):
        w = _layer_weights(w_in[l], w_uq[l], w_ukv[l], q_norm_w[l], k_norm_w[l], ml_conv_w[l],
                           ml_wq[l], ml_wk[l], ml_gate_b[l], gla_wa[l], gla_ba[l], w_out[l],
                           w_grp[l], b_grp[l], w_erouter[l], b_erouter[l])
        mods = mods_all[l]
        p = _inproj(xs, mods, norm1_w[l], w["w_in_p"])
        q, k, v = _mla_prep(p, q_a_norm[l], w["wuq_p"], kv_a_norm[l], w["wuk_p"], w["wuv_p"],
                            w["qn_p"], w["kn_p"], cos_t, sa_t, sb_t)
        a_lat = _attention_lat(q, k, v)
        a_ctx = _attention_ctx(q, k, v) if l < depth - 1 else jnp.zeros((b, TM, MLA_HEADS * MLA_V), BF16)
        xconv, mq, mk = _ml_prep(p, w["conv_w8"], ml_conv_b[l], w["wq_bd"], w["wk_bd"])
        mh = _ml_scan(mq, mk, p, w["gate_b_p"])
        go = _gla_scan(p, w["wa_p"], w["ba_p"])
        xs, h2, route = _outproj(a_ctx, a_lat, mh, xconv, p, go, xs, mods, ml_norm_w[l], ml_skip[l], gla_norm_w[l],
                                w["w_out_b"], norm2_w[l], w["wr_p"], w["br_p"])
        xs = _moe(h2, route, xs, mods, w_gate, w_up, w_down, l)
    return xs[:, n_ctx:, :]
```

```python
import functools

import jax
import jax.numpy as jnp
import numpy as np
from jax import lax
from jax.experimental import pallas as pl
from jax.experimental.pallas import tpu as pltpu

F32 = jnp.float32
BF16 = jnp.bfloat16
HIGHEST = lax.Precision.HIGHEST

EPS = 1e-6
GRID_W = 64
ROPE_THETA = 10000.0

MLA_HEADS = 8
MLA_NOPE = 64
MLA_ROPE = 32
MLA_QK = MLA_NOPE + MLA_ROPE
MLA_V = 64
Q_LORA = 256
KV_LORA = 128

ML_HEADS = 4
ML_DH = 64
ML_W = ML_HEADS * ML_DH
ML_CONV = 5

GLA_HEADS = 4
GLA_DK = 32
GLA_DV = 64
GLA_LR = 16
GLA_TAU = 16.0

N_GROUPS = 4
EXP_PER_GROUP = 8
N_EXPERTS = N_GROUPS * EXP_PER_GROUP
D_EXPERT = 256

LANE = 128
TM = 256
ML_CHUNK = 256
GLA_CHUNK = 128
GLA_SAFE_SPAN = 80.0
VMEM_LIMIT = 56 * 1024 * 1024

C_CQ, C_MX, C_MV, C_MO, C_GV, C_GR = 0, 256, 512, 768, 1024, 1280
C_CKV, C_GQ, C_GK, C_SMF, C_SMB = 1536, 1664, 1792, 1920, 2048
D_INP = 2176
SM_GATE = 0
SM_GA = 8
SM_KR = 64
R_GRP = 0
R_EXP = 4
RT_E1 = 0
RT_W1 = 2
GATHER_UNROLL = 8


def _cparams(sem):
    return pltpu.CompilerParams(dimension_semantics=sem, vmem_limit_bytes=VMEM_LIMIT)


def _silu(x):
    return x * jax.nn.sigmoid(x)


def _log_sigmoid(x):
    return -(jnp.maximum(-x, 0.0) + jnp.log1p(jnp.exp(-jnp.abs(x))))


def _nt_dot(a, b, **kw):
    return lax.dot_general(a, b, (((1,), (1,)), ((), ())), preferred_element_type=F32, **kw)


def _split3(x):
    hi = x.astype(BF16)
    r = x - hi.astype(F32)
    mid = r.astype(BF16)
    return hi, mid, (r - mid.astype(F32)).astype(BF16)


def _dot_sel(sel, x, nt=False, right=False):
    s = sel.astype(BF16)
    dot = _nt_dot if nt else functools.partial(jnp.dot, preferred_element_type=F32)
    return sum((dot(piece, s) if right else dot(s, piece)) for piece in _split3(x))


def _tn_dot(a, b, **kw):
    return lax.dot_general(a, b, (((0,), (0,)), ((), ())), preferred_element_type=F32, **kw)


def _mods_kernel(cc_ref, w_ref, b_ref, o_ref):
    a = _silu(cc_ref[...])
    o_ref[0] = jnp.dot(a, w_ref[0], precision=HIGHEST, preferred_element_type=F32) + b_ref[0]


def _mods(cc, w_mod, b_mod):
    depth, d, d6 = w_mod.shape
    nb = 1536
    return pl.pallas_call(
        _mods_kernel,
        grid=(depth, d6 // nb),
        in_specs=[
            pl.BlockSpec((8, d), lambda l, j: (0, 0)),
            pl.BlockSpec((1, d, nb), lambda l, j: (l, 0, j)),
            pl.BlockSpec((1, 1, nb), lambda l, j: (l, 0, j)),
        ],
        out_specs=pl.BlockSpec((1, 8, nb), lambda l, j: (l, 0, j)),
        out_shape=jax.ShapeDtypeStruct((depth, 8, d6), F32),
        compiler_params=_cparams(("arbitrary", "arbitrary")),
        name="adaln_mods",
    )(cc, w_mod, b_mod.reshape(depth, 1, d6))


def _mod_row(b, i):
    return jnp.where(i == 0, 2, b)


def _inproj_kernel(x_ref, mod_ref, nw_ref, w_ref, o_ref):
    x = x_ref[0]
    y = x * lax.rsqrt(jnp.mean(x * x, axis=-1, keepdims=True) + EPS) * nw_ref[...]
    mod = mod_ref[0]
    h = y * (1.0 + mod[1:2]) + mod[0:1]
    o_ref[0] = jnp.dot(h.astype(BF16), w_ref[...], preferred_element_type=F32)


def _inproj(xs, mods, norm_w, w_in_p):
    b, t, d = xs.shape
    nt = t // TM
    return pl.pallas_call(
        _inproj_kernel,
        grid=(b, nt),
        in_specs=[
            pl.BlockSpec((1, TM, d), lambda bi, i: (bi, i, 0)),
            pl.BlockSpec((1, 6, d), lambda bi, i: (_mod_row(bi, i), 0, 0)),
            pl.BlockSpec((1, d), lambda bi, i: (0, 0)),
            pl.BlockSpec((d, D_INP), lambda bi, i: (0, 0)),
        ],
        out_specs=pl.BlockSpec((1, TM, D_INP), lambda bi, i: (bi, i, 0)),
        out_shape=jax.ShapeDtypeStruct((b, t, D_INP), F32),
        compiler_params=_cparams(("parallel", "parallel")),
        name="in_proj",
    )(xs, mods, norm_w.reshape(1, d), w_in_p)


def _mla_prep_kernel(cq_ref, ckv_ref, sm_ref, qan_ref, wuq_ref, kvan_ref, wuk_ref, wuv_ref,
                     qn_ref, kn_ref, cos_ref, sa_ref, sb_ref, q_ref, k_ref, v_ref):
    cq = cq_ref[0]
    cqn = cq * lax.rsqrt(jnp.mean(cq * cq, axis=-1, keepdims=True) + EPS) * qan_ref[...]
    qall = jnp.dot(cqn.astype(BF16), wuq_ref[...], preferred_element_type=F32)
    ckv = ckv_ref[0]
    ckvn = (ckv * lax.rsqrt(jnp.mean(ckv * ckv, axis=-1, keepdims=True) + EPS)
            * kvan_ref[...]).astype(BF16)
    kall = jnp.dot(ckvn, wuk_ref[...], preferred_element_type=F32)
    vall = jnp.dot(ckvn, wuv_ref[...], preferred_element_type=F32)
    lane = lax.broadcasted_iota(jnp.int32, (1, LANE), 1)
    kr = jnp.where((lane >= SM_KR) & (lane < SM_KR + MLA_ROPE), sm_ref[0], 0.0)
    cos, sin = cos_ref[...], sa_ref[...] - sb_ref[...]
    r_i = lax.broadcasted_iota(jnp.int32, (LANE, LANE), 0)
    c_i = lax.broadcasted_iota(jnp.int32, (LANE, LANE), 1)
    half = MLA_ROPE // 2
    first = (c_i >= MLA_NOPE) & (c_i < MLA_NOPE + half)
    second = (c_i >= MLA_NOPE + half) & (c_i < MLA_QK)
    rot = jnp.where(first & (r_i == c_i + half), -1.0,
                    jnp.where(second & (r_i == c_i - half), 1.0, 0.0)).astype(BF16)

    def rope(x):
        return x * cos + jnp.dot(x.astype(BF16), rot, preferred_element_type=F32) * sin

    def head_norm(x, w):
        return x * lax.rsqrt(jnp.sum(x * x, axis=-1, keepdims=True) * (1.0 / MLA_QK) + EPS) * w

    for h in range(MLA_HEADS):
        sl = slice(h * LANE, (h + 1) * LANE)
        qh = rope(head_norm(qall[:, sl], qn_ref[...]))
        q_ref[0, h] = (qh * Q_SCALE).astype(BF16)
        kh = rope(head_norm(kall[:, sl] + kr, kn_ref[...]))
        k_ref[0, h] = jnp.where(lane == ATT_SHIFT_LANE, 1.0, kh).astype(BF16)
        v_ref[0, h] = jnp.where(lane == MLA_V, 1.0, vall[:, sl]).astype(BF16)


def _mla_prep(p, q_a_norm, wuq_p, kv_a_norm, wuk_p, wuv_p, qn_p, kn_p, cos_t, sa_t, sb_t):
    b, t, _ = p.shape
    nt = t // TM
    hw = MLA_HEADS * LANE
    full = lambda shape: pl.BlockSpec(shape, lambda bi, i: (0,) * len(shape))
    tab = pl.BlockSpec((TM, LANE), lambda bi, i: (i, 0))
    out = pl.BlockSpec((1, MLA_HEADS, TM, LANE), lambda bi, i: (bi, 0, i, 0))
    shp = jax.ShapeDtypeStruct((b, MLA_HEADS, t, LANE), BF16)
    return pl.pallas_call(
        _mla_prep_kernel,
        grid=(b, nt),
        in_specs=[
            pl.BlockSpec((1, TM, Q_LORA), lambda bi, i: (bi, i, C_CQ // Q_LORA)),
            pl.BlockSpec((1, TM, KV_LORA), lambda bi, i: (bi, i, C_CKV // KV_LORA)),
            pl.BlockSpec((1, TM, LANE), lambda bi, i: (bi, i, C_SMF // LANE)),
            full((1, Q_LORA)), full((Q_LORA, hw)), full((1, KV_LORA)),
            full((KV_LORA, hw)), full((KV_LORA, hw)), full((1, LANE)), full((1, LANE)),
            tab, tab, tab,
        ],
        out_specs=[out, out, out],
        out_shape=[shp, shp, shp],
        compiler_params=_cparams(("parallel", "parallel")),
        name="mla_prep",
    )(p, p, p, q_a_norm.reshape(1, -1), wuq_p, kv_a_norm.reshape(1, -1), wuk_p, wuv_p,
      qn_p, kn_p, cos_t, sa_t, sb_t)


Q_SCALE = float(MLA_QK ** -0.5 * np.log2(np.e))
ATT_HP = 2
ATT_NQ = 4
ATT_TK = 2048
ATT_SHIFT_LANE = MLA_QK
ATT_SAFE_MAX = 2.0 ** 100


def _softmax_step(q, kb, vb, m, acc):
    s = _nt_dot(q, kb)
    m_new = jnp.max(s, axis=-1, keepdims=True)
    if m is None:
        return m_new, jnp.dot(jnp.exp2((s - m_new).astype(BF16)), vb, preferred_element_type=F32)
    m_new = jnp.maximum(m, m_new)
    p = jnp.exp2((s - m_new).astype(BF16))
    return m_new, acc * jnp.exp2(m - m_new) + jnp.dot(p, vb, preferred_element_type=F32)


def _attn_ctx_kernel(q_ref, k_ref, v_ref, o_ref):
    outs = []
    for h in range(MLA_HEADS):
        _, acc = _softmax_step(q_ref[0, h], k_ref[0, h], v_ref[0, h], None, None)
        outs.append(acc[:, :MLA_V] / acc[:, MLA_V:MLA_V + 1])
    o_ref[0] = jnp.concatenate(outs, axis=-1).astype(BF16)


def _attention_ctx(q, k, v):
    b, h, _, _ = q.shape
    blk = pl.BlockSpec((1, h, TM, LANE), lambda bi: (bi, 0, 0, 0))
    return pl.pallas_call(
        _attn_ctx_kernel,
        grid=(b,),
        in_specs=[blk, blk, blk],
        out_specs=pl.BlockSpec((1, TM, h * MLA_V), lambda bi: (bi, 0, 0)),
        out_shape=jax.ShapeDtypeStruct((b, TM, h * MLA_V), BF16),
        compiler_params=_cparams(("parallel",)),
        name="mla_attention_ctx",
    )(q, k, v)


def _attn_lat_kernel(*refs, n_blk):
    q_refs, (k_ref, v_ref, o_ref, q_buf, qs_buf) = refs[:ATT_NQ], refs[ATT_NQ:]
    lane = lax.broadcasted_iota(jnp.int32, (1, LANE), 1)

    def kv_block(hh, j):
        off = pl.multiple_of(TM + j * ATT_TK, TM)
        return k_ref[0, hh, pl.ds(off, ATT_TK), :], v_ref[0, hh, pl.ds(off, ATT_TK), :]

    def finish(accs):
        outs = [acc[:, :MLA_V] / acc[:, MLA_V:MLA_V + 1] for acc in accs]
        o_ref[0] = jnp.concatenate(outs, axis=-1).astype(BF16)

    accs = []
    for hh in range(ATT_HP):
        q = jnp.concatenate([qr[0, hh] for qr in q_refs], axis=0)
        q_buf[hh] = q
        kb, vb = k_ref[0, hh, 0:TM, :], v_ref[0, hh, 0:TM, :]
        shift = jnp.max(_nt_dot(q, kb), axis=-1, keepdims=True).astype(BF16)
        qs_buf[hh] = jnp.where(lane == ATT_SHIFT_LANE, -shift, q)
        accs.append(jnp.dot(jnp.exp2(_nt_dot(qs_buf[hh], kb).astype(BF16)), vb, preferred_element_type=F32))

    def fast_body(j, accs):
        new = []
        for hh in range(ATT_HP):
            kb, vb = kv_block(hh, j)
            p = jnp.exp2(_nt_dot(qs_buf[hh], kb).astype(BF16))
            new.append(accs[hh] + jnp.dot(p, vb, preferred_element_type=F32))
        return tuple(new)

    accs = lax.fori_loop(0, n_blk, fast_body, tuple(accs))
    bad = sum(jnp.max(jnp.where(jnp.abs(acc) < ATT_SAFE_MAX, 0.0, 1.0)) for acc in accs)

    @pl.when(bad == 0.0)
    def _():
        finish(accs)

    @pl.when(bad != 0.0)
    def _():
        init = []
        for hh in range(ATT_HP):
            init += _softmax_step(q_buf[hh], k_ref[0, hh, 0:TM, :], v_ref[0, hh, 0:TM, :], None, None)

        def body(j, carry):
            new = []
            for hh in range(ATT_HP):
                new += _softmax_step(q_buf[hh], *kv_block(hh, j), carry[2 * hh], carry[2 * hh + 1])
            return tuple(new)

        carry = lax.fori_loop(0, n_blk, body, tuple(init))
        finish(carry[1::2])


def _attention_lat(q, k, v):
    b, h, t, _ = q.shape
    s = t - TM
    tq = ATT_NQ * TM
    assert s % ATT_TK == 0 and s % tq == 0
    kv = pl.BlockSpec((1, ATT_HP, t, LANE), lambda bi, hp, i: (bi, hp, 0, 0))
    qs = [pl.BlockSpec((1, ATT_HP, TM, LANE), lambda bi, hp, i, u=u: (bi, hp, 1 + ATT_NQ * i + u, 0))
          for u in range(ATT_NQ)]
    return pl.pallas_call(
        functools.partial(_attn_lat_kernel, n_blk=s // ATT_TK),
        grid=(b, h // ATT_HP, s // tq),
        in_specs=qs + [kv, kv],
        out_specs=pl.BlockSpec((1, tq, ATT_HP * MLA_V), lambda bi, hp, i: (bi, i, hp)),
        out_shape=jax.ShapeDtypeStruct((b, s, h * MLA_V), BF16),
        scratch_shapes=[pltpu.VMEM((ATT_HP, tq, LANE), BF16), pltpu.VMEM((ATT_HP, tq, LANE), BF16)],
        compiler_params=_cparams(("parallel", "parallel", "arbitrary")),
        name="mla_attention",
    )(*([q] * ATT_NQ), k, v)


def _ml_prep_kernel(x_ref, prev_ref, next_ref, cw_ref, cb_ref, wq_ref, wk_ref,
                    xc_ref, q_ref, k_ref, *, n_tiles):
    i = pl.program_id(1)
    x = x_ref[0]
    prev = jnp.where(i <= 1, 0.0, prev_ref[0])
    nxt = jnp.where((i == 0) | (i == n_tiles - 1), 0.0, next_ref[0])
    ext = jnp.concatenate([prev, x, nxt], axis=0)
    n_ext = TM + 16
    cw = cw_ref[...]
    acc = jnp.zeros((TM, ML_W), F32) + cb_ref[...]
    for kk in range(ML_CONV):
        sh = (ML_CONV // 2 - kk) % n_ext
        shifted = ext if sh == 0 else pltpu.roll(ext, sh, 0)
        acc = acc + cw[kk:kk + 1] * shifted[8:8 + TM]
    xc = _silu(acc)
    xc_ref[0] = xc
    xb = xc.astype(BF16)
    q_ref[0] = jnp.dot(xb, wq_ref[...], preferred_element_type=F32).astype(BF16)
    k_ref[0] = jnp.dot(xb, wk_ref[...], preferred_element_type=F32).astype(BF16)


def _ml_prep(p, conv_w8, conv_b, wq_bd, wk_bd):
    b, t, _ = p.shape
    nt = t // TM
    r8 = TM // 8
    full = lambda shape: pl.BlockSpec(shape, lambda bi, i: (0,) * len(shape))
    cb = C_MX // ML_W
    blk = pl.BlockSpec((1, TM, ML_W), lambda bi, i: (bi, i, 0))
    return pl.pallas_call(
        functools.partial(_ml_prep_kernel, n_tiles=nt),
        grid=(b, nt),
        in_specs=[
            pl.BlockSpec((1, TM, ML_W), lambda bi, i: (bi, i, cb)),
            pl.BlockSpec((1, 8, ML_W), lambda bi, i: (bi, jnp.maximum(i * r8 - 1, 0), cb)),
            pl.BlockSpec((1, 8, ML_W), lambda bi, i: (bi, jnp.minimum((i + 1) * r8, nt * r8 - 1), cb)),
            full((8, ML_W)), full((1, ML_W)), full((ML_W, ML_W)), full((ML_W, ML_W)),
        ],
        out_specs=[blk, blk, blk],
        out_shape=[jax.ShapeDtypeStruct((b, t, ML_W), F32),
                   jax.ShapeDtypeStruct((b, t, ML_W), BF16),
                   jax.ShapeDtypeStruct((b, t, ML_W), BF16)],
        compiler_params=_cparams(("parallel", "parallel")),
        name="mlstm_prep",
    )(p, p, p, conv_w8, conv_b.reshape(1, ML_W), wq_bd, wk_bd)


def _scan_chunk(d, step, n_ctx_chunks, n_chunks):
    bwd = jnp.where(step < n_ctx_chunks, n_ctx_chunks - 1 - step, n_chunks - 1 - (step - n_ctx_chunks))
    return jnp.where(d == 0, step, bwd)


def _ml_chunk(d, q, k, v, g, s_ref, m_ref, base):
    L = ML_CHUNK
    row = lax.broadcasted_iota(jnp.int32, (L, L), 0)
    col = lax.broadcasted_iota(jnp.int32, (L, L), 1)
    mask = col <= row if d == 0 else col >= row
    tri = mask.astype(F32)
    lane = lax.broadcasted_iota(jnp.int32, (1, LANE), 1)
    eye8 = (lax.broadcasted_iota(jnp.int32, (8, LANE), 0)
            == lax.broadcasted_iota(jnp.int32, (8, LANE), 1)).astype(F32)

    lf = _log_sigmoid(g)
    bc = jnp.dot(tri, lf, precision=HIGHEST, preferred_element_type=F32)
    g_rows = _nt_dot(eye8, g, precision=HIGHEST)
    bc = pltpu.roll(bc, LANE - ML_HEADS, 1)
    bc_rows = _nt_dot(eye8, bc, precision=HIGHEST)

    outs = []
    for pair in range(ML_HEADS // 2):
        sl = slice(pair * LANE, (pair + 1) * LANE)
        q_blk, k_blk, v_blk = q[:, sl], k[:, sl], v[:, sl]
        pair_out = []
        for sub in range(2):
            h = pair * 2 + sub
            head_lanes = (lane >= sub * ML_DH) & (lane < (sub + 1) * ML_DH)
            qh = jnp.where(head_lanes, q_blk, jnp.zeros_like(q_blk))
            vs = v_blk if sub == 0 else pltpu.roll(v_blk, ML_DH, 1)
            v_ext = jnp.where(lane < ML_DH, vs, jnp.where(lane == ML_DH, 1.0, 0.0)).astype(BF16)

            li_c = g[:, SM_GATE + h:SM_GATE + h + 1]
            bc_c = bc[:, SM_GATE + h:SM_GATE + h + 1]
            li_r = g_rows[h:h + 1, :]
            bc_r = bc_rows[h:h + 1, :]
            m_st = m_ref[base + h][0:1, 0:1]

            c_row = jnp.where(mask, li_r - bc_r, -jnp.inf)
            m_rel = jnp.maximum(m_st, jnp.max(c_row, axis=-1, keepdims=True))
            m_t = bc_c + m_rel
            e = jnp.exp(c_row - m_rel)
            s = (_nt_dot(qh, k_blk) * e).astype(BF16)
            tot = (jnp.dot(s, v_ext, preferred_element_type=F32)
                   + jnp.exp(m_st - m_rel) * jnp.dot(qh, s_ref[base + h].astype(BF16),
                                                     preferred_element_type=F32))
            den = tot[:, ML_DH:ML_DH + 1]
            pair_out.append(tot / jnp.maximum(jnp.abs(den), jnp.exp(-m_t)))

            b_end = bc_c[L - 1:L] if d == 0 else bc_c[0:1]
            g_col = b_end - bc_c + li_c
            m_new = jnp.maximum(b_end + m_st, jnp.max(g_col, axis=0, keepdims=True))
            kw = jnp.where(head_lanes, k_blk.astype(F32) * jnp.exp(g_col - m_new), 0.0).astype(BF16)
            s_ref[base + h] = jnp.exp(b_end + m_st - m_new) * s_ref[base + h] + _tn_dot(kw, v_ext)
            m_ref[base + h] = jnp.broadcast_to(m_new, m_ref.shape[1:])
        outs.append(jnp.where(lane < ML_DH, pair_out[0], pltpu.roll(pair_out[1], ML_DH, 1)))
    return jnp.concatenate(outs, axis=-1)


def _ml_scan_kernel(qf_ref, kf_ref, vf_ref, smf_ref, qb_ref, kb_ref, vb_ref, smb_ref, gb_ref,
                    hf_ref, hb_ref, s_ref, m_ref):
    @pl.when(pl.program_id(0) == 0)
    def _():
        s_ref[...] = jnp.zeros_like(s_ref)
        m_ref[...] = jnp.zeros_like(m_ref)

    nb = qf_ref.shape[0]
    streams = ((0, qf_ref, kf_ref, vf_ref, smf_ref, hf_ref), (1, qb_ref, kb_ref, vb_ref, smb_ref, hb_ref))
    for d, q_ref, k_ref, v_ref, sm_ref, h_ref in streams:
        for bi in range(nb):
            h_ref[bi] = _ml_chunk(d, q_ref[bi], k_ref[bi], v_ref[bi], sm_ref[bi] + gb_ref[d],
                                  s_ref, m_ref, (d * nb + bi) * ML_HEADS)


def _ml_scan(q, k, p, gate_b_p):
    b, t, _ = q.shape
    nc = t // ML_CHUNK
    chunk = functools.partial(_scan_chunk, n_ctx_chunks=TM // ML_CHUNK, n_chunks=nc)
    blk = lambda d, w, cb: pl.BlockSpec((b, ML_CHUNK, w), lambda s: (0, chunk(d, s), cb))
    stream = lambda d: [blk(d, ML_W, 0), blk(d, ML_W, 0), blk(d, ML_W, C_MV // ML_W),
                        blk(d, LANE, C_SMF // LANE + d)]
    shp = jax.ShapeDtypeStruct((b, t, ML_W), F32)
    return pl.pallas_call(
        _ml_scan_kernel,
        grid=(nc,),
        in_specs=stream(0) + stream(1) + [pl.BlockSpec((2, 1, LANE), lambda s: (0, 0, 0))],
        out_specs=[blk(0, ML_W, 0), blk(1, ML_W, 0)],
        out_shape=[shp, shp],
        scratch_shapes=[pltpu.VMEM((2 * b * ML_HEADS, LANE, LANE), F32),
                        pltpu.VMEM((2 * b * ML_HEADS, 8, LANE), F32)],
        compiler_params=_cparams(("arbitrary",)),
        name="mlstm_scan",
    )(q, k, p, p, q, k, p, p, gate_b_p)


def _gla_mask(d):
    row = lax.broadcasted_iota(jnp.int32, (GLA_CHUNK, GLA_CHUNK), 0)
    col = lax.broadcasted_iota(jnp.int32, (GLA_CHUNK, GLA_CHUNK), 1)
    return col <= row if d == 0 else col >= row


def _gla_gates(d, sm, wa, ba):
    pre = jnp.dot(sm.astype(BF16), wa.astype(BF16), preferred_element_type=F32) + ba
    loga = _log_sigmoid(pre) * (1.0 / GLA_TAU)
    bc = _dot_sel(_gla_mask(d).astype(F32), loga)
    return loga, bc, jnp.max(jnp.abs(bc - bc[GLA_CHUNK // 2 - 1:GLA_CHUNK // 2]))


def _gla_chunk(d, q_ref, k_ref, v_ref, o_ref, bi, r0, loga, bc, s_ref, si, la_ref, token_form):
    L = GLA_CHUNK
    rows = slice(r0, r0 + L)
    mask = _gla_mask(d)
    lane_k = lax.broadcasted_iota(jnp.int32, (1, GLA_HEADS * GLA_DK), 1)
    lane_v = lax.broadcasted_iota(jnp.int32, (1, GLA_HEADS * GLA_DV), 1)
    eye = (lax.broadcasted_iota(jnp.int32, (LANE, LANE), 0)
           == lax.broadcasted_iota(jnp.int32, (LANE, LANE), 1)).astype(F32)
    blockdiag = (lax.broadcasted_iota(jnp.int32, s_ref.shape[1:], 0) // GLA_DK
                 == lax.broadcasted_iota(jnp.int32, s_ref.shape[1:], 1) // GLA_DV)
    scale = GLA_DK ** -0.5

    ref_row = bc[L // 2 - 1:L // 2]

    if not token_form:
        b_end = bc[L - 1:L] if d == 0 else bc[0:1]
        q = q_ref[bi, rows] * scale
        k = k_ref[bi, rows]
        v = v_ref[bi, rows].astype(BF16)
        q_in = (q * jnp.exp(bc - ref_row))
        k_in = (k * jnp.exp(ref_row - bc)).astype(BF16)
        q_st = (q * jnp.exp(bc)).astype(BF16)
        k_st = (k * jnp.exp(b_end - bc)).astype(BF16)
        s_old = s_ref[si]
        o = jnp.dot(q_st, jnp.where(blockdiag, s_old, 0.0).astype(BF16), preferred_element_type=F32)
        for h in range(GLA_HEADS):
            qh = jnp.where(lane_k // GLA_DK == h, q_in, 0.0).astype(BF16)
            att = jnp.where(mask, _nt_dot(qh, k_in), 0.0).astype(BF16)
            oh = jnp.dot(att, v, preferred_element_type=F32)
            o = o + jnp.where(lane_v // GLA_DV == h, oh, 0.0)
        o_ref[bi, rows] = o
        decay_col = jnp.exp(_dot_sel(eye, jnp.broadcast_to(b_end, (8, LANE)), nt=True)[:, 0:1])
        s_ref[si] = decay_col * s_old + _tn_dot(k_st, v)

    else:
        la_ref[...] = loga
        sub = lax.broadcasted_iota(jnp.int32, (8, LANE), 0)

        def body(i, carry):
            t = i if d == 0 else L - 1 - i
            la_t = la_ref[pl.ds(t, 1), :]
            k_t = k_ref[bi, pl.ds(r0 + t, 1), :]
            q_t = q_ref[bi, pl.ds(r0 + t, 1), :] * scale
            v_t = v_ref[bi, pl.ds(r0 + t, 1), :]
            stacked = jnp.where(sub == 0, la_t, jnp.where(sub == 1, k_t, 0.0))
            cols = _dot_sel(eye, stacked, nt=True)
            s_new = jnp.exp(cols[:, 0:1]) * s_ref[si] + cols[:, 1:2] * v_t
            s_ref[si] = s_new
            o_t = jnp.dot(jnp.broadcast_to(q_t, (8, LANE)), jnp.where(blockdiag, s_new, 0.0),
                          precision=HIGHEST, preferred_element_type=F32)
            o_ref[bi, pl.ds(r0 + t, 1), :] = o_t[0:1]
            return carry

        lax.fori_loop(0, L, body, 0)


def _gla_scan_kernel(qf_ref, kf_ref, vf_ref, smf_ref, qb_ref, kb_ref, vb_ref, smb_ref, wa_ref, ba_ref,
                     of_ref, ob_ref, s_ref, la_ref):
    @pl.when(pl.program_id(0) == 0)
    def _():
        s_ref[...] = jnp.zeros_like(s_ref)

    nb = qf_ref.shape[0]
    n_sub = TM // GLA_CHUNK
    streams = ((0, qf_ref, kf_ref, vf_ref, smf_ref, of_ref), (1, qb_ref, kb_ref, vb_ref, smb_ref, ob_ref))
    chunks, span = [], 0.0
    for d, q_ref, k_ref, v_ref, sm_ref, o_ref in streams:
        for bi in range(nb):
            for c in (range(n_sub) if d == 0 else reversed(range(n_sub))):
                r0 = c * GLA_CHUNK
                loga, bc, chunk_span = _gla_gates(d, sm_ref[bi, r0:r0 + GLA_CHUNK], wa_ref[d], ba_ref[d])
                span = jnp.maximum(span, chunk_span)
                chunks.append((d, q_ref, k_ref, v_ref, o_ref, bi, r0, loga, bc, s_ref, d * nb + bi, la_ref))

    in_range = span < GLA_SAFE_SPAN
    for token_form, pred in ((False, in_range), (True, jnp.logical_not(in_range))):
        @pl.when(pred)
        def _(token_form=token_form):
            for chunk in chunks:
                _gla_chunk(*chunk, token_form)


def _gla_scan(p, wa_p, ba_p):
    b, t, _ = p.shape
    nc = t // TM
    chunk = functools.partial(_scan_chunk, n_ctx_chunks=1, n_chunks=nc)
    kw, vw = GLA_HEADS * GLA_DK, GLA_HEADS * GLA_DV
    blk = lambda d, w, col: pl.BlockSpec((b, TM, w), lambda s: (0, chunk(d, s), col // w))
    stream = lambda d: [blk(d, kw, C_GQ), blk(d, kw, C_GK), blk(d, vw, C_GV), blk(d, LANE, C_SMF + d * LANE)]
    shp = jax.ShapeDtypeStruct((b, t, vw), F32)
    return pl.pallas_call(
        _gla_scan_kernel,
        grid=(nc,),
        in_specs=stream(0) + stream(1) + [pl.BlockSpec((2, LANE, kw), lambda s: (0, 0, 0)),
                                          pl.BlockSpec((2, 1, kw), lambda s: (0, 0, 0))],
        out_specs=[blk(0, vw, 0), blk(1, vw, 0)],
        out_shape=[shp, shp],
        scratch_shapes=[pltpu.VMEM((2 * b, kw, vw), F32), pltpu.VMEM((GLA_CHUNK, kw), F32)],
        compiler_params=_cparams(("arbitrary",)),
        name="gla_scan",
    )(p, p, p, p, p, p, p, p, wa_p, ba_p)


OUTPROJ_TILES = 2
OUTPROJ_N_TOK = 11


def _outproj_kernel(*refs, n_tiles):
    n_in = OUTPROJ_TILES * OUTPROJ_N_TOK
    shared = refs[n_in:n_in + 7]
    outs = refs[n_in + 7:]
    for u in range(OUTPROJ_TILES):
        tile = jnp.minimum(OUTPROJ_TILES * pl.program_id(1) + u, n_tiles - 1)
        _outproj_tile(tile == 0, slice(u * TM, (u + 1) * TM),
                      *refs[u * OUTPROJ_N_TOK:(u + 1) * OUTPROJ_N_TOK], *shared, *outs)


def _outproj_tile(is_ctx, rows, ac_ref, al_ref, mhf_ref, mhb_ref, xc_ref, mo_ref, gof_ref, gob_ref, gr_ref, x_ref,
                  mod_ref, mnw_ref, msk_ref, gnw_ref, wout_ref, n2w_ref, wr_ref, br_ref,
                  xo_ref, h2_ref, route_ref):
    grp64 = (lax.broadcasted_iota(jnp.int32, (ML_W, ML_W), 0) // ML_DH
             == lax.broadcasted_iota(jnp.int32, (ML_W, ML_W), 1) // ML_DH).astype(F32) * (1.0 / ML_DH)

    def head_norm(x, w):
        ms = _dot_sel(grp64, x * x, right=True)
        return x * lax.rsqrt(ms + EPS) * w

    m_l = jax.nn.sigmoid(mo_ref[0]) * (head_norm(mhf_ref[0] + mhb_ref[0], mnw_ref[...])
                                       + msk_ref[...] * xc_ref[0])
    g_l = head_norm(gof_ref[0] + gob_ref[0], gnw_ref[...]) * _silu(gr_ref[0])
    na = MLA_HEADS * MLA_V
    a = jnp.where(is_ctx, ac_ref[0], al_ref[0])
    res = (jnp.dot(a, wout_ref[0:na], preferred_element_type=F32)
           + jnp.dot(m_l.astype(BF16), wout_ref[na:na + ML_W], preferred_element_type=F32)
           + jnp.dot(g_l.astype(BF16), wout_ref[na + ML_W:], preferred_element_type=F32))
    mod = mod_ref[0]
    x = x_ref[0] + mod[2:3] * res
    xo_ref[0, rows] = x
    h2 = (x * lax.rsqrt(jnp.mean(x * x, axis=-1, keepdims=True) + EPS) * n2w_ref[...]
          * (1.0 + mod[4:5]) + mod[3:4])
    half = h2.shape[-1] // 2
    h2_ref[0, rows] = _pack_bf16_pair(h2[:, :half], h2[:, half:])

    h_hi, h_lo, _ = _split3(h2)
    w_hi, w_lo, _ = _split3(wr_ref[...])
    dot = functools.partial(jnp.dot, preferred_element_type=F32)
    logits = dot(h_hi, w_hi) + dot(h_hi, w_lo) + dot(h_lo, w_hi) + br_ref[...]
    lane = lax.broadcasted_iota(jnp.int32, (1, LANE), 1)
    lane_f = lane.astype(F32)
    neg = -jnp.inf
    gl = jnp.where(lane < R_EXP, logits, neg)
    gmax = jnp.max(gl, axis=-1, keepdims=True)
    g_w = 1.0 / jnp.sum(jnp.exp(gl - gmax), axis=-1, keepdims=True)
    g_i = jnp.min(jnp.where(gl == gmax, lane_f, float(LANE)), axis=-1, keepdims=True)
    grp_of_lane = ((lane - R_EXP) // EXP_PER_GROUP).astype(F32)
    in_grp = (lane >= R_EXP) & (lane < R_EXP + N_EXPERTS) & (grp_of_lane == g_i)
    el = jnp.where(in_grp, logits, neg)
    m1 = jnp.max(el, axis=-1, keepdims=True)
    i1 = jnp.min(jnp.where(el == m1, lane_f, float(LANE)), axis=-1, keepdims=True)
    el2 = jnp.where(lane_f == i1, neg, el)
    m2 = jnp.max(el2, axis=-1, keepdims=True)
    i2 = jnp.min(jnp.where(el2 == m2, lane_f, float(LANE)), axis=-1, keepdims=True)
    p2 = jnp.exp(m2 - m1)
    w1 = g_w / (1.0 + p2)
    route_ref[0, rows] = jnp.where(lane == RT_E1, i1 - R_EXP, jnp.where(lane == RT_E1 + 1, i2 - R_EXP,
                             jnp.where(lane == RT_W1, w1, jnp.where(lane == RT_W1 + 1, p2 * w1, 0.0))))


def _outproj(a_ctx, a_lat, mh, xconv, p, go, xs, mods, ml_norm_w, ml_skip, gla_norm_w, w_out_b, norm2_w, wr_p, br_p):
    b, t, d = xs.shape
    nt = t // TM
    full = lambda shape: pl.BlockSpec(shape, lambda bi, i: (0,) * len(shape))
    na = MLA_HEADS * MLA_V

    out = lambda w: pl.BlockSpec((1, OUTPROJ_TILES * TM, w), lambda bi, i: (bi, i, 0))

    def tile_specs(u):
        tile = lambda i: jnp.minimum(OUTPROJ_TILES * i + u, nt - 1)
        tok = lambda w, cb=0: pl.BlockSpec((1, TM, w), lambda bi, i: (bi, tile(i), cb))
        ins = [pl.BlockSpec((1, TM, na), lambda bi, i: (bi, 0, 0)),
               pl.BlockSpec((1, TM, na), lambda bi, i: (bi, jnp.maximum(tile(i) - 1, 0), 0)),
               tok(ML_W), tok(ML_W), tok(ML_W), tok(ML_W, C_MO // ML_W),
               tok(ML_W), tok(ML_W), tok(ML_W, C_GR // ML_W), tok(d),
               pl.BlockSpec((1, 6, d), lambda bi, i: (_mod_row(bi, tile(i)), 0, 0))]
        return ins

    specs = [tile_specs(u) for u in range(OUTPROJ_TILES)]
    tok_args = (a_ctx, a_lat, mh[0], mh[1], xconv, p, go[0], go[1], p, xs, mods)
    assert len(tok_args) == OUTPROJ_N_TOK
    shapes = [jax.ShapeDtypeStruct((b, t, d), F32), jax.ShapeDtypeStruct((b, t, d // 2), jnp.uint32),
              jax.ShapeDtypeStruct((b, t, LANE), F32)]
    outs = pl.pallas_call(
        functools.partial(_outproj_kernel, n_tiles=nt),
        grid=(b, pl.cdiv(nt, OUTPROJ_TILES)),
        in_specs=sum(specs, []) + [
            full((1, ML_W)), full((1, ML_W)), full((1, ML_W)), full((d, d)), full((1, d)),
            full((d, LANE)), full((1, LANE))],
        out_specs=[out(d), out(d // 2), out(LANE)],
        out_shape=shapes,
        compiler_params=_cparams(("parallel", "arbitrary")),
        name="out_proj_router",
    )(*(tok_args * OUTPROJ_TILES), ml_norm_w.reshape(1, -1), ml_skip.reshape(1, -1),
      gla_norm_w.reshape(1, -1), w_out_b, norm2_w.reshape(1, -1), wr_p, br_p)
    return outs


def _dispatch(route, n_tiles):
    n = route.shape[0]
    flat = route[:, RT_E1:RT_E1 + 2].astype(jnp.int32).reshape(-1)
    onehot = (flat[:, None] == jnp.arange(N_EXPERTS, dtype=jnp.int32)[None, :]).astype(jnp.int32)
    csum = jnp.cumsum(onehot, axis=0)
    rank = jnp.sum(csum * onehot, axis=1) - 1
    padded = (csum[-1] + TM - 1) // TM * TM
    ends = jnp.cumsum(padded)
    pos = (ends - padded)[flat] + rank
    tile_start = jnp.arange(n_tiles, dtype=jnp.int32) * TM
    tile_exp = jnp.minimum(jnp.sum((ends[None, :] <= tile_start[:, None]).astype(jnp.int32), axis=1),
                           N_EXPERTS - 1)
    tile_on = (tile_start < ends[-1]).astype(jnp.int32)
    pos = pos.reshape(n // TM, TM, 2)
    return (tile_exp, tile_on,
            pos[:, :, 0].reshape(n // TM, 1, TM), pos[:, :, 1].reshape(n // TM, 1, TM))


def _gather_rows(src_hbm, idx_ref, dst, sem):
    def body(j, carry):
        pltpu.make_async_copy(src_hbm.at[pl.ds(idx_ref[0, 0, j], 1)], dst.at[pl.ds(j, 1)], sem).start()
        return carry

    lax.fori_loop(0, TM, body, 0, unroll=GATHER_UNROLL)


def _wait_rows(src_hbm, dst, sem):
    pltpu.make_async_copy(src_hbm.at[pl.ds(0, TM)], dst, sem).wait()


def _pack_bf16_pair(lo, hi):
    lo_b = lax.bitcast_convert_type(lo.astype(BF16).astype(F32), jnp.uint32) >> 16
    hi_b = lax.bitcast_convert_type(hi.astype(BF16).astype(F32), jnp.uint32) & jnp.uint32(0xFFFF0000)
    return hi_b | lo_b


def _unpack_bf16_pair(w):
    return (lax.bitcast_convert_type(w << 16, F32),
            lax.bitcast_convert_type(w & jnp.uint32(0xFFFF0000), F32))


def _scatter_kernel(p1_ref, p2_ref, h2_ref, init_hbm, xs_hbm, sem):
    del init_hbm

    def body(j, carry):
        row = h2_ref.at[pl.ds(j, 1)]
        pltpu.make_async_copy(row, xs_hbm.at[pl.ds(p1_ref[0, 0, j], 1)], sem).start()
        pltpu.make_async_copy(row, xs_hbm.at[pl.ds(p2_ref[0, 0, j], 1)], sem).start()
        return carry

    lax.fori_loop(0, TM, body, 0, unroll=GATHER_UNROLL)
    for _ in range(2):
        pltpu.make_async_copy(h2_ref, xs_hbm.at[pl.ds(0, TM)], sem).wait()


def _scatter_rows(h2p, pos1, pos2, init):
    n, dh = h2p.shape
    n_rows = init.shape[0]
    idx = pl.BlockSpec((1, 1, TM), lambda g: (g, 0, 0), memory_space=pltpu.SMEM)
    return pl.pallas_call(
        _scatter_kernel,
        grid=(n // TM,),
        in_specs=[idx, idx, pl.BlockSpec((TM, dh), lambda g: (g, 0)), pl.BlockSpec(memory_space=pl.ANY)],
        out_specs=pl.BlockSpec(memory_space=pl.ANY),
        out_shape=jax.ShapeDtypeStruct((n_rows, dh), jnp.uint32),
        scratch_shapes=[pltpu.SemaphoreType.DMA(())],
        input_output_aliases={3: 0},
        compiler_params=_cparams(("arbitrary",)),
        name="moe_scatter",
    )(pos1, pos2, h2p, init)


def _experts_kernel(texp_ref, ton_ref, x_ref, wg_ref, wu_ref, wd_ref, y_ref, wgu_b, wd_b):
    r = pl.program_id(0)
    half = wgu_b.shape[0] // 2

    @pl.when(ton_ref[r] == 1)
    def _():
        @pl.when(jnp.logical_or(r == 0, texp_ref[r] != texp_ref[jnp.maximum(r - 1, 0)]))
        def _():
            wgu_b[:, :D_EXPERT] = wg_ref[0, 0].astype(BF16)
            wgu_b[:, D_EXPERT:] = wu_ref[0, 0].astype(BF16)
            wd_b[...] = wd_ref[0, 0].astype(BF16)

        x_lo, x_hi = _unpack_bf16_pair(x_ref[...])
        gu = (jnp.dot(x_lo.astype(BF16), wgu_b[0:half], preferred_element_type=F32)
              + jnp.dot(x_hi.astype(BF16), wgu_b[half:], preferred_element_type=F32))
        act = (_silu(gu[:, :D_EXPERT]) * gu[:, D_EXPERT:]).astype(BF16)
        y = jnp.dot(act, wd_b[...], preferred_element_type=F32)
        y_ref[...] = _pack_bf16_pair(y[:, :half], y[:, half:])

    @pl.when(ton_ref[r] == 0)
    def _():
        y_ref[...] = jnp.zeros_like(y_ref)


def _experts(xs, tile_exp, tile_on, w_gate, w_up, w_down, layer):
    n_rows, dh = xs.shape
    d = 2 * dh
    wspec = lambda shape: pl.BlockSpec((1, 1) + shape, lambda r, te, to: (layer, te[r], 0, 0))
    rows = pl.BlockSpec((TM, dh), lambda r, te, to: (r, 0))
    return pl.pallas_call(
        _experts_kernel,
        grid_spec=pltpu.PrefetchScalarGridSpec(
            num_scalar_prefetch=2,
            grid=(n_rows // TM,),
            in_specs=[rows, wspec((d, D_EXPERT)), wspec((d, D_EXPERT)), wspec((D_EXPERT, d))],
            out_specs=rows,
            scratch_shapes=[pltpu.VMEM((d, 2 * D_EXPERT), BF16), pltpu.VMEM((D_EXPERT, d), BF16)],
        ),
        out_shape=jax.ShapeDtypeStruct((n_rows, dh), jnp.uint32),
        compiler_params=_cparams(("arbitrary",)),
        name="moe_experts",
    )(tile_exp, tile_on, xs, w_gate, w_up, w_down)


def _combine_kernel(p1_ref, p2_ref, p1n_ref, p2n_ref, route_ref, x_ref, mod_ref, y_hbm, o_ref, buf, sem):
    g = pl.program_id(0)
    slot = g % 2
    half = buf.shape[-1]

    def gather(pa, pb, s):
        _gather_rows(y_hbm, pa, buf.at[s, 0], sem.at[s])
        _gather_rows(y_hbm, pb, buf.at[s, 1], sem.at[s])

    @pl.when(g == 0)
    def _():
        gather(p1_ref, p2_ref, 0)

    @pl.when(g + 1 < pl.num_programs(0))
    def _():
        gather(p1n_ref, p2n_ref, 1 - slot)

    _wait_rows(y_hbm, buf.at[slot, 0], sem.at[slot])
    _wait_rows(y_hbm, buf.at[slot, 1], sem.at[slot])
    route = route_ref[...]
    w1, w2 = route[:, RT_W1:RT_W1 + 1], route[:, RT_W1 + 1:RT_W1 + 2]
    gate = mod_ref[0][5:6]
    for part, (y1, y2) in enumerate(zip(_unpack_bf16_pair(buf[slot, 0]), _unpack_bf16_pair(buf[slot, 1]))):
        cols = slice(part * half, (part + 1) * half)
        o_ref[:, cols] = x_ref[:, cols] + gate[:, cols] * (w1 * y1 + w2 * y2)


def _combine(y, pos1, pos2, route, xs2, mods, nt):
    n, d = xs2.shape
    n_tok_tiles = n // TM
    idx = lambda nxt: pl.BlockSpec((1, 1, TM), lambda g: (jnp.minimum(g + nxt, n_tok_tiles - 1), 0, 0),
                                   memory_space=pltpu.SMEM)
    tok = lambda w: pl.BlockSpec((TM, w), lambda g: (g, 0))
    return pl.pallas_call(
        _combine_kernel,
        grid=(n_tok_tiles,),
        in_specs=[
            idx(0), idx(0), idx(1), idx(1), tok(LANE), tok(d),
            pl.BlockSpec((1, 6, d), lambda g: (_mod_row(g // nt, g % nt), 0, 0)),
            pl.BlockSpec(memory_space=pl.ANY),
        ],
        out_specs=tok(d),
        out_shape=jax.ShapeDtypeStruct((n, d), F32),
        scratch_shapes=[pltpu.VMEM((2, 2, TM, d // 2), jnp.uint32), pltpu.SemaphoreType.DMA((2,))],
        compiler_params=_cparams(("arbitrary",)),
        name="moe_combine",
    )(pos1, pos2, pos1, pos2, route, xs2, mods, y)


def _moe(h2p, route, xs, mods, w_gate, w_up, w_down, layer, sorted_buf):
    b, t, d = xs.shape
    n = b * t
    n_tiles = 2 * n // TM + N_EXPERTS
    route2 = route.reshape(n, LANE)
    tile_exp, tile_on, pos1, pos2 = _dispatch(route2, n_tiles)
    if sorted_buf is None:
        sorted_buf = jnp.zeros((n_tiles * TM, d // 2), jnp.uint32)
    xs_sorted = _scatter_rows(h2p.reshape(n, d // 2), pos1, pos2, sorted_buf)
    y = _experts(xs_sorted, tile_exp, tile_on, w_gate, w_up, w_down, layer)
    out = _combine(y, pos1, pos2, route2, xs.reshape(n, d), mods, t // TM).reshape(b, t, d)
    return out, xs_sorted


def _rope_tables(n_ctx, n_lat):
    rows = n_lat // GRID_W
    row = jnp.broadcast_to(jnp.arange(rows, dtype=F32)[:, None], (rows, GRID_W)).reshape(-1)
    col = jnp.broadcast_to(jnp.arange(GRID_W, dtype=F32)[None, :], (rows, GRID_W)).reshape(-1)
    n_freq = MLA_ROPE // 4
    inv = ROPE_THETA ** (-jnp.arange(n_freq, dtype=F32) / n_freq)
    ang = jnp.concatenate([row[:, None] * inv, col[:, None] * inv], axis=-1)
    cos, sin = jnp.cos(ang), jnp.sin(ang)
    half = MLA_ROPE // 2
    z = lambda w: jnp.zeros((n_lat, w), F32)
    o = lambda w: jnp.ones((n_lat, w), F32)
    tail = LANE - MLA_QK
    cos_t = jnp.concatenate([o(MLA_NOPE), cos, cos, o(tail)], axis=-1)
    sa_t = jnp.concatenate([z(MLA_NOPE + half), sin, z(tail)], axis=-1)
    sb_t = jnp.concatenate([z(MLA_NOPE), -sin, z(half + tail)], axis=-1)
    ctx1 = jnp.ones((n_ctx, LANE), F32)
    ctx0 = jnp.zeros((n_ctx, LANE), F32)
    return (jnp.concatenate([ctx1, cos_t], 0), jnp.concatenate([ctx0, sa_t], 0),
            jnp.concatenate([ctx0, sb_t], 0))


def _pad_cols(a, width):
    return jnp.pad(a, [(0, 0)] * (a.ndim - 1) + [(0, width - a.shape[-1])])


def _layer_weights(w_in, w_uq, w_ukv, q_norm_w, k_norm_w, ml_conv_w, ml_wq, ml_wk, ml_gate_b,
                   gla_wa, gla_ba, w_out, w_grp, b_grp, w_erouter, b_erouter):
    d = w_in.shape[0]
    o = np.cumsum((0, Q_LORA, KV_LORA, MLA_ROPE, ML_W, ML_W, ML_W, 4 * ML_HEADS, GLA_HEADS * GLA_DK,
                   GLA_HEADS * GLA_DK, GLA_HEADS * GLA_DV, GLA_HEADS * GLA_DV, 2 * GLA_LR))
    seg = lambda j: w_in[:, o[j]:o[j + 1]]
    cq, ckv, kr, mx, mv, mo, mg, gq, gk, gv, gr, ga = (seg(j) for j in range(12))
    z = lambda w: jnp.zeros((d, w), F32)

    def small(di):
        return jnp.concatenate([mg[:, di * 8:(di + 1) * 8], ga[:, di * GLA_LR:(di + 1) * GLA_LR],
                                z(SM_KR - SM_GA - GLA_LR), kr, z(LANE - SM_KR - MLA_ROPE)], axis=-1)

    w_in_p = jnp.concatenate([cq, mx, mv, mo, gv, gr, ckv, gq, gk, small(0), small(1)], axis=-1).astype(BF16)

    wuq_p = _pad_cols(w_uq.reshape(Q_LORA, MLA_HEADS, MLA_QK), LANE).reshape(Q_LORA, -1).astype(BF16)
    ukv = w_ukv.reshape(KV_LORA, MLA_HEADS, MLA_NOPE + MLA_V)
    wuk_p = _pad_cols(ukv[..., :MLA_NOPE], LANE).reshape(KV_LORA, -1).astype(BF16)
    wuv_p = _pad_cols(ukv[..., MLA_NOPE:], LANE).reshape(KV_LORA, -1).astype(BF16)
    qn_p = _pad_cols(q_norm_w.reshape(1, -1), LANE)
    kn_p = _pad_cols(k_norm_w.reshape(1, -1), LANE)

    conv_w8 = jnp.pad(ml_conv_w, ((0, 8 - ML_CONV), (0, 0)))
    bd = lambda w: jax.scipy.linalg.block_diag(*[w[h] for h in range(ML_HEADS)])
    wq_bd = (bd(ml_wq) * (ML_DH ** -0.5)).astype(BF16)
    wk_bd = bd(ml_wk).astype(BF16)
    gate_b_p = _pad_cols(ml_gate_b.reshape(2, 1, 2 * ML_HEADS), LANE)

    wa_p = jnp.pad(gla_wa, ((0, 0), (SM_GA, LANE - SM_GA - GLA_LR), (0, 0)))
    ba_p = gla_ba.reshape(2, 1, -1)

    wr_p = _pad_cols(jnp.concatenate([w_grp, w_erouter], axis=-1), LANE)
    br_p = _pad_cols(jnp.concatenate([b_grp, b_erouter]).reshape(1, -1), LANE)
    return dict(w_in_p=w_in_p, wuq_p=wuq_p, wuk_p=wuk_p, wuv_p=wuv_p, qn_p=qn_p, kn_p=kn_p,
                conv_w8=conv_w8, wq_bd=wq_bd, wk_bd=wk_bd, gate_b_p=gate_b_p, wa_p=wa_p, ba_p=ba_p,
                w_out_b=w_out.astype(BF16), wr_p=wr_p, br_p=br_p)


def kernel(x, c, ctx, c_ctx, w_mod, b_mod, norm1_w, w_in, q_a_norm, w_uq, kv_a_norm, w_ukv,
           q_norm_w, k_norm_w, ml_conv_w, ml_conv_b, ml_wq, ml_wk, ml_gate_b, ml_norm_w, ml_skip,
           gla_wa, gla_ba, gla_norm_w, w_out, norm2_w, w_grp, b_grp, w_erouter, b_erouter,
           w_gate, w_up, w_down):
    b, s, d = x.shape
    n_ctx = ctx.shape[1]
    depth = w_mod.shape[0]
    assert n_ctx == TM and s % TM == 0 and b == 2

    cc = jnp.concatenate([c, c_ctx[None, :], jnp.zeros((8 - b - 1, d), F32)], axis=0)
    mods_all = _mods(cc, w_mod, b_mod).reshape(depth, 8, 6, d)
    cos_t, sa_t, sb_t = _rope_tables(n_ctx, s)
    xs = jnp.concatenate([ctx, x], axis=1)

    sorted_buf = None
    for l in range(depth):
        w = _layer_weights(w_in[l], w_uq[l], w_ukv[l], q_norm_w[l], k_norm_w[l], ml_conv_w[l],
                           ml_wq[l], ml_wk[l], ml_gate_b[l], gla_wa[l], gla_ba[l], w_out[l],
                           w_grp[l], b_grp[l], w_erouter[l], b_erouter[l])
        mods = mods_all[l]
        p = _inproj(xs, mods, norm1_w[l], w["w_in_p"])
        q, k, v = _mla_prep(p, q_a_norm[l], w["wuq_p"], kv_a_norm[l], w["wuk_p"], w["wuv_p"],
                            w["qn_p"], w["kn_p"], cos_t, sa_t, sb_t)
        a_lat = _attention_lat(q, k, v)
        a_ctx = _attention_ctx(q, k, v) if l < depth - 1 else jnp.zeros((b, TM, MLA_HEADS * MLA_V), BF16)
        xconv, mq, mk = _ml_prep(p, w["conv_w8"], ml_conv_b[l], w["wq_bd"], w["wk_bd"])
        mh = _ml_scan(mq, mk, p, w["gate_b_p"])
        go = _gla_scan(p, w["wa_p"], w["ba_p"])
        xs, h2, route = _outproj(a_ctx, a_lat, mh, xconv, p, go, xs, mods, ml_norm_w[l], ml_skip[l], gla_norm_w[l],
                                w["w_out_b"], norm2_w[l], w["wr_p"], w["br_p"])
        xs, sorted_buf = _moe(h2, route, xs, mods, w_gate, w_up, w_down, l, sorted_buf)
    return xs[:, n_ctx:, :]
```
